```python
import jax, jax.numpy as jnp
from jax import lax
import numpy as np

D_MODEL = 2048
BATCH = 4
SEQ = 2048
DEPTH = 1
DEC_BATCH = 128
DEC_SEQ = 4
PAST_LEN = 16384
PAGE_SIZE = 128

HEAD_DIM = 64
D_RWKV = D_MODEL // 2
N_HEADS = D_RWKV // HEAD_DIM
RANK_W = 64
RANK_A = 64
RANK_G = 160
D_SHIFT = 3 * D_RWKV + RANK_W + RANK_A + RANK_G
D_POOL = D_MODEL // 2
POOL_WINDOWS = (2, 4, 8, 16)
N_POOL_GROUPS = len(POOL_WINDOWS)
POOL_GROUP = D_POOL // N_POOL_GROUPS
POOL_HIST = max(POOL_WINDOWS) - 1
D_IN = D_SHIFT + D_POOL + 2 * D_MODEL
D_FF = 5632
CONV_W = 3
NORM_EPS = 1e-6
GN_EPS = 64e-5

kernel_name = 'rwkv7_pool_gated_hybrid_step'


def _rmsnorm(x, g):
    xf = x.astype(jnp.float32)
    y = xf * lax.rsqrt(jnp.mean(xf * xf, axis=-1, keepdims=True) + NORM_EPS)
    return (y * g.astype(jnp.float32)).astype(x.dtype)


def _wkv_scan(s0, r, decay, k, v, a, b):
    def step(s, inp):
        r_t, w_t, k_t, v_t, a_t, b_t = inp
        sa = jnp.einsum('bhvk,bhk->bhv', s, a_t)
        s = s * w_t[:, :, None, :] + sa[..., None] * b_t[:, :, None, :] + v_t[..., None] * k_t[:, :, None, :]
        return s, jnp.einsum('bhvk,bhk->bhv', s, r_t)
    xs = tuple(jnp.moveaxis(t, 1, 0) for t in (r, decay, k, v, a, b))
    s, ys = lax.scan(step, s0, xs)
    return jnp.moveaxis(ys, 0, 1), s


def _rwkv7_branch(zs, wkv0, w0, w2, a0, a2, g2, k_k, k_a, r_k, lnx_w, lnx_b):
    B, T, _ = zs.shape
    f = zs.astype(jnp.float32)
    cuts = [D_RWKV, 2 * D_RWKV, 3 * D_RWKV, 3 * D_RWKV + RANK_W, 3 * D_RWKV + RANK_W + RANK_A]
    r, k, v, zw, za, zg = jnp.split(f, cuts, axis=-1)
    w_log = -jax.nn.softplus(-(w0 + jnp.tanh(zw) @ w2)) - 0.5
    decay = jnp.exp(-jnp.exp(w_log))
    a = jax.nn.sigmoid(a0 + za @ a2)
    g = jax.nn.sigmoid(zg) @ g2
    hs = lambda t: t.reshape(B, T, N_HEADS, HEAD_DIM)
    kk = hs(k * k_k)
    kk = kk / jnp.maximum(jnp.sqrt(jnp.sum(kk * kk, axis=-1, keepdims=True)), 1e-12)
    k = k * (1.0 + (a - 1.0) * k_a)
    rh, kh, vh, ah = hs(r), hs(k), hs(v), hs(a)
    y, wkv = _wkv_scan(wkv0.astype(jnp.float32), rh, hs(decay), kh, vh, -kk, kk * ah)
    mu = jnp.mean(y, axis=-1, keepdims=True)
    var = jnp.mean(jnp.square(y - mu), axis=-1, keepdims=True)
    y = ((y - mu) * lax.rsqrt(var + GN_EPS)).reshape(B, T, D_RWKV) * lnx_w + lnx_b
    bonus = jnp.sum(rh * kh * r_k, axis=-1, keepdims=True) * vh
    y = (y + bonus.reshape(B, T, D_RWKV)) * g
    return y.astype(zs.dtype), wkv.astype(wkv0.dtype)


def _pool_branch(zp, pool0, pos0, pool_w, pool_scale):
    B, T, _ = zp.shape
    buf = jnp.concatenate([pool0.astype(zp.dtype), zp], axis=1)
    c = jnp.cumsum(buf.astype(jnp.float32), axis=1)
    c = jnp.concatenate([jnp.zeros((B, 1, D_POOL), jnp.float32), c], axis=1)
    end = c[:, POOL_HIST + 1:]
    pos = pos0 + jnp.arange(T)
    means = []
    for gi, win in enumerate(POOL_WINDOWS):
        lo, hi = gi * POOL_GROUP, (gi + 1) * POOL_GROUP
        start = c[:, POOL_HIST + 1 - win: POOL_HIST + 1 - win + T, lo:hi]
        cnt = jnp.minimum(win, pos + 1).astype(jnp.float32)[None, :, None]
        means.append((end[..., lo:hi] - start) / cnt)
    d = jnp.concatenate(means, axis=-1) - zp.astype(jnp.float32)
    d = d.reshape(B, T, N_POOL_GROUPS, POOL_GROUP)
    y = jnp.einsum('btgc,gcd->btgd', d, pool_w).reshape(B, T, D_POOL) * pool_scale
    return y.astype(zp.dtype), buf[:, -POOL_HIST:]


def _conv_ffn(h, conv0, w_ffn_in, conv_w, conv_b, w_ffn_out):
    T = h.shape[1]
    gate, up = jnp.split(h @ w_ffn_in, 2, axis=-1)
    buf = jnp.concatenate([conv0.astype(gate.dtype), gate], axis=1)
    cv = conv_b + sum(conv_w[j] * buf[:, j:j + T] for j in range(CONV_W))
    out = (jax.nn.gelu(cv, approximate=True) * up) @ w_ffn_out
    return out, buf[:, -(CONV_W - 1):]


def _layer(x, st_shift, st_wkv, st_pool, st_conv, pos0,
           norm_pre_mix, w_in, mu_shift, w0, w2, a0, a2, g2, k_k, k_a, r_k, lnx_w, lnx_b,
           w_branch_a, pool_w, pool_scale, w_branch_b, w_out, norm_post_mix,
           norm_pre_ffn, w_ffn_in, conv_w, conv_b, w_ffn_out, norm_post_ffn):
    h = _rmsnorm(x, norm_pre_mix)
    z = h @ w_in
    zs, zp, zga, zgb = jnp.split(z, [D_SHIFT, D_SHIFT + D_POOL, D_SHIFT + D_POOL + D_MODEL], axis=-1)
    prev = jnp.concatenate([st_shift.astype(zs.dtype), zs[:, :-1]], axis=1)
    zs_mix = zs + (prev - zs) * mu_shift
    ya, new_wkv = _rwkv7_branch(zs_mix, st_wkv, w0, w2, a0, a2, g2, k_k, k_a, r_k, lnx_w, lnx_b)
    yb, new_pool = _pool_branch(zp, st_pool, pos0, pool_w, pool_scale)
    m = jax.nn.sigmoid(zga) * (ya @ w_branch_a) + jax.nn.sigmoid(zgb) * (yb @ w_branch_b)
    x = x + _rmsnorm(m @ w_out, norm_post_mix)
    f, new_conv = _conv_ffn(_rmsnorm(x, norm_pre_ffn), st_conv, w_ffn_in, conv_w, conv_b, w_ffn_out)
    x = x + _rmsnorm(f.astype(x.dtype), norm_post_ffn)
    return x, zs[:, -1:], new_wkv, new_pool, new_conv


def setup_inputs(seed: int = 0) -> dict:
    key = jax.random.key(seed)
    ks = iter(jax.random.split(key, 40))
    nrm = lambda shape, scale: jax.random.normal(next(ks), shape, jnp.float32) * scale
    L = DEPTH
    return {
        'x_prompt': nrm((BATCH, SEQ, D_MODEL), 1.0),
        'x_sample': nrm((DEC_BATCH, DEC_SEQ, D_MODEL), 1.0),
        'state_shift': nrm((L, DEC_BATCH, 1, D_SHIFT), 1.0),
        'state_wkv': nrm((L, DEC_BATCH, N_HEADS, HEAD_DIM, HEAD_DIM), 0.3),
        'state_pool': nrm((L, DEC_BATCH, POOL_HIST, D_POOL), 1.0),
        'state_conv': nrm((L, DEC_BATCH, CONV_W - 1, D_FF), 1.0),
        'norm_pre_mix': 1.0 + nrm((L, D_MODEL), 0.02),
        'w_in': nrm((L, D_MODEL, D_IN), D_MODEL ** -0.5),
        'mu_shift': jax.random.uniform(next(ks), (L, D_SHIFT), jnp.float32),
        'w0': nrm((L, D_RWKV), 0.5) - 0.5,
        'w2': nrm((L, RANK_W, D_RWKV), 0.1),
        'a0': nrm((L, D_RWKV), 0.1),
        'a2': nrm((L, RANK_A, D_RWKV), 0.1),
        'g2': nrm((L, RANK_G, D_RWKV), RANK_G ** -0.5),
        'k_k': 0.85 + nrm((L, D_RWKV), 0.02),
        'k_a': 1.0 + nrm((L, D_RWKV), 0.02),
        'r_k': nrm((L, N_HEADS, HEAD_DIM), 0.1),
        'lnx_w': 1.0 + nrm((L, D_RWKV), 0.02),
        'lnx_b': nrm((L, D_RWKV), 0.02),
        'w_branch_a': nrm((L, D_RWKV, D_MODEL), D_RWKV ** -0.5),
        'pool_w': nrm((L, N_POOL_GROUPS, POOL_GROUP, POOL_GROUP), POOL_GROUP ** -0.5),
        'pool_scale': 1.0 + nrm((L, D_POOL), 0.02),
        'w_branch_b': nrm((L, D_POOL, D_MODEL), D_POOL ** -0.5),
        'w_out': nrm((L, D_MODEL, D_MODEL), D_MODEL ** -0.5),
        'norm_post_mix': 1.0 + nrm((L, D_MODEL), 0.02),
        'norm_pre_ffn': 1.0 + nrm((L, D_MODEL), 0.02),
        'w_ffn_in': nrm((L, D_MODEL, 2 * D_FF), D_MODEL ** -0.5),
        'conv_w': nrm((L, CONV_W, D_FF), CONV_W ** -0.5),
        'conv_b': nrm((L, D_FF), 0.02),
        'w_ffn_out': nrm((L, D_FF, D_MODEL), D_FF ** -0.5),
        'norm_post_ffn': 1.0 + nrm((L, D_MODEL), 0.02),
    }


def reference(x_prompt, x_sample, state_shift, state_wkv, state_pool, state_conv,
              norm_pre_mix, w_in, mu_shift, w0, w2, a0, a2, g2, k_k, k_a, r_k, lnx_w, lnx_b,
              w_branch_a, pool_w, pool_scale, w_branch_b, w_out, norm_post_mix,
              norm_pre_ffn, w_ffn_in, conv_w, conv_b, w_ffn_out, norm_post_ffn):
    weights = (norm_pre_mix, w_in, mu_shift, w0, w2, a0, a2, g2, k_k, k_a, r_k, lnx_w, lnx_b,
               w_branch_a, pool_w, pool_scale, w_branch_b, w_out, norm_post_mix,
               norm_pre_ffn, w_ffn_in, conv_w, conv_b, w_ffn_out, norm_post_ffn)
    bp = x_prompt.shape[0]
    dt = x_prompt.dtype
    yp, ys = x_prompt, x_sample
    p_shift, p_wkv, p_pool, p_conv = [], [], [], []
    s_shift, s_wkv, s_pool, s_conv = [], [], [], []
    for l in range(DEPTH):
        p = tuple(w[l] for w in weights)
        yp, a1, a2_, a3, a4 = _layer(
            yp,
            jnp.zeros((bp, 1, D_SHIFT), dt),
            jnp.zeros((bp, N_HEADS, HEAD_DIM, HEAD_DIM), state_wkv.dtype),
            jnp.zeros((bp, POOL_HIST, D_POOL), dt),
            jnp.zeros((bp, CONV_W - 1, D_FF), dt),
            0, *p)
        p_shift.append(a1); p_wkv.append(a2_); p_pool.append(a3); p_conv.append(a4)
        ys, b1, b2, b3, b4 = _layer(ys, state_shift[l], state_wkv[l], state_pool[l], state_conv[l],
                                    PAST_LEN, *p)
        s_shift.append(b1); s_wkv.append(b2); s_pool.append(b3); s_conv.append(b4)
    return (yp, ys,
            jnp.stack(p_shift), jnp.stack(p_wkv), jnp.stack(p_pool), jnp.stack(p_conv),
            jnp.stack(s_shift), jnp.stack(s_wkv), jnp.stack(s_pool), jnp.stack(s_conv))
```

```python
import functools

import jax
import jax.numpy as jnp
from jax import lax
from jax.experimental import pallas as pl
from jax.experimental.pallas import tpu as pltpu

F32 = jnp.float32
BF16 = jnp.bfloat16

D_MODEL = 2048
HEAD_DIM = 64
D_RWKV = 1024
N_HEADS = 16
RANK_W, RANK_A, RANK_G = 64, 64, 160
D_LORA = RANK_W + RANK_A + RANK_G
D_LORA_PAD = 384
D_SHIFT = 3 * D_RWKV + D_LORA
D_SHIFT_PAD = 3 * D_RWKV + 512
D_POOL = 1024
POOL_WINDOWS = (2, 4, 8, 16)
POOL_GROUP = 256
POOL_HIST = 15
D_FF = 5632
CONV_W = 3
NORM_EPS = 1e-6
GN_EPS = 64e-5
PAST_LEN = 16384

LANES_V7X = 128
MXU_DIM_V7X = 256
HEADS_PER_GROUP = MXU_DIM_V7X // HEAD_DIM
N_GROUPS = N_HEADS // HEADS_PER_GROUP
CHUNK = 64
VMEM_LIMIT_V7X = 56 * 1024 * 1024

COL_TILE = 512
N_COL_TILES = (D_SHIFT_PAD + D_POOL + 2 * D_MODEL) // COL_TILE
ZS_TILES = D_SHIFT_PAD // COL_TILE
ZP_TILES = D_POOL // COL_TILE
ZG_TILES = D_MODEL // COL_TILE


def _params(*sem):
    return pltpu.CompilerParams(dimension_semantics=sem, vmem_limit_bytes=VMEM_LIMIT_V7X)


def _mm(a, b):
    return jnp.dot(a.astype(BF16), b.astype(BF16), preferred_element_type=F32)


def _mm_nt(a, b):
    return lax.dot_general(a.astype(BF16), b.astype(BF16), (((1,), (1,)), ((), ())),
                           preferred_element_type=F32)


def _split_hi_lo(x):
    hi = x.astype(BF16)
    lo = (x - hi.astype(F32)).astype(BF16)
    return hi, lo


def _head_sum(x, ones_bd):
    hi, lo = _split_hi_lo(x)
    outs = []
    for gi in range(x.shape[1] // MXU_DIM_V7X):
        sl = slice(MXU_DIM_V7X * gi, MXU_DIM_V7X * (gi + 1))
        outs.append(jnp.dot(hi[:, sl], ones_bd, preferred_element_type=F32)
                    + jnp.dot(lo[:, sl], ones_bd, preferred_element_type=F32))
    return jnp.concatenate(outs, axis=1)


def _rmsnorm(x, g):
    return x * lax.rsqrt(jnp.mean(x * x, axis=-1, keepdims=True) + NORM_EPS) * g


def _in_proj_kernel(x_ref, g_ref, w_ref, zs_ref, zp_ref, zga_ref, zgb_ref, h_ref):
    j = pl.program_id(1)

    @pl.when(j == 0)
    def _():
        h_ref[...] = _rmsnorm(x_ref[...], g_ref[...]).astype(BF16)

    z = jnp.dot(h_ref[...], w_ref[...], preferred_element_type=F32)

    @pl.when(j < ZS_TILES)
    def _():
        zs_ref[...] = z

    @pl.when((j >= ZS_TILES) & (j < ZS_TILES + ZP_TILES))
    def _():
        zp_ref[...] = z

    @pl.when((j >= ZS_TILES + ZP_TILES) & (j < ZS_TILES + ZP_TILES + ZG_TILES))
    def _():
        zga_ref[...] = z

    @pl.when(j >= ZS_TILES + ZP_TILES + ZG_TILES)
    def _():
        zgb_ref[...] = z


def _in_proj(x2d, g, w_pad, tm):
    n = x2d.shape[0]

    def sect(first, count):
        return lambda i, j: (i, jnp.clip(j - first, 0, count - 1))

    return pl.pallas_call(
        _in_proj_kernel,
        grid=(n // tm, N_COL_TILES),
        in_specs=[
            pl.BlockSpec((tm, D_MODEL), lambda i, j: (i, 0)),
            pl.BlockSpec((1, D_MODEL), lambda i, j: (0, 0)),
            pl.BlockSpec((D_MODEL, COL_TILE), lambda i, j: (0, j)),
        ],
        out_specs=[
            pl.BlockSpec((tm, COL_TILE), sect(0, ZS_TILES)),
            pl.BlockSpec((tm, COL_TILE), sect(ZS_TILES, ZP_TILES)),
            pl.BlockSpec((tm, COL_TILE), sect(ZS_TILES + ZP_TILES, ZG_TILES)),
            pl.BlockSpec((tm, COL_TILE), sect(ZS_TILES + ZP_TILES + ZG_TILES, ZG_TILES)),
        ],
        out_shape=[
            jax.ShapeDtypeStruct((n, D_SHIFT_PAD), F32),
            jax.ShapeDtypeStruct((n, D_POOL), F32),
            jax.ShapeDtypeStruct((n, D_MODEL), F32),
            jax.ShapeDtypeStruct((n, D_MODEL), F32),
        ],
        scratch_shapes=[pltpu.VMEM((tm, D_MODEL), BF16)],
        compiler_params=_params("parallel", "arbitrary"),
        name="in_proj",
    )(x2d, g, w_pad)


def _prep_math(zs, prev, mu, w_lora, w0, a0, k_k, k_a, ones_bd):
    f = zs + (prev - zs) * mu
    r = f[:, 0:D_RWKV]
    k = f[:, D_RWKV:2 * D_RWKV]
    v = f[:, 2 * D_RWKV:3 * D_RWKV]
    low = f[:, 3 * D_RWKV:3 * D_RWKV + D_LORA_PAD]
    lane = lax.broadcasted_iota(jnp.int32, low.shape, 1)
    act = jnp.where(lane < RANK_W, jnp.tanh(low),
                    jnp.where(lane < RANK_W + RANK_A, low,
                              jnp.where(lane < D_LORA, jax.nn.sigmoid(low), 0.0)))
    lora = _mm(act, w_lora)
    u = w0 + lora[:, 0:D_RWKV]
    w_log = -(jnp.maximum(-u, 0.0) + jnp.log1p(jnp.exp(-jnp.abs(u)))) - 0.5
    lw = -jnp.exp(w_log)
    a = jax.nn.sigmoid(a0 + lora[:, D_RWKV:2 * D_RWKV])
    g = lora[:, 2 * D_RWKV:3 * D_RWKV]
    kk = k * k_k
    nrm = jnp.sqrt(_head_sum(kk * kk, ones_bd))
    kk = kk / jnp.maximum(nrm, 1e-12)
    k2 = k * (1.0 + (a - 1.0) * k_a)
    return r, lw, k2, v, -kk, kk * a, g


def _prep_prompt_kernel(zs_ref, mu_ref, wl_ref, w0_ref, a0_ref, kk_ref, ka_ref, ones_ref,
                        r_ref, lw_ref, k2_ref, v_ref, as_ref, bs_ref, g_ref, carry_ref):
    i = pl.program_id(1)

    @pl.when(i == 0)
    def _():
        carry_ref[...] = jnp.zeros_like(carry_ref)

    zs = zs_ref[...]
    rolled = pltpu.roll(zs, 1, 0)
    row = lax.broadcasted_iota(jnp.int32, zs.shape, 0)
    prev = jnp.where(row == 0, carry_ref[0:1, :], rolled)
    carry_ref[0:1, :] = zs[zs.shape[0] - 1:, :]
    outs = _prep_math(zs, prev, mu_ref[...], wl_ref[...], w0_ref[...], a0_ref[...],
                      kk_ref[...], ka_ref[...], ones_ref[...])
    for o_ref, o in zip((r_ref, lw_ref, k2_ref, v_ref, as_ref, bs_ref, g_ref), outs):
        o_ref[...] = o


def _prep_sample_kernel(zs_ref, st_ref, mu_ref, wl_ref, w0_ref, a0_ref, kk_ref, ka_ref, ones_ref,
                        r_ref, lw_ref, k2_ref, v_ref, as_ref, bs_ref, g_ref, carry_ref):
    t = pl.program_id(0)

    @pl.when(t == 0)
    def _():
        carry_ref[...] = st_ref[...]

    zs = zs_ref[...]
    prev = carry_ref[...]
    carry_ref[...] = zs
    outs = _prep_math(zs, prev, mu_ref[...], wl_ref[...], w0_ref[...], a0_ref[...],
                      kk_ref[...], ka_ref[...], ones_ref[...])
    for o_ref, o in zip((r_ref, lw_ref, k2_ref, v_ref, as_ref, bs_ref, g_ref), outs):
        o_ref[...] = o


def _const_spec(shape):
    nd = len(shape)
    return pl.BlockSpec(shape, lambda *_: (0,) * nd)


def _prep_prompt(zs, wts, tm):
    b, t, _ = zs.shape
    outs = [jax.ShapeDtypeStruct((b, t, D_RWKV), F32)] * 7
    row_spec = pl.BlockSpec((None, tm, D_RWKV), lambda bi, i: (bi, i, 0))
    return pl.pallas_call(
        _prep_prompt_kernel,
        grid=(b, t // tm),
        in_specs=[pl.BlockSpec((None, tm, D_SHIFT_PAD), lambda bi, i: (bi, i, 0))]
        + [_const_spec(w.shape) for w in wts],
        out_specs=[row_spec] * 7,
        out_shape=outs,
        scratch_shapes=[pltpu.VMEM((8, D_SHIFT_PAD), F32)],
        compiler_params=_params("parallel", "arbitrary"),
        name="prep_prompt",
    )(zs, *wts)


def _prep_sample(zs, st, wts):
    t, b, _ = zs.shape
    outs = [jax.ShapeDtypeStruct((t, b, D_RWKV), F32)] * 7
    row_spec = pl.BlockSpec((None, b, D_RWKV), lambda ti: (ti, 0, 0))
    return pl.pallas_call(
        _prep_sample_kernel,
        grid=(t,),
        in_specs=[pl.BlockSpec((None, b, D_SHIFT_PAD), lambda ti: (ti, 0, 0)),
                  _const_spec(st.shape)]
        + [_const_spec(w.shape) for w in wts],
        out_specs=[row_spec] * 7,
        out_shape=outs,
        scratch_shapes=[pltpu.VMEM((b, D_SHIFT_PAD), F32)],
        compiler_params=_params("arbitrary"),
        name="prep_sample",
    )(zs, st, *wts)


def _expand_bd(a, bd_mask):
    return jnp.where(bd_mask, jnp.concatenate([a] * HEADS_PER_GROUP, axis=0), 0.0)


def _chunk_cumsum(x):
    row = lax.broadcasted_iota(jnp.int32, x.shape, 0)
    s = 1
    while s < x.shape[0]:
        x = x + jnp.where(row >= s, pltpu.roll(x, s, 0), 0.0)
        s *= 2
    return x


def _wkv_group_chunk(r, lw, k2, v, a_s, b_s, s_bd, masks):
    bd_mask, strict_mask, incl_mask, eye_c, head_masks = masks
    c = r.shape[0]
    cl = _chunk_cumsum(lw)
    cl_last = cl[c - 1:c, :]
    e_pos = jnp.exp(cl)
    e_neg = jnp.exp(-cl)
    e_prev = jnp.exp(cl - lw)
    e_rem = jnp.exp(cl_last - cl)
    at = a_s * e_prev
    rt = r * e_pos
    bt = b_s * e_neg
    kt = k2 * e_neg
    x = jnp.concatenate([at, rt], axis=0)
    ystack = jnp.concatenate([jnp.where(m, bt, 0.0) for m in head_masks]
                             + [jnp.where(m, kt, 0.0) for m in head_masks], axis=0)
    gram = _mm_nt(x, ystack)
    l_ab = jnp.where(strict_mask, gram[0:c, 0:4 * c], 0.0)
    l_ak = jnp.where(strict_mask, gram[0:c, 4 * c:8 * c], 0.0)
    m_rb = jnp.where(incl_mask, gram[c:2 * c, 0:4 * c], 0.0)
    m_rk = jnp.where(incl_mask, gram[c:2 * c, 4 * c:8 * c], 0.0)

    tinv = eye_c + l_ab
    lp = _mm(l_ab, _expand_bd(l_ab, bd_mask))
    p = 2
    while True:
        rhs = _expand_bd(lp, bd_mask)
        if 2 * p >= c:
            tinv = tinv + _mm(tinv, rhs)
            break
        res = _mm(jnp.concatenate([lp, tinv], axis=0), rhs)
        lp = res[0:c]
        tinv = tinv + res[c:2 * c]
        p *= 2

    pz = _mm_nt(x, s_bd)
    v_bd = _expand_bd(v, bd_mask)
    w = pz[0:c] + _mm(l_ak, v_bd)
    u = _mm(tinv, _expand_bd(w, bd_mask))
    y = pz[c:2 * c] + _mm(jnp.concatenate([m_rb, m_rk], axis=1),
                          jnp.concatenate([_expand_bd(u, bd_mask), v_bd], axis=0))
    uv_t = jnp.concatenate([u, v], axis=0).T
    bk = jnp.concatenate([b_s * e_rem, k2 * e_rem], axis=0)
    ds = _mm(uv_t, bk)
    s_new = s_bd * jnp.exp(cl_last) + jnp.where(bd_mask, ds, 0.0)
    return y, s_new


def _wkv_masks(c):
    n = HEADS_PER_GROUP * c
    rr = lax.broadcasted_iota(jnp.int32, (n, n), 0)
    cc = lax.broadcasted_iota(jnp.int32, (n, n), 1)
    bd_mask = (rr // c) == (cc // c)
    t = lax.broadcasted_iota(jnp.int32, (c, n), 0)
    s = lax.broadcasted_iota(jnp.int32, (c, n), 1) % c
    strict_mask = t > s
    incl_mask = t >= s
    eye_c = jnp.where(t == s, 1.0, 0.0).astype(F32)
    lane = lax.broadcasted_iota(jnp.int32, (c, MXU_DIM_V7X), 1)
    head_masks = [(lane // HEAD_DIM) == h for h in range(HEADS_PER_GROUP)]
    return bd_mask, strict_mask, incl_mask, eye_c, head_masks


def _wkv_prompt_kernel(r_ref, lw_ref, k2_ref, v_ref, as_ref, bs_ref, y_ref, sout_ref, s_ref):
    ci = pl.program_id(1)

    @pl.when(ci == 0)
    def _():
        s_ref[...] = jnp.zeros_like(s_ref)

    masks = _wkv_masks(CHUNK)
    for gi in range(N_GROUPS):
        sl = slice(MXU_DIM_V7X * gi, MXU_DIM_V7X * (gi + 1))
        y, s_new = _wkv_group_chunk(r_ref[:, sl], lw_ref[:, sl], k2_ref[:, sl], v_ref[:, sl],
                                    as_ref[:, sl], bs_ref[:, sl], s_ref[gi], masks)
        y_ref[:, sl] = y
        s_ref[gi] = s_new

    @pl.when(ci == pl.num_programs(1) - 1)
    def _():
        sout_ref[...] = s_ref[...]


def _wkv_prompt(r, lw, k2, v, a_s, b_s):
    b, t, _ = r.shape
    spec = pl.BlockSpec((None, CHUNK, D_RWKV), lambda bi, ci: (bi, ci, 0))
    return pl.pallas_call(
        _wkv_prompt_kernel,
        grid=(b, t // CHUNK),
        in_specs=[spec] * 6,
        out_specs=[spec,
                   pl.BlockSpec((None, N_GROUPS, MXU_DIM_V7X, MXU_DIM_V7X), lambda bi, ci: (bi, 0, 0, 0))],
        out_shape=[jax.ShapeDtypeStruct((b, t, D_RWKV), F32),
                   jax.ShapeDtypeStruct((b, N_GROUPS, MXU_DIM_V7X, MXU_DIM_V7X), F32)],
        scratch_shapes=[pltpu.VMEM((N_GROUPS, MXU_DIM_V7X, MXU_DIM_V7X), F32)],
        compiler_params=_params("parallel", "arbitrary"),
        name="wkv_prompt",
    )(r, lw, k2, v, a_s, b_s)


def _row_sum_bcast(z, ones64):
    hi, lo = _split_hi_lo(z)
    return (jnp.dot(hi, ones64, preferred_element_type=F32)
            + jnp.dot(lo, ones64, preferred_element_type=F32))


def _wkv_sample_kernel(r_ref, lw_ref, k2_ref, v_ref, as_ref, bs_ref, s0_ref, y_ref, sout_ref):
    nt, bt = r_ref.shape[0], r_ref.shape[1]
    ones64 = jnp.ones((HEAD_DIM, HEAD_DIM), BF16)
    rr = lax.broadcasted_iota(jnp.int32, (HEAD_DIM, HEAD_DIM), 0)
    cc = lax.broadcasted_iota(jnp.int32, (HEAD_DIM, HEAD_DIM), 1)
    eye = jnp.where(rr == cc, 1.0, 0.0).astype(F32)

    def body(bi, carry):
        for h in range(N_HEADS):
            s = s0_ref[bi, h]
            sa = _row_sum_bcast(s * as_ref[0, bi, h:h + 1, :], ones64)
            for t in range(nt):
                w = jnp.exp(lw_ref[t, bi, h:h + 1, :])
                vcol = _row_sum_bcast(eye * v_ref[t, bi, h:h + 1, :], ones64)
                s = s * w + sa * bs_ref[t, bi, h:h + 1, :] + vcol * k2_ref[t, bi, h:h + 1, :]
                ycol = _row_sum_bcast(s * r_ref[t, bi, h:h + 1, :], ones64)
                y_ref[t, bi, h:h + 1, :] = jnp.sum(eye * ycol, axis=0, keepdims=True)
                if t + 1 < nt:
                    sa = _row_sum_bcast(s * as_ref[t + 1, bi, h:h + 1, :], ones64)
            sout_ref[bi, h] = s
        return carry

    lax.fori_loop(0, bt, body, 0)


def _wkv_sample(r, lw, k2, v, a_s, b_s, s0, bt):
    t, b = r.shape[0], r.shape[1]
    vec = pl.BlockSpec((t, bt, N_HEADS, HEAD_DIM), lambda i: (0, i, 0, 0))
    st = pl.BlockSpec((bt, N_HEADS, HEAD_DIM, HEAD_DIM), lambda i: (i, 0, 0, 0))
    return pl.pallas_call(
        _wkv_sample_kernel,
        grid=(b // bt,),
        in_specs=[vec] * 6 + [st],
        out_specs=[vec, st],
        out_shape=[jax.ShapeDtypeStruct((t, b, N_HEADS, HEAD_DIM), F32),
                   jax.ShapeDtypeStruct(s0.shape, F32)],
        compiler_params=_params("parallel"),
        name="wkv_sample",
    )(r, lw, k2, v, a_s, b_s, s0)


def _pool_prompt_kernel(zp_ref, d_ref, carry_ref):
    i = pl.program_id(1)
    tm = zp_ref.shape[0]

    @pl.when(i == 0)
    def _():
        carry_ref[...] = jnp.zeros_like(carry_ref)

    zp = zp_ref[...]
    buf = jnp.concatenate([carry_ref[...], zp], axis=0)
    carry_ref[...] = zp[tm - 16:, :]
    pos = (i * tm + lax.broadcasted_iota(jnp.int32, (tm, POOL_GROUP), 0) + 1).astype(F32)
    for gi, win in enumerate(POOL_WINDOWS):
        sl = slice(POOL_GROUP * gi, POOL_GROUP * (gi + 1))
        acc = buf[:, sl]
        s = 1
        while s < win:
            acc = acc + pltpu.roll(acc, s, 0)
            s *= 2
        cnt = jnp.minimum(float(win), pos)
        d_ref[:, sl] = acc[16:, :] / cnt - zp[:, sl]


def _pool_prompt(zp, tm):
    b, t, _ = zp.shape
    spec = pl.BlockSpec((None, tm, D_POOL), lambda bi, i: (bi, i, 0))
    return pl.pallas_call(
        _pool_prompt_kernel,
        grid=(b, t // tm),
        in_specs=[spec],
        out_specs=spec,
        out_shape=jax.ShapeDtypeStruct(zp.shape, F32),
        scratch_shapes=[pltpu.VMEM((16, D_POOL), F32)],
        compiler_params=_params("parallel", "arbitrary"),
        name="pool_prompt",
    )(zp)


def _pool_sample_kernel(hist_ref, zp_ref, d_ref):
    nt = zp_ref.shape[0]
    for t in range(nt):
        for gi, win in enumerate(POOL_WINDOWS):
            sl = slice(POOL_GROUP * gi, POOL_GROUP * (gi + 1))
            acc = None
            for j in range(win):
                src = t - j
                term = zp_ref[src, :, sl] if src >= 0 else hist_ref[POOL_HIST + src, :, sl]
                acc = term if acc is None else acc + term
            d_ref[t, :, sl] = acc / float(min(win, PAST_LEN + 1)) - zp_ref[t, :, sl]


def _pool_sample(hist, zp):
    return pl.pallas_call(
        _pool_sample_kernel,
        grid=(1,),
        in_specs=[_const_spec(hist.shape), _const_spec(zp.shape)],
        out_specs=_const_spec(zp.shape),
        out_shape=jax.ShapeDtypeStruct(zp.shape, F32),
        compiler_params=_params("arbitrary"),
        name="pool_sample",
    )(hist, zp)


def _merge_kernel(y_ref, r_ref, k2_ref, v_ref, g_ref, d_ref, zga_ref, zgb_ref,
                  rk_ref, lw_ref, lb_ref, ones_ref, wa_ref, pw_ref, ps_ref, wb_ref, m_ref):
    ones_bd = ones_ref[...]
    y = y_ref[...]
    mu = _head_sum(y, ones_bd) * (1.0 / HEAD_DIM)
    yc = y - mu
    var = _head_sum(yc * yc, ones_bd) * (1.0 / HEAD_DIM)
    yn = yc * lax.rsqrt(var + GN_EPS) * lw_ref[...] + lb_ref[...]
    v = v_ref[...]
    bonus = _head_sum(r_ref[...] * k2_ref[...] * rk_ref[...], ones_bd) * v
    ya = (yn + bonus) * g_ref[...]
    d = d_ref[...]
    yb = jnp.concatenate(
        [_mm(d[:, POOL_GROUP * gi:POOL_GROUP * (gi + 1)], pw_ref[gi]) for gi in range(len(POOL_WINDOWS))],
        axis=1) * ps_ref[...]
    m_ref[...] = (jax.nn.sigmoid(zga_ref[...]) * _mm(ya, wa_ref[...])
                  + jax.nn.sigmoid(zgb_ref[...]) * _mm(yb, wb_ref[...]))


def _merge(y, r, k2, v, g, d, zga, zgb, wts, tm):
    n = y.shape[0]
    half = pl.BlockSpec((tm, D_RWKV), lambda i: (i, 0))
    full = pl.BlockSpec((tm, D_MODEL), lambda i: (i, 0))
    return pl.pallas_call(
        _merge_kernel,
        grid=(n // tm,),
        in_specs=[half] * 6 + [full] * 2 + [_const_spec(w.shape) for w in wts],
        out_specs=full,
        out_shape=jax.ShapeDtypeStruct((n, D_MODEL), F32),
        compiler_params=_params("parallel"),
        name="merge",
    )(y, r, k2, v, g, d, zga, zgb, *wts)


def _mix_out_kernel(m_ref, x_ref, w_ref, g_ref, o_ref):
    mo = _mm(m_ref[...], w_ref[...])
    o_ref[...] = x_ref[...] + _rmsnorm(mo, g_ref[...])


def _mix_out(m, x2d, w_out, g, tm):
    n = m.shape[0]
    full = pl.BlockSpec((tm, D_MODEL), lambda i: (i, 0))
    return pl.pallas_call(
        _mix_out_kernel,
        grid=(n // tm,),
        in_specs=[full, full, _const_spec(w_out.shape), _const_spec(g.shape)],
        out_specs=full,
        out_shape=jax.ShapeDtypeStruct((n, D_MODEL), F32),
        compiler_params=_params("parallel"),
        name="mix_out",
    )(m, x2d, w_out, g)


def _gelu_tanh(x):
    return 0.5 * x * (1.0 + jnp.tanh(0.7978845608028654 * (x + 0.044715 * x * x * x)))


def _ffn_body(x_ref, gn_ref, wg_ref, wu_ref, cw_ref, cb_ref, wo_ref, gp_ref,
              o_ref, tail_ref, h_ref, acc_ref, prev1, prev2):
    c = pl.program_id(1)

    @pl.when(c == 0)
    def _():
        h_ref[...] = _rmsnorm(x_ref[...], gn_ref[...]).astype(BF16)
        acc_ref[...] = jnp.zeros_like(acc_ref)

    h = h_ref[...]
    gate = jnp.dot(h, wg_ref[...], preferred_element_type=F32)
    up = jnp.dot(h, wu_ref[...], preferred_element_type=F32)
    cw = cw_ref[...]
    cv = cb_ref[...] + cw[0:1, :] * prev2(gate) + cw[1:2, :] * prev1(gate) + cw[2:3, :] * gate
    tail_ref[...] = gate[gate.shape[0] - tail_ref.shape[0]:, :]
    acc_ref[...] += _mm(_gelu_tanh(cv) * up, wo_ref[...])

    @pl.when(c == pl.num_programs(1) - 1)
    def _():
        o_ref[...] = x_ref[...] + _rmsnorm(acc_ref[...], gp_ref[...])


def _ffn_prompt_kernel(tiles_per_seq, x_ref, gn_ref, wg_ref, wu_ref, cw_ref, cb_ref, wo_ref, gp_ref,
                       o_ref, tail_ref, h_ref, acc_ref, carry_ref):
    i = pl.program_id(0)
    c = pl.program_id(1)

    @pl.when(i % tiles_per_seq == 0)
    def _():
        carry_ref[c] = jnp.zeros(carry_ref.shape[1:], F32)

    hist = carry_ref[c]

    def shifted(gate, s):
        row = lax.broadcasted_iota(jnp.int32, gate.shape, 0)
        rolled = pltpu.roll(gate, s, 0)
        out = rolled
        for j in range(s):
            out = jnp.where(row == j, hist[8 - s + j:9 - s + j, :], out)
        return out

    _ffn_body(x_ref, gn_ref, wg_ref, wu_ref, cw_ref, cb_ref, wo_ref, gp_ref, o_ref, tail_ref,
              h_ref, acc_ref, lambda g: shifted(g, 1), lambda g: shifted(g, 2))
    carry_ref[c] = tail_ref[...]


def _ffn_sample_kernel(x_ref, st_ref, gn_ref, wg_ref, wu_ref, cw_ref, cb_ref, wo_ref, gp_ref,
                       o_ref, tail_ref, h_ref, acc_ref, carry_ref):
    t = pl.program_id(0)
    c = pl.program_id(1)

    @pl.when(t == 0)
    def _():
        carry_ref[c] = st_ref[...]

    p2 = carry_ref[c, 0]
    p1 = carry_ref[c, 1]
    _ffn_body(x_ref, gn_ref, wg_ref, wu_ref, cw_ref, cb_ref, wo_ref, gp_ref, o_ref, tail_ref,
              h_ref, acc_ref, lambda g: p1, lambda g: p2)
    carry_ref[c, 0] = p1
    carry_ref[c, 1] = tail_ref[...]


def _ffn_prompt(x2d, seq_len, wts, tm, fk):
    n = x2d.shape[0]
    nc = D_FF // fk
    gn, w_in, cw, cb, w_out, gp = wts
    full = pl.BlockSpec((tm, D_MODEL), lambda i, c: (i, 0))
    return pl.pallas_call(
        functools.partial(_ffn_prompt_kernel, seq_len // tm),
        grid=(n // tm, nc),
        in_specs=[full, _const_spec(gn.shape),
                  pl.BlockSpec((D_MODEL, fk), lambda i, c: (0, c)),
                  pl.BlockSpec((D_MODEL, fk), lambda i, c: (0, c + nc)),
                  pl.BlockSpec((CONV_W, fk), lambda i, c: (0, c)),
                  pl.BlockSpec((1, fk), lambda i, c: (0, c)),
                  pl.BlockSpec((fk, D_MODEL), lambda i, c: (c, 0)),
                  _const_spec(gp.shape)],
        out_specs=[full, pl.BlockSpec((None, 8, fk), lambda i, c: (i, 0, c))],
        out_shape=[jax.ShapeDtypeStruct((n, D_MODEL), F32),
                   jax.ShapeDtypeStruct((n // tm, 8, D_FF), F32)],
        scratch_shapes=[pltpu.VMEM((tm, D_MODEL), BF16), pltpu.VMEM((tm, D_MODEL), F32),
                        pltpu.VMEM((nc, 8, fk), F32)],
        compiler_params=_params("arbitrary", "arbitrary"),
        name="ffn_prompt",
    )(x2d, gn, w_in, w_in, cw, cb, w_out, gp)


def _ffn_sample(x2d, st, batch, wts, fk):
    n = x2d.shape[0]
    nc = D_FF // fk
    gn, w_in, cw, cb, w_out, gp = wts
    full = pl.BlockSpec((batch, D_MODEL), lambda t, c: (t, 0))
    return pl.pallas_call(
        _ffn_sample_kernel,
        grid=(n // batch, nc),
        in_specs=[full, pl.BlockSpec((CONV_W - 1, batch, fk), lambda t, c: (0, 0, c)),
                  _const_spec(gn.shape),
                  pl.BlockSpec((D_MODEL, fk), lambda t, c: (0, c)),
                  pl.BlockSpec((D_MODEL, fk), lambda t, c: (0, c + nc)),
                  pl.BlockSpec((CONV_W, fk), lambda t, c: (0, c)),
                  pl.BlockSpec((1, fk), lambda t, c: (0, c)),
                  pl.BlockSpec((fk, D_MODEL), lambda t, c: (c, 0)),
                  _const_spec(gp.shape)],
        out_specs=[full, pl.BlockSpec((None, batch, fk), lambda t, c: (t, 0, c))],
        out_shape=[jax.ShapeDtypeStruct((n, D_MODEL), F32),
                   jax.ShapeDtypeStruct((n // batch, batch, D_FF), F32)],
        scratch_shapes=[pltpu.VMEM((batch, D_MODEL), BF16), pltpu.VMEM((batch, D_MODEL), F32),
                        pltpu.VMEM((nc, CONV_W - 1, batch, fk), F32)],
        compiler_params=_params("arbitrary", "arbitrary"),
        name="ffn_sample",
    )(x2d, st, gn, w_in, w_in, cw, cb, w_out, gp)


def _row(v):
    return v.reshape(1, -1).astype(F32)


def _ones_bd():
    i = jnp.arange(MXU_DIM_V7X) // HEAD_DIM
    return (i[:, None] == i[None, :]).astype(BF16)


def _layer_weights(l, norm_pre_mix, w_in, mu_shift, w0, w2, a0, a2, g2, k_k, k_a, r_k, lnx_w, lnx_b,
                   w_branch_a, pool_w, pool_scale, w_branch_b, w_out, norm_post_mix,
                   norm_pre_ffn, w_ffn_in, conv_w, conv_b, w_ffn_out, norm_post_ffn):
    pad = D_SHIFT_PAD - D_SHIFT
    w_pad = jnp.concatenate(
        [w_in[l][:, :D_SHIFT], jnp.zeros((D_MODEL, pad), F32), w_in[l][:, D_SHIFT:]], axis=1).astype(BF16)
    mu = jnp.pad(mu_shift[l], (0, pad)).reshape(1, -1)
    w_lora = jnp.zeros((D_LORA_PAD, 3 * D_RWKV), F32)
    w_lora = w_lora.at[0:RANK_W, 0:D_RWKV].set(w2[l])
    w_lora = w_lora.at[RANK_W:RANK_W + RANK_A, D_RWKV:2 * D_RWKV].set(a2[l])
    w_lora = w_lora.at[RANK_W + RANK_A:D_LORA, 2 * D_RWKV:].set(g2[l])
    ones_bd = _ones_bd()
    return dict(
        in_proj=(_row(norm_pre_mix[l]), w_pad),
        prep=(mu, w_lora.astype(BF16), _row(w0[l]), _row(a0[l]), _row(k_k[l]), _row(k_a[l]), ones_bd),
        merge=(_row(r_k[l]), _row(lnx_w[l]), _row(lnx_b[l]), ones_bd, w_branch_a[l].astype(BF16),
               pool_w[l].astype(BF16), _row(pool_scale[l]), w_branch_b[l].astype(BF16)),
        mix_out=(w_out[l].astype(BF16), _row(norm_post_mix[l])),
        ffn=(_row(norm_pre_ffn[l]), w_ffn_in[l].astype(BF16), conv_w[l].astype(F32), _row(conv_b[l]),
             w_ffn_out[l].astype(BF16), _row(norm_post_ffn[l])),
    )


def _prompt_layer(x, wts):
    b, t, _ = x.shape
    n = b * t
    x2d = x.reshape(n, D_MODEL)
    zs, zp, zga, zgb = _in_proj(x2d, *wts["in_proj"], tm=512)
    zs3 = zs.reshape(b, t, D_SHIFT_PAD)
    r, lw, k2, v, a_s, b_s, g = _prep_prompt(zs3, wts["prep"], tm=256)
    y, s_bd = _wkv_prompt(r, lw, k2, v, a_s, b_s)
    zp3 = zp.reshape(b, t, D_POOL)
    d = _pool_prompt(zp3, tm=512)
    flat = lambda a: a.reshape(n, a.shape[-1])
    m = _merge(flat(y), flat(r), flat(k2), flat(v), flat(g), flat(d), zga, zgb, wts["merge"], tm=256)
    x1 = _mix_out(m, x2d, *wts["mix_out"], tm=512)
    out, tail = _ffn_prompt(x1, t, wts["ffn"], tm=512, fk=512)
    shift = zs3[:, t - 1:, :D_SHIFT]
    s5 = s_bd.reshape(b, N_GROUPS, HEADS_PER_GROUP, HEAD_DIM, HEADS_PER_GROUP, HEAD_DIM)
    idx = jnp.arange(HEADS_PER_GROUP)
    wkv = s5[:, :, idx, :, idx, :]
    wkv = jnp.moveaxis(wkv, 0, 2).reshape(b, N_HEADS, HEAD_DIM, HEAD_DIM)
    pool = zp3[:, t - POOL_HIST:, :]
    tiles = t // 512
    conv = tail.reshape(b, tiles, 8, D_FF)[:, tiles - 1, 8 - (CONV_W - 1):, :]
    return out.reshape(b, t, D_MODEL), shift, wkv, pool, conv


def _sample_layer(x, st_shift, st_wkv, st_pool, st_conv, wts):
    b, t, _ = x.shape
    n = b * t
    x2d = jnp.swapaxes(x, 0, 1).reshape(n, D_MODEL)
    zs, zp, zga, zgb = _in_proj(x2d, *wts["in_proj"], tm=n)
    zs3 = zs.reshape(t, b, D_SHIFT_PAD)
    st = jnp.pad(st_shift.reshape(b, D_SHIFT), ((0, 0), (0, D_SHIFT_PAD - D_SHIFT)))
    r, lw, k2, v, a_s, b_s, g = _prep_sample(zs3, st, wts["prep"])
    heads = lambda a: a.reshape(t, b, N_HEADS, HEAD_DIM)
    y, wkv = _wkv_sample(heads(r), heads(lw), heads(k2), heads(v), heads(a_s), heads(b_s), st_wkv, bt=8)
    zp3 = zp.reshape(t, b, D_POOL)
    d = _pool_sample(jnp.swapaxes(st_pool, 0, 1), zp3)
    flat = lambda a: a.reshape(n, -1)
    m = _merge(flat(y), flat(r), flat(k2), flat(v), flat(g), flat(d), zga, zgb, wts["merge"], tm=min(256, n))
    x1 = _mix_out(m, x2d, *wts["mix_out"], tm=n)
    out, gate = _ffn_sample(x1, jnp.swapaxes(st_conv, 0, 1), b, wts["ffn"], fk=512)
    shift = jnp.swapaxes(zs3[t - 1:, :, :D_SHIFT], 0, 1)
    pool = jnp.concatenate([st_pool, jnp.swapaxes(zp3, 0, 1)], axis=1)[:, -POOL_HIST:]
    conv = jnp.concatenate([st_conv, jnp.swapaxes(gate, 0, 1)], axis=1)[:, -(CONV_W - 1):]
    return jnp.swapaxes(out.reshape(t, b, D_MODEL), 0, 1), shift, wkv, pool, conv


def kernel(x_prompt, x_sample, state_shift, state_wkv, state_pool, state_conv, norm_pre_mix, w_in, mu_shift, w0, w2, a0, a2, g2, k_k, k_a, r_k, lnx_w, lnx_b, w_branch_a, pool_w, pool_scale, w_branch_b, w_out, norm_post_mix, norm_pre_ffn, w_ffn_in, conv_w, conv_b, w_ffn_out, norm_post_ffn):
    weights = (norm_pre_mix, w_in, mu_shift, w0, w2, a0, a2, g2, k_k, k_a, r_k, lnx_w, lnx_b,
               w_branch_a, pool_w, pool_scale, w_branch_b, w_out, norm_post_mix,
               norm_pre_ffn, w_ffn_in, conv_w, conv_b, w_ffn_out, norm_post_ffn)
    depth = w_in.shape[0]
    yp, ys = x_prompt, x_sample
    p_states, s_states = [], []
    for l in range(depth):
        wts = _layer_weights(l, *weights)
        yp, *ps = _prompt_layer(yp, wts)
        ys, *ss = _sample_layer(ys, state_shift[l], state_wkv[l], state_pool[l], state_conv[l], wts)
        p_states.append(ps)
        s_states.append(ss)
    stack = lambda states, i: jnp.stack([s[i] for s in states])
    return (yp, ys,
            stack(p_states, 0), stack(p_states, 1), stack(p_states, 2), stack(p_states, 3),
            stack(s_states, 0), stack(s_states, 1), stack(s_states, 2), stack(s_states, 3))
```

```python
import functools

import jax
import jax.numpy as jnp
from jax import lax
from jax.experimental import pallas as pl
from jax.experimental.pallas import tpu as pltpu

F32 = jnp.float32
BF16 = jnp.bfloat16

D_MODEL = 2048
HEAD_DIM = 64
D_RWKV = 1024
N_HEADS = 16
RANK_W, RANK_A, RANK_G = 64, 64, 160
D_LORA = RANK_W + RANK_A + RANK_G
D_LORA_PAD = 384
D_SHIFT = 3 * D_RWKV + D_LORA
D_SHIFT_PAD = 3 * D_RWKV + 512
D_POOL = 1024
POOL_WINDOWS = (2, 4, 8, 16)
POOL_GROUP = 256
POOL_HIST = 15
D_FF = 5632
CONV_W = 3
NORM_EPS = 1e-6
GN_EPS = 64e-5
PAST_LEN = 16384

LANES_V7X = 128
MXU_DIM_V7X = 256
HEADS_PER_GROUP = MXU_DIM_V7X // HEAD_DIM
N_GROUPS = N_HEADS // HEADS_PER_GROUP
CHUNK = 64
VMEM_LIMIT_V7X = 56 * 1024 * 1024

COL_TILE = 512
N_COL_TILES = (D_SHIFT_PAD + D_POOL + 2 * D_MODEL) // COL_TILE
ZS_TILES = D_SHIFT_PAD // COL_TILE
ZP_TILES = D_POOL // COL_TILE
ZG_TILES = D_MODEL // COL_TILE


def _params(*sem):
    return pltpu.CompilerParams(dimension_semantics=sem, vmem_limit_bytes=VMEM_LIMIT_V7X)


def _mm(a, b):
    return jnp.dot(a.astype(BF16), b.astype(BF16), preferred_element_type=F32)


def _mm_nt(a, b):
    return lax.dot_general(a.astype(BF16), b.astype(BF16), (((1,), (1,)), ((), ())),
                           preferred_element_type=F32)


def _split_hi_lo(x):
    hi = x.astype(BF16)
    lo = (x - hi.astype(F32)).astype(BF16)
    return hi, lo


def _head_sum(x, ones_bd):
    hi, lo = _split_hi_lo(x)
    outs = []
    for gi in range(x.shape[1] // MXU_DIM_V7X):
        sl = slice(MXU_DIM_V7X * gi, MXU_DIM_V7X * (gi + 1))
        outs.append(jnp.dot(hi[:, sl], ones_bd, preferred_element_type=F32)
                    + jnp.dot(lo[:, sl], ones_bd, preferred_element_type=F32))
    return jnp.concatenate(outs, axis=1)


def _rmsnorm(x, g):
    return x * lax.rsqrt(jnp.mean(x * x, axis=-1, keepdims=True) + NORM_EPS) * g


def _in_proj_kernel(x_ref, g_ref, w_ref, zs_ref, zp_ref, zga_ref, zgb_ref, h_ref):
    j = pl.program_id(1)

    @pl.when(j == 0)
    def _():
        h_ref[...] = _rmsnorm(x_ref[...], g_ref[...]).astype(BF16)

    z = jnp.dot(h_ref[...], w_ref[...], preferred_element_type=F32)

    @pl.when(j < ZS_TILES)
    def _():
        zs_ref[...] = z

    @pl.when((j >= ZS_TILES) & (j < ZS_TILES + ZP_TILES))
    def _():
        zp_ref[...] = z

    @pl.when((j >= ZS_TILES + ZP_TILES) & (j < ZS_TILES + ZP_TILES + ZG_TILES))
    def _():
        zga_ref[...] = z

    @pl.when(j >= ZS_TILES + ZP_TILES + ZG_TILES)
    def _():
        zgb_ref[...] = z


def _in_proj(x2d, g, w_pad, tm):
    n = x2d.shape[0]

    def sect(first, count):
        return lambda i, j: (i, jnp.clip(j - first, 0, count - 1))

    return pl.pallas_call(
        _in_proj_kernel,
        grid=(n // tm, N_COL_TILES),
        in_specs=[
            pl.BlockSpec((tm, D_MODEL), lambda i, j: (i, 0)),
            pl.BlockSpec((1, D_MODEL), lambda i, j: (0, 0)),
            pl.BlockSpec((D_MODEL, COL_TILE), lambda i, j: (0, j)),
        ],
        out_specs=[
            pl.BlockSpec((tm, COL_TILE), sect(0, ZS_TILES)),
            pl.BlockSpec((tm, COL_TILE), sect(ZS_TILES, ZP_TILES)),
            pl.BlockSpec((tm, COL_TILE), sect(ZS_TILES + ZP_TILES, ZG_TILES)),
            pl.BlockSpec((tm, COL_TILE), sect(ZS_TILES + ZP_TILES + ZG_TILES, ZG_TILES)),
        ],
        out_shape=[
            jax.ShapeDtypeStruct((n, D_SHIFT_PAD), F32),
            jax.ShapeDtypeStruct((n, D_POOL), F32),
            jax.ShapeDtypeStruct((n, D_MODEL), F32),
            jax.ShapeDtypeStruct((n, D_MODEL), F32),
        ],
        scratch_shapes=[pltpu.VMEM((tm, D_MODEL), BF16)],
        compiler_params=_params("parallel", "arbitrary"),
        name="in_proj",
    )(x2d, g, w_pad)


def _prep_math(zs, prev, mu, w_lora, w0, a0, k_k, k_a, ones_bd):
    f = zs + (prev - zs) * mu
    r = f[:, 0:D_RWKV]
    k = f[:, D_RWKV:2 * D_RWKV]
    v = f[:, 2 * D_RWKV:3 * D_RWKV]
    low = f[:, 3 * D_RWKV:3 * D_RWKV + D_LORA_PAD]
    lane = lax.broadcasted_iota(jnp.int32, low.shape, 1)
    act = jnp.where(lane < RANK_W, jnp.tanh(low),
                    jnp.where(lane < RANK_W + RANK_A, low,
                              jnp.where(lane < D_LORA, jax.nn.sigmoid(low), 0.0)))
    lora = _mm(act, w_lora)
    u = w0 + lora[:, 0:D_RWKV]
    w_log = -(jnp.maximum(-u, 0.0) + jnp.log1p(jnp.exp(-jnp.abs(u)))) - 0.5
    lw = -jnp.exp(w_log)
    a = jax.nn.sigmoid(a0 + lora[:, D_RWKV:2 * D_RWKV])
    g = lora[:, 2 * D_RWKV:3 * D_RWKV]
    kk = k * k_k
    nrm = jnp.sqrt(_head_sum(kk * kk, ones_bd))
    kk = kk / jnp.maximum(nrm, 1e-12)
    k2 = k * (1.0 + (a - 1.0) * k_a)
    return r, lw, k2, v, -kk, kk * a, g


def _prep_prompt_kernel(zs_ref, mu_ref, wl_ref, w0_ref, a0_ref, kk_ref, ka_ref, ones_ref,
                        r_ref, lw_ref, k2_ref, v_ref, as_ref, bs_ref, g_ref, carry_ref):
    i = pl.program_id(1)

    @pl.when(i == 0)
    def _():
        carry_ref[...] = jnp.zeros_like(carry_ref)

    zs = zs_ref[...]
    rolled = pltpu.roll(zs, 1, 0)
    row = lax.broadcasted_iota(jnp.int32, zs.shape, 0)
    prev = jnp.where(row == 0, carry_ref[0:1, :], rolled)
    carry_ref[0:1, :] = zs[zs.shape[0] - 1:, :]
    outs = _prep_math(zs, prev, mu_ref[...], wl_ref[...], w0_ref[...], a0_ref[...],
                      kk_ref[...], ka_ref[...], ones_ref[...])
    for o_ref, o in zip((r_ref, lw_ref, k2_ref, v_ref, as_ref, bs_ref, g_ref), outs):
        o_ref[...] = o


def _prep_sample_kernel(seq, zs_ref, st_ref, mu_ref, wl_ref, w0_ref, a0_ref, kk_ref, ka_ref, ones_ref,
                        r_ref, lw_ref, k2_ref, v_ref, as_ref, bs_ref, g_ref):
    zs = zs_ref[...]
    row = lax.broadcasted_iota(jnp.int32, zs.shape, 0)
    prev = jnp.where(row % seq == 0, st_ref[...], pltpu.roll(zs, 1, 0))
    outs = _prep_math(zs, prev, mu_ref[...], wl_ref[...], w0_ref[...], a0_ref[...],
                      kk_ref[...], ka_ref[...], ones_ref[...])
    for o_ref, o in zip((r_ref, lw_ref, k2_ref, v_ref, as_ref, bs_ref, g_ref), outs):
        o_ref[...] = o


def _const_spec(shape):
    nd = len(shape)
    return pl.BlockSpec(shape, lambda *_: (0,) * nd)


def _prep_prompt(zs, wts, tm):
    b, t, _ = zs.shape
    outs = [jax.ShapeDtypeStruct((b, t, D_RWKV), F32)] * 7
    row_spec = pl.BlockSpec((None, tm, D_RWKV), lambda bi, i: (bi, i, 0))
    return pl.pallas_call(
        _prep_prompt_kernel,
        grid=(b, t // tm),
        in_specs=[pl.BlockSpec((None, tm, D_SHIFT_PAD), lambda bi, i: (bi, i, 0))]
        + [_const_spec(w.shape) for w in wts],
        out_specs=[row_spec] * 7,
        out_shape=outs,
        scratch_shapes=[pltpu.VMEM((8, D_SHIFT_PAD), F32)],
        compiler_params=_params("parallel", "arbitrary"),
        name="prep_prompt",
    )(zs, *wts)


def _prep_sample(zs, st_rows, seq, wts, tm):
    n = zs.shape[0]
    assert tm % seq == 0 and n % tm == 0
    outs = [jax.ShapeDtypeStruct((n, D_RWKV), F32)] * 7
    row_spec = pl.BlockSpec((tm, D_RWKV), lambda i: (i, 0))
    wide = pl.BlockSpec((tm, D_SHIFT_PAD), lambda i: (i, 0))
    return pl.pallas_call(
        functools.partial(_prep_sample_kernel, seq),
        grid=(n // tm,),
        in_specs=[wide, wide] + [_const_spec(w.shape) for w in wts],
        out_specs=[row_spec] * 7,
        out_shape=outs,
        compiler_params=_params("parallel"),
        name="prep_sample",
    )(zs, st_rows, *wts)


def _expand_bd(a, bd_mask):
    return jnp.where(bd_mask, jnp.concatenate([a] * HEADS_PER_GROUP, axis=0), 0.0)


def _chunk_cumsum(x):
    row = lax.broadcasted_iota(jnp.int32, x.shape, 0)
    s = 1
    while s < x.shape[0]:
        x = x + jnp.where(row >= s, pltpu.roll(x, s, 0), 0.0)
        s *= 2
    return x


def _wkv_intra(x, bt, kt, v, pz, masks, order):
    bd_mask, strict_mask, incl_mask, eye_c, head_masks = masks
    c = v.shape[0]
    ystack = jnp.concatenate([jnp.where(m, bt, 0.0) for m in head_masks]
                             + [jnp.where(m, kt, 0.0) for m in head_masks], axis=0)
    gram = _mm_nt(x, ystack)
    l_ab = jnp.where(strict_mask, gram[0:c, 0:4 * c], 0.0)
    l_ak = jnp.where(strict_mask, gram[0:c, 4 * c:8 * c], 0.0)
    m_rb = jnp.where(incl_mask, gram[c:2 * c, 0:4 * c], 0.0)
    m_rk = jnp.where(incl_mask, gram[c:2 * c, 4 * c:8 * c], 0.0)

    tinv = eye_c + l_ab
    lp = _mm(l_ab, _expand_bd(l_ab, bd_mask))
    p = 2
    while True:
        rhs = _expand_bd(lp, bd_mask)
        if 2 * p >= order:
            tinv = tinv + _mm(tinv, rhs)
            break
        res = _mm(jnp.concatenate([lp, tinv], axis=0), rhs)
        lp = res[0:c]
        tinv = tinv + res[c:2 * c]
        p *= 2

    v_bd = _expand_bd(v, bd_mask)
    w = pz[0:c] + _mm(l_ak, v_bd)
    u = _mm(tinv, _expand_bd(w, bd_mask))
    y = pz[c:2 * c] + _mm(jnp.concatenate([m_rb, m_rk], axis=1),
                          jnp.concatenate([_expand_bd(u, bd_mask), v_bd], axis=0))
    return y, u


def _wkv_group_chunk(r, lw, k2, v, a_s, b_s, s_bd, masks):
    c = r.shape[0]
    cl = _chunk_cumsum(lw)
    cl_last = cl[c - 1:c, :]
    e_neg = jnp.exp(-cl)
    e_rem = jnp.exp(cl_last - cl)
    x = jnp.concatenate([a_s * jnp.exp(cl - lw), r * jnp.exp(cl)], axis=0)
    pz = _mm_nt(x, s_bd)
    y, u = _wkv_intra(x, b_s * e_neg, k2 * e_neg, v, pz, masks, c)
    uv_t = jnp.concatenate([u, v], axis=0).T
    bk = jnp.concatenate([b_s * e_rem, k2 * e_rem], axis=0)
    ds = _mm(uv_t, bk)
    s_new = s_bd * jnp.exp(cl_last) + jnp.where(masks[0], ds, 0.0)
    return y, s_new


def _wkv_masks(c, seq):
    n = HEADS_PER_GROUP * c
    rr = lax.broadcasted_iota(jnp.int32, (n, n), 0)
    cc = lax.broadcasted_iota(jnp.int32, (n, n), 1)
    bd_mask = (rr // c) == (cc // c)
    t = lax.broadcasted_iota(jnp.int32, (c, n), 0)
    s = lax.broadcasted_iota(jnp.int32, (c, n), 1) % c
    same = (t // seq) == (s // seq)
    strict_mask = same & (t > s)
    incl_mask = same & (t >= s)
    eye_c = jnp.where(t == s, 1.0, 0.0).astype(F32)
    lane = lax.broadcasted_iota(jnp.int32, (c, MXU_DIM_V7X), 1)
    head_masks = [(lane // HEAD_DIM) == h for h in range(HEADS_PER_GROUP)]
    return bd_mask, strict_mask, incl_mask, eye_c, head_masks


def _wkv_prompt_kernel(r_ref, lw_ref, k2_ref, v_ref, as_ref, bs_ref, y_ref, sout_ref, s_ref):
    ci = pl.program_id(1)

    @pl.when(ci == 0)
    def _():
        s_ref[...] = jnp.zeros_like(s_ref)

    masks = _wkv_masks(CHUNK, CHUNK)
    for gi in range(N_GROUPS):
        sl = slice(MXU_DIM_V7X * gi, MXU_DIM_V7X * (gi + 1))
        y, s_new = _wkv_group_chunk(r_ref[:, sl], lw_ref[:, sl], k2_ref[:, sl], v_ref[:, sl],
                                    as_ref[:, sl], bs_ref[:, sl], s_ref[gi], masks)
        y_ref[:, sl] = y
        s_ref[gi] = s_new

    @pl.when(ci == pl.num_programs(1) - 1)
    def _():
        sout_ref[...] = s_ref[...]


def _wkv_prompt(r, lw, k2, v, a_s, b_s):
    b, t, _ = r.shape
    spec = pl.BlockSpec((None, CHUNK, D_RWKV), lambda bi, ci: (bi, ci, 0))
    return pl.pallas_call(
        _wkv_prompt_kernel,
        grid=(b, t // CHUNK),
        in_specs=[spec] * 6,
        out_specs=[spec,
                   pl.BlockSpec((None, N_GROUPS, MXU_DIM_V7X, MXU_DIM_V7X), lambda bi, ci: (bi, 0, 0, 0))],
        out_shape=[jax.ShapeDtypeStruct((b, t, D_RWKV), F32),
                   jax.ShapeDtypeStruct((b, N_GROUPS, MXU_DIM_V7X, MXU_DIM_V7X), F32)],
        scratch_shapes=[pltpu.VMEM((N_GROUPS, MXU_DIM_V7X, MXU_DIM_V7X), F32)],
        compiler_params=_params("parallel", "arbitrary"),
        name="wkv_prompt",
    )(r, lw, k2, v, a_s, b_s)


SLAB = 8


def _slab_pair(ref, r0, c):
    return jnp.concatenate([ref[pl.ds(r0, SLAB), :], ref[pl.ds(c + r0, SLAB), :]], axis=0)


def _wkv_sample_kernel(seq, r_ref, lw_ref, k2_ref, v_ref, as_ref, bs_ref, s0_ref, y_ref, sout_ref,
                       x_ref, btk_ref, bk_ref, p_ref, uv_ref, gam_ref):
    c = r_ref.shape[0]
    per_slab = SLAB // seq

    lw = lw_ref[...]
    tpos = lax.broadcasted_iota(jnp.int32, lw.shape, 0) % seq
    cl = lw
    s = 1
    while s < seq:
        cl = cl + jnp.where(tpos >= s, pltpu.roll(cl, s, 0), 0.0)
        s *= 2
    tot = jnp.where(tpos == seq - 1, cl, 0.0)
    s = 1
    while s < seq:
        tot = tot + pltpu.roll(tot, c - s, 0)
        s *= 2
    e_neg = jnp.exp(-cl)
    e_rem = jnp.exp(tot - cl)
    b_s = bs_ref[...]
    k2 = k2_ref[...]
    x_ref[0:c, :] = as_ref[...] * jnp.exp(cl - lw)
    x_ref[c:2 * c, :] = r_ref[...] * jnp.exp(cl)
    btk_ref[0:c, :] = b_s * e_neg
    btk_ref[c:2 * c, :] = k2 * e_neg
    bk_ref[0:c, :] = b_s * e_rem
    bk_ref[c:2 * c, :] = k2 * e_rem
    gam_ref[...] = jnp.exp(tot)

    first_seq = (lax.broadcasted_iota(jnp.int32, (2 * SLAB, HEAD_DIM), 0) % SLAB) < seq

    def state_in(p, carry):
        r0 = pl.multiple_of(p * SLAB, SLAB)
        xp = _slab_pair(x_ref, r0, c)
        outs = []
        for h in range(N_HEADS):
            xs = xp[:, HEAD_DIM * h:HEAD_DIM * (h + 1)]
            p0 = _mm_nt(xs, s0_ref[per_slab * p, h])
            p1 = _mm_nt(xs, s0_ref[per_slab * p + 1, h])
            outs.append(jnp.where(first_seq, p0, p1))
        pp = jnp.concatenate(outs, axis=1)
        p_ref[pl.ds(r0, SLAB), :] = pp[0:SLAB]
        p_ref[pl.ds(c + r0, SLAB), :] = pp[SLAB:2 * SLAB]
        return carry

    lax.fori_loop(0, c // SLAB, state_in, 0)

    masks = _wkv_masks(c, seq)
    for gi in range(N_GROUPS):
        sl = slice(MXU_DIM_V7X * gi, MXU_DIM_V7X * (gi + 1))
        y, u = _wkv_intra(x_ref[:, sl], btk_ref[0:c, sl], btk_ref[c:2 * c, sl], v_ref[:, sl],
                          p_ref[:, sl], masks, seq)
        y_ref[:, sl] = y
        uv_ref[0:c, sl] = u
    uv_ref[c:2 * c, :] = v_ref[...]

    first_rows = (lax.broadcasted_iota(jnp.int32, (2 * SLAB, D_RWKV), 0) % SLAB) < seq

    def state_out(p, carry):
        r0 = pl.multiple_of(p * SLAB, SLAB)
        uvp = _slab_pair(uv_ref, r0, c)
        bkp = _slab_pair(bk_ref, r0, c)
        gam8 = gam_ref[pl.ds(r0, SLAB), :]
        for bb in range(per_slab):
            uvm = jnp.where(first_rows if bb == 0 else jnp.logical_not(first_rows), uvp, 0.0)
            gam = gam8[seq * bb:seq * bb + 1, :]
            for j in range(N_HEADS // 2):
                ls = slice(2 * HEAD_DIM * j, 2 * HEAD_DIM * (j + 1))
                ds2 = _mm(uvm[:, ls].T, bkp[:, ls])
                for hh in range(2):
                    h = 2 * j + hh
                    hs = slice(HEAD_DIM * hh, HEAD_DIM * (hh + 1))
                    sout_ref[per_slab * p + bb, h] = (
                        s0_ref[per_slab * p + bb, h] * gam[:, HEAD_DIM * h:HEAD_DIM * (h + 1)] + ds2[hs, hs])
        return carry

    lax.fori_loop(0, c // SLAB, state_out, 0)


def _wkv_sample(r, lw, k2, v, a_s, b_s, s0, seq):
    n = r.shape[0]
    assert SLAB % seq == 0 and SLAB // seq == 2 and n % CHUNK == 0
    nb = CHUNK // seq
    vec = pl.BlockSpec((CHUNK, D_RWKV), lambda i: (i, 0))
    st = pl.BlockSpec((nb, N_HEADS, HEAD_DIM, HEAD_DIM), lambda i: (i, 0, 0, 0))
    return pl.pallas_call(
        functools.partial(_wkv_sample_kernel, seq),
        grid=(n // CHUNK,),
        in_specs=[vec] * 6 + [st],
        out_specs=[vec, st],
        out_shape=[jax.ShapeDtypeStruct((n, D_RWKV), F32), jax.ShapeDtypeStruct(s0.shape, F32)],
        scratch_shapes=[pltpu.VMEM((2 * CHUNK, D_RWKV), F32)] * 5 + [pltpu.VMEM((CHUNK, D_RWKV), F32)],
        compiler_params=_params("parallel"),
        name="wkv_sample",
    )(r, lw, k2, v, a_s, b_s, s0)


def _pool_prompt_kernel(zp_ref, d_ref, carry_ref):
    i = pl.program_id(1)
    tm = zp_ref.shape[0]

    @pl.when(i == 0)
    def _():
        carry_ref[...] = jnp.zeros_like(carry_ref)

    zp = zp_ref[...]
    buf = jnp.concatenate([carry_ref[...], zp], axis=0)
    carry_ref[...] = zp[tm - 16:, :]
    pos = (i * tm + lax.broadcasted_iota(jnp.int32, (tm, POOL_GROUP), 0) + 1).astype(F32)
    for gi, win in enumerate(POOL_WINDOWS):
        sl = slice(POOL_GROUP * gi, POOL_GROUP * (gi + 1))
        acc = buf[:, sl]
        s = 1
        while s < win:
            acc = acc + pltpu.roll(acc, s, 0)
            s *= 2
        cnt = jnp.minimum(float(win), pos)
        d_ref[:, sl] = acc[16:, :] / cnt - zp[:, sl]


def _pool_prompt(zp, tm):
    b, t, _ = zp.shape
    spec = pl.BlockSpec((None, tm, D_POOL), lambda bi, i: (bi, i, 0))
    return pl.pallas_call(
        _pool_prompt_kernel,
        grid=(b, t // tm),
        in_specs=[spec],
        out_specs=spec,
        out_shape=jax.ShapeDtypeStruct(zp.shape, F32),
        scratch_shapes=[pltpu.VMEM((16, D_POOL), F32)],
        compiler_params=_params("parallel", "arbitrary"),
        name="pool_prompt",
    )(zp)


def _pool_sample_kernel(buf_ref, d_ref):
    nt = d_ref.shape[1]
    for t in range(nt):
        for gi, win in enumerate(POOL_WINDOWS):
            sl = slice(POOL_GROUP * gi, POOL_GROUP * (gi + 1))
            acc = buf_ref[:, POOL_HIST + t, sl]
            for j in range(1, win):
                acc = acc + buf_ref[:, POOL_HIST + t - j, sl]
            d_ref[:, t, sl] = acc / float(min(win, PAST_LEN + 1)) - buf_ref[:, POOL_HIST + t, sl]


def _pool_sample(buf, seq, bt):
    b = buf.shape[0]
    return pl.pallas_call(
        _pool_sample_kernel,
        grid=(b // bt,),
        in_specs=[pl.BlockSpec((bt, buf.shape[1], D_POOL), lambda i: (i, 0, 0))],
        out_specs=pl.BlockSpec((bt, seq, D_POOL), lambda i: (i, 0, 0)),
        out_shape=jax.ShapeDtypeStruct((b, seq, D_POOL), F32),
        compiler_params=_params("parallel"),
        name="pool_sample",
    )(buf)


def _merge_kernel(y_ref, r_ref, k2_ref, v_ref, g_ref, d_ref, zga_ref, zgb_ref,
                  rk_ref, lw_ref, lb_ref, ones_ref, wa_ref, pw_ref, ps_ref, wb_ref, m_ref):
    ones_bd = ones_ref[...]
    y = y_ref[...]
    mu = _head_sum(y, ones_bd) * (1.0 / HEAD_DIM)
    yc = y - mu
    var = _head_sum(yc * yc, ones_bd) * (1.0 / HEAD_DIM)
    yn = yc * lax.rsqrt(var + GN_EPS) * lw_ref[...] + lb_ref[...]
    v = v_ref[...]
    bonus = _head_sum(r_ref[...] * k2_ref[...] * rk_ref[...], ones_bd) * v
    ya = (yn + bonus) * g_ref[...]
    d = d_ref[...]
    yb = jnp.concatenate(
        [_mm(d[:, POOL_GROUP * gi:POOL_GROUP * (gi + 1)], pw_ref[gi]) for gi in range(len(POOL_WINDOWS))],
        axis=1) * ps_ref[...]
    m_ref[...] = (jax.nn.sigmoid(zga_ref[...]) * _mm(ya, wa_ref[...])
                  + jax.nn.sigmoid(zgb_ref[...]) * _mm(yb, wb_ref[...]))


def _merge(y, r, k2, v, g, d, zga, zgb, wts, tm):
    n = y.shape[0]
    half = pl.BlockSpec((tm, D_RWKV), lambda i: (i, 0))
    full = pl.BlockSpec((tm, D_MODEL), lambda i: (i, 0))
    return pl.pallas_call(
        _merge_kernel,
        grid=(n // tm,),
        in_specs=[half] * 6 + [full] * 2 + [_const_spec(w.shape) for w in wts],
        out_specs=full,
        out_shape=jax.ShapeDtypeStruct((n, D_MODEL), F32),
        compiler_params=_params("parallel"),
        name="merge",
    )(y, r, k2, v, g, d, zga, zgb, *wts)


def _mix_out_kernel(m_ref, x_ref, w_ref, g_ref, o_ref):
    mo = _mm(m_ref[...], w_ref[...])
    o_ref[...] = x_ref[...] + _rmsnorm(mo, g_ref[...])


def _mix_out(m, x2d, w_out, g, tm):
    n = m.shape[0]
    full = pl.BlockSpec((tm, D_MODEL), lambda i: (i, 0))
    return pl.pallas_call(
        _mix_out_kernel,
        grid=(n // tm,),
        in_specs=[full, full, _const_spec(w_out.shape), _const_spec(g.shape)],
        out_specs=full,
        out_shape=jax.ShapeDtypeStruct((n, D_MODEL), F32),
        compiler_params=_params("parallel"),
        name="mix_out",
    )(m, x2d, w_out, g)


def _gelu_tanh(x):
    return 0.5 * x * (1.0 + jnp.tanh(0.7978845608028654 * (x + 0.044715 * x * x * x)))


def _ffn_body(x_ref, gn_ref, wg_ref, wu_ref, cw_ref, cb_ref, wo_ref, gp_ref,
              o_ref, tail_ref, h_ref, acc_ref, prev1, prev2):
    c = pl.program_id(1)

    @pl.when(c == 0)
    def _():
        h_ref[...] = _rmsnorm(x_ref[...], gn_ref[...]).astype(BF16)
        acc_ref[...] = jnp.zeros_like(acc_ref)

    h = h_ref[...]
    gate = jnp.dot(h, wg_ref[...], preferred_element_type=F32)
    up = jnp.dot(h, wu_ref[...], preferred_element_type=F32)
    cw = cw_ref[...]
    cv = cb_ref[...] + cw[0:1, :] * prev2(gate) + cw[1:2, :] * prev1(gate) + cw[2:3, :] * gate
    tail_ref[...] = gate[gate.shape[0] - tail_ref.shape[0]:, :]
    acc_ref[...] += _mm(_gelu_tanh(cv) * up, wo_ref[...])

    @pl.when(c == pl.num_programs(1) - 1)
    def _():
        o_ref[...] = x_ref[...] + _rmsnorm(acc_ref[...], gp_ref[...])


def _ffn_prompt_kernel(tiles_per_seq, x_ref, gn_ref, wg_ref, wu_ref, cw_ref, cb_ref, wo_ref, gp_ref,
                       o_ref, tail_ref, h_ref, acc_ref, carry_ref):
    i = pl.program_id(0)
    c = pl.program_id(1)

    @pl.when(i % tiles_per_seq == 0)
    def _():
        carry_ref[c] = jnp.zeros(carry_ref.shape[1:], F32)

    hist = carry_ref[c]

    def shifted(gate, s):
        row = lax.broadcasted_iota(jnp.int32, gate.shape, 0)
        rolled = pltpu.roll(gate, s, 0)
        out = rolled
        for j in range(s):
            out = jnp.where(row == j, hist[8 - s + j:9 - s + j, :], out)
        return out

    _ffn_body(x_ref, gn_ref, wg_ref, wu_ref, cw_ref, cb_ref, wo_ref, gp_ref, o_ref, tail_ref,
              h_ref, acc_ref, lambda g: shifted(g, 1), lambda g: shifted(g, 2))
    carry_ref[c] = tail_ref[...]


def _ffn_sample_kernel(seq, x_ref, st_ref, gn_ref, wg_ref, wu_ref, cw_ref, cb_ref, wo_ref, gp_ref,
                       o_ref, tail_ref, h_ref, acc_ref):
    st = st_ref[...]
    n = st.shape[0]
    tpos = lax.broadcasted_iota(jnp.int32, st.shape, 0) % seq
    prev1 = lambda g: jnp.where(tpos == 0, pltpu.roll(st, n - 1, 0), pltpu.roll(g, 1, 0))
    prev2 = lambda g: jnp.where(tpos < CONV_W - 1, st, pltpu.roll(g, 2, 0))
    _ffn_body(x_ref, gn_ref, wg_ref, wu_ref, cw_ref, cb_ref, wo_ref, gp_ref, o_ref, tail_ref,
              h_ref, acc_ref, prev1, prev2)


def _ffn_prompt(x2d, seq_len, wts, tm, fk):
    n = x2d.shape[0]
    nc = D_FF // fk
    gn, w_in, cw, cb, w_out, gp = wts
    full = pl.BlockSpec((tm, D_MODEL), lambda i, c: (i, 0))
    return pl.pallas_call(
        functools.partial(_ffn_prompt_kernel, seq_len // tm),
        grid=(n // tm, nc),
        in_specs=[full, _const_spec(gn.shape),
                  pl.BlockSpec((D_MODEL, fk), lambda i, c: (0, c)),
                  pl.BlockSpec((D_MODEL, fk), lambda i, c: (0, c + nc)),
                  pl.BlockSpec((CONV_W, fk), lambda i, c: (0, c)),
                  pl.BlockSpec((1, fk), lambda i, c: (0, c)),
                  pl.BlockSpec((fk, D_MODEL), lambda i, c: (c, 0)),
                  _const_spec(gp.shape)],
        out_specs=[full, pl.BlockSpec((None, 8, fk), lambda i, c: (i, 0, c))],
        out_shape=[jax.ShapeDtypeStruct((n, D_MODEL), F32),
                   jax.ShapeDtypeStruct((n // tm, 8, D_FF), F32)],
        scratch_shapes=[pltpu.VMEM((tm, D_MODEL), BF16), pltpu.VMEM((tm, D_MODEL), F32),
                        pltpu.VMEM((nc, 8, fk), F32)],
        compiler_params=_params("arbitrary", "arbitrary"),
        name="ffn_prompt",
    )(x2d, gn, w_in, w_in, cw, cb, w_out, gp)


def _ffn_sample(x2d, st_rows, seq, wts, fk):
    n = x2d.shape[0]
    assert seq >= CONV_W - 1
    nc = D_FF // fk
    gn, w_in, cw, cb, w_out, gp = wts
    full = pl.BlockSpec((n, D_MODEL), lambda i, c: (0, 0))
    cols = pl.BlockSpec((n, fk), lambda i, c: (0, c))
    return pl.pallas_call(
        functools.partial(_ffn_sample_kernel, seq),
        grid=(1, nc),
        in_specs=[full, cols, _const_spec(gn.shape),
                  pl.BlockSpec((D_MODEL, fk), lambda i, c: (0, c)),
                  pl.BlockSpec((D_MODEL, fk), lambda i, c: (0, c + nc)),
                  pl.BlockSpec((CONV_W, fk), lambda i, c: (0, c)),
                  pl.BlockSpec((1, fk), lambda i, c: (0, c)),
                  pl.BlockSpec((fk, D_MODEL), lambda i, c: (c, 0)),
                  _const_spec(gp.shape)],
        out_specs=[full, cols],
        out_shape=[jax.ShapeDtypeStruct((n, D_MODEL), F32), jax.ShapeDtypeStruct((n, D_FF), F32)],
        scratch_shapes=[pltpu.VMEM((n, D_MODEL), BF16), pltpu.VMEM((n, D_MODEL), F32)],
        compiler_params=_params("arbitrary", "arbitrary"),
        name="ffn_sample",
    )(x2d, st_rows, gn, w_in, w_in, cw, cb, w_out, gp)


def _row(v):
    return v.reshape(1, -1).astype(F32)


def _ones_bd():
    i = jnp.arange(MXU_DIM_V7X) // HEAD_DIM
    return (i[:, None] == i[None, :]).astype(BF16)


def _layer_weights(l, norm_pre_mix, w_in, mu_shift, w0, w2, a0, a2, g2, k_k, k_a, r_k, lnx_w, lnx_b,
                   w_branch_a, pool_w, pool_scale, w_branch_b, w_out, norm_post_mix,
                   norm_pre_ffn, w_ffn_in, conv_w, conv_b, w_ffn_out, norm_post_ffn):
    pad = D_SHIFT_PAD - D_SHIFT
    w_pad = jnp.concatenate(
        [w_in[l][:, :D_SHIFT], jnp.zeros((D_MODEL, pad), F32), w_in[l][:, D_SHIFT:]], axis=1).astype(BF16)
    mu = jnp.pad(mu_shift[l], (0, pad)).reshape(1, -1)
    w_lora = jnp.zeros((D_LORA_PAD, 3 * D_RWKV), F32)
    w_lora = w_lora.at[0:RANK_W, 0:D_RWKV].set(w2[l])
    w_lora = w_lora.at[RANK_W:RANK_W + RANK_A, D_RWKV:2 * D_RWKV].set(a2[l])
    w_lora = w_lora.at[RANK_W + RANK_A:D_LORA, 2 * D_RWKV:].set(g2[l])
    ones_bd = _ones_bd()
    return dict(
        in_proj=(_row(norm_pre_mix[l]), w_pad),
        prep=(mu, w_lora.astype(BF16), _row(w0[l]), _row(a0[l]), _row(k_k[l]), _row(k_a[l]), ones_bd),
        merge=(_row(r_k[l]), _row(lnx_w[l]), _row(lnx_b[l]), ones_bd, w_branch_a[l].astype(BF16),
               pool_w[l].astype(BF16), _row(pool_scale[l]), w_branch_b[l].astype(BF16)),
        mix_out=(w_out[l].astype(BF16), _row(norm_post_mix[l])),
        ffn=(_row(norm_pre_ffn[l]), w_ffn_in[l].astype(BF16), conv_w[l].astype(F32), _row(conv_b[l]),
             w_ffn_out[l].astype(BF16), _row(norm_post_ffn[l])),
    )


def _prompt_layer(x, wts):
    b, t, _ = x.shape
    n = b * t
    x2d = x.reshape(n, D_MODEL)
    zs, zp, zga, zgb = _in_proj(x2d, *wts["in_proj"], tm=512)
    zs3 = zs.reshape(b, t, D_SHIFT_PAD)
    r, lw, k2, v, a_s, b_s, g = _prep_prompt(zs3, wts["prep"], tm=256)
    y, s_bd = _wkv_prompt(r, lw, k2, v, a_s, b_s)
    zp3 = zp.reshape(b, t, D_POOL)
    d = _pool_prompt(zp3, tm=512)
    flat = lambda a: a.reshape(n, a.shape[-1])
    m = _merge(flat(y), flat(r), flat(k2), flat(v), flat(g), flat(d), zga, zgb, wts["merge"], tm=256)
    x1 = _mix_out(m, x2d, *wts["mix_out"], tm=512)
    out, tail = _ffn_prompt(x1, t, wts["ffn"], tm=512, fk=512)
    shift = zs3[:, t - 1:, :D_SHIFT]
    s5 = s_bd.reshape(b, N_GROUPS, HEADS_PER_GROUP, HEAD_DIM, HEADS_PER_GROUP, HEAD_DIM)
    idx = jnp.arange(HEADS_PER_GROUP)
    wkv = s5[:, :, idx, :, idx, :]
    wkv = jnp.moveaxis(wkv, 0, 2).reshape(b, N_HEADS, HEAD_DIM, HEAD_DIM)
    pool = zp3[:, t - POOL_HIST:, :]
    tiles = t // 512
    conv = tail.reshape(b, tiles, 8, D_FF)[:, tiles - 1, 8 - (CONV_W - 1):, :]
    return out.reshape(b, t, D_MODEL), shift, wkv, pool, conv


def _sample_layer(x, st_shift, st_wkv, st_pool, st_conv, wts):
    b, t, _ = x.shape
    n = b * t
    x2d = x.reshape(n, D_MODEL)
    zs, zp, zga, zgb = _in_proj(x2d, *wts["in_proj"], tm=n)
    st_rows = jnp.pad(st_shift, ((0, 0), (0, t - 1), (0, D_SHIFT_PAD - D_SHIFT))).reshape(n, D_SHIFT_PAD)
    r, lw, k2, v, a_s, b_s, g = _prep_sample(zs, st_rows, t, wts["prep"], tm=min(128, n))
    y, wkv = _wkv_sample(r, lw, k2, v, a_s, b_s, st_wkv, t)
    buf = jnp.concatenate([st_pool, zp.reshape(b, t, D_POOL)], axis=1)
    d = _pool_sample(buf, t, bt=min(32, b)).reshape(n, D_POOL)
    m = _merge(y, r, k2, v, g, d, zga, zgb, wts["merge"], tm=min(256, n))
    x1 = _mix_out(m, x2d, *wts["mix_out"], tm=n)
    conv_rows = jnp.pad(st_conv, ((0, 0), (0, t - (CONV_W - 1)), (0, 0))).reshape(n, D_FF)
    out, gate = _ffn_sample(x1, conv_rows, t, wts["ffn"], fk=512)
    shift = zs.reshape(b, t, D_SHIFT_PAD)[:, t - 1:, :D_SHIFT]
    pool = buf[:, -POOL_HIST:]
    conv = jnp.concatenate([st_conv, gate.reshape(b, t, D_FF)], axis=1)[:, -(CONV_W - 1):]
    return out.reshape(b, t, D_MODEL), shift, wkv, pool, conv


def kernel(x_prompt, x_sample, state_shift, state_wkv, state_pool, state_conv, norm_pre_mix, w_in, mu_shift, w0, w2, a0, a2, g2, k_k, k_a, r_k, lnx_w, lnx_b, w_branch_a, pool_w, pool_scale, w_branch_b, w_out, norm_post_mix, norm_pre_ffn, w_ffn_in, conv_w, conv_b, w_ffn_out, norm_post_ffn):
    weights = (norm_pre_mix, w_in, mu_shift, w0, w2, a0, a2, g2, k_k, k_a, r_k, lnx_w, lnx_b,
               w_branch_a, pool_w, pool_scale, w_branch_b, w_out, norm_post_mix,
               norm_pre_ffn, w_ffn_in, conv_w, conv_b, w_ffn_out, norm_post_ffn)
    depth = w_in.shape[0]
    yp, ys = x_prompt, x_sample
    p_states, s_states = [], []
    for l in range(depth):
        wts = _layer_weights(l, *weights)
        yp, *ps = _prompt_layer(yp, wts)
        ys, *ss = _sample_layer(ys, state_shift[l], state_wkv[l], state_pool[l], state_conv[l], wts)
        p_states.append(ps)
        s_states.append(ss)
    stack = lambda states, i: jnp.stack([s[i] for s in states])
    return (yp, ys,
            stack(p_states, 0), stack(p_states, 1), stack(p_states, 2), stack(p_states, 3),
            stack(s_states, 0), stack(s_states, 1), stack(s_states, 2), stack(s_states, 3))
```

```python
import functools

import jax
import jax.numpy as jnp
from jax import lax
from jax.experimental import pallas as pl
from jax.experimental.pallas import tpu as pltpu

F32 = jnp.float32
BF16 = jnp.bfloat16

D_MODEL = 2048
HEAD_DIM = 64
D_RWKV = 1024
N_HEADS = 16
RANK_W, RANK_A, RANK_G = 64, 64, 160
D_LORA = RANK_W + RANK_A + RANK_G
D_LORA_PAD = 384
D_SHIFT = 3 * D_RWKV + D_LORA
D_SHIFT_PAD = 3 * D_RWKV + 512
D_POOL = 1024
POOL_WINDOWS = (2, 4, 8, 16)
POOL_GROUP = 256
POOL_HIST = 15
D_FF = 5632
CONV_W = 3
NORM_EPS = 1e-6
GN_EPS = 64e-5
PAST_LEN = 16384

LANES_V7X = 128
MXU_DIM_V7X = 256
HEADS_PER_GROUP = MXU_DIM_V7X // HEAD_DIM
N_GROUPS = N_HEADS // HEADS_PER_GROUP
CHUNK = 64
VMEM_LIMIT_V7X = 56 * 1024 * 1024

COL_TILE = 512
N_COL_TILES = (D_SHIFT_PAD + D_POOL + 2 * D_MODEL) // COL_TILE
ZS_TILES = D_SHIFT_PAD // COL_TILE
ZP_TILES = D_POOL // COL_TILE
ZG_TILES = D_MODEL // COL_TILE


def _params(*sem):
    return pltpu.CompilerParams(dimension_semantics=sem, vmem_limit_bytes=VMEM_LIMIT_V7X)


def _mm(a, b):
    return jnp.dot(a.astype(BF16), b.astype(BF16), preferred_element_type=F32)


def _mm_nt(a, b):
    return lax.dot_general(a.astype(BF16), b.astype(BF16), (((1,), (1,)), ((), ())),
                           preferred_element_type=F32)


def _split_hi_lo(x):
    hi = x.astype(BF16)
    lo = (x - hi.astype(F32)).astype(BF16)
    return hi, lo


def _head_sum(x, ones_bd):
    hi, lo = _split_hi_lo(x)
    outs = []
    for gi in range(x.shape[1] // MXU_DIM_V7X):
        sl = slice(MXU_DIM_V7X * gi, MXU_DIM_V7X * (gi + 1))
        outs.append(jnp.dot(hi[:, sl], ones_bd, preferred_element_type=F32)
                    + jnp.dot(lo[:, sl], ones_bd, preferred_element_type=F32))
    return jnp.concatenate(outs, axis=1)


def _rmsnorm(x, g):
    return x * lax.rsqrt(jnp.mean(x * x, axis=-1, keepdims=True) + NORM_EPS) * g


def _w_in_pad_kernel(w_ref, o_ref):
    w = w_ref[...]
    pad = jnp.zeros((w.shape[0], D_SHIFT_PAD - D_SHIFT), F32)
    o_ref[...] = jnp.concatenate([w[:, :D_SHIFT], pad, w[:, D_SHIFT:]], axis=1).astype(BF16)


def _w_in_pad(w, rows):
    k, n = w.shape
    n_pad = n + D_SHIFT_PAD - D_SHIFT
    return pl.pallas_call(
        _w_in_pad_kernel,
        grid=(k // rows,),
        in_specs=[pl.BlockSpec((rows, n), lambda i: (i, 0))],
        out_specs=pl.BlockSpec((rows, n_pad), lambda i: (i, 0)),
        out_shape=jax.ShapeDtypeStruct((k, n_pad), BF16),
        compiler_params=_params("parallel"),
        name="w_in_pad",
    )(w)


def _in_proj_kernel(x_ref, g_ref, w_ref, zs_ref, zp_ref, zga_ref, zgb_ref, h_ref):
    j = pl.program_id(1)

    @pl.when(j == 0)
    def _():
        h_ref[...] = _rmsnorm(x_ref[...], g_ref[...]).astype(BF16)

    z = jnp.dot(h_ref[...], w_ref[...], preferred_element_type=F32)

    @pl.when(j < ZS_TILES)
    def _():
        zs_ref[...] = z

    @pl.when((j >= ZS_TILES) & (j < ZS_TILES + ZP_TILES))
    def _():
        zp_ref[...] = z

    @pl.when((j >= ZS_TILES + ZP_TILES) & (j < ZS_TILES + ZP_TILES + ZG_TILES))
    def _():
        zga_ref[...] = z

    @pl.when(j >= ZS_TILES + ZP_TILES + ZG_TILES)
    def _():
        zgb_ref[...] = z


def _in_proj(x2d, g, w_pad, tm):
    n = x2d.shape[0]

    def sect(first, count):
        return lambda i, j: (i, jnp.clip(j - first, 0, count - 1))

    return pl.pallas_call(
        _in_proj_kernel,
        grid=(n // tm, N_COL_TILES),
        in_specs=[
            pl.BlockSpec((tm, D_MODEL), lambda i, j: (i, 0)),
            pl.BlockSpec((1, D_MODEL), lambda i, j: (0, 0)),
            pl.BlockSpec((D_MODEL, COL_TILE), lambda i, j: (0, j)),
        ],
        out_specs=[
            pl.BlockSpec((tm, COL_TILE), sect(0, ZS_TILES)),
            pl.BlockSpec((tm, COL_TILE), sect(ZS_TILES, ZP_TILES)),
            pl.BlockSpec((tm, COL_TILE), sect(ZS_TILES + ZP_TILES, ZG_TILES)),
            pl.BlockSpec((tm, COL_TILE), sect(ZS_TILES + ZP_TILES + ZG_TILES, ZG_TILES)),
        ],
        out_shape=[
            jax.ShapeDtypeStruct((n, D_SHIFT_PAD), F32),
            jax.ShapeDtypeStruct((n, D_POOL), F32),
            jax.ShapeDtypeStruct((n, D_MODEL), F32),
            jax.ShapeDtypeStruct((n, D_MODEL), F32),
        ],
        scratch_shapes=[pltpu.VMEM((tm, D_MODEL), BF16)],
        compiler_params=_params("parallel", "arbitrary"),
        name="in_proj",
    )(x2d, g, w_pad)


def _prep_math(zs, prev, mu, w_lora, w0, a0, k_k, k_a, ones_bd):
    f = zs + (prev - zs) * mu
    r = f[:, 0:D_RWKV]
    k = f[:, D_RWKV:2 * D_RWKV]
    v = f[:, 2 * D_RWKV:3 * D_RWKV]
    low = f[:, 3 * D_RWKV:3 * D_RWKV + D_LORA_PAD]
    lane = lax.broadcasted_iota(jnp.int32, low.shape, 1)
    act = jnp.where(lane < RANK_W, jnp.tanh(low),
                    jnp.where(lane < RANK_W + RANK_A, low,
                              jnp.where(lane < D_LORA, jax.nn.sigmoid(low), 0.0)))
    lora = _mm(act, w_lora)
    u = w0 + lora[:, 0:D_RWKV]
    w_log = -(jnp.maximum(-u, 0.0) + jnp.log1p(jnp.exp(-jnp.abs(u)))) - 0.5
    lw = -jnp.exp(w_log)
    a = jax.nn.sigmoid(a0 + lora[:, D_RWKV:2 * D_RWKV])
    g = lora[:, 2 * D_RWKV:3 * D_RWKV]
    kk = k * k_k
    nrm = jnp.sqrt(_head_sum(kk * kk, ones_bd))
    kk = kk / jnp.maximum(nrm, 1e-12)
    k2 = k * (1.0 + (a - 1.0) * k_a)
    return r, lw, k2, v, -kk, kk * a, g


def _prep_prompt_kernel(zs_ref, mu_ref, wl_ref, w0_ref, a0_ref, kk_ref, ka_ref, ones_ref,
                        r_ref, lw_ref, k2_ref, v_ref, as_ref, bs_ref, g_ref, carry_ref):
    i = pl.program_id(1)

    @pl.when(i == 0)
    def _():
        carry_ref[...] = jnp.zeros_like(carry_ref)

    zs = zs_ref[...]
    rolled = pltpu.roll(zs, 1, 0)
    row = lax.broadcasted_iota(jnp.int32, zs.shape, 0)
    prev = jnp.where(row == 0, carry_ref[0:1, :], rolled)
    carry_ref[0:1, :] = zs[zs.shape[0] - 1:, :]
    outs = _prep_math(zs, prev, mu_ref[...], wl_ref[...], w0_ref[...], a0_ref[...],
                      kk_ref[...], ka_ref[...], ones_ref[...])
    for o_ref, o in zip((r_ref, lw_ref, k2_ref, v_ref, as_ref, bs_ref, g_ref), outs):
        o_ref[...] = o


def _prep_sample_kernel(seq, zs_ref, st_ref, mu_ref, wl_ref, w0_ref, a0_ref, kk_ref, ka_ref, ones_ref,
                        r_ref, lw_ref, k2_ref, v_ref, as_ref, bs_ref, g_ref):
    zs = zs_ref[...]
    row = lax.broadcasted_iota(jnp.int32, zs.shape, 0)
    prev = jnp.where(row % seq == 0, st_ref[...], pltpu.roll(zs, 1, 0))
    outs = _prep_math(zs, prev, mu_ref[...], wl_ref[...], w0_ref[...], a0_ref[...],
                      kk_ref[...], ka_ref[...], ones_ref[...])
    for o_ref, o in zip((r_ref, lw_ref, k2_ref, v_ref, as_ref, bs_ref, g_ref), outs):
        o_ref[...] = o


def _const_spec(shape):
    nd = len(shape)
    return pl.BlockSpec(shape, lambda *_: (0,) * nd)


def _prep_prompt(zs, wts, tm):
    b, t, _ = zs.shape
    outs = [jax.ShapeDtypeStruct((b, t, D_RWKV), F32)] * 7
    row_spec = pl.BlockSpec((None, tm, D_RWKV), lambda bi, i: (bi, i, 0))
    return pl.pallas_call(
        _prep_prompt_kernel,
        grid=(b, t // tm),
        in_specs=[pl.BlockSpec((None, tm, D_SHIFT_PAD), lambda bi, i: (bi, i, 0))]
        + [_const_spec(w.shape) for w in wts],
        out_specs=[row_spec] * 7,
        out_shape=outs,
        scratch_shapes=[pltpu.VMEM((8, D_SHIFT_PAD), F32)],
        compiler_params=_params("parallel", "arbitrary"),
        name="prep_prompt",
    )(zs, *wts)


def _prep_sample(zs, st_rows, seq, wts, tm):
    n = zs.shape[0]
    assert tm % seq == 0 and n % tm == 0
    outs = [jax.ShapeDtypeStruct((n, D_RWKV), F32)] * 7
    row_spec = pl.BlockSpec((tm, D_RWKV), lambda i: (i, 0))
    wide = pl.BlockSpec((tm, D_SHIFT_PAD), lambda i: (i, 0))
    return pl.pallas_call(
        functools.partial(_prep_sample_kernel, seq),
        grid=(n // tm,),
        in_specs=[wide, wide] + [_const_spec(w.shape) for w in wts],
        out_specs=[row_spec] * 7,
        out_shape=outs,
        compiler_params=_params("parallel"),
        name="prep_sample",
    )(zs, st_rows, *wts)


def _expand_bd(a, bd_mask):
    return jnp.where(bd_mask, jnp.concatenate([a] * HEADS_PER_GROUP, axis=0), 0.0)


def _chunk_cumsum(x):
    row = lax.broadcasted_iota(jnp.int32, x.shape, 0)
    s = 1
    while s < x.shape[0]:
        x = x + jnp.where(row >= s, pltpu.roll(x, s, 0), 0.0)
        s *= 2
    return x


def _wkv_intra(xs, bts, kts, vs, pzs, masks, order):
    bd_mask, strict_mask, incl_mask, eye_c, head_masks = masks
    c = vs[0].shape[0]
    ystacks = [jnp.concatenate([jnp.where(m, bt, 0.0) for m in head_masks]
                               + [jnp.where(m, kt, 0.0) for m in head_masks], axis=0)
               for bt, kt in zip(bts, kts)]
    grams = [_mm_nt(x, ys) for x, ys in zip(xs, ystacks)]
    l_abs = [jnp.where(strict_mask, g[0:c, 0:4 * c], 0.0) for g in grams]
    l_aks = [jnp.where(strict_mask, g[0:c, 4 * c:8 * c], 0.0) for g in grams]
    m_rbs = [jnp.where(incl_mask, g[c:2 * c, 0:4 * c], 0.0) for g in grams]
    m_rks = [jnp.where(incl_mask, g[c:2 * c, 4 * c:8 * c], 0.0) for g in grams]

    tinvs = [eye_c + l for l in l_abs]
    lps = [_mm(l, _expand_bd(l, bd_mask)) for l in l_abs]
    p = 2
    while True:
        rhss = [_expand_bd(lp, bd_mask) for lp in lps]
        if 2 * p >= order:
            tinvs = [t + _mm(t, rhs) for t, rhs in zip(tinvs, rhss)]
            break
        ress = [_mm(jnp.concatenate([lp, t], axis=0), rhs) for lp, t, rhs in zip(lps, tinvs, rhss)]
        lps = [res[0:c] for res in ress]
        tinvs = [t + res[c:2 * c] for t, res in zip(tinvs, ress)]
        p *= 2

    v_bds = [_expand_bd(v, bd_mask) for v in vs]
    ws = [pz[0:c] + _mm(l_ak, v_bd) for pz, l_ak, v_bd in zip(pzs, l_aks, v_bds)]
    us = [_mm(t, _expand_bd(w, bd_mask)) for t, w in zip(tinvs, ws)]
    ys = [pz[c:2 * c] + _mm(jnp.concatenate([m_rb, m_rk], axis=1),
                            jnp.concatenate([_expand_bd(u, bd_mask), v_bd], axis=0))
          for pz, m_rb, m_rk, u, v_bd in zip(pzs, m_rbs, m_rks, us, v_bds)]
    return ys, us


def _wkv_chunk(groups, masks):
    c = groups[0][0].shape[0]
    cls = [_chunk_cumsum(g[1]) for g in groups]
    xs = [jnp.concatenate([g[4] * jnp.exp(cl - g[1]), g[0] * jnp.exp(cl)], axis=0)
          for g, cl in zip(groups, cls)]
    pzs = [_mm_nt(x, g[6]) for x, g in zip(xs, groups)]
    e_negs = [jnp.exp(-cl) for cl in cls]
    ys, us = _wkv_intra(xs, [g[5] * e for g, e in zip(groups, e_negs)],
                        [g[2] * e for g, e in zip(groups, e_negs)], [g[3] for g in groups], pzs, masks, c)
    dss = []
    for g, cl, u in zip(groups, cls, us):
        e_rem = jnp.exp(cl[c - 1:c, :] - cl)
        uv_t = jnp.concatenate([u, g[3]], axis=0).T
        bk = jnp.concatenate([g[5] * e_rem, g[2] * e_rem], axis=0)
        dss.append(_mm(uv_t, bk))
    return [(y, g[6] * jnp.exp(cl[c - 1:c, :]) + jnp.where(masks[0], ds, 0.0))
            for y, g, cl, ds in zip(ys, groups, cls, dss)]


def _wkv_masks(c, seq):
    n = HEADS_PER_GROUP * c
    rr = lax.broadcasted_iota(jnp.int32, (n, n), 0)
    cc = lax.broadcasted_iota(jnp.int32, (n, n), 1)
    bd_mask = (rr // c) == (cc // c)
    t = lax.broadcasted_iota(jnp.int32, (c, n), 0)
    s = lax.broadcasted_iota(jnp.int32, (c, n), 1) % c
    same = (t // seq) == (s // seq)
    strict_mask = same & (t > s)
    incl_mask = same & (t >= s)
    eye_c = jnp.where(t == s, 1.0, 0.0).astype(F32)
    lane = lax.broadcasted_iota(jnp.int32, (c, MXU_DIM_V7X), 1)
    head_masks = [(lane // HEAD_DIM) == h for h in range(HEADS_PER_GROUP)]
    return bd_mask, strict_mask, incl_mask, eye_c, head_masks


def _wkv_prompt_kernel(r_ref, lw_ref, k2_ref, v_ref, as_ref, bs_ref, y_ref, sout_ref, s_ref):
    ci = pl.program_id(0)

    @pl.when(ci == 0)
    def _():
        s_ref[...] = jnp.zeros_like(s_ref)

    masks = _wkv_masks(CHUNK, CHUNK)
    chains = [(bi, gi) for bi in range(r_ref.shape[0]) for gi in range(N_GROUPS)]
    lanes = lambda gi: slice(MXU_DIM_V7X * gi, MXU_DIM_V7X * (gi + 1))
    groups = [tuple(ref[bi, :, lanes(gi)] for ref in (r_ref, lw_ref, k2_ref, v_ref, as_ref, bs_ref))
              + (s_ref[bi, gi],) for bi, gi in chains]
    for (bi, gi), (y, s_new) in zip(chains, _wkv_chunk(groups, masks)):
        y_ref[bi, :, lanes(gi)] = y
        s_ref[bi, gi] = s_new

    @pl.when(ci == pl.num_programs(0) - 1)
    def _():
        sout_ref[...] = s_ref[...]


def _wkv_prompt(r, lw, k2, v, a_s, b_s):
    b, t, _ = r.shape
    spec = pl.BlockSpec((b, CHUNK, D_RWKV), lambda ci: (0, ci, 0))
    state = (b, N_GROUPS, MXU_DIM_V7X, MXU_DIM_V7X)
    return pl.pallas_call(
        _wkv_prompt_kernel,
        grid=(t // CHUNK,),
        in_specs=[spec] * 6,
        out_specs=[spec, _const_spec(state)],
        out_shape=[jax.ShapeDtypeStruct((b, t, D_RWKV), F32), jax.ShapeDtypeStruct(state, F32)],
        scratch_shapes=[pltpu.VMEM(state, F32)],
        compiler_params=_params("arbitrary"),
        name="wkv_prompt",
    )(r, lw, k2, v, a_s, b_s)


SLAB = 8


def _slab_pair(ref, r0, c):
    return jnp.concatenate([ref[pl.ds(r0, SLAB), :], ref[pl.ds(c + r0, SLAB), :]], axis=0)


def _wkv_sample_kernel(seq, r_ref, lw_ref, k2_ref, v_ref, as_ref, bs_ref, s0_ref, y_ref, sout_ref,
                       x_ref, btk_ref, bk_ref, p_ref, uv_ref, gam_ref):
    c = r_ref.shape[0]
    per_slab = SLAB // seq

    lw = lw_ref[...]
    tpos = lax.broadcasted_iota(jnp.int32, lw.shape, 0) % seq
    cl = lw
    s = 1
    while s < seq:
        cl = cl + jnp.where(tpos >= s, pltpu.roll(cl, s, 0), 0.0)
        s *= 2
    tot = jnp.where(tpos == seq - 1, cl, 0.0)
    s = 1
    while s < seq:
        tot = tot + pltpu.roll(tot, c - s, 0)
        s *= 2
    e_neg = jnp.exp(-cl)
    e_rem = jnp.exp(tot - cl)
    b_s = bs_ref[...]
    k2 = k2_ref[...]
    x_ref[0:c, :] = as_ref[...] * jnp.exp(cl - lw)
    x_ref[c:2 * c, :] = r_ref[...] * jnp.exp(cl)
    btk_ref[0:c, :] = b_s * e_neg
    btk_ref[c:2 * c, :] = k2 * e_neg
    bk_ref[0:c, :] = b_s * e_rem
    bk_ref[c:2 * c, :] = k2 * e_rem
    gam_ref[...] = jnp.exp(tot)

    first_seq = (lax.broadcasted_iota(jnp.int32, (2 * SLAB, HEAD_DIM), 0) % SLAB) < seq

    def state_in(p, carry):
        r0 = pl.multiple_of(p * SLAB, SLAB)
        xp = _slab_pair(x_ref, r0, c)
        outs = []
        for h in range(N_HEADS):
            xs = xp[:, HEAD_DIM * h:HEAD_DIM * (h + 1)]
            p0 = _mm_nt(xs, s0_ref[per_slab * p, h])
            p1 = _mm_nt(xs, s0_ref[per_slab * p + 1, h])
            outs.append(jnp.where(first_seq, p0, p1))
        pp = jnp.concatenate(outs, axis=1)
        p_ref[pl.ds(r0, SLAB), :] = pp[0:SLAB]
        p_ref[pl.ds(c + r0, SLAB), :] = pp[SLAB:2 * SLAB]
        return carry

    lax.fori_loop(0, c // SLAB, state_in, 0)

    masks = _wkv_masks(c, seq)
    sls = [slice(MXU_DIM_V7X * gi, MXU_DIM_V7X * (gi + 1)) for gi in range(N_GROUPS)]
    ys, us = _wkv_intra([x_ref[:, sl] for sl in sls], [btk_ref[0:c, sl] for sl in sls],
                        [btk_ref[c:2 * c, sl] for sl in sls], [v_ref[:, sl] for sl in sls],
                        [p_ref[:, sl] for sl in sls], masks, seq)
    for sl, y, u in zip(sls, ys, us):
        y_ref[:, sl] = y
        uv_ref[0:c, sl] = u
    uv_ref[c:2 * c, :] = v_ref[...]

    first_rows = (lax.broadcasted_iota(jnp.int32, (2 * SLAB, D_RWKV), 0) % SLAB) < seq

    def state_out(p, carry):
        r0 = pl.multiple_of(p * SLAB, SLAB)
        uvp = _slab_pair(uv_ref, r0, c)
        bkp = _slab_pair(bk_ref, r0, c)
        gam8 = gam_ref[pl.ds(r0, SLAB), :]
        for bb in range(per_slab):
            uvm = jnp.where(first_rows if bb == 0 else jnp.logical_not(first_rows), uvp, 0.0)
            gam = gam8[seq * bb:seq * bb + 1, :]
            for j in range(N_HEADS // 2):
                ls = slice(2 * HEAD_DIM * j, 2 * HEAD_DIM * (j + 1))
                ds2 = _mm(uvm[:, ls].T, bkp[:, ls])
                for hh in range(2):
                    h = 2 * j + hh
                    hs = slice(HEAD_DIM * hh, HEAD_DIM * (hh + 1))
                    sout_ref[per_slab * p + bb, h] = (
                        s0_ref[per_slab * p + bb, h] * gam[:, HEAD_DIM * h:HEAD_DIM * (h + 1)] + ds2[hs, hs])
        return carry

    lax.fori_loop(0, c // SLAB, state_out, 0)


def _wkv_sample(r, lw, k2, v, a_s, b_s, s0, seq):
    n = r.shape[0]
    assert SLAB % seq == 0 and SLAB // seq == 2 and n % CHUNK == 0
    nb = CHUNK // seq
    vec = pl.BlockSpec((CHUNK, D_RWKV), lambda i: (i, 0))
    st = pl.BlockSpec((nb, N_HEADS, HEAD_DIM, HEAD_DIM), lambda i: (i, 0, 0, 0))
    return pl.pallas_call(
        functools.partial(_wkv_sample_kernel, seq),
        grid=(n // CHUNK,),
        in_specs=[vec] * 6 + [st],
        out_specs=[vec, st],
        out_shape=[jax.ShapeDtypeStruct((n, D_RWKV), F32), jax.ShapeDtypeStruct(s0.shape, F32)],
        scratch_shapes=[pltpu.VMEM((2 * CHUNK, D_RWKV), F32)] * 5 + [pltpu.VMEM((CHUNK, D_RWKV), F32)],
        compiler_params=_params("parallel"),
        name="wkv_sample",
    )(r, lw, k2, v, a_s, b_s, s0)


def _pool_prompt_kernel(zp_ref, d_ref, carry_ref):
    i = pl.program_id(1)
    tm = zp_ref.shape[0]

    @pl.when(i == 0)
    def _():
        carry_ref[...] = jnp.zeros_like(carry_ref)

    zp = zp_ref[...]
    buf = jnp.concatenate([carry_ref[...], zp], axis=0)
    carry_ref[...] = zp[tm - 16:, :]
    pos = (i * tm + lax.broadcasted_iota(jnp.int32, (tm, POOL_GROUP), 0) + 1).astype(F32)
    for gi, win in enumerate(POOL_WINDOWS):
        sl = slice(POOL_GROUP * gi, POOL_GROUP * (gi + 1))
        acc = buf[:, sl]
        s = 1
        while s < win:
            acc = acc + pltpu.roll(acc, s, 0)
            s *= 2
        cnt = jnp.minimum(float(win), pos)
        d_ref[:, sl] = acc[16:, :] / cnt - zp[:, sl]


def _pool_prompt(zp, tm):
    b, t, _ = zp.shape
    spec = pl.BlockSpec((None, tm, D_POOL), lambda bi, i: (bi, i, 0))
    return pl.pallas_call(
        _pool_prompt_kernel,
        grid=(b, t // tm),
        in_specs=[spec],
        out_specs=spec,
        out_shape=jax.ShapeDtypeStruct(zp.shape, F32),
        scratch_shapes=[pltpu.VMEM((16, D_POOL), F32)],
        compiler_params=_params("parallel", "arbitrary"),
        name="pool_prompt",
    )(zp)


def _pool_sample_kernel(buf_ref, d_ref):
    nt = d_ref.shape[1]
    for t in range(nt):
        for gi, win in enumerate(POOL_WINDOWS):
            sl = slice(POOL_GROUP * gi, POOL_GROUP * (gi + 1))
            acc = buf_ref[:, POOL_HIST + t, sl]
            for j in range(1, win):
                acc = acc + buf_ref[:, POOL_HIST + t - j, sl]
            d_ref[:, t, sl] = acc / float(min(win, PAST_LEN + 1)) - buf_ref[:, POOL_HIST + t, sl]


def _pool_sample(buf, seq, bt):
    b = buf.shape[0]
    return pl.pallas_call(
        _pool_sample_kernel,
        grid=(b // bt,),
        in_specs=[pl.BlockSpec((bt, buf.shape[1], D_POOL), lambda i: (i, 0, 0))],
        out_specs=pl.BlockSpec((bt, seq, D_POOL), lambda i: (i, 0, 0)),
        out_shape=jax.ShapeDtypeStruct((b, seq, D_POOL), F32),
        compiler_params=_params("parallel"),
        name="pool_sample",
    )(buf)


def _merge_kernel(y_ref, r_ref, k2_ref, v_ref, g_ref, d_ref, zga_ref, zgb_ref,
                  rk_ref, lw_ref, lb_ref, ones_ref, wa_ref, pw_ref, ps_ref, wb_ref, m_ref):
    ones_bd = ones_ref[...]
    y = y_ref[...]
    mu = _head_sum(y, ones_bd) * (1.0 / HEAD_DIM)
    yc = y - mu
    var = _head_sum(yc * yc, ones_bd) * (1.0 / HEAD_DIM)
    yn = yc * lax.rsqrt(var + GN_EPS) * lw_ref[...] + lb_ref[...]
    v = v_ref[...]
    bonus = _head_sum(r_ref[...] * k2_ref[...] * rk_ref[...], ones_bd) * v
    ya = (yn + bonus) * g_ref[...]
    d = d_ref[...]
    yb = jnp.concatenate(
        [_mm(d[:, POOL_GROUP * gi:POOL_GROUP * (gi + 1)], pw_ref[gi]) for gi in range(len(POOL_WINDOWS))],
        axis=1) * ps_ref[...]
    m_ref[...] = (jax.nn.sigmoid(zga_ref[...]) * _mm(ya, wa_ref[...])
                  + jax.nn.sigmoid(zgb_ref[...]) * _mm(yb, wb_ref[...]))


def _merge(y, r, k2, v, g, d, zga, zgb, wts, tm):
    n = y.shape[0]
    half = pl.BlockSpec((tm, D_RWKV), lambda i: (i, 0))
    full = pl.BlockSpec((tm, D_MODEL), lambda i: (i, 0))
    return pl.pallas_call(
        _merge_kernel,
        grid=(n // tm,),
        in_specs=[half] * 6 + [full] * 2 + [_const_spec(w.shape) for w in wts],
        out_specs=full,
        out_shape=jax.ShapeDtypeStruct((n, D_MODEL), F32),
        compiler_params=_params("parallel"),
        name="merge",
    )(y, r, k2, v, g, d, zga, zgb, *wts)


def _mix_out_kernel(m_ref, x_ref, w_ref, g_ref, o_ref):
    mo = _mm(m_ref[...], w_ref[...])
    o_ref[...] = x_ref[...] + _rmsnorm(mo, g_ref[...])


def _mix_out(m, x2d, w_out, g, tm):
    n = m.shape[0]
    full = pl.BlockSpec((tm, D_MODEL), lambda i: (i, 0))
    return pl.pallas_call(
        _mix_out_kernel,
        grid=(n // tm,),
        in_specs=[full, full, _const_spec(w_out.shape), _const_spec(g.shape)],
        out_specs=full,
        out_shape=jax.ShapeDtypeStruct((n, D_MODEL), F32),
        compiler_params=_params("parallel"),
        name="mix_out",
    )(m, x2d, w_out, g)


def _gelu_tanh(x):
    return 0.5 * x * (1.0 + jnp.tanh(0.7978845608028654 * (x + 0.044715 * x * x * x)))


def _ffn_body(x_ref, gn_ref, wg_ref, wu_ref, cw_ref, cb_ref, wo_ref, gp_ref,
              o_ref, tail_ref, h_ref, acc_ref, prev1, prev2):
    c = pl.program_id(1)

    @pl.when(c == 0)
    def _():
        h_ref[...] = _rmsnorm(x_ref[...], gn_ref[...]).astype(BF16)
        acc_ref[...] = jnp.zeros_like(acc_ref)

    h = h_ref[...]
    gate = jnp.dot(h, wg_ref[...], preferred_element_type=F32)
    up = jnp.dot(h, wu_ref[...], preferred_element_type=F32)
    cw = cw_ref[...]
    cv = cb_ref[...] + cw[0:1, :] * prev2(gate) + cw[1:2, :] * prev1(gate) + cw[2:3, :] * gate
    tail_ref[...] = gate[gate.shape[0] - tail_ref.shape[0]:, :]
    acc_ref[...] += _mm(_gelu_tanh(cv) * up, wo_ref[...])

    @pl.when(c == pl.num_programs(1) - 1)
    def _():
        o_ref[...] = x_ref[...] + _rmsnorm(acc_ref[...], gp_ref[...])


def _ffn_prompt_kernel(tiles_per_seq, x_ref, gn_ref, wg_ref, wu_ref, cw_ref, cb_ref, wo_ref, gp_ref,
                       o_ref, tail_ref, h_ref, acc_ref, carry_ref):
    i = pl.program_id(0)
    c = pl.program_id(1)

    @pl.when(i % tiles_per_seq == 0)
    def _():
        carry_ref[c] = jnp.zeros(carry_ref.shape[1:], F32)

    hist = carry_ref[c]

    def shifted(gate, s):
        row = lax.broadcasted_iota(jnp.int32, gate.shape, 0)
        rolled = pltpu.roll(gate, s, 0)
        out = rolled
        for j in range(s):
            out = jnp.where(row == j, hist[8 - s + j:9 - s + j, :], out)
        return out

    _ffn_body(x_ref, gn_ref, wg_ref, wu_ref, cw_ref, cb_ref, wo_ref, gp_ref, o_ref, tail_ref,
              h_ref, acc_ref, lambda g: shifted(g, 1), lambda g: shifted(g, 2))
    carry_ref[c] = tail_ref[...]


def _ffn_sample_kernel(seq, x_ref, st_ref, gn_ref, wg_ref, wu_ref, cw_ref, cb_ref, wo_ref, gp_ref,
                       o_ref, tail_ref, h_ref, acc_ref):
    st = st_ref[...]
    n = st.shape[0]
    tpos = lax.broadcasted_iota(jnp.int32, st.shape, 0) % seq
    prev1 = lambda g: jnp.where(tpos == 0, pltpu.roll(st, n - 1, 0), pltpu.roll(g, 1, 0))
    prev2 = lambda g: jnp.where(tpos < CONV_W - 1, st, pltpu.roll(g, 2, 0))
    _ffn_body(x_ref, gn_ref, wg_ref, wu_ref, cw_ref, cb_ref, wo_ref, gp_ref, o_ref, tail_ref,
              h_ref, acc_ref, prev1, prev2)


def _ffn_prompt(x2d, seq_len, wts, tm, fk):
    n = x2d.shape[0]
    nc = D_FF // fk
    gn, w_in, cw, cb, w_out, gp = wts
    full = pl.BlockSpec((tm, D_MODEL), lambda i, c: (i, 0))
    return pl.pallas_call(
        functools.partial(_ffn_prompt_kernel, seq_len // tm),
        grid=(n // tm, nc),
        in_specs=[full, _const_spec(gn.shape),
                  pl.BlockSpec((D_MODEL, fk), lambda i, c: (0, c)),
                  pl.BlockSpec((D_MODEL, fk), lambda i, c: (0, c + nc)),
                  pl.BlockSpec((CONV_W, fk), lambda i, c: (0, c)),
                  pl.BlockSpec((1, fk), lambda i, c: (0, c)),
                  pl.BlockSpec((fk, D_MODEL), lambda i, c: (c, 0)),
                  _const_spec(gp.shape)],
        out_specs=[full, pl.BlockSpec((None, 8, fk), lambda i, c: (i, 0, c))],
        out_shape=[jax.ShapeDtypeStruct((n, D_MODEL), F32),
                   jax.ShapeDtypeStruct((n // tm, 8, D_FF), F32)],
        scratch_shapes=[pltpu.VMEM((tm, D_MODEL), BF16), pltpu.VMEM((tm, D_MODEL), F32),
                        pltpu.VMEM((nc, 8, fk), F32)],
        compiler_params=_params("arbitrary", "arbitrary"),
        name="ffn_prompt",
    )(x2d, gn, w_in, w_in, cw, cb, w_out, gp)


def _ffn_sample(x2d, st_rows, seq, wts, fk):
    n = x2d.shape[0]
    assert seq >= CONV_W - 1
    nc = D_FF // fk
    gn, w_in, cw, cb, w_out, gp = wts
    full = pl.BlockSpec((n, D_MODEL), lambda i, c: (0, 0))
    cols = pl.BlockSpec((n, fk), lambda i, c: (0, c))
    return pl.pallas_call(
        functools.partial(_ffn_sample_kernel, seq),
        grid=(1, nc),
        in_specs=[full, cols, _const_spec(gn.shape),
                  pl.BlockSpec((D_MODEL, fk), lambda i, c: (0, c)),
                  pl.BlockSpec((D_MODEL, fk), lambda i, c: (0, c + nc)),
                  pl.BlockSpec((CONV_W, fk), lambda i, c: (0, c)),
                  pl.BlockSpec((1, fk), lambda i, c: (0, c)),
                  pl.BlockSpec((fk, D_MODEL), lambda i, c: (c, 0)),
                  _const_spec(gp.shape)],
        out_specs=[full, cols],
        out_shape=[jax.ShapeDtypeStruct((n, D_MODEL), F32), jax.ShapeDtypeStruct((n, D_FF), F32)],
        scratch_shapes=[pltpu.VMEM((n, D_MODEL), BF16), pltpu.VMEM((n, D_MODEL), F32)],
        compiler_params=_params("arbitrary", "arbitrary"),
        name="ffn_sample",
    )(x2d, st_rows, gn, w_in, w_in, cw, cb, w_out, gp)


def _row(v):
    return v.reshape(1, -1).astype(F32)


def _ones_bd():
    i = jnp.arange(MXU_DIM_V7X) // HEAD_DIM
    return (i[:, None] == i[None, :]).astype(BF16)


def _layer_weights(l, norm_pre_mix, w_in, mu_shift, w0, w2, a0, a2, g2, k_k, k_a, r_k, lnx_w, lnx_b,
                   w_branch_a, pool_w, pool_scale, w_branch_b, w_out, norm_post_mix,
                   norm_pre_ffn, w_ffn_in, conv_w, conv_b, w_ffn_out, norm_post_ffn):
    w_pad = _w_in_pad(w_in[l], rows=256)
    mu = jnp.pad(mu_shift[l], (0, D_SHIFT_PAD - D_SHIFT)).reshape(1, -1)
    w_lora = jnp.zeros((D_LORA_PAD, 3 * D_RWKV), F32)
    w_lora = w_lora.at[0:RANK_W, 0:D_RWKV].set(w2[l])
    w_lora = w_lora.at[RANK_W:RANK_W + RANK_A, D_RWKV:2 * D_RWKV].set(a2[l])
    w_lora = w_lora.at[RANK_W + RANK_A:D_LORA, 2 * D_RWKV:].set(g2[l])
    ones_bd = _ones_bd()
    return dict(
        in_proj=(_row(norm_pre_mix[l]), w_pad),
        prep=(mu, w_lora.astype(BF16), _row(w0[l]), _row(a0[l]), _row(k_k[l]), _row(k_a[l]), ones_bd),
        merge=(_row(r_k[l]), _row(lnx_w[l]), _row(lnx_b[l]), ones_bd, w_branch_a[l].astype(BF16),
               pool_w[l].astype(BF16), _row(pool_scale[l]), w_branch_b[l].astype(BF16)),
        mix_out=(w_out[l].astype(BF16), _row(norm_post_mix[l])),
        ffn=(_row(norm_pre_ffn[l]), w_ffn_in[l].astype(BF16), conv_w[l].astype(F32), _row(conv_b[l]),
             w_ffn_out[l].astype(BF16), _row(norm_post_ffn[l])),
    )


def _prompt_layer(x, wts):
    b, t, _ = x.shape
    n = b * t
    x2d = x.reshape(n, D_MODEL)
    zs, zp, zga, zgb = _in_proj(x2d, *wts["in_proj"], tm=512)
    zs3 = zs.reshape(b, t, D_SHIFT_PAD)
    r, lw, k2, v, a_s, b_s, g = _prep_prompt(zs3, wts["prep"], tm=256)
    y, s_bd = _wkv_prompt(r, lw, k2, v, a_s, b_s)
    zp3 = zp.reshape(b, t, D_POOL)
    d = _pool_prompt(zp3, tm=512)
    flat = lambda a: a.reshape(n, a.shape[-1])
    m = _merge(flat(y), flat(r), flat(k2), flat(v), flat(g), flat(d), zga, zgb, wts["merge"], tm=256)
    x1 = _mix_out(m, x2d, *wts["mix_out"], tm=512)
    out, tail = _ffn_prompt(x1, t, wts["ffn"], tm=512, fk=512)
    shift = zs3[:, t - 1:, :D_SHIFT]
    s5 = s_bd.reshape(b, N_GROUPS, HEADS_PER_GROUP, HEAD_DIM, HEADS_PER_GROUP, HEAD_DIM)
    idx = jnp.arange(HEADS_PER_GROUP)
    wkv = s5[:, :, idx, :, idx, :]
    wkv = jnp.moveaxis(wkv, 0, 2).reshape(b, N_HEADS, HEAD_DIM, HEAD_DIM)
    pool = zp3[:, t - POOL_HIST:, :]
    tiles = t // 512
    conv = tail.reshape(b, tiles, 8, D_FF)[:, tiles - 1, 8 - (CONV_W - 1):, :]
    return out.reshape(b, t, D_MODEL), shift, wkv, pool, conv


def _sample_layer(x, st_shift, st_wkv, st_pool, st_conv, wts):
    b, t, _ = x.shape
    n = b * t
    x2d = x.reshape(n, D_MODEL)
    zs, zp, zga, zgb = _in_proj(x2d, *wts["in_proj"], tm=n)
    st_rows = jnp.pad(st_shift, ((0, 0), (0, t - 1), (0, D_SHIFT_PAD - D_SHIFT))).reshape(n, D_SHIFT_PAD)
    r, lw, k2, v, a_s, b_s, g = _prep_sample(zs, st_rows, t, wts["prep"], tm=min(128, n))
    y, wkv = _wkv_sample(r, lw, k2, v, a_s, b_s, st_wkv, t)
    buf = jnp.concatenate([st_pool, zp.reshape(b, t, D_POOL)], axis=1)
    d = _pool_sample(buf, t, bt=min(32, b)).reshape(n, D_POOL)
    m = _merge(y, r, k2, v, g, d, zga, zgb, wts["merge"], tm=min(256, n))
    x1 = _mix_out(m, x2d, *wts["mix_out"], tm=n)
    conv_rows = jnp.pad(st_conv, ((0, 0), (0, t - (CONV_W - 1)), (0, 0))).reshape(n, D_FF)
    out, gate = _ffn_sample(x1, conv_rows, t, wts["ffn"], fk=512)
    shift = zs.reshape(b, t, D_SHIFT_PAD)[:, t - 1:, :D_SHIFT]
    pool = buf[:, -POOL_HIST:]
    conv = jnp.concatenate([st_conv, gate.reshape(b, t, D_FF)], axis=1)[:, -(CONV_W - 1):]
    return out.reshape(b, t, D_MODEL), shift, wkv, pool, conv


def kernel(x_prompt, x_sample, state_shift, state_wkv, state_pool, state_conv, norm_pre_mix, w_in, mu_shift, w0, w2, a0, a2, g2, k_k, k_a, r_k, lnx_w, lnx_b, w_branch_a, pool_w, pool_scale, w_branch_b, w_out, norm_post_mix, norm_pre_ffn, w_ffn_in, conv_w, conv_b, w_ffn_out, norm_post_ffn):
    weights = (norm_pre_mix, w_in, mu_shift, w0, w2, a0, a2, g2, k_k, k_a, r_k, lnx_w, lnx_b,
               w_branch_a, pool_w, pool_scale, w_branch_b, w_out, norm_post_mix,
               norm_pre_ffn, w_ffn_in, conv_w, conv_b, w_ffn_out, norm_post_ffn)
    depth = w_in.shape[0]
    yp, ys = x_prompt, x_sample
    p_states, s_states = [], []
    for l in range(depth):
        wts = _layer_weights(l, *weights)
        yp, *ps = _prompt_layer(yp, wts)
        ys, *ss = _sample_layer(ys, state_shift[l], state_wkv[l], state_pool[l], state_conv[l], wts)
        p_states.append(ps)
        s_states.append(ss)
    stack = lambda states, i: jnp.stack([s[i] for s in states])
    return (yp, ys,
            stack(p_states, 0), stack(p_states, 1), stack(p_states, 2), stack(p_states, 3),
            stack(s_states, 0), stack(s_states, 1), stack(s_states, 2), stack(s_states, 3))
```

```python
import functools
import math

import jax
import jax.numpy as jnp
from jax import lax
from jax.experimental import pallas as pl
from jax.experimental.pallas import tpu as pltpu

F32 = jnp.float32
BF16 = jnp.bfloat16

D_MODEL = 2048
HEAD_DIM = 64
D_RWKV = 1024
N_HEADS = 16
RANK_W, RANK_A, RANK_G = 64, 64, 160
D_LORA = RANK_W + RANK_A + RANK_G
D_LORA_PAD = 384
D_SHIFT = 3 * D_RWKV + D_LORA
D_SHIFT_PAD = 3 * D_RWKV + D_LORA_PAD
D_POOL = 1024
POOL_WINDOWS = (2, 4, 8, 16)
POOL_GROUP = 256
POOL_HIST = 15
D_FF = 5632
CONV_W = 3
NORM_EPS = 1e-6
GN_EPS = 64e-5
PAST_LEN = 16384

LANES_V7X = 128
MXU_DIM_V7X = 256
HEADS_PER_GROUP = MXU_DIM_V7X // HEAD_DIM
N_GROUPS = N_HEADS // HEADS_PER_GROUP
CHUNK = 64
VMEM_LIMIT_V7X = 56 * 1024 * 1024


def _params(*sem):
    return pltpu.CompilerParams(dimension_semantics=sem, vmem_limit_bytes=VMEM_LIMIT_V7X)


def _mm(a, b):
    return jnp.dot(a.astype(BF16), b.astype(BF16), preferred_element_type=F32)


def _mm_nt(a, b):
    return lax.dot_general(a.astype(BF16), b.astype(BF16), (((1,), (1,)), ((), ())),
                           preferred_element_type=F32)


def _split_hi_lo(x):
    hi = x.astype(BF16)
    lo = (x - hi.astype(F32)).astype(BF16)
    return hi, lo


def _head_sum(x, ones_bd):
    hi, lo = _split_hi_lo(x)
    outs = []
    for gi in range(x.shape[1] // MXU_DIM_V7X):
        sl = slice(MXU_DIM_V7X * gi, MXU_DIM_V7X * (gi + 1))
        outs.append(jnp.dot(hi[:, sl], ones_bd, preferred_element_type=F32)
                    + jnp.dot(lo[:, sl], ones_bd, preferred_element_type=F32))
    return jnp.concatenate(outs, axis=1)


def _rmsnorm(x, g):
    return x * lax.rsqrt(jnp.mean(x * x, axis=-1, keepdims=True) + NORM_EPS) * g


def _pre_norm_kernel(prompt_tiles, xp_ref, xs_ref, g_ref, h_ref):
    i = pl.program_id(0)

    @pl.when(i < prompt_tiles)
    def _():
        h_ref[...] = _rmsnorm(xp_ref[...], g_ref[...]).astype(BF16)

    @pl.when(i >= prompt_tiles)
    def _():
        h_ref[...] = _rmsnorm(xs_ref[...], g_ref[...]).astype(BF16)


def _pre_norm(xp, xs, g, tm):
    pt, st = xp.shape[0] // tm, xs.shape[0] // tm
    return pl.pallas_call(
        functools.partial(_pre_norm_kernel, pt),
        grid=(pt + st,),
        in_specs=[pl.BlockSpec((tm, D_MODEL), lambda i: (jnp.minimum(i, pt - 1), 0)),
                  pl.BlockSpec((tm, D_MODEL), lambda i: (jnp.maximum(i - pt, 0), 0)),
                  _const_spec(g.shape)],
        out_specs=pl.BlockSpec((tm, D_MODEL), lambda i: (i, 0)),
        out_shape=jax.ShapeDtypeStruct((xp.shape[0] + xs.shape[0], D_MODEL), BF16),
        compiler_params=_params("arbitrary"),
        name="pre_norm",
    )(xp, xs, g)


def _in_proj_kernel(h_ref, w_ref, o_ref, wb_ref):
    @pl.when(pl.program_id(1) == 0)
    def _():
        wb_ref[...] = w_ref[...].astype(BF16)

    o_ref[...] = lax.dot_general(h_ref[...], wb_ref[...], (((1,), (1,)), ((), ())),
                                 preferred_element_type=F32)


def _in_proj(h, w_t, off, width, tn, tm):
    n = h.shape[0]
    assert width % tn == 0 and n % tm == 0
    return pl.pallas_call(
        _in_proj_kernel,
        grid=(width // tn, n // tm),
        in_specs=[pl.BlockSpec((tm, D_MODEL), lambda j, i: (i, 0)),
                  pl.BlockSpec((pl.Element(tn), pl.Element(D_MODEL)),
                               lambda j, i: (SLAB * (off // SLAB + j * (tn // SLAB)), 0))],
        out_specs=pl.BlockSpec((tm, tn), lambda j, i: (i, j)),
        out_shape=jax.ShapeDtypeStruct((n, width), F32),
        scratch_shapes=[pltpu.VMEM((tn, D_MODEL), BF16)],
        compiler_params=_params("arbitrary", "arbitrary"),
        name="in_proj",
    )(h, w_t)


def _prep_math(zs, prev, mu, w_lora, w0, a0, k_k, k_a, ones_bd):
    f = zs + (prev - zs) * mu
    r = f[:, 0:D_RWKV]
    k = f[:, D_RWKV:2 * D_RWKV]
    v = f[:, 2 * D_RWKV:3 * D_RWKV]
    low = f[:, 3 * D_RWKV:3 * D_RWKV + D_LORA_PAD]
    lane = lax.broadcasted_iota(jnp.int32, low.shape, 1)
    act = jnp.where(lane < RANK_W, jnp.tanh(low),
                    jnp.where(lane < RANK_W + RANK_A, low,
                              jnp.where(lane < D_LORA, jax.nn.sigmoid(low), 0.0)))
    lora = _mm(act, w_lora)
    u = w0 + lora[:, 0:D_RWKV]
    w_log = -(jnp.maximum(-u, 0.0) + jnp.log1p(jnp.exp(-jnp.abs(u)))) - 0.5
    lw = -jnp.exp(w_log)
    a = jax.nn.sigmoid(a0 + lora[:, D_RWKV:2 * D_RWKV])
    g = lora[:, 2 * D_RWKV:3 * D_RWKV]
    kk = k * k_k
    nrm = jnp.sqrt(_head_sum(kk * kk, ones_bd))
    kk = kk / jnp.maximum(nrm, 1e-12)
    k2 = k * (1.0 + (a - 1.0) * k_a)
    return r, lw, k2, v, -kk, kk * a, g


def _prep_prompt_kernel(zrkv_ref, zl_ref, mu_ref, wl_ref, w0_ref, a0_ref, kk_ref, ka_ref, ones_ref,
                        r_ref, lw_ref, k2_ref, v_ref, as_ref, bs_ref, g_ref, carry_ref):
    i = pl.program_id(1)

    @pl.when(i == 0)
    def _():
        carry_ref[...] = jnp.zeros_like(carry_ref)

    zs = jnp.concatenate([zrkv_ref[...], zl_ref[...]], axis=1)
    rolled = pltpu.roll(zs, 1, 0)
    row = lax.broadcasted_iota(jnp.int32, zs.shape, 0)
    prev = jnp.where(row == 0, carry_ref[0:1, :], rolled)
    carry_ref[0:1, :] = zs[zs.shape[0] - 1:, :]
    outs = _prep_math(zs, prev, mu_ref[...], wl_ref[...], w0_ref[...], a0_ref[...],
                      kk_ref[...], ka_ref[...], ones_ref[...])
    for o_ref, o in zip((r_ref, lw_ref, k2_ref, v_ref, as_ref, bs_ref, g_ref), outs):
        o_ref[...] = o


def _prep_sample_kernel(seq, zrkv_ref, zl_ref, st_ref, mu_ref, wl_ref, w0_ref, a0_ref, kk_ref, ka_ref,
                        ones_ref, r_ref, lw_ref, k2_ref, v_ref, as_ref, bs_ref, g_ref):
    zs = jnp.concatenate([zrkv_ref[...], zl_ref[...]], axis=1)
    row = lax.broadcasted_iota(jnp.int32, zs.shape, 0)
    prev = jnp.where(row % seq == 0, st_ref[...], pltpu.roll(zs, 1, 0))
    outs = _prep_math(zs, prev, mu_ref[...], wl_ref[...], w0_ref[...], a0_ref[...],
                      kk_ref[...], ka_ref[...], ones_ref[...])
    for o_ref, o in zip((r_ref, lw_ref, k2_ref, v_ref, as_ref, bs_ref, g_ref), outs):
        o_ref[...] = o


def _const_spec(shape):
    nd = len(shape)
    return pl.BlockSpec(shape, lambda *_: (0,) * nd)


def _prep_prompt(zrkv, zl, b, t, wts, tm):
    tiles = t // tm
    outs = [jax.ShapeDtypeStruct((b, t, D_RWKV), F32)] * 7
    row_spec = pl.BlockSpec((None, tm, D_RWKV), lambda bi, i: (bi, i, 0))
    return pl.pallas_call(
        _prep_prompt_kernel,
        grid=(b, tiles),
        in_specs=[pl.BlockSpec((tm, 3 * D_RWKV), lambda bi, i: (bi * tiles + i, 0)),
                  pl.BlockSpec((tm, D_LORA_PAD), lambda bi, i: (bi * tiles + i, 0))]
        + [_const_spec(w.shape) for w in wts],
        out_specs=[row_spec] * 7,
        out_shape=outs,
        scratch_shapes=[pltpu.VMEM((8, D_SHIFT_PAD), F32)],
        compiler_params=_params("parallel", "arbitrary"),
        name="prep_prompt",
    )(zrkv, zl, *wts)


def _prep_sample(zrkv, zl, row0, n, st_rows, seq, wts, tm):
    assert tm % seq == 0 and n % tm == 0 and row0 % tm == 0
    first = row0 // tm
    outs = [jax.ShapeDtypeStruct((n, D_RWKV), F32)] * 7
    row_spec = pl.BlockSpec((tm, D_RWKV), lambda i: (i, 0))
    return pl.pallas_call(
        functools.partial(_prep_sample_kernel, seq),
        grid=(n // tm,),
        in_specs=[pl.BlockSpec((tm, 3 * D_RWKV), lambda i: (first + i, 0)),
                  pl.BlockSpec((tm, D_LORA_PAD), lambda i: (first + i, 0)),
                  pl.BlockSpec((tm, D_SHIFT_PAD), lambda i: (i, 0))]
        + [_const_spec(w.shape) for w in wts],
        out_specs=[row_spec] * 7,
        out_shape=outs,
        compiler_params=_params("parallel"),
        name="prep_sample",
    )(zrkv, zl, st_rows, *wts)


def _expand_bd(a, bd_mask):
    return jnp.where(bd_mask, jnp.concatenate([a] * HEADS_PER_GROUP, axis=0), 0.0)


def _chunk_cumsum(x):
    row = lax.broadcasted_iota(jnp.int32, x.shape, 0)
    s = 1
    while s < x.shape[0]:
        x = x + jnp.where(row >= s, pltpu.roll(x, s, 0), 0.0)
        s *= 2
    return x


def _wkv_intra(xs, bts, kts, vs, pzs, masks, order):
    bd_mask, strict_mask, incl_mask, eye_c, head_masks = masks
    c = vs[0].shape[0]
    ystacks = [jnp.concatenate([jnp.where(m, bt, 0.0) for m in head_masks]
                               + [jnp.where(m, kt, 0.0) for m in head_masks], axis=0)
               for bt, kt in zip(bts, kts)]
    grams = [_mm_nt(x, ys) for x, ys in zip(xs, ystacks)]
    l_abs = [jnp.where(strict_mask, g[0:c, 0:4 * c], 0.0) for g in grams]
    l_aks = [jnp.where(strict_mask, g[0:c, 4 * c:8 * c], 0.0) for g in grams]
    m_rbs = [jnp.where(incl_mask, g[c:2 * c, 0:4 * c], 0.0) for g in grams]
    m_rks = [jnp.where(incl_mask, g[c:2 * c, 4 * c:8 * c], 0.0) for g in grams]

    tinvs = [eye_c + l for l in l_abs]
    lps = [_mm(l, _expand_bd(l, bd_mask)) for l in l_abs]
    p = 2
    while True:
        rhss = [_expand_bd(lp, bd_mask) for lp in lps]
        if 2 * p >= order:
            tinvs = [t + _mm(t, rhs) for t, rhs in zip(tinvs, rhss)]
            break
        ress = [_mm(jnp.concatenate([lp, t], axis=0), rhs) for lp, t, rhs in zip(lps, tinvs, rhss)]
        lps = [res[0:c] for res in ress]
        tinvs = [t + res[c:2 * c] for t, res in zip(tinvs, ress)]
        p *= 2

    v_bds = [_expand_bd(v, bd_mask) for v in vs]
    ws = [pz[0:c] + _mm(l_ak, v_bd) for pz, l_ak, v_bd in zip(pzs, l_aks, v_bds)]
    us = [_mm(t, _expand_bd(w, bd_mask)) for t, w in zip(tinvs, ws)]
    ys = [pz[c:2 * c] + _mm(jnp.concatenate([m_rb, m_rk], axis=1),
                            jnp.concatenate([_expand_bd(u, bd_mask), v_bd], axis=0))
          for pz, m_rb, m_rk, u, v_bd in zip(pzs, m_rbs, m_rks, us, v_bds)]
    return ys, us


def _wkv_chunk(groups, masks):
    c = groups[0][0].shape[0]
    cls = [_chunk_cumsum(g[1]) for g in groups]
    xs = [jnp.concatenate([g[4] * jnp.exp(cl - g[1]), g[0] * jnp.exp(cl)], axis=0)
          for g, cl in zip(groups, cls)]
    pzs = [_mm_nt(x, g[6]) for x, g in zip(xs, groups)]
    e_negs = [jnp.exp(-cl) for cl in cls]
    ys, us = _wkv_intra(xs, [g[5] * e for g, e in zip(groups, e_negs)],
                        [g[2] * e for g, e in zip(groups, e_negs)], [g[3] for g in groups], pzs, masks, c)
    dss = []
    for g, cl, u in zip(groups, cls, us):
        e_rem = jnp.exp(cl[c - 1:c, :] - cl)
        uv_t = jnp.concatenate([u, g[3]], axis=0).T
        bk = jnp.concatenate([g[5] * e_rem, g[2] * e_rem], axis=0)
        dss.append(_mm(uv_t, bk))
    return [(y, g[6] * jnp.exp(cl[c - 1:c, :]) + jnp.where(masks[0], ds, 0.0))
            for y, g, cl, ds in zip(ys, groups, cls, dss)]


def _wkv_masks(c, seq):
    n = HEADS_PER_GROUP * c
    rr = lax.broadcasted_iota(jnp.int32, (n, n), 0)
    cc = lax.broadcasted_iota(jnp.int32, (n, n), 1)
    bd_mask = (rr // c) == (cc // c)
    t = lax.broadcasted_iota(jnp.int32, (c, n), 0)
    s = lax.broadcasted_iota(jnp.int32, (c, n), 1) % c
    same = (t // seq) == (s // seq)
    strict_mask = same & (t > s)
    incl_mask = same & (t >= s)
    eye_c = jnp.where(t == s, 1.0, 0.0).astype(F32)
    lane = lax.broadcasted_iota(jnp.int32, (c, MXU_DIM_V7X), 1)
    head_masks = [(lane // HEAD_DIM) == h for h in range(HEADS_PER_GROUP)]
    return bd_mask, strict_mask, incl_mask, eye_c, head_masks


def _wkv_prompt_kernel(r_ref, lw_ref, k2_ref, v_ref, as_ref, bs_ref, y_ref, sout_ref, s_ref):
    ci = pl.program_id(0)

    @pl.when(ci == 0)
    def _():
        s_ref[...] = jnp.zeros_like(s_ref)

    masks = _wkv_masks(CHUNK, CHUNK)
    chains = [(bi, gi) for bi in range(r_ref.shape[0]) for gi in range(N_GROUPS)]
    lanes = lambda gi: slice(MXU_DIM_V7X * gi, MXU_DIM_V7X * (gi + 1))
    groups = [tuple(ref[bi, :, lanes(gi)] for ref in (r_ref, lw_ref, k2_ref, v_ref, as_ref, bs_ref))
              + (s_ref[bi, gi],) for bi, gi in chains]
    for (bi, gi), (y, s_new) in zip(chains, _wkv_chunk(groups, masks)):
        y_ref[bi, :, lanes(gi)] = y
        s_ref[bi, gi] = s_new

    @pl.when(ci == pl.num_programs(0) - 1)
    def _():
        sout_ref[...] = s_ref[...]


def _wkv_prompt(r, lw, k2, v, a_s, b_s):
    b, t, _ = r.shape
    spec = pl.BlockSpec((b, CHUNK, D_RWKV), lambda ci: (0, ci, 0))
    state = (b, N_GROUPS, MXU_DIM_V7X, MXU_DIM_V7X)
    return pl.pallas_call(
        _wkv_prompt_kernel,
        grid=(t // CHUNK,),
        in_specs=[spec] * 6,
        out_specs=[spec, _const_spec(state)],
        out_shape=[jax.ShapeDtypeStruct((b, t, D_RWKV), F32), jax.ShapeDtypeStruct(state, F32)],
        scratch_shapes=[pltpu.VMEM(state, F32)],
        compiler_params=_params("arbitrary"),
        name="wkv_prompt",
    )(r, lw, k2, v, a_s, b_s)


SLAB = 8


def _slab_pair(ref, r0, c):
    return jnp.concatenate([ref[pl.ds(r0, SLAB), :], ref[pl.ds(c + r0, SLAB), :]], axis=0)


def _wkv_sample_kernel(seq, r_ref, lw_ref, k2_ref, v_ref, as_ref, bs_ref, s0_ref, y_ref, sout_ref,
                       x_ref, btk_ref, bk_ref, p_ref, uv_ref, gam_ref):
    c = r_ref.shape[0]
    per_slab = SLAB // seq

    lw = lw_ref[...]
    tpos = lax.broadcasted_iota(jnp.int32, lw.shape, 0) % seq
    cl = lw
    s = 1
    while s < seq:
        cl = cl + jnp.where(tpos >= s, pltpu.roll(cl, s, 0), 0.0)
        s *= 2
    tot = jnp.where(tpos == seq - 1, cl, 0.0)
    s = 1
    while s < seq:
        tot = tot + pltpu.roll(tot, c - s, 0)
        s *= 2
    e_neg = jnp.exp(-cl)
    e_rem = jnp.exp(tot - cl)
    b_s = bs_ref[...]
    k2 = k2_ref[...]
    x_ref[0:c, :] = as_ref[...] * jnp.exp(cl - lw)
    x_ref[c:2 * c, :] = r_ref[...] * jnp.exp(cl)
    btk_ref[0:c, :] = b_s * e_neg
    btk_ref[c:2 * c, :] = k2 * e_neg
    bk_ref[0:c, :] = b_s * e_rem
    bk_ref[c:2 * c, :] = k2 * e_rem
    gam_ref[...] = jnp.exp(tot)

    first_seq = (lax.broadcasted_iota(jnp.int32, (2 * SLAB, HEAD_DIM), 0) % SLAB) < seq

    def state_in(p, carry):
        r0 = pl.multiple_of(p * SLAB, SLAB)
        xp = _slab_pair(x_ref, r0, c)
        outs = []
        for h in range(N_HEADS):
            xs = xp[:, HEAD_DIM * h:HEAD_DIM * (h + 1)]
            p0 = _mm_nt(xs, s0_ref[per_slab * p, h])
            p1 = _mm_nt(xs, s0_ref[per_slab * p + 1, h])
            outs.append(jnp.where(first_seq, p0, p1))
        pp = jnp.concatenate(outs, axis=1)
        p_ref[pl.ds(r0, SLAB), :] = pp[0:SLAB]
        p_ref[pl.ds(c + r0, SLAB), :] = pp[SLAB:2 * SLAB]
        return carry

    lax.fori_loop(0, c // SLAB, state_in, 0)

    masks = _wkv_masks(c, seq)
    sls = [slice(MXU_DIM_V7X * gi, MXU_DIM_V7X * (gi + 1)) for gi in range(N_GROUPS)]
    ys, us = _wkv_intra([x_ref[:, sl] for sl in sls], [btk_ref[0:c, sl] for sl in sls],
                        [btk_ref[c:2 * c, sl] for sl in sls], [v_ref[:, sl] for sl in sls],
                        [p_ref[:, sl] for sl in sls], masks, seq)
    for sl, y, u in zip(sls, ys, us):
        y_ref[:, sl] = y
        uv_ref[0:c, sl] = u
    uv_ref[c:2 * c, :] = v_ref[...]

    first_rows = (lax.broadcasted_iota(jnp.int32, (2 * SLAB, D_RWKV), 0) % SLAB) < seq

    def state_out(p, carry):
        r0 = pl.multiple_of(p * SLAB, SLAB)
        uvp = _slab_pair(uv_ref, r0, c)
        bkp = _slab_pair(bk_ref, r0, c)
        gam8 = gam_ref[pl.ds(r0, SLAB), :]
        for bb in range(per_slab):
            uvm = jnp.where(first_rows if bb == 0 else jnp.logical_not(first_rows), uvp, 0.0)
            gam = gam8[seq * bb:seq * bb + 1, :]
            for j in range(N_HEADS // 2):
                ls = slice(2 * HEAD_DIM * j, 2 * HEAD_DIM * (j + 1))
                ds2 = _mm(uvm[:, ls].T, bkp[:, ls])
                for hh in range(2):
                    h = 2 * j + hh
                    hs = slice(HEAD_DIM * hh, HEAD_DIM * (hh + 1))
                    sout_ref[per_slab * p + bb, h] = (
                        s0_ref[per_slab * p + bb, h] * gam[:, HEAD_DIM * h:HEAD_DIM * (h + 1)] + ds2[hs, hs])
        return carry

    lax.fori_loop(0, c // SLAB, state_out, 0)


def _wkv_sample(r, lw, k2, v, a_s, b_s, s0, seq):
    n = r.shape[0]
    assert SLAB % seq == 0 and SLAB // seq == 2 and n % CHUNK == 0
    nb = CHUNK // seq
    vec = pl.BlockSpec((CHUNK, D_RWKV), lambda i: (i, 0))
    st = pl.BlockSpec((nb, N_HEADS, HEAD_DIM, HEAD_DIM), lambda i: (i, 0, 0, 0))
    return pl.pallas_call(
        functools.partial(_wkv_sample_kernel, seq),
        grid=(n // CHUNK,),
        in_specs=[vec] * 6 + [st],
        out_specs=[vec, st],
        out_shape=[jax.ShapeDtypeStruct((n, D_RWKV), F32), jax.ShapeDtypeStruct(s0.shape, F32)],
        scratch_shapes=[pltpu.VMEM((2 * CHUNK, D_RWKV), F32)] * 5 + [pltpu.VMEM((CHUNK, D_RWKV), F32)],
        compiler_params=_params("parallel"),
        name="wkv_sample",
    )(r, lw, k2, v, a_s, b_s, s0)


def _pool_prompt_kernel(zp_ref, d_ref, carry_ref):
    i = pl.program_id(1)
    tm = zp_ref.shape[0]

    @pl.when(i == 0)
    def _():
        carry_ref[...] = jnp.zeros_like(carry_ref)

    zp = zp_ref[...]
    buf = jnp.concatenate([carry_ref[...], zp], axis=0)
    carry_ref[...] = zp[tm - 16:, :]
    pos = (i * tm + lax.broadcasted_iota(jnp.int32, (tm, POOL_GROUP), 0) + 1).astype(F32)
    for gi, win in enumerate(POOL_WINDOWS):
        sl = slice(POOL_GROUP * gi, POOL_GROUP * (gi + 1))
        acc = buf[:, sl]
        s = 1
        while s < win:
            acc = acc + pltpu.roll(acc, s, 0)
            s *= 2
        cnt = jnp.minimum(float(win), pos)
        d_ref[:, sl] = acc[16:, :] / cnt - zp[:, sl]


def _pool_prompt(zp, b, t, tm):
    tiles = t // tm
    return pl.pallas_call(
        _pool_prompt_kernel,
        grid=(b, tiles),
        in_specs=[pl.BlockSpec((tm, D_POOL), lambda bi, i: (bi * tiles + i, 0))],
        out_specs=pl.BlockSpec((None, tm, D_POOL), lambda bi, i: (bi, i, 0)),
        out_shape=jax.ShapeDtypeStruct((b, t, D_POOL), F32),
        scratch_shapes=[pltpu.VMEM((16, D_POOL), F32)],
        compiler_params=_params("parallel", "arbitrary"),
        name="pool_prompt",
    )(zp)


def _pool_sample_kernel(buf_ref, d_ref):
    nt = d_ref.shape[1]
    for t in range(nt):
        for gi, win in enumerate(POOL_WINDOWS):
            sl = slice(POOL_GROUP * gi, POOL_GROUP * (gi + 1))
            acc = buf_ref[:, POOL_HIST + t, sl]
            for j in range(1, win):
                acc = acc + buf_ref[:, POOL_HIST + t - j, sl]
            d_ref[:, t, sl] = acc / float(min(win, PAST_LEN + 1)) - buf_ref[:, POOL_HIST + t, sl]


def _pool_sample(buf, seq, bt):
    b = buf.shape[0]
    return pl.pallas_call(
        _pool_sample_kernel,
        grid=(b // bt,),
        in_specs=[pl.BlockSpec((bt, buf.shape[1], D_POOL), lambda i: (i, 0, 0))],
        out_specs=pl.BlockSpec((bt, seq, D_POOL), lambda i: (i, 0, 0)),
        out_shape=jax.ShapeDtypeStruct((b, seq, D_POOL), F32),
        compiler_params=_params("parallel"),
        name="pool_sample",
    )(buf)


def _merge_kernel(y_ref, r_ref, k2_ref, v_ref, g_ref, d_ref, zga_ref, zgb_ref,
                  rk_ref, lw_ref, lb_ref, ones_ref, wa_ref, pw_ref, ps_ref, wb_ref, m_ref):
    ones_bd = ones_ref[...]
    y = y_ref[...]
    mu = _head_sum(y, ones_bd) * (1.0 / HEAD_DIM)
    yc = y - mu
    var = _head_sum(yc * yc, ones_bd) * (1.0 / HEAD_DIM)
    yn = yc * lax.rsqrt(var + GN_EPS) * lw_ref[...] + lb_ref[...]
    v = v_ref[...]
    bonus = _head_sum(r_ref[...] * k2_ref[...] * rk_ref[...], ones_bd) * v
    ya = (yn + bonus) * g_ref[...]
    d = d_ref[...]
    yb = jnp.concatenate(
        [_mm(d[:, POOL_GROUP * gi:POOL_GROUP * (gi + 1)], pw_ref[gi]) for gi in range(len(POOL_WINDOWS))],
        axis=1) * ps_ref[...]
    m_ref[...] = (jax.nn.sigmoid(zga_ref[...]) * _mm(ya, wa_ref[...])
                  + jax.nn.sigmoid(zgb_ref[...]) * _mm(yb, wb_ref[...]))


def _merge(y, r, k2, v, g, d, zga, zgb, row0, wts, tm):
    n = y.shape[0]
    assert row0 % tm == 0
    first = row0 // tm
    half = pl.BlockSpec((tm, D_RWKV), lambda i: (i, 0))
    full = pl.BlockSpec((tm, D_MODEL), lambda i: (i, 0))
    gate = pl.BlockSpec((tm, D_MODEL), lambda i: (first + i, 0))
    return pl.pallas_call(
        _merge_kernel,
        grid=(n // tm,),
        in_specs=[half] * 6 + [gate] * 2 + [_const_spec(w.shape) for w in wts],
        out_specs=full,
        out_shape=jax.ShapeDtypeStruct((n, D_MODEL), F32),
        compiler_params=_params("parallel"),
        name="merge",
    )(y, r, k2, v, g, d, zga, zgb, *wts)


def _mix_out_kernel(m_ref, x_ref, w_ref, g_ref, o_ref):
    mo = _mm(m_ref[...], w_ref[...])
    o_ref[...] = x_ref[...] + _rmsnorm(mo, g_ref[...])


def _mix_out(m, x2d, w_out, g, tm):
    n = m.shape[0]
    full = pl.BlockSpec((tm, D_MODEL), lambda i: (i, 0))
    return pl.pallas_call(
        _mix_out_kernel,
        grid=(n // tm,),
        in_specs=[full, full, _const_spec(w_out.shape), _const_spec(g.shape)],
        out_specs=full,
        out_shape=jax.ShapeDtypeStruct((n, D_MODEL), F32),
        compiler_params=_params("parallel"),
        name="mix_out",
    )(m, x2d, w_out, g)


def _gelu_tanh(x):
    return 0.5 * x * (1.0 + jnp.tanh(0.7978845608028654 * (x + 0.044715 * x * x * x)))


def _ffn_body(x_ref, gn_ref, wg_ref, wu_ref, cw_ref, cb_ref, wo_ref, gp_ref,
              o_ref, tail_ref, h_ref, acc_ref, prev1, prev2):
    c = pl.program_id(1)

    @pl.when(c == 0)
    def _():
        h_ref[...] = _rmsnorm(x_ref[...], gn_ref[...]).astype(BF16)
        acc_ref[...] = jnp.zeros_like(acc_ref)

    h = h_ref[...]
    gate = jnp.dot(h, wg_ref[...], preferred_element_type=F32)
    up = jnp.dot(h, wu_ref[...], preferred_element_type=F32)
    cw = cw_ref[...]
    cv = cb_ref[...] + cw[0:1, :] * prev2(gate) + cw[1:2, :] * prev1(gate) + cw[2:3, :] * gate
    tail_ref[...] = gate[gate.shape[0] - tail_ref.shape[0]:, :]
    acc_ref[...] += _mm(_gelu_tanh(cv) * up, wo_ref[...])

    @pl.when(c == pl.num_programs(1) - 1)
    def _():
        o_ref[...] = x_ref[...] + _rmsnorm(acc_ref[...], gp_ref[...])


def _ffn_prompt_kernel(tiles_per_seq, x_ref, gn_ref, wg_ref, wu_ref, cw_ref, cb_ref, wo_ref, gp_ref,
                       o_ref, tail_ref, h_ref, acc_ref, carry_ref):
    i = pl.program_id(0)
    c = pl.program_id(1)

    @pl.when(i % tiles_per_seq == 0)
    def _():
        carry_ref[c] = jnp.zeros(carry_ref.shape[1:], F32)

    hist = carry_ref[c]

    def shifted(gate, s):
        row = lax.broadcasted_iota(jnp.int32, gate.shape, 0)
        rolled = pltpu.roll(gate, s, 0)
        out = rolled
        for j in range(s):
            out = jnp.where(row == j, hist[8 - s + j:9 - s + j, :], out)
        return out

    _ffn_body(x_ref, gn_ref, wg_ref, wu_ref, cw_ref, cb_ref, wo_ref, gp_ref, o_ref, tail_ref,
              h_ref, acc_ref, lambda g: shifted(g, 1), lambda g: shifted(g, 2))
    carry_ref[c] = tail_ref[...]


def _ffn_sample_kernel(seq, x_ref, st_ref, gn_ref, wg_ref, wu_ref, cw_ref, cb_ref, wo_ref, gp_ref,
                       o_ref, tail_ref, h_ref, acc_ref):
    st = st_ref[...]
    n = st.shape[0]
    tpos = lax.broadcasted_iota(jnp.int32, st.shape, 0) % seq
    prev1 = lambda g: jnp.where(tpos == 0, pltpu.roll(st, n - 1, 0), pltpu.roll(g, 1, 0))
    prev2 = lambda g: jnp.where(tpos < CONV_W - 1, st, pltpu.roll(g, 2, 0))
    _ffn_body(x_ref, gn_ref, wg_ref, wu_ref, cw_ref, cb_ref, wo_ref, gp_ref, o_ref, tail_ref,
              h_ref, acc_ref, prev1, prev2)


def _ffn_prompt(x2d, seq_len, wts, tm, fk):
    n = x2d.shape[0]
    nc = D_FF // fk
    gn, w_in, cw, cb, w_out, gp = wts
    full = pl.BlockSpec((tm, D_MODEL), lambda i, c: (i, 0))
    return pl.pallas_call(
        functools.partial(_ffn_prompt_kernel, seq_len // tm),
        grid=(n // tm, nc),
        in_specs=[full, _const_spec(gn.shape),
                  pl.BlockSpec((D_MODEL, fk), lambda i, c: (0, c)),
                  pl.BlockSpec((D_MODEL, fk), lambda i, c: (0, c + nc)),
                  pl.BlockSpec((CONV_W, fk), lambda i, c: (0, c)),
                  pl.BlockSpec((1, fk), lambda i, c: (0, c)),
                  pl.BlockSpec((fk, D_MODEL), lambda i, c: (c, 0)),
                  _const_spec(gp.shape)],
        out_specs=[full, pl.BlockSpec((None, 8, fk), lambda i, c: (i, 0, c))],
        out_shape=[jax.ShapeDtypeStruct((n, D_MODEL), F32),
                   jax.ShapeDtypeStruct((n // tm, 8, D_FF), F32)],
        scratch_shapes=[pltpu.VMEM((tm, D_MODEL), BF16), pltpu.VMEM((tm, D_MODEL), F32),
                        pltpu.VMEM((nc, 8, fk), F32)],
        compiler_params=_params("arbitrary", "arbitrary"),
        name="ffn_prompt",
    )(x2d, gn, w_in, w_in, cw, cb, w_out, gp)


def _ffn_sample(x2d, st_rows, seq, wts, fk):
    n = x2d.shape[0]
    assert seq >= CONV_W - 1
    nc = D_FF // fk
    gn, w_in, cw, cb, w_out, gp = wts
    full = pl.BlockSpec((n, D_MODEL), lambda i, c: (0, 0))
    cols = pl.BlockSpec((n, fk), lambda i, c: (0, c))
    return pl.pallas_call(
        functools.partial(_ffn_sample_kernel, seq),
        grid=(1, nc),
        in_specs=[full, cols, _const_spec(gn.shape),
                  pl.BlockSpec((D_MODEL, fk), lambda i, c: (0, c)),
                  pl.BlockSpec((D_MODEL, fk), lambda i, c: (0, c + nc)),
                  pl.BlockSpec((CONV_W, fk), lambda i, c: (0, c)),
                  pl.BlockSpec((1, fk), lambda i, c: (0, c)),
                  pl.BlockSpec((fk, D_MODEL), lambda i, c: (c, 0)),
                  _const_spec(gp.shape)],
        out_specs=[full, cols],
        out_shape=[jax.ShapeDtypeStruct((n, D_MODEL), F32), jax.ShapeDtypeStruct((n, D_FF), F32)],
        scratch_shapes=[pltpu.VMEM((n, D_MODEL), BF16), pltpu.VMEM((n, D_MODEL), F32)],
        compiler_params=_params("arbitrary", "arbitrary"),
        name="ffn_sample",
    )(x2d, st_rows, gn, w_in, w_in, cw, cb, w_out, gp)


def _row(v):
    return v.reshape(1, -1).astype(F32)


def _ones_bd():
    i = jnp.arange(MXU_DIM_V7X) // HEAD_DIM
    return (i[:, None] == i[None, :]).astype(BF16)


def _layer_weights(l, norm_pre_mix, w_in, mu_shift, w0, w2, a0, a2, g2, k_k, k_a, r_k, lnx_w, lnx_b,
                   w_branch_a, pool_w, pool_scale, w_branch_b, w_out, norm_post_mix,
                   norm_pre_ffn, w_ffn_in, conv_w, conv_b, w_ffn_out, norm_post_ffn):
    w_t = jnp.swapaxes(w_in[l], 0, 1)
    mu = jnp.pad(mu_shift[l], (0, D_SHIFT_PAD - D_SHIFT)).reshape(1, -1)
    w_lora = jnp.zeros((D_LORA_PAD, 3 * D_RWKV), F32)
    w_lora = w_lora.at[0:RANK_W, 0:D_RWKV].set(w2[l])
    w_lora = w_lora.at[RANK_W:RANK_W + RANK_A, D_RWKV:2 * D_RWKV].set(a2[l])
    w_lora = w_lora.at[RANK_W + RANK_A:D_LORA, 2 * D_RWKV:].set(g2[l])
    ones_bd = _ones_bd()
    return dict(
        in_proj=(_row(norm_pre_mix[l]), w_t),
        prep=(mu, w_lora.astype(BF16), _row(w0[l]), _row(a0[l]), _row(k_k[l]), _row(k_a[l]), ones_bd),
        merge=(_row(r_k[l]), _row(lnx_w[l]), _row(lnx_b[l]), ones_bd, w_branch_a[l].astype(BF16),
               pool_w[l].astype(BF16), _row(pool_scale[l]), w_branch_b[l].astype(BF16)),
        mix_out=(w_out[l].astype(BF16), _row(norm_post_mix[l])),
        ffn=(_row(norm_pre_ffn[l]), w_ffn_in[l].astype(BF16), conv_w[l].astype(F32), _row(conv_b[l]),
             w_ffn_out[l].astype(BF16), _row(norm_post_ffn[l])),
    )


def _largest_tile(n, cap, mult=16):
    best = None
    for d in range(mult, min(n, cap) + 1, mult):
        if n % d == 0:
            best = d
    assert best is not None, (n, cap)
    return best


def _project(xp2d, xs2d, wts):
    g, w_t = wts["in_proj"]
    tm_norm = _largest_tile(math.gcd(xp2d.shape[0], xs2d.shape[0]), 512)
    h = _pre_norm(xp2d, xs2d, g, tm_norm)
    tm = _largest_tile(h.shape[0], 1088)
    zrkv = _in_proj(h, w_t, 0, 3 * D_RWKV, D_RWKV, tm)
    zl = _in_proj(h, w_t, 3 * D_RWKV, D_LORA_PAD, D_LORA_PAD, tm)
    zp = _in_proj(h, w_t, D_SHIFT, D_POOL, D_POOL, tm)
    zga = _in_proj(h, w_t, D_SHIFT + D_POOL, D_MODEL, D_MODEL // 2, tm)
    zgb = _in_proj(h, w_t, D_SHIFT + D_POOL + D_MODEL, D_MODEL, D_MODEL // 2, tm)
    return zrkv, zl, zp, zga, zgb


def _last_shift_row(zrkv, zl, rows):
    return jnp.concatenate([zrkv[rows], zl[rows, :D_LORA]], axis=-1)


def _prompt_layer(x, z, wts):
    b, t, _ = x.shape
    n = b * t
    x2d = x.reshape(n, D_MODEL)
    zrkv, zl, zp, zga, zgb = z
    r, lw, k2, v, a_s, b_s, g = _prep_prompt(zrkv, zl, b, t, wts["prep"], tm=256)
    y, s_bd = _wkv_prompt(r, lw, k2, v, a_s, b_s)
    d = _pool_prompt(zp, b, t, tm=512)
    flat = lambda a: a.reshape(n, a.shape[-1])
    m = _merge(flat(y), flat(r), flat(k2), flat(v), flat(g), flat(d), zga, zgb, 0, wts["merge"], tm=256)
    x1 = _mix_out(m, x2d, *wts["mix_out"], tm=512)
    out, tail = _ffn_prompt(x1, t, wts["ffn"], tm=512, fk=512)
    shift = _last_shift_row(zrkv, zl, jnp.arange(b) * t + t - 1)[:, None, :]
    zp3 = zp[:n].reshape(b, t, D_POOL)
    s5 = s_bd.reshape(b, N_GROUPS, HEADS_PER_GROUP, HEAD_DIM, HEADS_PER_GROUP, HEAD_DIM)
    idx = jnp.arange(HEADS_PER_GROUP)
    wkv = s5[:, :, idx, :, idx, :]
    wkv = jnp.moveaxis(wkv, 0, 2).reshape(b, N_HEADS, HEAD_DIM, HEAD_DIM)
    pool = zp3[:, t - POOL_HIST:, :]
    tiles = t // 512
    conv = tail.reshape(b, tiles, 8, D_FF)[:, tiles - 1, 8 - (CONV_W - 1):, :]
    return out.reshape(b, t, D_MODEL), shift, wkv, pool, conv


def _sample_layer(x, z, row0, st_shift, st_wkv, st_pool, st_conv, wts):
    b, t, _ = x.shape
    n = b * t
    x2d = x.reshape(n, D_MODEL)
    zrkv, zl, zp, zga, zgb = z
    st_rows = jnp.pad(st_shift, ((0, 0), (0, t - 1), (0, D_SHIFT_PAD - D_SHIFT))).reshape(n, D_SHIFT_PAD)
    tm = _largest_tile(math.gcd(row0, n), 128, mult=8)
    r, lw, k2, v, a_s, b_s, g = _prep_sample(zrkv, zl, row0, n, st_rows, t, wts["prep"], tm=tm)
    y, wkv = _wkv_sample(r, lw, k2, v, a_s, b_s, st_wkv, t)
    buf = jnp.concatenate([st_pool, zp[row0:].reshape(b, t, D_POOL)], axis=1)
    d = _pool_sample(buf, t, bt=min(32, b)).reshape(n, D_POOL)
    m = _merge(y, r, k2, v, g, d, zga, zgb, row0, wts["merge"], tm=_largest_tile(math.gcd(row0, n), 256, mult=8))
    x1 = _mix_out(m, x2d, *wts["mix_out"], tm=n)
    conv_rows = jnp.pad(st_conv, ((0, 0), (0, t - (CONV_W - 1)), (0, 0))).reshape(n, D_FF)
    out, gate = _ffn_sample(x1, conv_rows, t, wts["ffn"], fk=512)
    shift = _last_shift_row(zrkv, zl, row0 + jnp.arange(b) * t + t - 1)[:, None, :]
    pool = buf[:, -POOL_HIST:]
    conv = jnp.concatenate([st_conv, gate.reshape(b, t, D_FF)], axis=1)[:, -(CONV_W - 1):]
    return out.reshape(b, t, D_MODEL), shift, wkv, pool, conv


def kernel(x_prompt, x_sample, state_shift, state_wkv, state_pool, state_conv, norm_pre_mix, w_in, mu_shift, w0, w2, a0, a2, g2, k_k, k_a, r_k, lnx_w, lnx_b, w_branch_a, pool_w, pool_scale, w_branch_b, w_out, norm_post_mix, norm_pre_ffn, w_ffn_in, conv_w, conv_b, w_ffn_out, norm_post_ffn):
    weights = (norm_pre_mix, w_in, mu_shift, w0, w2, a0, a2, g2, k_k, k_a, r_k, lnx_w, lnx_b,
               w_branch_a, pool_w, pool_scale, w_branch_b, w_out, norm_post_mix,
               norm_pre_ffn, w_ffn_in, conv_w, conv_b, w_ffn_out, norm_post_ffn)
    depth = w_in.shape[0]
    yp, ys = x_prompt, x_sample
    p_states, s_states = [], []
    for l in range(depth):
        wts = _layer_weights(l, *weights)
        n_prompt = yp.shape[0] * yp.shape[1]
        z = _project(yp.reshape(n_prompt, D_MODEL), ys.reshape(-1, D_MODEL), wts)
        yp, *ps = _prompt_layer(yp, z, wts)
        ys, *ss = _sample_layer(ys, z, n_prompt, state_shift[l], state_wkv[l], state_pool[l], state_conv[l], wts)
        p_states.append(ps)
        s_states.append(ss)
    stack = lambda states, i: jnp.stack([s[i] for s in states])
    return (yp, ys,
            stack(p_states, 0), stack(p_states, 1), stack(p_states, 2), stack(p_states, 3),
            stack(s_states, 0), stack(s_states, 1), stack(s_states, 2), stack(s_states, 3))
```

```python
import functools
import math

import jax
import jax.numpy as jnp
from jax import lax
from jax.experimental import pallas as pl
from jax.experimental.pallas import tpu as pltpu

F32 = jnp.float32
BF16 = jnp.bfloat16

D_MODEL = 2048
HEAD_DIM = 64
D_RWKV = 1024
N_HEADS = 16
RANK_W, RANK_A, RANK_G = 64, 64, 160
D_LORA = RANK_W + RANK_A + RANK_G
D_LORA_PAD = 384
D_SHIFT = 3 * D_RWKV + D_LORA
D_SHIFT_PAD = 3 * D_RWKV + D_LORA_PAD
D_POOL = 1024
POOL_WINDOWS = (2, 4, 8, 16)
POOL_GROUP = 256
POOL_HIST = 15
D_FF = 5632
CONV_W = 3
NORM_EPS = 1e-6
GN_EPS = 64e-5
PAST_LEN = 16384

LANES_V7X = 128
MXU_DIM_V7X = 256
HEADS_PER_GROUP = MXU_DIM_V7X // HEAD_DIM
N_GROUPS = N_HEADS // HEADS_PER_GROUP
CHUNK = 64
VMEM_LIMIT_V7X = 56 * 1024 * 1024


def _params(*sem):
    return pltpu.CompilerParams(dimension_semantics=sem, vmem_limit_bytes=VMEM_LIMIT_V7X)


def _mm(a, b):
    return jnp.dot(a.astype(BF16), b.astype(BF16), preferred_element_type=F32)


def _mm_nt(a, b):
    return lax.dot_general(a.astype(BF16), b.astype(BF16), (((1,), (1,)), ((), ())),
                           preferred_element_type=F32)


def _split_hi_lo(x):
    hi = x.astype(BF16)
    lo = (x - hi.astype(F32)).astype(BF16)
    return hi, lo


def _head_sum(x, ones_bd):
    hi, lo = _split_hi_lo(x)
    outs = []
    for gi in range(x.shape[1] // MXU_DIM_V7X):
        sl = slice(MXU_DIM_V7X * gi, MXU_DIM_V7X * (gi + 1))
        outs.append(jnp.dot(hi[:, sl], ones_bd, preferred_element_type=F32)
                    + jnp.dot(lo[:, sl], ones_bd, preferred_element_type=F32))
    return jnp.concatenate(outs, axis=1)


def _rmsnorm(x, g):
    return x * lax.rsqrt(jnp.mean(x * x, axis=-1, keepdims=True) + NORM_EPS) * g


def _pre_norm_kernel(prompt_tiles, xp_ref, xs_ref, g_ref, h_ref):
    i = pl.program_id(0)

    @pl.when(i < prompt_tiles)
    def _():
        h_ref[...] = _rmsnorm(xp_ref[...], g_ref[...]).astype(BF16)

    @pl.when(i >= prompt_tiles)
    def _():
        h_ref[...] = _rmsnorm(xs_ref[...], g_ref[...]).astype(BF16)


def _pre_norm(xp, xs, g, tm):
    pt, st = xp.shape[0] // tm, xs.shape[0] // tm
    return pl.pallas_call(
        functools.partial(_pre_norm_kernel, pt),
        grid=(pt + st,),
        in_specs=[pl.BlockSpec((tm, D_MODEL), lambda i: (jnp.minimum(i, pt - 1), 0)),
                  pl.BlockSpec((tm, D_MODEL), lambda i: (jnp.maximum(i - pt, 0), 0)),
                  _const_spec(g.shape)],
        out_specs=pl.BlockSpec((tm, D_MODEL), lambda i: (i, 0)),
        out_shape=jax.ShapeDtypeStruct((xp.shape[0] + xs.shape[0], D_MODEL), BF16),
        compiler_params=_params("arbitrary"),
        name="pre_norm",
    )(xp, xs, g)


def _in_proj_kernel(h_ref, w_ref, o_ref, wb_ref):
    @pl.when(pl.program_id(1) == 0)
    def _():
        wb_ref[...] = w_ref[...].astype(BF16)

    o_ref[...] = lax.dot_general(h_ref[...], wb_ref[...], (((1,), (1,)), ((), ())),
                                 preferred_element_type=F32)


def _in_proj(h, w_t, off, width, tn, tm):
    n = h.shape[0]
    assert width % tn == 0 and n % tm == 0
    return pl.pallas_call(
        _in_proj_kernel,
        grid=(width // tn, n // tm),
        in_specs=[pl.BlockSpec((tm, D_MODEL), lambda j, i: (i, 0)),
                  pl.BlockSpec((pl.Element(tn), pl.Element(D_MODEL)),
                               lambda j, i: (SLAB * (off // SLAB + j * (tn // SLAB)), 0))],
        out_specs=pl.BlockSpec((tm, tn), lambda j, i: (i, j)),
        out_shape=jax.ShapeDtypeStruct((n, width), F32),
        scratch_shapes=[pltpu.VMEM((tn, D_MODEL), BF16)],
        compiler_params=_params("arbitrary", "arbitrary"),
        name="in_proj",
    )(h, w_t)


def _prep_math(zs, prev, mu, w_lora, w0, a0, k_k, k_a, ones_bd):
    f = zs + (prev - zs) * mu
    r = f[:, 0:D_RWKV]
    k = f[:, D_RWKV:2 * D_RWKV]
    v = f[:, 2 * D_RWKV:3 * D_RWKV]
    low = f[:, 3 * D_RWKV:3 * D_RWKV + D_LORA_PAD]
    lane = lax.broadcasted_iota(jnp.int32, low.shape, 1)
    act = jnp.where(lane < RANK_W, jnp.tanh(low),
                    jnp.where(lane < RANK_W + RANK_A, low,
                              jnp.where(lane < D_LORA, jax.nn.sigmoid(low), 0.0)))
    lora = _mm(act, w_lora)
    u = w0 + lora[:, 0:D_RWKV]
    w_log = -(jnp.maximum(-u, 0.0) + jnp.log1p(jnp.exp(-jnp.abs(u)))) - 0.5
    lw = -jnp.exp(w_log)
    a = jax.nn.sigmoid(a0 + lora[:, D_RWKV:2 * D_RWKV])
    g = lora[:, 2 * D_RWKV:3 * D_RWKV]
    kk = k * k_k
    nrm = jnp.sqrt(_head_sum(kk * kk, ones_bd))
    kk = kk / jnp.maximum(nrm, 1e-12)
    k2 = k * (1.0 + (a - 1.0) * k_a)
    return r, lw, k2, v, -kk, kk * a, g


def _prep_prompt_kernel(zrkv_ref, zl_ref, mu_ref, wl_ref, w0_ref, a0_ref, kk_ref, ka_ref, ones_ref,
                        r_ref, lw_ref, k2_ref, v_ref, as_ref, bs_ref, g_ref, carry_ref):
    i = pl.program_id(1)

    @pl.when(i == 0)
    def _():
        carry_ref[...] = jnp.zeros_like(carry_ref)

    zs = jnp.concatenate([zrkv_ref[...], zl_ref[...]], axis=1)
    rolled = pltpu.roll(zs, 1, 0)
    row = lax.broadcasted_iota(jnp.int32, zs.shape, 0)
    prev = jnp.where(row == 0, carry_ref[0:1, :], rolled)
    carry_ref[0:1, :] = zs[zs.shape[0] - 1:, :]
    outs = _prep_math(zs, prev, mu_ref[...], wl_ref[...], w0_ref[...], a0_ref[...],
                      kk_ref[...], ka_ref[...], ones_ref[...])
    for o_ref, o in zip((r_ref, lw_ref, k2_ref, v_ref, as_ref, bs_ref, g_ref), outs):
        o_ref[...] = o


def _prep_sample_kernel(seq, zrkv_ref, zl_ref, st_ref, mu_ref, wl_ref, w0_ref, a0_ref, kk_ref, ka_ref,
                        ones_ref, r_ref, lw_ref, k2_ref, v_ref, as_ref, bs_ref, g_ref):
    zs = jnp.concatenate([zrkv_ref[...], zl_ref[...]], axis=1)
    row = lax.broadcasted_iota(jnp.int32, zs.shape, 0)
    prev = jnp.where(row % seq == 0, st_ref[...], pltpu.roll(zs, 1, 0))
    outs = _prep_math(zs, prev, mu_ref[...], wl_ref[...], w0_ref[...], a0_ref[...],
                      kk_ref[...], ka_ref[...], ones_ref[...])
    for o_ref, o in zip((r_ref, lw_ref, k2_ref, v_ref, as_ref, bs_ref, g_ref), outs):
        o_ref[...] = o


def _const_spec(shape):
    nd = len(shape)
    return pl.BlockSpec(shape, lambda *_: (0,) * nd)


def _prep_prompt(zrkv, zl, b, t, wts, tm):
    tiles = t // tm
    outs = [jax.ShapeDtypeStruct((b, t, D_RWKV), F32)] * 7
    row_spec = pl.BlockSpec((None, tm, D_RWKV), lambda bi, i: (bi, i, 0))
    return pl.pallas_call(
        _prep_prompt_kernel,
        grid=(b, tiles),
        in_specs=[pl.BlockSpec((tm, 3 * D_RWKV), lambda bi, i: (bi * tiles + i, 0)),
                  pl.BlockSpec((tm, D_LORA_PAD), lambda bi, i: (bi * tiles + i, 0))]
        + [_const_spec(w.shape) for w in wts],
        out_specs=[row_spec] * 7,
        out_shape=outs,
        scratch_shapes=[pltpu.VMEM((8, D_SHIFT_PAD), F32)],
        compiler_params=_params("parallel", "arbitrary"),
        name="prep_prompt",
    )(zrkv, zl, *wts)


def _prep_sample(zrkv, zl, row0, n, st_rows, seq, wts, tm):
    assert tm % seq == 0 and n % tm == 0 and row0 % tm == 0
    first = row0 // tm
    outs = [jax.ShapeDtypeStruct((n, D_RWKV), F32)] * 7
    row_spec = pl.BlockSpec((tm, D_RWKV), lambda i: (i, 0))
    return pl.pallas_call(
        functools.partial(_prep_sample_kernel, seq),
        grid=(n // tm,),
        in_specs=[pl.BlockSpec((tm, 3 * D_RWKV), lambda i: (first + i, 0)),
                  pl.BlockSpec((tm, D_LORA_PAD), lambda i: (first + i, 0)),
                  pl.BlockSpec((tm, D_SHIFT_PAD), lambda i: (i, 0))]
        + [_const_spec(w.shape) for w in wts],
        out_specs=[row_spec] * 7,
        out_shape=outs,
        compiler_params=_params("parallel"),
        name="prep_sample",
    )(zrkv, zl, st_rows, *wts)


def _expand_bd(a, bd_mask):
    return jnp.where(bd_mask, jnp.concatenate([a] * HEADS_PER_GROUP, axis=0), 0.0)


def _chunk_cumsum(x):
    row = lax.broadcasted_iota(jnp.int32, x.shape, 0)
    s = 1
    while s < x.shape[0]:
        x = x + jnp.where(row >= s, pltpu.roll(x, s, 0), 0.0)
        s *= 2
    return x


def _wkv_intra(xs, bts, kts, vs, pzs, masks, order):
    bd_mask, strict_mask, incl_mask, eye_c, head_masks = masks
    c = vs[0].shape[0]
    ystacks = [jnp.concatenate([jnp.where(m, bt, 0.0) for m in head_masks]
                               + [jnp.where(m, kt, 0.0) for m in head_masks], axis=0)
               for bt, kt in zip(bts, kts)]
    grams = [_mm_nt(x, ys) for x, ys in zip(xs, ystacks)]
    l_abs = [jnp.where(strict_mask, g[0:c, 0:4 * c], 0.0) for g in grams]
    l_aks = [jnp.where(strict_mask, g[0:c, 4 * c:8 * c], 0.0) for g in grams]
    m_rbs = [jnp.where(incl_mask, g[c:2 * c, 0:4 * c], 0.0) for g in grams]
    m_rks = [jnp.where(incl_mask, g[c:2 * c, 4 * c:8 * c], 0.0) for g in grams]

    tinvs = [eye_c + l for l in l_abs]
    lps = [_mm(l, _expand_bd(l, bd_mask)) for l in l_abs]
    p = 2
    while True:
        rhss = [_expand_bd(lp, bd_mask) for lp in lps]
        if 2 * p >= order:
            tinvs = [t + _mm(t, rhs) for t, rhs in zip(tinvs, rhss)]
            break
        ress = [_mm(jnp.concatenate([lp, t], axis=0), rhs) for lp, t, rhs in zip(lps, tinvs, rhss)]
        lps = [res[0:c] for res in ress]
        tinvs = [t + res[c:2 * c] for t, res in zip(tinvs, ress)]
        p *= 2

    v_bds = [_expand_bd(v, bd_mask) for v in vs]
    ws = [pz[0:c] + _mm(l_ak, v_bd) for pz, l_ak, v_bd in zip(pzs, l_aks, v_bds)]
    us = [_mm(t, _expand_bd(w, bd_mask)) for t, w in zip(tinvs, ws)]
    ys = [pz[c:2 * c] + _mm(jnp.concatenate([m_rb, m_rk], axis=1),
                            jnp.concatenate([_expand_bd(u, bd_mask), v_bd], axis=0))
          for pz, m_rb, m_rk, u, v_bd in zip(pzs, m_rbs, m_rks, us, v_bds)]
    return ys, us


def _wkv_chunk(groups, masks):
    c = groups[0][0].shape[0]
    cls = [_chunk_cumsum(g[1]) for g in groups]
    xs = [jnp.concatenate([g[4] * jnp.exp(cl - g[1]), g[0] * jnp.exp(cl)], axis=0)
          for g, cl in zip(groups, cls)]
    pzs = [_mm_nt(x, g[6]) for x, g in zip(xs, groups)]
    e_negs = [jnp.exp(-cl) for cl in cls]
    ys, us = _wkv_intra(xs, [g[5] * e for g, e in zip(groups, e_negs)],
                        [g[2] * e for g, e in zip(groups, e_negs)], [g[3] for g in groups], pzs, masks, c)
    dss = []
    for g, cl, u in zip(groups, cls, us):
        e_rem = jnp.exp(cl[c - 1:c, :] - cl)
        uv_t = jnp.concatenate([u, g[3]], axis=0).T
        bk = jnp.concatenate([g[5] * e_rem, g[2] * e_rem], axis=0)
        dss.append(_mm(uv_t, bk))
    return [(y, g[6] * jnp.exp(cl[c - 1:c, :]) + jnp.where(masks[0], ds, 0.0))
            for y, g, cl, ds in zip(ys, groups, cls, dss)]


def _wkv_masks(c, seq):
    n = HEADS_PER_GROUP * c
    rr = lax.broadcasted_iota(jnp.int32, (n, n), 0)
    cc = lax.broadcasted_iota(jnp.int32, (n, n), 1)
    bd_mask = (rr // c) == (cc // c)
    t = lax.broadcasted_iota(jnp.int32, (c, n), 0)
    s = lax.broadcasted_iota(jnp.int32, (c, n), 1) % c
    same = (t // seq) == (s // seq)
    strict_mask = same & (t > s)
    incl_mask = same & (t >= s)
    eye_c = jnp.where(t == s, 1.0, 0.0).astype(F32)
    lane = lax.broadcasted_iota(jnp.int32, (c, MXU_DIM_V7X), 1)
    head_masks = [(lane // HEAD_DIM) == h for h in range(HEADS_PER_GROUP)]
    return bd_mask, strict_mask, incl_mask, eye_c, head_masks


def _wkv_prompt_kernel(r_ref, lw_ref, k2_ref, v_ref, as_ref, bs_ref, y_ref, sout_ref, s_ref):
    ci = pl.program_id(0)

    @pl.when(ci == 0)
    def _():
        s_ref[...] = jnp.zeros_like(s_ref)

    masks = _wkv_masks(CHUNK, CHUNK)
    chains = [(bi, gi) for bi in range(r_ref.shape[0]) for gi in range(N_GROUPS)]
    lanes = lambda gi: slice(MXU_DIM_V7X * gi, MXU_DIM_V7X * (gi + 1))
    groups = [tuple(ref[bi, :, lanes(gi)] for ref in (r_ref, lw_ref, k2_ref, v_ref, as_ref, bs_ref))
              + (s_ref[bi, gi],) for bi, gi in chains]
    for (bi, gi), (y, s_new) in zip(chains, _wkv_chunk(groups, masks)):
        y_ref[bi, :, lanes(gi)] = y
        s_ref[bi, gi] = s_new

    @pl.when(ci == pl.num_programs(0) - 1)
    def _():
        sout_ref[...] = s_ref[...]


def _wkv_prompt(r, lw, k2, v, a_s, b_s):
    b, t, _ = r.shape
    spec = pl.BlockSpec((b, CHUNK, D_RWKV), lambda ci: (0, ci, 0))
    state = (b, N_GROUPS, MXU_DIM_V7X, MXU_DIM_V7X)
    return pl.pallas_call(
        _wkv_prompt_kernel,
        grid=(t // CHUNK,),
        in_specs=[spec] * 6,
        out_specs=[spec, _const_spec(state)],
        out_shape=[jax.ShapeDtypeStruct((b, t, D_RWKV), F32), jax.ShapeDtypeStruct(state, F32)],
        scratch_shapes=[pltpu.VMEM(state, F32)],
        compiler_params=_params("arbitrary"),
        name="wkv_prompt",
    )(r, lw, k2, v, a_s, b_s)


SLAB = 8


def _slab_pair(ref, r0, c):
    return jnp.concatenate([ref[pl.ds(r0, SLAB), :], ref[pl.ds(c + r0, SLAB), :]], axis=0)


def _wkv_sample_kernel(seq, r_ref, lw_ref, k2_ref, v_ref, as_ref, bs_ref, s0_ref, y_ref, sout_ref,
                       x_ref, btk_ref, bk_ref, p_ref, uv_ref, gam_ref):
    c = r_ref.shape[0]
    per_slab = SLAB // seq

    lw = lw_ref[...]
    tpos = lax.broadcasted_iota(jnp.int32, lw.shape, 0) % seq
    cl = lw
    s = 1
    while s < seq:
        cl = cl + jnp.where(tpos >= s, pltpu.roll(cl, s, 0), 0.0)
        s *= 2
    tot = jnp.where(tpos == seq - 1, cl, 0.0)
    s = 1
    while s < seq:
        tot = tot + pltpu.roll(tot, c - s, 0)
        s *= 2
    e_neg = jnp.exp(-cl)
    e_rem = jnp.exp(tot - cl)
    b_s = bs_ref[...]
    k2 = k2_ref[...]
    x_ref[0:c, :] = as_ref[...] * jnp.exp(cl - lw)
    x_ref[c:2 * c, :] = r_ref[...] * jnp.exp(cl)
    btk_ref[0:c, :] = b_s * e_neg
    btk_ref[c:2 * c, :] = k2 * e_neg
    bk_ref[0:c, :] = b_s * e_rem
    bk_ref[c:2 * c, :] = k2 * e_rem
    gam_ref[...] = jnp.exp(tot)

    first_seq = (lax.broadcasted_iota(jnp.int32, (2 * SLAB, HEAD_DIM), 0) % SLAB) < seq

    def state_in(p, carry):
        r0 = pl.multiple_of(p * SLAB, SLAB)
        xp = _slab_pair(x_ref, r0, c)
        outs = []
        for h in range(N_HEADS):
            xs = xp[:, HEAD_DIM * h:HEAD_DIM * (h + 1)]
            p0 = _mm_nt(xs, s0_ref[per_slab * p, h])
            p1 = _mm_nt(xs, s0_ref[per_slab * p + 1, h])
            outs.append(jnp.where(first_seq, p0, p1))
        pp = jnp.concatenate(outs, axis=1)
        p_ref[pl.ds(r0, SLAB), :] = pp[0:SLAB]
        p_ref[pl.ds(c + r0, SLAB), :] = pp[SLAB:2 * SLAB]
        return carry

    lax.fori_loop(0, c // SLAB, state_in, 0)

    masks = _wkv_masks(c, seq)
    sls = [slice(MXU_DIM_V7X * gi, MXU_DIM_V7X * (gi + 1)) for gi in range(N_GROUPS)]
    ys, us = _wkv_intra([x_ref[:, sl] for sl in sls], [btk_ref[0:c, sl] for sl in sls],
                        [btk_ref[c:2 * c, sl] for sl in sls], [v_ref[:, sl] for sl in sls],
                        [p_ref[:, sl] for sl in sls], masks, seq)
    for sl, y, u in zip(sls, ys, us):
        y_ref[:, sl] = y
        uv_ref[0:c, sl] = u
    uv_ref[c:2 * c, :] = v_ref[...]

    first_rows = (lax.broadcasted_iota(jnp.int32, (2 * SLAB, D_RWKV), 0) % SLAB) < seq

    def state_out(p, carry):
        r0 = pl.multiple_of(p * SLAB, SLAB)
        uvp = _slab_pair(uv_ref, r0, c)
        bkp = _slab_pair(bk_ref, r0, c)
        gam8 = gam_ref[pl.ds(r0, SLAB), :]
        for bb in range(per_slab):
            uvm = jnp.where(first_rows if bb == 0 else jnp.logical_not(first_rows), uvp, 0.0)
            gam = gam8[seq * bb:seq * bb + 1, :]
            for j in range(N_HEADS // 2):
                ls = slice(2 * HEAD_DIM * j, 2 * HEAD_DIM * (j + 1))
                ds2 = _mm(uvm[:, ls].T, bkp[:, ls])
                for hh in range(2):
                    h = 2 * j + hh
                    hs = slice(HEAD_DIM * hh, HEAD_DIM * (hh + 1))
                    sout_ref[per_slab * p + bb, h] = (
                        s0_ref[per_slab * p + bb, h] * gam[:, HEAD_DIM * h:HEAD_DIM * (h + 1)] + ds2[hs, hs])
        return carry

    lax.fori_loop(0, c // SLAB, state_out, 0)


def _wkv_sample(r, lw, k2, v, a_s, b_s, s0, seq):
    n = r.shape[0]
    assert SLAB % seq == 0 and SLAB // seq == 2 and n % CHUNK == 0
    nb = CHUNK // seq
    vec = pl.BlockSpec((CHUNK, D_RWKV), lambda i: (i, 0))
    st = pl.BlockSpec((nb, N_HEADS, HEAD_DIM, HEAD_DIM), lambda i: (i, 0, 0, 0))
    return pl.pallas_call(
        functools.partial(_wkv_sample_kernel, seq),
        grid=(n // CHUNK,),
        in_specs=[vec] * 6 + [st],
        out_specs=[vec, st],
        out_shape=[jax.ShapeDtypeStruct((n, D_RWKV), F32), jax.ShapeDtypeStruct(s0.shape, F32)],
        scratch_shapes=[pltpu.VMEM((2 * CHUNK, D_RWKV), F32)] * 5 + [pltpu.VMEM((CHUNK, D_RWKV), F32)],
        compiler_params=_params("parallel"),
        name="wkv_sample",
    )(r, lw, k2, v, a_s, b_s, s0)


HEAD_PAIR = 2 * HEAD_DIM


def _wkv_steps_kernel(seq, r_ref, lw_ref, k2_ref, v_ref, as_ref, bs_ref, s0_ref, y_ref, sout_ref,
                      op_ref, vt_ref, yt_ref):
    nb = r_ref.shape[0] // seq
    for t in range(seq):
        rows = pl.ds(t, nb, stride=seq)
        op_ref[0, t] = r_ref[rows, :].T
        op_ref[1, t] = jnp.exp(lw_ref[rows, :]).T
        op_ref[2, t] = k2_ref[rows, :].T
        op_ref[3, t] = as_ref[rows, :].T
        op_ref[4, t] = bs_ref[rows, :].T
        vt_ref[t] = v_ref[rows, :].T

    for hh in range(HEAD_PAIR // HEAD_DIM):
        ks = slice(HEAD_DIM * hh, HEAD_DIM * (hh + 1))

        def body(vi, carry):
            s = s0_ref[hh, vi]
            for t in range(seq):
                sa = jnp.sum(s * op_ref[3, t, ks, :], axis=0, keepdims=True)
                vrow = vt_ref[t, pl.ds(HEAD_DIM * hh + vi, 1), :]
                s = s * op_ref[1, t, ks, :] + sa * op_ref[4, t, ks, :] + vrow * op_ref[2, t, ks, :]
                yt_ref[t, pl.ds(HEAD_DIM * hh + vi, 1), :] = jnp.sum(s * op_ref[0, t, ks, :], axis=0,
                                                                      keepdims=True)
            sout_ref[hh, vi] = s
            return carry

        lax.fori_loop(0, HEAD_DIM, body, 0, unroll=4)

    for t in range(seq):
        y_ref[pl.ds(t, nb, stride=seq), :] = yt_ref[t].T


def _wkv_steps(r, lw, k2, v, a_s, b_s, s0_native, seq):
    n = r.shape[0]
    nb = n // seq
    assert s0_native.shape == (N_HEADS, HEAD_DIM, HEAD_DIM, nb) and nb == LANES_V7X
    vec = pl.BlockSpec((n, HEAD_PAIR), lambda i: (0, i))
    st = pl.BlockSpec((HEAD_PAIR // HEAD_DIM, HEAD_DIM, HEAD_DIM, nb), lambda i: (i, 0, 0, 0))
    return pl.pallas_call(
        functools.partial(_wkv_steps_kernel, seq),
        grid=(N_HEADS * HEAD_DIM // HEAD_PAIR,),
        in_specs=[vec] * 6 + [st],
        out_specs=[vec, st],
        out_shape=[jax.ShapeDtypeStruct((n, D_RWKV), F32), jax.ShapeDtypeStruct(s0_native.shape, F32)],
        scratch_shapes=[pltpu.VMEM((5, seq, HEAD_PAIR, nb), F32), pltpu.VMEM((seq, HEAD_PAIR, nb), F32),
                        pltpu.VMEM((seq, HEAD_PAIR, nb), F32)],
        compiler_params=_params("parallel"),
        name="wkv_steps",
    )(r, lw, k2, v, a_s, b_s, s0_native)


def _pool_prompt_kernel(zp_ref, d_ref, carry_ref):
    i = pl.program_id(1)
    tm = zp_ref.shape[0]

    @pl.when(i == 0)
    def _():
        carry_ref[...] = jnp.zeros_like(carry_ref)

    zp = zp_ref[...]
    buf = jnp.concatenate([carry_ref[...], zp], axis=0)
    carry_ref[...] = zp[tm - 16:, :]
    pos = (i * tm + lax.broadcasted_iota(jnp.int32, (tm, POOL_GROUP), 0) + 1).astype(F32)
    for gi, win in enumerate(POOL_WINDOWS):
        sl = slice(POOL_GROUP * gi, POOL_GROUP * (gi + 1))
        acc = buf[:, sl]
        s = 1
        while s < win:
            acc = acc + pltpu.roll(acc, s, 0)
            s *= 2
        cnt = jnp.minimum(float(win), pos)
        d_ref[:, sl] = acc[16:, :] / cnt - zp[:, sl]


def _pool_prompt(zp, b, t, tm):
    tiles = t // tm
    return pl.pallas_call(
        _pool_prompt_kernel,
        grid=(b, tiles),
        in_specs=[pl.BlockSpec((tm, D_POOL), lambda bi, i: (bi * tiles + i, 0))],
        out_specs=pl.BlockSpec((None, tm, D_POOL), lambda bi, i: (bi, i, 0)),
        out_shape=jax.ShapeDtypeStruct((b, t, D_POOL), F32),
        scratch_shapes=[pltpu.VMEM((16, D_POOL), F32)],
        compiler_params=_params("parallel", "arbitrary"),
        name="pool_prompt",
    )(zp)


def _pool_sample_kernel(buf_ref, d_ref):
    nt = d_ref.shape[1]
    for t in range(nt):
        for gi, win in enumerate(POOL_WINDOWS):
            sl = slice(POOL_GROUP * gi, POOL_GROUP * (gi + 1))
            acc = buf_ref[:, POOL_HIST + t, sl]
            for j in range(1, win):
                acc = acc + buf_ref[:, POOL_HIST + t - j, sl]
            d_ref[:, t, sl] = acc / float(min(win, PAST_LEN + 1)) - buf_ref[:, POOL_HIST + t, sl]


def _pool_sample(buf, seq, bt):
    b = buf.shape[0]
    return pl.pallas_call(
        _pool_sample_kernel,
        grid=(b // bt,),
        in_specs=[pl.BlockSpec((bt, buf.shape[1], D_POOL), lambda i: (i, 0, 0))],
        out_specs=pl.BlockSpec((bt, seq, D_POOL), lambda i: (i, 0, 0)),
        out_shape=jax.ShapeDtypeStruct((b, seq, D_POOL), F32),
        compiler_params=_params("parallel"),
        name="pool_sample",
    )(buf)


def _merge_kernel(y_ref, r_ref, k2_ref, v_ref, g_ref, d_ref, zga_ref, zgb_ref,
                  rk_ref, lw_ref, lb_ref, ones_ref, wa_ref, pw_ref, ps_ref, wb_ref, m_ref):
    ones_bd = ones_ref[...]
    y = y_ref[...]
    mu = _head_sum(y, ones_bd) * (1.0 / HEAD_DIM)
    yc = y - mu
    var = _head_sum(yc * yc, ones_bd) * (1.0 / HEAD_DIM)
    yn = yc * lax.rsqrt(var + GN_EPS) * lw_ref[...] + lb_ref[...]
    v = v_ref[...]
    bonus = _head_sum(r_ref[...] * k2_ref[...] * rk_ref[...], ones_bd) * v
    ya = (yn + bonus) * g_ref[...]
    d = d_ref[...]
    yb = jnp.concatenate(
        [_mm(d[:, POOL_GROUP * gi:POOL_GROUP * (gi + 1)], pw_ref[gi]) for gi in range(len(POOL_WINDOWS))],
        axis=1) * ps_ref[...]
    m_ref[...] = (jax.nn.sigmoid(zga_ref[...]) * _mm(ya, wa_ref[...])
                  + jax.nn.sigmoid(zgb_ref[...]) * _mm(yb, wb_ref[...]))


def _merge(y, r, k2, v, g, d, zga, zgb, row0, wts, tm):
    n = y.shape[0]
    assert row0 % tm == 0
    first = row0 // tm
    half = pl.BlockSpec((tm, D_RWKV), lambda i: (i, 0))
    full = pl.BlockSpec((tm, D_MODEL), lambda i: (i, 0))
    gate = pl.BlockSpec((tm, D_MODEL), lambda i: (first + i, 0))
    return pl.pallas_call(
        _merge_kernel,
        grid=(n // tm,),
        in_specs=[half] * 6 + [gate] * 2 + [_const_spec(w.shape) for w in wts],
        out_specs=full,
        out_shape=jax.ShapeDtypeStruct((n, D_MODEL), F32),
        compiler_params=_params("parallel"),
        name="merge",
    )(y, r, k2, v, g, d, zga, zgb, *wts)


def _mix_out_kernel(m_ref, x_ref, w_ref, g_ref, o_ref):
    mo = _mm(m_ref[...], w_ref[...])
    o_ref[...] = x_ref[...] + _rmsnorm(mo, g_ref[...])


def _mix_out(m, x2d, w_out, g, tm):
    n = m.shape[0]
    full = pl.BlockSpec((tm, D_MODEL), lambda i: (i, 0))
    return pl.pallas_call(
        _mix_out_kernel,
        grid=(n // tm,),
        in_specs=[full, full, _const_spec(w_out.shape), _const_spec(g.shape)],
        out_specs=full,
        out_shape=jax.ShapeDtypeStruct((n, D_MODEL), F32),
        compiler_params=_params("parallel"),
        name="mix_out",
    )(m, x2d, w_out, g)


def _gelu_tanh(x):
    return 0.5 * x * (1.0 + jnp.tanh(0.7978845608028654 * (x + 0.044715 * x * x * x)))


def _ffn_body(x_ref, gn_ref, wg_ref, wu_ref, cw_ref, cb_ref, wo_ref, gp_ref,
              o_ref, tail_ref, h_ref, acc_ref, prev1, prev2):
    c = pl.program_id(1)

    @pl.when(c == 0)
    def _():
        h_ref[...] = _rmsnorm(x_ref[...], gn_ref[...]).astype(BF16)
        acc_ref[...] = jnp.zeros_like(acc_ref)

    h = h_ref[...]
    gate = jnp.dot(h, wg_ref[...], preferred_element_type=F32)
    up = jnp.dot(h, wu_ref[...], preferred_element_type=F32)
    cw = cw_ref[...]
    cv = cb_ref[...] + cw[0:1, :] * prev2(gate) + cw[1:2, :] * prev1(gate) + cw[2:3, :] * gate
    tail_ref[...] = gate[gate.shape[0] - tail_ref.shape[0]:, :]
    acc_ref[...] += _mm(_gelu_tanh(cv) * up, wo_ref[...])

    @pl.when(c == pl.num_programs(1) - 1)
    def _():
        o_ref[...] = x_ref[...] + _rmsnorm(acc_ref[...], gp_ref[...])


def _ffn_prompt_kernel(tiles_per_seq, x_ref, gn_ref, wg_ref, wu_ref, cw_ref, cb_ref, wo_ref, gp_ref,
                       o_ref, tail_ref, h_ref, acc_ref, carry_ref):
    i = pl.program_id(0)
    c = pl.program_id(1)

    @pl.when(i % tiles_per_seq == 0)
    def _():
        carry_ref[c] = jnp.zeros(carry_ref.shape[1:], F32)

    hist = carry_ref[c]

    def shifted(gate, s):
        row = lax.broadcasted_iota(jnp.int32, gate.shape, 0)
        rolled = pltpu.roll(gate, s, 0)
        out = rolled
        for j in range(s):
            out = jnp.where(row == j, hist[8 - s + j:9 - s + j, :], out)
        return out

    _ffn_body(x_ref, gn_ref, wg_ref, wu_ref, cw_ref, cb_ref, wo_ref, gp_ref, o_ref, tail_ref,
              h_ref, acc_ref, lambda g: shifted(g, 1), lambda g: shifted(g, 2))
    carry_ref[c] = tail_ref[...]


def _ffn_sample_kernel(seq, x_ref, st_ref, gn_ref, wg_ref, wu_ref, cw_ref, cb_ref, wo_ref, gp_ref,
                       o_ref, tail_ref, h_ref, acc_ref):
    st = st_ref[...]
    n = st.shape[0]
    tpos = lax.broadcasted_iota(jnp.int32, st.shape, 0) % seq
    prev1 = lambda g: jnp.where(tpos == 0, pltpu.roll(st, n - 1, 0), pltpu.roll(g, 1, 0))
    prev2 = lambda g: jnp.where(tpos < CONV_W - 1, st, pltpu.roll(g, 2, 0))
    _ffn_body(x_ref, gn_ref, wg_ref, wu_ref, cw_ref, cb_ref, wo_ref, gp_ref, o_ref, tail_ref,
              h_ref, acc_ref, prev1, prev2)


def _ffn_prompt(x2d, seq_len, wts, tm, fk):
    n = x2d.shape[0]
    nc = D_FF // fk
    gn, w_in, cw, cb, w_out, gp = wts
    full = pl.BlockSpec((tm, D_MODEL), lambda i, c: (i, 0))
    return pl.pallas_call(
        functools.partial(_ffn_prompt_kernel, seq_len // tm),
        grid=(n // tm, nc),
        in_specs=[full, _const_spec(gn.shape),
                  pl.BlockSpec((D_MODEL, fk), lambda i, c: (0, c)),
                  pl.BlockSpec((D_MODEL, fk), lambda i, c: (0, c + nc)),
                  pl.BlockSpec((CONV_W, fk), lambda i, c: (0, c)),
                  pl.BlockSpec((1, fk), lambda i, c: (0, c)),
                  pl.BlockSpec((fk, D_MODEL), lambda i, c: (c, 0)),
                  _const_spec(gp.shape)],
        out_specs=[full, pl.BlockSpec((None, 8, fk), lambda i, c: (i, 0, c))],
        out_shape=[jax.ShapeDtypeStruct((n, D_MODEL), F32),
                   jax.ShapeDtypeStruct((n // tm, 8, D_FF), F32)],
        scratch_shapes=[pltpu.VMEM((tm, D_MODEL), BF16), pltpu.VMEM((tm, D_MODEL), F32),
                        pltpu.VMEM((nc, 8, fk), F32)],
        compiler_params=_params("arbitrary", "arbitrary"),
        name="ffn_prompt",
    )(x2d, gn, w_in, w_in, cw, cb, w_out, gp)


def _ffn_sample(x2d, st_rows, seq, wts, fk):
    n = x2d.shape[0]
    assert seq >= CONV_W - 1
    nc = D_FF // fk
    gn, w_in, cw, cb, w_out, gp = wts
    full = pl.BlockSpec((n, D_MODEL), lambda i, c: (0, 0))
    cols = pl.BlockSpec((n, fk), lambda i, c: (0, c))
    return pl.pallas_call(
        functools.partial(_ffn_sample_kernel, seq),
        grid=(1, nc),
        in_specs=[full, cols, _const_spec(gn.shape),
                  pl.BlockSpec((D_MODEL, fk), lambda i, c: (0, c)),
                  pl.BlockSpec((D_MODEL, fk), lambda i, c: (0, c + nc)),
                  pl.BlockSpec((CONV_W, fk), lambda i, c: (0, c)),
                  pl.BlockSpec((1, fk), lambda i, c: (0, c)),
                  pl.BlockSpec((fk, D_MODEL), lambda i, c: (c, 0)),
                  _const_spec(gp.shape)],
        out_specs=[full, cols],
        out_shape=[jax.ShapeDtypeStruct((n, D_MODEL), F32), jax.ShapeDtypeStruct((n, D_FF), F32)],
        scratch_shapes=[pltpu.VMEM((n, D_MODEL), BF16), pltpu.VMEM((n, D_MODEL), F32)],
        compiler_params=_params("arbitrary", "arbitrary"),
        name="ffn_sample",
    )(x2d, st_rows, gn, w_in, w_in, cw, cb, w_out, gp)


def _row(v):
    return v.reshape(1, -1).astype(F32)


def _ones_bd():
    i = jnp.arange(MXU_DIM_V7X) // HEAD_DIM
    return (i[:, None] == i[None, :]).astype(BF16)


def _layer_weights(l, norm_pre_mix, w_in, mu_shift, w0, w2, a0, a2, g2, k_k, k_a, r_k, lnx_w, lnx_b,
                   w_branch_a, pool_w, pool_scale, w_branch_b, w_out, norm_post_mix,
                   norm_pre_ffn, w_ffn_in, conv_w, conv_b, w_ffn_out, norm_post_ffn):
    w_t = jnp.swapaxes(w_in[l], 0, 1)
    mu = jnp.pad(mu_shift[l], (0, D_SHIFT_PAD - D_SHIFT)).reshape(1, -1)
    w_lora = jnp.zeros((D_LORA_PAD, 3 * D_RWKV), F32)
    w_lora = w_lora.at[0:RANK_W, 0:D_RWKV].set(w2[l])
    w_lora = w_lora.at[RANK_W:RANK_W + RANK_A, D_RWKV:2 * D_RWKV].set(a2[l])
    w_lora = w_lora.at[RANK_W + RANK_A:D_LORA, 2 * D_RWKV:].set(g2[l])
    ones_bd = _ones_bd()
    return dict(
        in_proj=(_row(norm_pre_mix[l]), w_t),
        prep=(mu, w_lora.astype(BF16), _row(w0[l]), _row(a0[l]), _row(k_k[l]), _row(k_a[l]), ones_bd),
        merge=(_row(r_k[l]), _row(lnx_w[l]), _row(lnx_b[l]), ones_bd, w_branch_a[l].astype(BF16),
               pool_w[l].astype(BF16), _row(pool_scale[l]), w_branch_b[l].astype(BF16)),
        mix_out=(w_out[l].astype(BF16), _row(norm_post_mix[l])),
        ffn=(_row(norm_pre_ffn[l]), w_ffn_in[l].astype(BF16), conv_w[l].astype(F32), _row(conv_b[l]),
             w_ffn_out[l].astype(BF16), _row(norm_post_ffn[l])),
    )


def _largest_tile(n, cap, mult=16):
    best = None
    for d in range(mult, min(n, cap) + 1, mult):
        if n % d == 0:
            best = d
    assert best is not None, (n, cap)
    return best


def _project(xp2d, xs2d, wts):
    g, w_t = wts["in_proj"]
    tm_norm = _largest_tile(math.gcd(xp2d.shape[0], xs2d.shape[0]), 512)
    h = _pre_norm(xp2d, xs2d, g, tm_norm)
    tm = _largest_tile(h.shape[0], 1088)
    zrkv = _in_proj(h, w_t, 0, 3 * D_RWKV, D_RWKV, tm)
    zl = _in_proj(h, w_t, 3 * D_RWKV, D_LORA_PAD, D_LORA_PAD, tm)
    zp = _in_proj(h, w_t, D_SHIFT, D_POOL, D_POOL, tm)
    zga = _in_proj(h, w_t, D_SHIFT + D_POOL, D_MODEL, D_MODEL // 2, tm)
    zgb = _in_proj(h, w_t, D_SHIFT + D_POOL + D_MODEL, D_MODEL, D_MODEL // 2, tm)
    return zrkv, zl, zp, zga, zgb


def _last_shift_row(zrkv, zl, row0, b, t):
    last = lambda z, w: z[row0:row0 + b * t].reshape(b, t, z.shape[1])[:, t - 1:, :w]
    return jnp.concatenate([last(zrkv, 3 * D_RWKV), last(zl, D_LORA)], axis=-1)


def _prompt_layer(x, z, wts):
    b, t, _ = x.shape
    n = b * t
    x2d = x.reshape(n, D_MODEL)
    zrkv, zl, zp, zga, zgb = z
    r, lw, k2, v, a_s, b_s, g = _prep_prompt(zrkv, zl, b, t, wts["prep"], tm=256)
    y, s_bd = _wkv_prompt(r, lw, k2, v, a_s, b_s)
    d = _pool_prompt(zp, b, t, tm=512)
    flat = lambda a: a.reshape(n, a.shape[-1])
    m = _merge(flat(y), flat(r), flat(k2), flat(v), flat(g), flat(d), zga, zgb, 0, wts["merge"], tm=256)
    x1 = _mix_out(m, x2d, *wts["mix_out"], tm=512)
    out, tail = _ffn_prompt(x1, t, wts["ffn"], tm=512, fk=512)
    shift = _last_shift_row(zrkv, zl, 0, b, t)
    zp3 = zp[:n].reshape(b, t, D_POOL)
    s5 = s_bd.reshape(b, N_GROUPS, HEADS_PER_GROUP, HEAD_DIM, HEADS_PER_GROUP, HEAD_DIM)
    idx = jnp.arange(HEADS_PER_GROUP)
    wkv = s5[:, :, idx, :, idx, :]
    wkv = jnp.moveaxis(wkv, 0, 2).reshape(b, N_HEADS, HEAD_DIM, HEAD_DIM)
    pool = zp3[:, t - POOL_HIST:, :]
    tiles = t // 512
    conv = tail.reshape(b, tiles, 8, D_FF)[:, tiles - 1, 8 - (CONV_W - 1):, :]
    return out.reshape(b, t, D_MODEL), shift, wkv, pool, conv


def _sample_layer(x, z, row0, st_shift, st_wkv, st_pool, st_conv, wts):
    b, t, _ = x.shape
    n = b * t
    x2d = x.reshape(n, D_MODEL)
    zrkv, zl, zp, zga, zgb = z
    st_rows = jnp.pad(st_shift, ((0, 0), (0, t - 1), (0, D_SHIFT_PAD - D_SHIFT))).reshape(n, D_SHIFT_PAD)
    tm = _largest_tile(math.gcd(row0, n), 128, mult=8)
    r, lw, k2, v, a_s, b_s, g = _prep_sample(zrkv, zl, row0, n, st_rows, t, wts["prep"], tm=tm)
    y, wkv = _wkv_steps(r, lw, k2, v, a_s, b_s, jnp.transpose(st_wkv, (1, 2, 3, 0)), t)
    wkv = jnp.transpose(wkv, (3, 0, 1, 2))
    buf = jnp.concatenate([st_pool, zp[row0:].reshape(b, t, D_POOL)], axis=1)
    d = _pool_sample(buf, t, bt=min(32, b)).reshape(n, D_POOL)
    m = _merge(y, r, k2, v, g, d, zga, zgb, row0, wts["merge"], tm=_largest_tile(math.gcd(row0, n), 256, mult=8))
    x1 = _mix_out(m, x2d, *wts["mix_out"], tm=n)
    conv_rows = jnp.pad(st_conv, ((0, 0), (0, t - (CONV_W - 1)), (0, 0))).reshape(n, D_FF)
    out, gate = _ffn_sample(x1, conv_rows, t, wts["ffn"], fk=512)
    shift = _last_shift_row(zrkv, zl, row0, b, t)
    pool = buf[:, -POOL_HIST:]
    conv = jnp.concatenate([st_conv, gate.reshape(b, t, D_FF)], axis=1)[:, -(CONV_W - 1):]
    return out.reshape(b, t, D_MODEL), shift, wkv, pool, conv


def kernel(x_prompt, x_sample, state_shift, state_wkv, state_pool, state_conv, norm_pre_mix, w_in, mu_shift, w0, w2, a0, a2, g2, k_k, k_a, r_k, lnx_w, lnx_b, w_branch_a, pool_w, pool_scale, w_branch_b, w_out, norm_post_mix, norm_pre_ffn, w_ffn_in, conv_w, conv_b, w_ffn_out, norm_post_ffn):
    weights = (norm_pre_mix, w_in, mu_shift, w0, w2, a0, a2, g2, k_k, k_a, r_k, lnx_w, lnx_b,
               w_branch_a, pool_w, pool_scale, w_branch_b, w_out, norm_post_mix,
               norm_pre_ffn, w_ffn_in, conv_w, conv_b, w_ffn_out, norm_post_ffn)
    depth = w_in.shape[0]
    yp, ys = x_prompt, x_sample
    p_states, s_states = [], []
    for l in range(depth):
        wts = _layer_weights(l, *weights)
        n_prompt = yp.shape[0] * yp.shape[1]
        z = _project(yp.reshape(n_prompt, D_MODEL), ys.reshape(-1, D_MODEL), wts)
        yp, *ps = _prompt_layer(yp, z, wts)
        ys, *ss = _sample_layer(ys, z, n_prompt, state_shift[l], state_wkv[l], state_pool[l], state_conv[l], wts)
        p_states.append(ps)
        s_states.append(ss)
    stack = lambda states, i: jnp.stack([s[i] for s in states])
    return (yp, ys,
            stack(p_states, 0), stack(p_states, 1), stack(p_states, 2), stack(p_states, 3),
            stack(s_states, 0), stack(s_states, 1), stack(s_states, 2), stack(s_states, 3))
```

```python
import functools
import math

import jax
import jax.numpy as jnp
from jax import lax
from jax.experimental import pallas as pl
from jax.experimental.pallas import tpu as pltpu

F32 = jnp.float32
BF16 = jnp.bfloat16

D_MODEL = 2048
HEAD_DIM = 64
D_RWKV = 1024
N_HEADS = 16
RANK_W, RANK_A, RANK_G = 64, 64, 160
D_LORA = RANK_W + RANK_A + RANK_G
D_LORA_PAD = 384
D_SHIFT = 3 * D_RWKV + D_LORA
D_SHIFT_PAD = 3 * D_RWKV + D_LORA_PAD
D_POOL = 1024
POOL_WINDOWS = (2, 4, 8, 16)
POOL_GROUP = 256
POOL_HIST = 15
D_FF = 5632
CONV_W = 3
NORM_EPS = 1e-6
GN_EPS = 64e-5
PAST_LEN = 16384

LANES_V7X = 128
MXU_DIM_V7X = 256
HEADS_PER_GROUP = MXU_DIM_V7X // HEAD_DIM
N_GROUPS = N_HEADS // HEADS_PER_GROUP
CHUNK = 64
VMEM_LIMIT_V7X = 56 * 1024 * 1024


def _params(*sem):
    return pltpu.CompilerParams(dimension_semantics=sem, vmem_limit_bytes=VMEM_LIMIT_V7X)


def _mm(a, b):
    return jnp.dot(a.astype(BF16), b.astype(BF16), preferred_element_type=F32)


def _mm_nt(a, b):
    return lax.dot_general(a.astype(BF16), b.astype(BF16), (((1,), (1,)), ((), ())),
                           preferred_element_type=F32)


def _split_hi_lo(x):
    hi = x.astype(BF16)
    lo = (x - hi.astype(F32)).astype(BF16)
    return hi, lo


def _head_sum(x, ones_bd):
    hi, lo = _split_hi_lo(x)
    outs = []
    for gi in range(x.shape[1] // MXU_DIM_V7X):
        sl = slice(MXU_DIM_V7X * gi, MXU_DIM_V7X * (gi + 1))
        outs.append(jnp.dot(hi[:, sl], ones_bd, preferred_element_type=F32)
                    + jnp.dot(lo[:, sl], ones_bd, preferred_element_type=F32))
    return jnp.concatenate(outs, axis=1)


def _rmsnorm(x, g):
    return x * lax.rsqrt(jnp.mean(x * x, axis=-1, keepdims=True) + NORM_EPS) * g


def _replace_step_rows(scr_ref, base, seq, steps):
    nb = base.shape[0] // seq
    for j in range(base.shape[1] // LANES_V7X):
        ls = slice(LANES_V7X * j, LANES_V7X * (j + 1))
        scr_ref[j] = base[:, ls]
        for t, vals in steps:
            scr_ref[j, pl.ds(t, nb, stride=seq), :] = vals[:, ls]
    return jnp.concatenate([scr_ref[j] for j in range(base.shape[1] // LANES_V7X)], axis=1)


def _pre_norm_kernel(prompt_tiles, xp_ref, xs_ref, g_ref, h_ref):
    i = pl.program_id(0)

    @pl.when(i < prompt_tiles)
    def _():
        h_ref[...] = _rmsnorm(xp_ref[...], g_ref[...]).astype(BF16)

    @pl.when(i >= prompt_tiles)
    def _():
        h_ref[...] = _rmsnorm(xs_ref[...], g_ref[...]).astype(BF16)


def _pre_norm(xp, xs, g, tm):
    pt, st = xp.shape[0] // tm, xs.shape[0] // tm
    return pl.pallas_call(
        functools.partial(_pre_norm_kernel, pt),
        grid=(pt + st,),
        in_specs=[pl.BlockSpec((tm, D_MODEL), lambda i: (jnp.minimum(i, pt - 1), 0)),
                  pl.BlockSpec((tm, D_MODEL), lambda i: (jnp.maximum(i - pt, 0), 0)),
                  _const_spec(g.shape)],
        out_specs=pl.BlockSpec((tm, D_MODEL), lambda i: (i, 0)),
        out_shape=jax.ShapeDtypeStruct((xp.shape[0] + xs.shape[0], D_MODEL), BF16),
        compiler_params=_params("arbitrary"),
        name="pre_norm",
    )(xp, xs, g)


def _in_proj_kernel(h_ref, w_ref, o_ref, wb_ref):
    @pl.when(pl.program_id(1) == 0)
    def _():
        wb_ref[...] = w_ref[...].astype(BF16)

    o_ref[...] = lax.dot_general(h_ref[...], wb_ref[...], (((1,), (1,)), ((), ())),
                                 preferred_element_type=F32)


def _in_proj(h, w_t, off, width, tn, tm):
    n = h.shape[0]
    assert width % tn == 0 and n % tm == 0
    return pl.pallas_call(
        _in_proj_kernel,
        grid=(width // tn, n // tm),
        in_specs=[pl.BlockSpec((tm, D_MODEL), lambda j, i: (i, 0)),
                  pl.BlockSpec((pl.Element(tn), pl.Element(D_MODEL)),
                               lambda j, i: (SLAB * (off // SLAB + j * (tn // SLAB)), 0))],
        out_specs=pl.BlockSpec((tm, tn), lambda j, i: (i, j)),
        out_shape=jax.ShapeDtypeStruct((n, width), F32),
        scratch_shapes=[pltpu.VMEM((tn, D_MODEL), BF16)],
        compiler_params=_params("arbitrary", "arbitrary"),
        name="in_proj",
    )(h, w_t)


def _prep_math(zs, prev, mu, w_lora, w0, a0, k_k, k_a, ones_bd):
    f = zs + (prev - zs) * mu
    r = f[:, 0:D_RWKV]
    k = f[:, D_RWKV:2 * D_RWKV]
    v = f[:, 2 * D_RWKV:3 * D_RWKV]
    low = f[:, 3 * D_RWKV:3 * D_RWKV + D_LORA_PAD]
    lane = lax.broadcasted_iota(jnp.int32, low.shape, 1)
    act = jnp.where(lane < RANK_W, jnp.tanh(low),
                    jnp.where(lane < RANK_W + RANK_A, low,
                              jnp.where(lane < D_LORA, jax.nn.sigmoid(low), 0.0)))
    lora = _mm(act, w_lora)
    u = w0 + lora[:, 0:D_RWKV]
    w_log = -(jnp.maximum(-u, 0.0) + jnp.log1p(jnp.exp(-jnp.abs(u)))) - 0.5
    lw = -jnp.exp(w_log)
    a = jax.nn.sigmoid(a0 + lora[:, D_RWKV:2 * D_RWKV])
    g = lora[:, 2 * D_RWKV:3 * D_RWKV]
    kk = k * k_k
    nrm = jnp.sqrt(_head_sum(kk * kk, ones_bd))
    kk = kk / jnp.maximum(nrm, 1e-12)
    k2 = k * (1.0 + (a - 1.0) * k_a)
    return r, lw, k2, v, -kk, kk * a, g


def _prep_prompt_kernel(zrkv_ref, zl_ref, mu_ref, wl_ref, w0_ref, a0_ref, kk_ref, ka_ref, ones_ref,
                        r_ref, lw_ref, k2_ref, v_ref, as_ref, bs_ref, g_ref, carry_ref):
    i = pl.program_id(1)

    @pl.when(i == 0)
    def _():
        carry_ref[...] = jnp.zeros_like(carry_ref)

    zs = jnp.concatenate([zrkv_ref[...], zl_ref[...]], axis=1)
    rolled = pltpu.roll(zs, 1, 0)
    row = lax.broadcasted_iota(jnp.int32, zs.shape, 0)
    prev = jnp.where(row == 0, carry_ref[0:1, :], rolled)
    carry_ref[0:1, :] = zs[zs.shape[0] - 1:, :]
    outs = _prep_math(zs, prev, mu_ref[...], wl_ref[...], w0_ref[...], a0_ref[...],
                      kk_ref[...], ka_ref[...], ones_ref[...])
    for o_ref, o in zip((r_ref, lw_ref, k2_ref, v_ref, as_ref, bs_ref, g_ref), outs):
        o_ref[...] = o


def _prep_sample_kernel(seq, zrkv_ref, zl_ref, st_ref, mu_ref, wl_ref, w0_ref, a0_ref, kk_ref, ka_ref,
                        ones_ref, r_ref, lw_ref, k2_ref, v_ref, as_ref, bs_ref, g_ref, prev_ref):
    zs = jnp.concatenate([zrkv_ref[...], zl_ref[...]], axis=1)
    prev = _replace_step_rows(prev_ref, pltpu.roll(zs, 1, 0), seq, [(0, st_ref[...])])
    outs = _prep_math(zs, prev, mu_ref[...], wl_ref[...], w0_ref[...], a0_ref[...],
                      kk_ref[...], ka_ref[...], ones_ref[...])
    for o_ref, o in zip((r_ref, lw_ref, k2_ref, v_ref, as_ref, bs_ref, g_ref), outs):
        o_ref[...] = o


def _const_spec(shape):
    nd = len(shape)
    return pl.BlockSpec(shape, lambda *_: (0,) * nd)


def _prep_prompt(zrkv, zl, b, t, wts, tm):
    tiles = t // tm
    outs = [jax.ShapeDtypeStruct((b, t, D_RWKV), F32)] * 7
    row_spec = pl.BlockSpec((None, tm, D_RWKV), lambda bi, i: (bi, i, 0))
    return pl.pallas_call(
        _prep_prompt_kernel,
        grid=(b, tiles),
        in_specs=[pl.BlockSpec((tm, 3 * D_RWKV), lambda bi, i: (bi * tiles + i, 0)),
                  pl.BlockSpec((tm, D_LORA_PAD), lambda bi, i: (bi * tiles + i, 0))]
        + [_const_spec(w.shape) for w in wts],
        out_specs=[row_spec] * 7,
        out_shape=outs,
        scratch_shapes=[pltpu.VMEM((8, D_SHIFT_PAD), F32)],
        compiler_params=_params("parallel", "arbitrary"),
        name="prep_prompt",
    )(zrkv, zl, *wts)


def _prep_sample(zrkv, zl, row0, n, st, seq, wts, tm):
    assert tm % (SLAB * seq) == 0 and n % tm == 0 and row0 % tm == 0
    first = row0 // tm
    outs = [jax.ShapeDtypeStruct((n, D_RWKV), F32)] * 7
    row_spec = pl.BlockSpec((tm, D_RWKV), lambda i: (i, 0))
    return pl.pallas_call(
        functools.partial(_prep_sample_kernel, seq),
        grid=(n // tm,),
        in_specs=[pl.BlockSpec((tm, 3 * D_RWKV), lambda i: (first + i, 0)),
                  pl.BlockSpec((tm, D_LORA_PAD), lambda i: (first + i, 0)),
                  pl.BlockSpec((tm // seq, D_SHIFT_PAD), lambda i: (i, 0))]
        + [_const_spec(w.shape) for w in wts],
        out_specs=[row_spec] * 7,
        out_shape=outs,
        scratch_shapes=[pltpu.VMEM((D_SHIFT_PAD // LANES_V7X, tm, LANES_V7X), F32)],
        compiler_params=_params("parallel"),
        name="prep_sample",
    )(zrkv, zl, st, *wts)


def _expand_bd(a, bd_mask):
    return jnp.where(bd_mask, jnp.concatenate([a] * HEADS_PER_GROUP, axis=0), 0.0)


def _chunk_cumsum(x):
    row = lax.broadcasted_iota(jnp.int32, x.shape, 0)
    s = 1
    while s < x.shape[0]:
        x = x + jnp.where(row >= s, pltpu.roll(x, s, 0), 0.0)
        s *= 2
    return x


def _wkv_intra(xs, bts, kts, vs, pzs, masks, order):
    bd_mask, strict_mask, incl_mask, eye_c, head_masks = masks
    c = vs[0].shape[0]
    ystacks = [jnp.concatenate([jnp.where(m, bt, 0.0) for m in head_masks]
                               + [jnp.where(m, kt, 0.0) for m in head_masks], axis=0)
               for bt, kt in zip(bts, kts)]
    grams = [_mm_nt(x, ys) for x, ys in zip(xs, ystacks)]
    l_abs = [jnp.where(strict_mask, g[0:c, 0:4 * c], 0.0) for g in grams]
    l_aks = [jnp.where(strict_mask, g[0:c, 4 * c:8 * c], 0.0) for g in grams]
    m_rbs = [jnp.where(incl_mask, g[c:2 * c, 0:4 * c], 0.0) for g in grams]
    m_rks = [jnp.where(incl_mask, g[c:2 * c, 4 * c:8 * c], 0.0) for g in grams]

    tinvs = [eye_c + l for l in l_abs]
    lps = [_mm(l, _expand_bd(l, bd_mask)) for l in l_abs]
    p = 2
    while True:
        rhss = [_expand_bd(lp, bd_mask) for lp in lps]
        if 2 * p >= order:
            tinvs = [t + _mm(t, rhs) for t, rhs in zip(tinvs, rhss)]
            break
        ress = [_mm(jnp.concatenate([lp, t], axis=0), rhs) for lp, t, rhs in zip(lps, tinvs, rhss)]
        lps = [res[0:c] for res in ress]
        tinvs = [t + res[c:2 * c] for t, res in zip(tinvs, ress)]
        p *= 2

    v_bds = [_expand_bd(v, bd_mask) for v in vs]
    ws = [pz[0:c] + _mm(l_ak, v_bd) for pz, l_ak, v_bd in zip(pzs, l_aks, v_bds)]
    us = [_mm(t, _expand_bd(w, bd_mask)) for t, w in zip(tinvs, ws)]
    ys = [pz[c:2 * c] + _mm(jnp.concatenate([m_rb, m_rk], axis=1),
                            jnp.concatenate([_expand_bd(u, bd_mask), v_bd], axis=0))
          for pz, m_rb, m_rk, u, v_bd in zip(pzs, m_rbs, m_rks, us, v_bds)]
    return ys, us


def _wkv_chunk(groups, masks):
    c = groups[0][0].shape[0]
    cls = [_chunk_cumsum(g[1]) for g in groups]
    xs = [jnp.concatenate([g[4] * jnp.exp(cl - g[1]), g[0] * jnp.exp(cl)], axis=0)
          for g, cl in zip(groups, cls)]
    pzs = [_mm_nt(x, g[6]) for x, g in zip(xs, groups)]
    e_negs = [jnp.exp(-cl) for cl in cls]
    ys, us = _wkv_intra(xs, [g[5] * e for g, e in zip(groups, e_negs)],
                        [g[2] * e for g, e in zip(groups, e_negs)], [g[3] for g in groups], pzs, masks, c)
    dss = []
    for g, cl, u in zip(groups, cls, us):
        e_rem = jnp.exp(cl[c - 1:c, :] - cl)
        uv_t = jnp.concatenate([u, g[3]], axis=0).T
        bk = jnp.concatenate([g[5] * e_rem, g[2] * e_rem], axis=0)
        dss.append(_mm(uv_t, bk))
    return [(y, g[6] * jnp.exp(cl[c - 1:c, :]) + jnp.where(masks[0], ds, 0.0))
            for y, g, cl, ds in zip(ys, groups, cls, dss)]


def _wkv_masks(c, seq):
    n = HEADS_PER_GROUP * c
    rr = lax.broadcasted_iota(jnp.int32, (n, n), 0)
    cc = lax.broadcasted_iota(jnp.int32, (n, n), 1)
    bd_mask = (rr // c) == (cc // c)
    t = lax.broadcasted_iota(jnp.int32, (c, n), 0)
    s = lax.broadcasted_iota(jnp.int32, (c, n), 1) % c
    same = (t // seq) == (s // seq)
    strict_mask = same & (t > s)
    incl_mask = same & (t >= s)
    eye_c = jnp.where(t == s, 1.0, 0.0).astype(F32)
    lane = lax.broadcasted_iota(jnp.int32, (c, MXU_DIM_V7X), 1)
    head_masks = [(lane // HEAD_DIM) == h for h in range(HEADS_PER_GROUP)]
    return bd_mask, strict_mask, incl_mask, eye_c, head_masks


def _wkv_prompt_kernel(r_ref, lw_ref, k2_ref, v_ref, as_ref, bs_ref, y_ref, sout_ref, s_ref):
    ci = pl.program_id(0)

    @pl.when(ci == 0)
    def _():
        s_ref[...] = jnp.zeros_like(s_ref)

    masks = _wkv_masks(CHUNK, CHUNK)
    chains = [(bi, gi) for bi in range(r_ref.shape[0]) for gi in range(N_GROUPS)]
    lanes = lambda gi: slice(MXU_DIM_V7X * gi, MXU_DIM_V7X * (gi + 1))
    groups = [tuple(ref[bi, :, lanes(gi)] for ref in (r_ref, lw_ref, k2_ref, v_ref, as_ref, bs_ref))
              + (s_ref[bi, gi],) for bi, gi in chains]
    for (bi, gi), (y, s_new) in zip(chains, _wkv_chunk(groups, masks)):
        y_ref[bi, :, lanes(gi)] = y
        s_ref[bi, gi] = s_new

    @pl.when(ci == pl.num_programs(0) - 1)
    def _():
        sout_ref[...] = s_ref[...]


def _wkv_prompt(r, lw, k2, v, a_s, b_s):
    b, t, _ = r.shape
    spec = pl.BlockSpec((b, CHUNK, D_RWKV), lambda ci: (0, ci, 0))
    state = (b, N_GROUPS, MXU_DIM_V7X, MXU_DIM_V7X)
    return pl.pallas_call(
        _wkv_prompt_kernel,
        grid=(t // CHUNK,),
        in_specs=[spec] * 6,
        out_specs=[spec, _const_spec(state)],
        out_shape=[jax.ShapeDtypeStruct((b, t, D_RWKV), F32), jax.ShapeDtypeStruct(state, F32)],
        scratch_shapes=[pltpu.VMEM(state, F32)],
        compiler_params=_params("arbitrary"),
        name="wkv_prompt",
    )(r, lw, k2, v, a_s, b_s)


SLAB = 8


def _slab_pair(ref, r0, c):
    return jnp.concatenate([ref[pl.ds(r0, SLAB), :], ref[pl.ds(c + r0, SLAB), :]], axis=0)


def _wkv_sample_kernel(seq, r_ref, lw_ref, k2_ref, v_ref, as_ref, bs_ref, s0_ref, y_ref, sout_ref,
                       x_ref, btk_ref, bk_ref, p_ref, uv_ref, gam_ref):
    c = r_ref.shape[0]
    per_slab = SLAB // seq

    lw = lw_ref[...]
    tpos = lax.broadcasted_iota(jnp.int32, lw.shape, 0) % seq
    cl = lw
    s = 1
    while s < seq:
        cl = cl + jnp.where(tpos >= s, pltpu.roll(cl, s, 0), 0.0)
        s *= 2
    tot = jnp.where(tpos == seq - 1, cl, 0.0)
    s = 1
    while s < seq:
        tot = tot + pltpu.roll(tot, c - s, 0)
        s *= 2
    e_neg = jnp.exp(-cl)
    e_rem = jnp.exp(tot - cl)
    b_s = bs_ref[...]
    k2 = k2_ref[...]
    x_ref[0:c, :] = as_ref[...] * jnp.exp(cl - lw)
    x_ref[c:2 * c, :] = r_ref[...] * jnp.exp(cl)
    btk_ref[0:c, :] = b_s * e_neg
    btk_ref[c:2 * c, :] = k2 * e_neg
    bk_ref[0:c, :] = b_s * e_rem
    bk_ref[c:2 * c, :] = k2 * e_rem
    gam_ref[...] = jnp.exp(tot)

    first_seq = (lax.broadcasted_iota(jnp.int32, (2 * SLAB, HEAD_DIM), 0) % SLAB) < seq

    def state_in(p, carry):
        r0 = pl.multiple_of(p * SLAB, SLAB)
        xp = _slab_pair(x_ref, r0, c)
        outs = []
        for h in range(N_HEADS):
            xs = xp[:, HEAD_DIM * h:HEAD_DIM * (h + 1)]
            p0 = _mm_nt(xs, s0_ref[per_slab * p, h])
            p1 = _mm_nt(xs, s0_ref[per_slab * p + 1, h])
            outs.append(jnp.where(first_seq, p0, p1))
        pp = jnp.concatenate(outs, axis=1)
        p_ref[pl.ds(r0, SLAB), :] = pp[0:SLAB]
        p_ref[pl.ds(c + r0, SLAB), :] = pp[SLAB:2 * SLAB]
        return carry

    lax.fori_loop(0, c // SLAB, state_in, 0)

    masks = _wkv_masks(c, seq)
    sls = [slice(MXU_DIM_V7X * gi, MXU_DIM_V7X * (gi + 1)) for gi in range(N_GROUPS)]
    ys, us = _wkv_intra([x_ref[:, sl] for sl in sls], [btk_ref[0:c, sl] for sl in sls],
                        [btk_ref[c:2 * c, sl] for sl in sls], [v_ref[:, sl] for sl in sls],
                        [p_ref[:, sl] for sl in sls], masks, seq)
    for sl, y, u in zip(sls, ys, us):
        y_ref[:, sl] = y
        uv_ref[0:c, sl] = u
    uv_ref[c:2 * c, :] = v_ref[...]

    first_rows = (lax.broadcasted_iota(jnp.int32, (2 * SLAB, D_RWKV), 0) % SLAB) < seq

    def state_out(p, carry):
        r0 = pl.multiple_of(p * SLAB, SLAB)
        uvp = _slab_pair(uv_ref, r0, c)
        bkp = _slab_pair(bk_ref, r0, c)
        gam8 = gam_ref[pl.ds(r0, SLAB), :]
        for bb in range(per_slab):
            uvm = jnp.where(first_rows if bb == 0 else jnp.logical_not(first_rows), uvp, 0.0)
            gam = gam8[seq * bb:seq * bb + 1, :]
            for j in range(N_HEADS // 2):
                ls = slice(2 * HEAD_DIM * j, 2 * HEAD_DIM * (j + 1))
                ds2 = _mm(uvm[:, ls].T, bkp[:, ls])
                for hh in range(2):
                    h = 2 * j + hh
                    hs = slice(HEAD_DIM * hh, HEAD_DIM * (hh + 1))
                    sout_ref[per_slab * p + bb, h] = (
                        s0_ref[per_slab * p + bb, h] * gam[:, HEAD_DIM * h:HEAD_DIM * (h + 1)] + ds2[hs, hs])
        return carry

    lax.fori_loop(0, c // SLAB, state_out, 0)


def _wkv_sample(r, lw, k2, v, a_s, b_s, s0, seq):
    n = r.shape[0]
    assert SLAB % seq == 0 and SLAB // seq == 2 and n % CHUNK == 0
    nb = CHUNK // seq
    vec = pl.BlockSpec((CHUNK, D_RWKV), lambda i: (i, 0))
    st = pl.BlockSpec((nb, N_HEADS, HEAD_DIM, HEAD_DIM), lambda i: (i, 0, 0, 0))
    return pl.pallas_call(
        functools.partial(_wkv_sample_kernel, seq),
        grid=(n // CHUNK,),
        in_specs=[vec] * 6 + [st],
        out_specs=[vec, st],
        out_shape=[jax.ShapeDtypeStruct((n, D_RWKV), F32), jax.ShapeDtypeStruct(s0.shape, F32)],
        scratch_shapes=[pltpu.VMEM((2 * CHUNK, D_RWKV), F32)] * 5 + [pltpu.VMEM((CHUNK, D_RWKV), F32)],
        compiler_params=_params("parallel"),
        name="wkv_sample",
    )(r, lw, k2, v, a_s, b_s, s0)


HEAD_PAIR = 2 * HEAD_DIM


def _wkv_steps_kernel(seq, r_ref, lw_ref, k2_ref, v_ref, as_ref, bs_ref, s0_ref, y_ref, sout_ref,
                      op_ref, vt_ref, yt_ref):
    nb = r_ref.shape[0] // seq
    for t in range(seq):
        rows = pl.ds(t, nb, stride=seq)
        op_ref[0, t] = r_ref[rows, :].T
        op_ref[1, t] = jnp.exp(lw_ref[rows, :]).T
        op_ref[2, t] = k2_ref[rows, :].T
        op_ref[3, t] = as_ref[rows, :].T
        op_ref[4, t] = bs_ref[rows, :].T
        vt_ref[t] = v_ref[rows, :].T

    for hh in range(HEAD_PAIR // HEAD_DIM):
        ks = slice(HEAD_DIM * hh, HEAD_DIM * (hh + 1))

        def body(vi, carry):
            s = s0_ref[hh, vi]
            for t in range(seq):
                sa = jnp.sum(s * op_ref[3, t, ks, :], axis=0, keepdims=True)
                vrow = vt_ref[t, pl.ds(HEAD_DIM * hh + vi, 1), :]
                s = s * op_ref[1, t, ks, :] + sa * op_ref[4, t, ks, :] + vrow * op_ref[2, t, ks, :]
                yt_ref[t, pl.ds(HEAD_DIM * hh + vi, 1), :] = jnp.sum(s * op_ref[0, t, ks, :], axis=0,
                                                                      keepdims=True)
            sout_ref[hh, vi] = s
            return carry

        lax.fori_loop(0, HEAD_DIM, body, 0, unroll=4)

    for t in range(seq):
        y_ref[pl.ds(t, nb, stride=seq), :] = yt_ref[t].T


def _wkv_steps(r, lw, k2, v, a_s, b_s, s0_native, seq):
    n = r.shape[0]
    nb = n // seq
    assert s0_native.shape == (N_HEADS, HEAD_DIM, HEAD_DIM, nb) and nb == LANES_V7X
    vec = pl.BlockSpec((n, HEAD_PAIR), lambda i: (0, i))
    st = pl.BlockSpec((HEAD_PAIR // HEAD_DIM, HEAD_DIM, HEAD_DIM, nb), lambda i: (i, 0, 0, 0))
    return pl.pallas_call(
        functools.partial(_wkv_steps_kernel, seq),
        grid=(N_HEADS * HEAD_DIM // HEAD_PAIR,),
        in_specs=[vec] * 6 + [st],
        out_specs=[vec, st],
        out_shape=[jax.ShapeDtypeStruct((n, D_RWKV), F32), jax.ShapeDtypeStruct(s0_native.shape, F32)],
        scratch_shapes=[pltpu.VMEM((5, seq, HEAD_PAIR, nb), F32), pltpu.VMEM((seq, HEAD_PAIR, nb), F32),
                        pltpu.VMEM((seq, HEAD_PAIR, nb), F32)],
        compiler_params=_params("parallel"),
        name="wkv_steps",
    )(r, lw, k2, v, a_s, b_s, s0_native)


def _pool_prompt_kernel(zp_ref, d_ref, carry_ref):
    i = pl.program_id(1)
    tm = zp_ref.shape[0]

    @pl.when(i == 0)
    def _():
        carry_ref[...] = jnp.zeros_like(carry_ref)

    zp = zp_ref[...]
    buf = jnp.concatenate([carry_ref[...], zp], axis=0)
    carry_ref[...] = zp[tm - 16:, :]
    pos = (i * tm + lax.broadcasted_iota(jnp.int32, (tm, POOL_GROUP), 0) + 1).astype(F32)
    for gi, win in enumerate(POOL_WINDOWS):
        sl = slice(POOL_GROUP * gi, POOL_GROUP * (gi + 1))
        acc = buf[:, sl]
        s = 1
        while s < win:
            acc = acc + pltpu.roll(acc, s, 0)
            s *= 2
        cnt = jnp.minimum(float(win), pos)
        d_ref[:, sl] = acc[16:, :] / cnt - zp[:, sl]


def _pool_prompt(zp, b, t, tm):
    tiles = t // tm
    return pl.pallas_call(
        _pool_prompt_kernel,
        grid=(b, tiles),
        in_specs=[pl.BlockSpec((tm, D_POOL), lambda bi, i: (bi * tiles + i, 0))],
        out_specs=pl.BlockSpec((None, tm, D_POOL), lambda bi, i: (bi, i, 0)),
        out_shape=jax.ShapeDtypeStruct((b, t, D_POOL), F32),
        scratch_shapes=[pltpu.VMEM((16, D_POOL), F32)],
        compiler_params=_params("parallel", "arbitrary"),
        name="pool_prompt",
    )(zp)


def _pool_sample_kernel(seq, hist_ref, zp_ref, d_ref, new_ref):
    nb = hist_ref.shape[1]
    step = lambda t: zp_ref[pl.ds(t, nb, stride=seq), :]
    past = lambda j: step(j) if j >= 0 else hist_ref[POOL_HIST + j]
    tiles_per_group = POOL_GROUP // LANES_V7X
    for gi, win in enumerate(POOL_WINDOWS):
        @pl.when(pl.program_id(0) // tiles_per_group == gi)
        def _():
            for t in range(seq):
                acc = past(t)
                for j in range(1, win):
                    acc = acc + past(t - j)
                d_ref[pl.ds(t, nb, stride=seq), :] = acc / float(min(win, PAST_LEN + 1)) - past(t)
    for j in range(POOL_HIST):
        src = j + seq
        new_ref[j] = hist_ref[src] if src < POOL_HIST else step(src - POOL_HIST)


def _pool_sample(hist, zp, row0, seq):
    nb = hist.shape[1]
    n = nb * seq
    assert row0 % n == 0
    hist_spec = pl.BlockSpec((POOL_HIST, nb, LANES_V7X), lambda j: (0, 0, j))
    return pl.pallas_call(
        functools.partial(_pool_sample_kernel, seq),
        grid=(D_POOL // LANES_V7X,),
        in_specs=[hist_spec, pl.BlockSpec((n, LANES_V7X), lambda j: (row0 // n, j))],
        out_specs=[pl.BlockSpec((n, LANES_V7X), lambda j: (0, j)), hist_spec],
        out_shape=[jax.ShapeDtypeStruct((n, D_POOL), F32), jax.ShapeDtypeStruct(hist.shape, F32)],
        compiler_params=_params("parallel"),
        name="pool_sample",
    )(hist, zp)


def _merge_kernel(y_ref, r_ref, k2_ref, v_ref, g_ref, d_ref, zga_ref, zgb_ref,
                  rk_ref, lw_ref, lb_ref, ones_ref, wa_ref, pw_ref, ps_ref, wb_ref, m_ref):
    ones_bd = ones_ref[...]
    y = y_ref[...]
    mu = _head_sum(y, ones_bd) * (1.0 / HEAD_DIM)
    yc = y - mu
    var = _head_sum(yc * yc, ones_bd) * (1.0 / HEAD_DIM)
    yn = yc * lax.rsqrt(var + GN_EPS) * lw_ref[...] + lb_ref[...]
    v = v_ref[...]
    bonus = _head_sum(r_ref[...] * k2_ref[...] * rk_ref[...], ones_bd) * v
    ya = (yn + bonus) * g_ref[...]
    d = d_ref[...]
    yb = jnp.concatenate(
        [_mm(d[:, POOL_GROUP * gi:POOL_GROUP * (gi + 1)], pw_ref[gi]) for gi in range(len(POOL_WINDOWS))],
        axis=1) * ps_ref[...]
    m_ref[...] = (jax.nn.sigmoid(zga_ref[...]) * _mm(ya, wa_ref[...])
                  + jax.nn.sigmoid(zgb_ref[...]) * _mm(yb, wb_ref[...]))


def _merge(y, r, k2, v, g, d, zga, zgb, row0, wts, tm):
    n = y.shape[0]
    assert row0 % tm == 0
    first = row0 // tm
    half = pl.BlockSpec((tm, D_RWKV), lambda i: (i, 0))
    full = pl.BlockSpec((tm, D_MODEL), lambda i: (i, 0))
    gate = pl.BlockSpec((tm, D_MODEL), lambda i: (first + i, 0))
    return pl.pallas_call(
        _merge_kernel,
        grid=(n // tm,),
        in_specs=[half] * 6 + [gate] * 2 + [_const_spec(w.shape) for w in wts],
        out_specs=full,
        out_shape=jax.ShapeDtypeStruct((n, D_MODEL), F32),
        compiler_params=_params("parallel"),
        name="merge",
    )(y, r, k2, v, g, d, zga, zgb, *wts)


def _mix_out_kernel(m_ref, x_ref, w_ref, g_ref, o_ref):
    mo = _mm(m_ref[...], w_ref[...])
    o_ref[...] = x_ref[...] + _rmsnorm(mo, g_ref[...])


def _mix_out(m, x2d, w_out, g, tm):
    n = m.shape[0]
    full = pl.BlockSpec((tm, D_MODEL), lambda i: (i, 0))
    return pl.pallas_call(
        _mix_out_kernel,
        grid=(n // tm,),
        in_specs=[full, full, _const_spec(w_out.shape), _const_spec(g.shape)],
        out_specs=full,
        out_shape=jax.ShapeDtypeStruct((n, D_MODEL), F32),
        compiler_params=_params("parallel"),
        name="mix_out",
    )(m, x2d, w_out, g)


def _gelu_tanh(x):
    return 0.5 * x * (1.0 + jnp.tanh(0.7978845608028654 * (x + 0.044715 * x * x * x)))


def _ffn_body(x_ref, gn_ref, wg_ref, wu_ref, cw_ref, cb_ref, wo_ref, gp_ref,
              o_ref, h_ref, acc_ref, prev1, prev2, emit_tail):
    c = pl.program_id(1)

    @pl.when(c == 0)
    def _():
        h_ref[...] = _rmsnorm(x_ref[...], gn_ref[...]).astype(BF16)
        acc_ref[...] = jnp.zeros_like(acc_ref)

    h = h_ref[...]
    gate = jnp.dot(h, wg_ref[...], preferred_element_type=F32)
    up = jnp.dot(h, wu_ref[...], preferred_element_type=F32)
    cw = cw_ref[...]
    cv = cb_ref[...] + cw[0:1, :] * prev2(gate) + cw[1:2, :] * prev1(gate) + cw[2:3, :] * gate
    emit_tail(gate)
    acc_ref[...] += _mm(_gelu_tanh(cv) * up, wo_ref[...])

    @pl.when(c == pl.num_programs(1) - 1)
    def _():
        o_ref[...] = x_ref[...] + _rmsnorm(acc_ref[...], gp_ref[...])


def _ffn_prompt_kernel(tiles_per_seq, x_ref, gn_ref, wg_ref, wu_ref, cw_ref, cb_ref, wo_ref, gp_ref,
                       o_ref, tail_ref, h_ref, acc_ref, carry_ref):
    i = pl.program_id(0)
    c = pl.program_id(1)

    @pl.when(i % tiles_per_seq == 0)
    def _():
        carry_ref[c] = jnp.zeros(carry_ref.shape[1:], F32)

    hist = carry_ref[c]

    def shifted(gate, s):
        row = lax.broadcasted_iota(jnp.int32, gate.shape, 0)
        rolled = pltpu.roll(gate, s, 0)
        out = rolled
        for j in range(s):
            out = jnp.where(row == j, hist[8 - s + j:9 - s + j, :], out)
        return out

    def emit_tail(gate):
        tail_ref[...] = gate[gate.shape[0] - tail_ref.shape[0]:, :]

    _ffn_body(x_ref, gn_ref, wg_ref, wu_ref, cw_ref, cb_ref, wo_ref, gp_ref, o_ref,
              h_ref, acc_ref, lambda g: shifted(g, 1), lambda g: shifted(g, 2), emit_tail)
    carry_ref[c] = tail_ref[...]


def _ffn_sample_kernel(seq, x_ref, st_ref, gn_ref, wg_ref, wu_ref, cw_ref, cb_ref, wo_ref, gp_ref,
                       o_ref, tail_ref, h_ref, acc_ref, g_ref, p1_ref, p2_ref):
    nb = st_ref.shape[0]
    st0 = st_ref[:, 0, :]
    st1 = st_ref[:, 1, :]
    prev1 = lambda gate: _replace_step_rows(p1_ref, pltpu.roll(gate, 1, 0), seq, [(0, st1)])
    prev2 = lambda gate: _replace_step_rows(p2_ref, pltpu.roll(gate, 2, 0), seq, [(0, st0), (1, st1)])

    def emit_tail(gate):
        tiles = gate.shape[1] // LANES_V7X
        for j in range(tiles):
            g_ref[j] = gate[:, LANES_V7X * j:LANES_V7X * (j + 1)]
        for s in range(CONV_W - 1):
            rows = pl.ds(seq - (CONV_W - 1) + s, nb, stride=seq)
            tail_ref[:, s, :] = jnp.concatenate([g_ref[j, rows, :] for j in range(tiles)], axis=1)

    _ffn_body(x_ref, gn_ref, wg_ref, wu_ref, cw_ref, cb_ref, wo_ref, gp_ref, o_ref,
              h_ref, acc_ref, prev1, prev2, emit_tail)


def _ffn_prompt(x2d, seq_len, wts, tm, fk):
    n = x2d.shape[0]
    nc = D_FF // fk
    gn, w_in, cw, cb, w_out, gp = wts
    full = pl.BlockSpec((tm, D_MODEL), lambda i, c: (i, 0))
    return pl.pallas_call(
        functools.partial(_ffn_prompt_kernel, seq_len // tm),
        grid=(n // tm, nc),
        in_specs=[full, _const_spec(gn.shape),
                  pl.BlockSpec((D_MODEL, fk), lambda i, c: (0, c)),
                  pl.BlockSpec((D_MODEL, fk), lambda i, c: (0, c + nc)),
                  pl.BlockSpec((CONV_W, fk), lambda i, c: (0, c)),
                  pl.BlockSpec((1, fk), lambda i, c: (0, c)),
                  pl.BlockSpec((fk, D_MODEL), lambda i, c: (c, 0)),
                  _const_spec(gp.shape)],
        out_specs=[full, pl.BlockSpec((None, 8, fk), lambda i, c: (i, 0, c))],
        out_shape=[jax.ShapeDtypeStruct((n, D_MODEL), F32),
                   jax.ShapeDtypeStruct((n // tm, 8, D_FF), F32)],
        scratch_shapes=[pltpu.VMEM((tm, D_MODEL), BF16), pltpu.VMEM((tm, D_MODEL), F32),
                        pltpu.VMEM((nc, 8, fk), F32)],
        compiler_params=_params("arbitrary", "arbitrary"),
        name="ffn_prompt",
    )(x2d, gn, w_in, w_in, cw, cb, w_out, gp)


def _ffn_sample(x2d, st, seq, wts, fk):
    n = x2d.shape[0]
    nb = n // seq
    assert seq >= CONV_W - 1 and st.shape == (nb, CONV_W - 1, D_FF)
    nc = D_FF // fk
    gn, w_in, cw, cb, w_out, gp = wts
    full = pl.BlockSpec((n, D_MODEL), lambda i, c: (0, 0))
    cols = pl.BlockSpec((nb, CONV_W - 1, fk), lambda i, c: (0, 0, c))
    return pl.pallas_call(
        functools.partial(_ffn_sample_kernel, seq),
        grid=(1, nc),
        in_specs=[full, cols, _const_spec(gn.shape),
                  pl.BlockSpec((D_MODEL, fk), lambda i, c: (0, c)),
                  pl.BlockSpec((D_MODEL, fk), lambda i, c: (0, c + nc)),
                  pl.BlockSpec((CONV_W, fk), lambda i, c: (0, c)),
                  pl.BlockSpec((1, fk), lambda i, c: (0, c)),
                  pl.BlockSpec((fk, D_MODEL), lambda i, c: (c, 0)),
                  _const_spec(gp.shape)],
        out_specs=[full, cols],
        out_shape=[jax.ShapeDtypeStruct((n, D_MODEL), F32), jax.ShapeDtypeStruct(st.shape, F32)],
        scratch_shapes=[pltpu.VMEM((n, D_MODEL), BF16), pltpu.VMEM((n, D_MODEL), F32)]
        + [pltpu.VMEM((fk // LANES_V7X, n, LANES_V7X), F32)] * 3,
        compiler_params=_params("arbitrary", "arbitrary"),
        name="ffn_sample",
    )(x2d, st, gn, w_in, w_in, cw, cb, w_out, gp)


def _row(v):
    return v.reshape(1, -1).astype(F32)


def _ones_bd():
    i = jnp.arange(MXU_DIM_V7X) // HEAD_DIM
    return (i[:, None] == i[None, :]).astype(BF16)


def _layer_weights(l, norm_pre_mix, w_in, mu_shift, w0, w2, a0, a2, g2, k_k, k_a, r_k, lnx_w, lnx_b,
                   w_branch_a, pool_w, pool_scale, w_branch_b, w_out, norm_post_mix,
                   norm_pre_ffn, w_ffn_in, conv_w, conv_b, w_ffn_out, norm_post_ffn):
    w_t = jnp.swapaxes(w_in[l], 0, 1)
    mu = jnp.pad(mu_shift[l], (0, D_SHIFT_PAD - D_SHIFT)).reshape(1, -1)
    w_lora = jnp.zeros((D_LORA_PAD, 3 * D_RWKV), F32)
    w_lora = w_lora.at[0:RANK_W, 0:D_RWKV].set(w2[l])
    w_lora = w_lora.at[RANK_W:RANK_W + RANK_A, D_RWKV:2 * D_RWKV].set(a2[l])
    w_lora = w_lora.at[RANK_W + RANK_A:D_LORA, 2 * D_RWKV:].set(g2[l])
    ones_bd = _ones_bd()
    return dict(
        in_proj=(_row(norm_pre_mix[l]), w_t),
        prep=(mu, w_lora.astype(BF16), _row(w0[l]), _row(a0[l]), _row(k_k[l]), _row(k_a[l]), ones_bd),
        merge=(_row(r_k[l]), _row(lnx_w[l]), _row(lnx_b[l]), ones_bd, w_branch_a[l].astype(BF16),
               pool_w[l].astype(BF16), _row(pool_scale[l]), w_branch_b[l].astype(BF16)),
        mix_out=(w_out[l].astype(BF16), _row(norm_post_mix[l])),
        ffn=(_row(norm_pre_ffn[l]), w_ffn_in[l].astype(BF16), conv_w[l].astype(F32), _row(conv_b[l]),
             w_ffn_out[l].astype(BF16), _row(norm_post_ffn[l])),
    )


def _largest_tile(n, cap, mult=16):
    best = None
    for d in range(mult, min(n, cap) + 1, mult):
        if n % d == 0:
            best = d
    assert best is not None, (n, cap)
    return best


def _project(xp2d, xs2d, wts):
    g, w_t = wts["in_proj"]
    tm_norm = _largest_tile(math.gcd(xp2d.shape[0], xs2d.shape[0]), 512)
    h = _pre_norm(xp2d, xs2d, g, tm_norm)
    tm = _largest_tile(h.shape[0], 1088)
    zrkv = _in_proj(h, w_t, 0, 3 * D_RWKV, D_RWKV, tm)
    zl = _in_proj(h, w_t, 3 * D_RWKV, D_LORA_PAD, D_LORA_PAD, tm)
    zp = _in_proj(h, w_t, D_SHIFT, D_POOL, D_POOL, tm)
    zga = _in_proj(h, w_t, D_SHIFT + D_POOL, D_MODEL, D_MODEL // 2, tm)
    zgb = _in_proj(h, w_t, D_SHIFT + D_POOL + D_MODEL, D_MODEL, D_MODEL // 2, tm)
    return zrkv, zl, zp, zga, zgb


def _last_shift_row(zrkv, zl, row0, b, t):
    last = lambda z, w: lax.slice(z, (row0 + t - 1, 0), (row0 + b * t, w), (t, 1))
    return jnp.concatenate([last(zrkv, 3 * D_RWKV), last(zl, D_LORA)], axis=-1)[:, None, :]


def _prompt_layer(x, z, wts):
    b, t, _ = x.shape
    n = b * t
    x2d = x.reshape(n, D_MODEL)
    zrkv, zl, zp, zga, zgb = z
    r, lw, k2, v, a_s, b_s, g = _prep_prompt(zrkv, zl, b, t, wts["prep"], tm=256)
    y, s_bd = _wkv_prompt(r, lw, k2, v, a_s, b_s)
    d = _pool_prompt(zp, b, t, tm=512)
    flat = lambda a: a.reshape(n, a.shape[-1])
    m = _merge(flat(y), flat(r), flat(k2), flat(v), flat(g), flat(d), zga, zgb, 0, wts["merge"], tm=256)
    x1 = _mix_out(m, x2d, *wts["mix_out"], tm=512)
    ffn_tm = _largest_tile(t, 512)
    out, tail = _ffn_prompt(x1, t, wts["ffn"], tm=ffn_tm, fk=512)
    shift = _last_shift_row(zrkv, zl, 0, b, t)
    s5 = s_bd.reshape(b, N_GROUPS, HEADS_PER_GROUP, HEAD_DIM, HEADS_PER_GROUP, HEAD_DIM)
    idx = jnp.arange(HEADS_PER_GROUP)
    wkv = s5[:, :, idx, :, idx, :]
    wkv = jnp.moveaxis(wkv, 0, 2).reshape(b, N_HEADS, HEAD_DIM, HEAD_DIM)
    pool = jnp.stack([zp[(i + 1) * t - POOL_HIST:(i + 1) * t] for i in range(b)])
    tiles = t // ffn_tm
    conv = tail.reshape(b, tiles, 8, D_FF)[:, tiles - 1, 8 - (CONV_W - 1):, :]
    return out.reshape(b, t, D_MODEL), shift, wkv, pool, conv


def _sample_layer(x, z, row0, st_shift, st_wkv, st_pool, st_conv, wts):
    b, t, _ = x.shape
    n = b * t
    x2d = x.reshape(n, D_MODEL)
    zrkv, zl, zp, zga, zgb = z
    st = jnp.pad(st_shift.reshape(b, D_SHIFT), ((0, 0), (0, D_SHIFT_PAD - D_SHIFT)))
    tm = _largest_tile(math.gcd(row0, n), 128, mult=SLAB * t)
    r, lw, k2, v, a_s, b_s, g = _prep_sample(zrkv, zl, row0, n, st, t, wts["prep"], tm=tm)
    y, wkv = _wkv_steps(r, lw, k2, v, a_s, b_s, jnp.transpose(st_wkv, (1, 2, 3, 0)), t)
    wkv = jnp.transpose(wkv, (3, 0, 1, 2))
    d, pool = _pool_sample(jnp.swapaxes(st_pool, 0, 1), zp, row0, t)
    pool = jnp.swapaxes(pool, 0, 1)
    m = _merge(y, r, k2, v, g, d, zga, zgb, row0, wts["merge"], tm=_largest_tile(math.gcd(row0, n), 256, mult=8))
    x1 = _mix_out(m, x2d, *wts["mix_out"], tm=n)
    out, conv = _ffn_sample(x1, st_conv, t, wts["ffn"], fk=512)
    shift = _last_shift_row(zrkv, zl, row0, b, t)
    return out.reshape(b, t, D_MODEL), shift, wkv, pool, conv


def kernel(x_prompt, x_sample, state_shift, state_wkv, state_pool, state_conv, norm_pre_mix, w_in, mu_shift, w0, w2, a0, a2, g2, k_k, k_a, r_k, lnx_w, lnx_b, w_branch_a, pool_w, pool_scale, w_branch_b, w_out, norm_post_mix, norm_pre_ffn, w_ffn_in, conv_w, conv_b, w_ffn_out, norm_post_ffn):
    weights = (norm_pre_mix, w_in, mu_shift, w0, w2, a0, a2, g2, k_k, k_a, r_k, lnx_w, lnx_b,
               w_branch_a, pool_w, pool_scale, w_branch_b, w_out, norm_post_mix,
               norm_pre_ffn, w_ffn_in, conv_w, conv_b, w_ffn_out, norm_post_ffn)
    depth = w_in.shape[0]
    yp, ys = x_prompt, x_sample
    p_states, s_states = [], []
    for l in range(depth):
        wts = _layer_weights(l, *weights)
        n_prompt = yp.shape[0] * yp.shape[1]
        z = _project(yp.reshape(n_prompt, D_MODEL), ys.reshape(-1, D_MODEL), wts)
        yp, *ps = _prompt_layer(yp, z, wts)
        ys, *ss = _sample_layer(ys, z, n_prompt, state_shift[l], state_wkv[l], state_pool[l], state_conv[l], wts)
        p_states.append(ps)
        s_states.append(ss)
    stack = lambda states, i: jnp.stack([s[i] for s in states])
    return (yp, ys,
            stack(p_states, 0), stack(p_states, 1), stack(p_states, 2), stack(p_states, 3),
            stack(s_states, 0), stack(s_states, 1), stack(s_states, 2), stack(s_states, 3))
```

```python
import functools
import math

import jax
import jax.numpy as jnp
from jax import lax
from jax.experimental import pallas as pl
from jax.experimental.pallas import tpu as pltpu

F32 = jnp.float32
BF16 = jnp.bfloat16

D_MODEL = 2048
HEAD_DIM = 64
D_RWKV = 1024
N_HEADS = 16
RANK_W, RANK_A, RANK_G = 64, 64, 160
D_LORA = RANK_W + RANK_A + RANK_G
D_LORA_PAD = 384
D_SHIFT = 3 * D_RWKV + D_LORA
D_SHIFT_PAD = 3 * D_RWKV + D_LORA_PAD
D_POOL = 1024
POOL_WINDOWS = (2, 4, 8, 16)
POOL_GROUP = 256
POOL_HIST = 15
D_FF = 5632
CONV_W = 3
NORM_EPS = 1e-6
GN_EPS = 64e-5
PAST_LEN = 16384

LANES_V7X = 128
MXU_DIM_V7X = 256
HEADS_PER_GROUP = MXU_DIM_V7X // HEAD_DIM
N_GROUPS = N_HEADS // HEADS_PER_GROUP
CHUNK = 64
VMEM_LIMIT_V7X = 56 * 1024 * 1024


def _params(*sem):
    return pltpu.CompilerParams(dimension_semantics=sem, vmem_limit_bytes=VMEM_LIMIT_V7X)


def _mm(a, b):
    return jnp.dot(a.astype(BF16), b.astype(BF16), preferred_element_type=F32)


def _mm_nt(a, b):
    return lax.dot_general(a.astype(BF16), b.astype(BF16), (((1,), (1,)), ((), ())),
                           preferred_element_type=F32)


def _split_hi_lo(x):
    hi = x.astype(BF16)
    lo = (x - hi.astype(F32)).astype(BF16)
    return hi, lo


def _head_sum(x, ones_bd):
    hi, lo = _split_hi_lo(x)
    outs = []
    for gi in range(x.shape[1] // MXU_DIM_V7X):
        sl = slice(MXU_DIM_V7X * gi, MXU_DIM_V7X * (gi + 1))
        outs.append(jnp.dot(hi[:, sl], ones_bd, preferred_element_type=F32)
                    + jnp.dot(lo[:, sl], ones_bd, preferred_element_type=F32))
    return jnp.concatenate(outs, axis=1)


def _rmsnorm(x, g):
    return x * lax.rsqrt(jnp.mean(x * x, axis=-1, keepdims=True) + NORM_EPS) * g


def _replace_step_rows(scr_ref, base, seq, steps):
    nb = base.shape[0] // seq
    for j in range(base.shape[1] // LANES_V7X):
        ls = slice(LANES_V7X * j, LANES_V7X * (j + 1))
        scr_ref[j] = base[:, ls]
        for t, vals in steps:
            scr_ref[j, pl.ds(t, nb, stride=seq), :] = vals[:, ls]
    return jnp.concatenate([scr_ref[j] for j in range(base.shape[1] // LANES_V7X)], axis=1)


def _pre_norm_kernel(prompt_tiles, xp_ref, xs_ref, g_ref, h_ref):
    i = pl.program_id(0)

    @pl.when(i < prompt_tiles)
    def _():
        h_ref[...] = _rmsnorm(xp_ref[...], g_ref[...]).astype(BF16)

    @pl.when(i >= prompt_tiles)
    def _():
        h_ref[...] = _rmsnorm(xs_ref[...], g_ref[...]).astype(BF16)


def _pre_norm(xp, xs, g, tm):
    pt, st = xp.shape[0] // tm, xs.shape[0] // tm
    return pl.pallas_call(
        functools.partial(_pre_norm_kernel, pt),
        grid=(pt + st,),
        in_specs=[pl.BlockSpec((tm, D_MODEL), lambda i: (jnp.minimum(i, pt - 1), 0)),
                  pl.BlockSpec((tm, D_MODEL), lambda i: (jnp.maximum(i - pt, 0), 0)),
                  _const_spec(g.shape)],
        out_specs=pl.BlockSpec((tm, D_MODEL), lambda i: (i, 0)),
        out_shape=jax.ShapeDtypeStruct((xp.shape[0] + xs.shape[0], D_MODEL), BF16),
        compiler_params=_params("arbitrary"),
        name="pre_norm",
    )(xp, xs, g)


def _in_proj_kernel(h_ref, w_ref, o_ref, wb_ref):
    @pl.when(pl.program_id(1) == 0)
    def _():
        wb_ref[...] = w_ref[...].astype(BF16)

    o_ref[...] = lax.dot_general(h_ref[...], wb_ref[...], (((1,), (1,)), ((), ())),
                                 preferred_element_type=F32)


def _in_proj(h, w_t, off, width, tn, tm):
    n = h.shape[0]
    assert width % tn == 0 and n % tm == 0
    return pl.pallas_call(
        _in_proj_kernel,
        grid=(width // tn, n // tm),
        in_specs=[pl.BlockSpec((tm, D_MODEL), lambda j, i: (i, 0)),
                  pl.BlockSpec((pl.Element(tn), pl.Element(D_MODEL)),
                               lambda j, i: (SLAB * (off // SLAB + j * (tn // SLAB)), 0))],
        out_specs=pl.BlockSpec((tm, tn), lambda j, i: (i, j)),
        out_shape=jax.ShapeDtypeStruct((n, width), F32),
        scratch_shapes=[pltpu.VMEM((tn, D_MODEL), BF16)],
        compiler_params=_params("arbitrary", "arbitrary"),
        name="in_proj",
    )(h, w_t)


def _prep_math(zs, prev, mu, w_lora, w0, a0, k_k, k_a, ones_bd):
    f = zs + (prev - zs) * mu
    r = f[:, 0:D_RWKV]
    k = f[:, D_RWKV:2 * D_RWKV]
    v = f[:, 2 * D_RWKV:3 * D_RWKV]
    low = f[:, 3 * D_RWKV:3 * D_RWKV + D_LORA_PAD]
    lane = lax.broadcasted_iota(jnp.int32, low.shape, 1)
    act = jnp.where(lane < RANK_W, jnp.tanh(low),
                    jnp.where(lane < RANK_W + RANK_A, low,
                              jnp.where(lane < D_LORA, jax.nn.sigmoid(low), 0.0)))
    lora = _mm(act, w_lora)
    u = w0 + lora[:, 0:D_RWKV]
    lw = -math.exp(-0.5) * jax.nn.sigmoid(u)
    a = jax.nn.sigmoid(a0 + lora[:, D_RWKV:2 * D_RWKV])
    g = lora[:, 2 * D_RWKV:3 * D_RWKV]
    kk = k * k_k
    kk = kk * jnp.minimum(lax.rsqrt(_head_sum(kk * kk, ones_bd)), 1e12)
    k2 = k * (1.0 + (a - 1.0) * k_a)
    return r, lw, k2, v, -kk, kk * a, g


def _prep_prompt_kernel(zrkv_ref, zl_ref, mu_ref, wl_ref, w0_ref, a0_ref, kk_ref, ka_ref, ones_ref,
                        r_ref, lw_ref, k2_ref, v_ref, as_ref, bs_ref, g_ref, carry_ref):
    i = pl.program_id(1)

    @pl.when(i == 0)
    def _():
        carry_ref[...] = jnp.zeros_like(carry_ref)

    zs = jnp.concatenate([zrkv_ref[...], zl_ref[...]], axis=1)
    rolled = pltpu.roll(zs, 1, 0)
    row = lax.broadcasted_iota(jnp.int32, zs.shape, 0)
    prev = jnp.where(row == 0, carry_ref[0:1, :], rolled)
    carry_ref[0:1, :] = zs[zs.shape[0] - 1:, :]
    outs = _prep_math(zs, prev, mu_ref[...], wl_ref[...], w0_ref[...], a0_ref[...],
                      kk_ref[...], ka_ref[...], ones_ref[...])
    for o_ref, o in zip((r_ref, lw_ref, k2_ref, v_ref, as_ref, bs_ref, g_ref), outs):
        o_ref[...] = o


def _prep_sample_kernel(seq, zrkv_ref, zl_ref, st_ref, mu_ref, wl_ref, w0_ref, a0_ref, kk_ref, ka_ref,
                        ones_ref, r_ref, lw_ref, k2_ref, v_ref, as_ref, bs_ref, g_ref, prev_ref):
    zs = jnp.concatenate([zrkv_ref[...], zl_ref[...]], axis=1)
    prev = _replace_step_rows(prev_ref, pltpu.roll(zs, 1, 0), seq, [(0, st_ref[...])])
    outs = _prep_math(zs, prev, mu_ref[...], wl_ref[...], w0_ref[...], a0_ref[...],
                      kk_ref[...], ka_ref[...], ones_ref[...])
    for o_ref, o in zip((r_ref, lw_ref, k2_ref, v_ref, as_ref, bs_ref, g_ref), outs):
        o_ref[...] = o


def _const_spec(shape):
    nd = len(shape)
    return pl.BlockSpec(shape, lambda *_: (0,) * nd)


def _prep_prompt(zrkv, zl, b, t, wts, tm):
    tiles = t // tm
    outs = [jax.ShapeDtypeStruct((b, t, D_RWKV), F32)] * 7
    row_spec = pl.BlockSpec((None, tm, D_RWKV), lambda bi, i: (bi, i, 0))
    return pl.pallas_call(
        _prep_prompt_kernel,
        grid=(b, tiles),
        in_specs=[pl.BlockSpec((tm, 3 * D_RWKV), lambda bi, i: (bi * tiles + i, 0)),
                  pl.BlockSpec((tm, D_LORA_PAD), lambda bi, i: (bi * tiles + i, 0))]
        + [_const_spec(w.shape) for w in wts],
        out_specs=[row_spec] * 7,
        out_shape=outs,
        scratch_shapes=[pltpu.VMEM((8, D_SHIFT_PAD), F32)],
        compiler_params=_params("parallel", "arbitrary"),
        name="prep_prompt",
    )(zrkv, zl, *wts)


def _prep_sample(zrkv, zl, row0, n, st, seq, wts, tm):
    assert tm % (SLAB * seq) == 0 and n % tm == 0 and row0 % tm == 0
    first = row0 // tm
    outs = [jax.ShapeDtypeStruct((n, D_RWKV), F32)] * 7
    row_spec = pl.BlockSpec((tm, D_RWKV), lambda i: (i, 0))
    return pl.pallas_call(
        functools.partial(_prep_sample_kernel, seq),
        grid=(n // tm,),
        in_specs=[pl.BlockSpec((tm, 3 * D_RWKV), lambda i: (first + i, 0)),
                  pl.BlockSpec((tm, D_LORA_PAD), lambda i: (first + i, 0)),
                  pl.BlockSpec((tm // seq, D_SHIFT_PAD), lambda i: (i, 0))]
        + [_const_spec(w.shape) for w in wts],
        out_specs=[row_spec] * 7,
        out_shape=outs,
        scratch_shapes=[pltpu.VMEM((D_SHIFT_PAD // LANES_V7X, tm, LANES_V7X), F32)],
        compiler_params=_params("parallel"),
        name="prep_sample",
    )(zrkv, zl, st, *wts)


def _expand_bd(a, bd_mask):
    return jnp.where(bd_mask, jnp.concatenate([a] * HEADS_PER_GROUP, axis=0), 0.0)


def _chunk_cumsum(x):
    row = lax.broadcasted_iota(jnp.int32, x.shape, 0)
    s = 1
    while s < x.shape[0]:
        x = x + jnp.where(row >= s, pltpu.roll(x, s, 0), 0.0)
        s *= 2
    return x


def _wkv_intra(xs, bts, kts, vs, pzs, masks, order):
    bd_mask, strict_mask, incl_mask, eye_c, head_masks = masks
    c = vs[0].shape[0]
    ystacks = [jnp.concatenate([jnp.where(m, bt, 0.0) for m in head_masks]
                               + [jnp.where(m, kt, 0.0) for m in head_masks], axis=0)
               for bt, kt in zip(bts, kts)]
    grams = [_mm_nt(x, ys) for x, ys in zip(xs, ystacks)]
    l_abs = [jnp.where(strict_mask, g[0:c, 0:4 * c], 0.0) for g in grams]
    l_aks = [jnp.where(strict_mask, g[0:c, 4 * c:8 * c], 0.0) for g in grams]
    m_rbs = [jnp.where(incl_mask, g[c:2 * c, 0:4 * c], 0.0) for g in grams]
    m_rks = [jnp.where(incl_mask, g[c:2 * c, 4 * c:8 * c], 0.0) for g in grams]

    tinvs = [eye_c + l for l in l_abs]
    lps = [_mm(l, _expand_bd(l, bd_mask)) for l in l_abs]
    p = 2
    while True:
        rhss = [_expand_bd(lp, bd_mask) for lp in lps]
        if 2 * p >= order:
            tinvs = [t + _mm(t, rhs) for t, rhs in zip(tinvs, rhss)]
            break
        ress = [_mm(jnp.concatenate([lp, t], axis=0), rhs) for lp, t, rhs in zip(lps, tinvs, rhss)]
        lps = [res[0:c] for res in ress]
        tinvs = [t + res[c:2 * c] for t, res in zip(tinvs, ress)]
        p *= 2

    v_bds = [_expand_bd(v, bd_mask) for v in vs]
    ws = [pz[0:c] + _mm(l_ak, v_bd) for pz, l_ak, v_bd in zip(pzs, l_aks, v_bds)]
    us = [_mm(t, _expand_bd(w, bd_mask)) for t, w in zip(tinvs, ws)]
    ys = [pz[c:2 * c] + _mm(jnp.concatenate([m_rb, m_rk], axis=1),
                            jnp.concatenate([_expand_bd(u, bd_mask), v_bd], axis=0))
          for pz, m_rb, m_rk, u, v_bd in zip(pzs, m_rbs, m_rks, us, v_bds)]
    return ys, us


def _wkv_chunk(groups, masks):
    c = groups[0][0].shape[0]
    cls = [_chunk_cumsum(g[1]) for g in groups]
    xs = [jnp.concatenate([g[4] * jnp.exp(cl - g[1]), g[0] * jnp.exp(cl)], axis=0)
          for g, cl in zip(groups, cls)]
    pzs = [_mm_nt(x, g[6]) for x, g in zip(xs, groups)]
    e_negs = [jnp.exp(-cl) for cl in cls]
    ys, us = _wkv_intra(xs, [g[5] * e for g, e in zip(groups, e_negs)],
                        [g[2] * e for g, e in zip(groups, e_negs)], [g[3] for g in groups], pzs, masks, c)
    dss = []
    for g, cl, u in zip(groups, cls, us):
        e_rem = jnp.exp(cl[c - 1:c, :] - cl)
        uv_t = jnp.concatenate([u, g[3]], axis=0).T
        bk = jnp.concatenate([g[5] * e_rem, g[2] * e_rem], axis=0)
        dss.append(_mm(uv_t, bk))
    return [(y, g[6] * jnp.exp(cl[c - 1:c, :]) + jnp.where(masks[0], ds, 0.0))
            for y, g, cl, ds in zip(ys, groups, cls, dss)]


def _wkv_masks(c, seq):
    n = HEADS_PER_GROUP * c
    rr = lax.broadcasted_iota(jnp.int32, (n, n), 0)
    cc = lax.broadcasted_iota(jnp.int32, (n, n), 1)
    bd_mask = (rr // c) == (cc // c)
    t = lax.broadcasted_iota(jnp.int32, (c, n), 0)
    s = lax.broadcasted_iota(jnp.int32, (c, n), 1) % c
    same = (t // seq) == (s // seq)
    strict_mask = same & (t > s)
    incl_mask = same & (t >= s)
    eye_c = jnp.where(t == s, 1.0, 0.0).astype(F32)
    lane = lax.broadcasted_iota(jnp.int32, (c, MXU_DIM_V7X), 1)
    head_masks = [(lane // HEAD_DIM) == h for h in range(HEADS_PER_GROUP)]
    return bd_mask, strict_mask, incl_mask, eye_c, head_masks


def _wkv_prompt_kernel(r_ref, lw_ref, k2_ref, v_ref, as_ref, bs_ref, y_ref, sout_ref, s_ref):
    ci = pl.program_id(0)

    @pl.when(ci == 0)
    def _():
        s_ref[...] = jnp.zeros_like(s_ref)

    masks = _wkv_masks(CHUNK, CHUNK)
    chains = [(bi, gi) for bi in range(r_ref.shape[0]) for gi in range(N_GROUPS)]
    lanes = lambda gi: slice(MXU_DIM_V7X * gi, MXU_DIM_V7X * (gi + 1))
    groups = [tuple(ref[bi, :, lanes(gi)] for ref in (r_ref, lw_ref, k2_ref, v_ref, as_ref, bs_ref))
              + (s_ref[bi, gi],) for bi, gi in chains]
    for (bi, gi), (y, s_new) in zip(chains, _wkv_chunk(groups, masks)):
        y_ref[bi, :, lanes(gi)] = y
        s_ref[bi, gi] = s_new

    @pl.when(ci == pl.num_programs(0) - 1)
    def _():
        for bi, gi in chains:
            s_bd = s_ref[bi, gi]
            for h in range(HEADS_PER_GROUP):
                hs = slice(HEAD_DIM * h, HEAD_DIM * (h + 1))
                sout_ref[bi, HEADS_PER_GROUP * gi + h] = s_bd[hs, hs]


def _wkv_prompt(r, lw, k2, v, a_s, b_s):
    b, t, _ = r.shape
    spec = pl.BlockSpec((b, CHUNK, D_RWKV), lambda ci: (0, ci, 0))
    state = (b, N_GROUPS, MXU_DIM_V7X, MXU_DIM_V7X)
    out_state = (b, N_HEADS, HEAD_DIM, HEAD_DIM)
    return pl.pallas_call(
        _wkv_prompt_kernel,
        grid=(t // CHUNK,),
        in_specs=[spec] * 6,
        out_specs=[spec, _const_spec(out_state)],
        out_shape=[jax.ShapeDtypeStruct((b, t, D_RWKV), F32), jax.ShapeDtypeStruct(out_state, F32)],
        scratch_shapes=[pltpu.VMEM(state, F32)],
        compiler_params=_params("arbitrary"),
        name="wkv_prompt",
    )(r, lw, k2, v, a_s, b_s)


SLAB = 8


def _slab_pair(ref, r0, c):
    return jnp.concatenate([ref[pl.ds(r0, SLAB), :], ref[pl.ds(c + r0, SLAB), :]], axis=0)


def _wkv_sample_kernel(seq, r_ref, lw_ref, k2_ref, v_ref, as_ref, bs_ref, s0_ref, y_ref, sout_ref,
                       x_ref, btk_ref, bk_ref, p_ref, uv_ref, gam_ref):
    c = r_ref.shape[0]
    per_slab = SLAB // seq

    lw = lw_ref[...]
    tpos = lax.broadcasted_iota(jnp.int32, lw.shape, 0) % seq
    cl = lw
    s = 1
    while s < seq:
        cl = cl + jnp.where(tpos >= s, pltpu.roll(cl, s, 0), 0.0)
        s *= 2
    tot = jnp.where(tpos == seq - 1, cl, 0.0)
    s = 1
    while s < seq:
        tot = tot + pltpu.roll(tot, c - s, 0)
        s *= 2
    e_neg = jnp.exp(-cl)
    e_rem = jnp.exp(tot - cl)
    b_s = bs_ref[...]
    k2 = k2_ref[...]
    x_ref[0:c, :] = as_ref[...] * jnp.exp(cl - lw)
    x_ref[c:2 * c, :] = r_ref[...] * jnp.exp(cl)
    btk_ref[0:c, :] = b_s * e_neg
    btk_ref[c:2 * c, :] = k2 * e_neg
    bk_ref[0:c, :] = b_s * e_rem
    bk_ref[c:2 * c, :] = k2 * e_rem
    gam_ref[...] = jnp.exp(tot)

    first_seq = (lax.broadcasted_iota(jnp.int32, (2 * SLAB, HEAD_DIM), 0) % SLAB) < seq

    def state_in(p, carry):
        r0 = pl.multiple_of(p * SLAB, SLAB)
        xp = _slab_pair(x_ref, r0, c)
        outs = []
        for h in range(N_HEADS):
            xs = xp[:, HEAD_DIM * h:HEAD_DIM * (h + 1)]
            p0 = _mm_nt(xs, s0_ref[per_slab * p, h])
            p1 = _mm_nt(xs, s0_ref[per_slab * p + 1, h])
            outs.append(jnp.where(first_seq, p0, p1))
        pp = jnp.concatenate(outs, axis=1)
        p_ref[pl.ds(r0, SLAB), :] = pp[0:SLAB]
        p_ref[pl.ds(c + r0, SLAB), :] = pp[SLAB:2 * SLAB]
        return carry

    lax.fori_loop(0, c // SLAB, state_in, 0)

    masks = _wkv_masks(c, seq)
    sls = [slice(MXU_DIM_V7X * gi, MXU_DIM_V7X * (gi + 1)) for gi in range(N_GROUPS)]
    ys, us = _wkv_intra([x_ref[:, sl] for sl in sls], [btk_ref[0:c, sl] for sl in sls],
                        [btk_ref[c:2 * c, sl] for sl in sls], [v_ref[:, sl] for sl in sls],
                        [p_ref[:, sl] for sl in sls], masks, seq)
    for sl, y, u in zip(sls, ys, us):
        y_ref[:, sl] = y
        uv_ref[0:c, sl] = u
    uv_ref[c:2 * c, :] = v_ref[...]

    first_rows = (lax.broadcasted_iota(jnp.int32, (2 * SLAB, D_RWKV), 0) % SLAB) < seq

    def state_out(p, carry):
        r0 = pl.multiple_of(p * SLAB, SLAB)
        uvp = _slab_pair(uv_ref, r0, c)
        bkp = _slab_pair(bk_ref, r0, c)
        gam8 = gam_ref[pl.ds(r0, SLAB), :]
        for bb in range(per_slab):
            uvm = jnp.where(first_rows if bb == 0 else jnp.logical_not(first_rows), uvp, 0.0)
            gam = gam8[seq * bb:seq * bb + 1, :]
            for j in range(N_HEADS // 2):
                ls = slice(2 * HEAD_DIM * j, 2 * HEAD_DIM * (j + 1))
                ds2 = _mm(uvm[:, ls].T, bkp[:, ls])
                for hh in range(2):
                    h = 2 * j + hh
                    hs = slice(HEAD_DIM * hh, HEAD_DIM * (hh + 1))
                    sout_ref[per_slab * p + bb, h] = (
                        s0_ref[per_slab * p + bb, h] * gam[:, HEAD_DIM * h:HEAD_DIM * (h + 1)] + ds2[hs, hs])
        return carry

    lax.fori_loop(0, c // SLAB, state_out, 0)


def _wkv_sample(r, lw, k2, v, a_s, b_s, s0, seq):
    n = r.shape[0]
    assert SLAB % seq == 0 and SLAB // seq == 2 and n % CHUNK == 0
    nb = CHUNK // seq
    vec = pl.BlockSpec((CHUNK, D_RWKV), lambda i: (i, 0))
    st = pl.BlockSpec((nb, N_HEADS, HEAD_DIM, HEAD_DIM), lambda i: (i, 0, 0, 0))
    return pl.pallas_call(
        functools.partial(_wkv_sample_kernel, seq),
        grid=(n // CHUNK,),
        in_specs=[vec] * 6 + [st],
        out_specs=[vec, st],
        out_shape=[jax.ShapeDtypeStruct((n, D_RWKV), F32), jax.ShapeDtypeStruct(s0.shape, F32)],
        scratch_shapes=[pltpu.VMEM((2 * CHUNK, D_RWKV), F32)] * 5 + [pltpu.VMEM((CHUNK, D_RWKV), F32)],
        compiler_params=_params("parallel"),
        name="wkv_sample",
    )(r, lw, k2, v, a_s, b_s, s0)


HEAD_PAIR = 2 * HEAD_DIM


def _wkv_steps_kernel(seq, r_ref, lw_ref, k2_ref, v_ref, as_ref, bs_ref, s0_ref, y_ref, sout_ref,
                      op_ref, vt_ref, yt_ref):
    nb = r_ref.shape[0] // seq
    for t in range(seq):
        rows = pl.ds(t, nb, stride=seq)
        op_ref[0, t] = r_ref[rows, :].T
        op_ref[1, t] = jnp.exp(lw_ref[rows, :]).T
        op_ref[2, t] = k2_ref[rows, :].T
        op_ref[3, t] = as_ref[rows, :].T
        op_ref[4, t] = bs_ref[rows, :].T
        vt_ref[t] = v_ref[rows, :].T

    for hh in range(HEAD_PAIR // HEAD_DIM):
        ks = slice(HEAD_DIM * hh, HEAD_DIM * (hh + 1))

        def body(vi, carry):
            s = s0_ref[hh, vi]
            for t in range(seq):
                sa = jnp.sum(s * op_ref[3, t, ks, :], axis=0, keepdims=True)
                vrow = vt_ref[t, pl.ds(HEAD_DIM * hh + vi, 1), :]
                s = s * op_ref[1, t, ks, :] + sa * op_ref[4, t, ks, :] + vrow * op_ref[2, t, ks, :]
                yt_ref[t, pl.ds(HEAD_DIM * hh + vi, 1), :] = jnp.sum(s * op_ref[0, t, ks, :], axis=0,
                                                                      keepdims=True)
            sout_ref[hh, vi] = s
            return carry

        lax.fori_loop(0, HEAD_DIM, body, 0, unroll=4)

    for t in range(seq):
        y_ref[pl.ds(t, nb, stride=seq), :] = yt_ref[t].T


def _wkv_steps(r, lw, k2, v, a_s, b_s, s0_native, seq):
    n = r.shape[0]
    nb = n // seq
    assert s0_native.shape == (N_HEADS, HEAD_DIM, HEAD_DIM, nb) and nb == LANES_V7X
    vec = pl.BlockSpec((n, HEAD_PAIR), lambda i: (0, i))
    st = pl.BlockSpec((HEAD_PAIR // HEAD_DIM, HEAD_DIM, HEAD_DIM, nb), lambda i: (i, 0, 0, 0))
    return pl.pallas_call(
        functools.partial(_wkv_steps_kernel, seq),
        grid=(N_HEADS * HEAD_DIM // HEAD_PAIR,),
        in_specs=[vec] * 6 + [st],
        out_specs=[vec, st],
        out_shape=[jax.ShapeDtypeStruct((n, D_RWKV), F32), jax.ShapeDtypeStruct(s0_native.shape, F32)],
        scratch_shapes=[pltpu.VMEM((5, seq, HEAD_PAIR, nb), F32), pltpu.VMEM((seq, HEAD_PAIR, nb), F32),
                        pltpu.VMEM((seq, HEAD_PAIR, nb), F32)],
        compiler_params=_params("parallel"),
        name="wkv_steps",
    )(r, lw, k2, v, a_s, b_s, s0_native)


def _pool_prompt_kernel(zp_ref, d_ref, carry_ref):
    i = pl.program_id(1)
    tm = zp_ref.shape[0]

    @pl.when(i == 0)
    def _():
        carry_ref[...] = jnp.zeros_like(carry_ref)

    zp = zp_ref[...]
    buf = jnp.concatenate([carry_ref[...], zp], axis=0)
    carry_ref[...] = zp[tm - 16:, :]
    pos = (i * tm + lax.broadcasted_iota(jnp.int32, (tm, POOL_GROUP), 0) + 1).astype(F32)
    for gi, win in enumerate(POOL_WINDOWS):
        sl = slice(POOL_GROUP * gi, POOL_GROUP * (gi + 1))
        acc = buf[:, sl]
        s = 1
        while s < win:
            acc = acc + pltpu.roll(acc, s, 0)
            s *= 2
        cnt = jnp.minimum(float(win), pos)
        d_ref[:, sl] = acc[16:, :] / cnt - zp[:, sl]


def _pool_prompt(zp, b, t, tm):
    tiles = t // tm
    return pl.pallas_call(
        _pool_prompt_kernel,
        grid=(b, tiles),
        in_specs=[pl.BlockSpec((tm, D_POOL), lambda bi, i: (bi * tiles + i, 0))],
        out_specs=pl.BlockSpec((None, tm, D_POOL), lambda bi, i: (bi, i, 0)),
        out_shape=jax.ShapeDtypeStruct((b, t, D_POOL), F32),
        scratch_shapes=[pltpu.VMEM((16, D_POOL), F32)],
        compiler_params=_params("parallel", "arbitrary"),
        name="pool_prompt",
    )(zp)


def _pool_sample_kernel(seq, hist_ref, zp_ref, d_ref, new_ref):
    nb = hist_ref.shape[1]
    step = lambda t: zp_ref[pl.ds(t, nb, stride=seq), :]
    past = lambda j: step(j) if j >= 0 else hist_ref[POOL_HIST + j]
    tiles_per_group = POOL_GROUP // LANES_V7X
    for gi, win in enumerate(POOL_WINDOWS):
        @pl.when(pl.program_id(0) // tiles_per_group == gi)
        def _():
            for t in range(seq):
                acc = past(t)
                for j in range(1, win):
                    acc = acc + past(t - j)
                d_ref[pl.ds(t, nb, stride=seq), :] = acc / float(min(win, PAST_LEN + 1)) - past(t)
    for j in range(POOL_HIST):
        src = j + seq
        new_ref[j] = hist_ref[src] if src < POOL_HIST else step(src - POOL_HIST)


def _pool_sample(hist, zp, row0, seq):
    nb = hist.shape[1]
    n = nb * seq
    assert row0 % n == 0
    hist_spec = pl.BlockSpec((POOL_HIST, nb, LANES_V7X), lambda j: (0, 0, j))
    return pl.pallas_call(
        functools.partial(_pool_sample_kernel, seq),
        grid=(D_POOL // LANES_V7X,),
        in_specs=[hist_spec, pl.BlockSpec((n, LANES_V7X), lambda j: (row0 // n, j))],
        out_specs=[pl.BlockSpec((n, LANES_V7X), lambda j: (0, j)), hist_spec],
        out_shape=[jax.ShapeDtypeStruct((n, D_POOL), F32), jax.ShapeDtypeStruct(hist.shape, F32)],
        compiler_params=_params("parallel"),
        name="pool_sample",
    )(hist, zp)


def _merge_kernel(y_ref, r_ref, k2_ref, v_ref, g_ref, d_ref, zga_ref, zgb_ref,
                  rk_ref, lw_ref, lb_ref, ones_ref, wa_ref, pw_ref, ps_ref, wb_ref, m_ref):
    ones_bd = ones_ref[...]
    y = y_ref[...]
    mu = _head_sum(y, ones_bd) * (1.0 / HEAD_DIM)
    yc = y - mu
    var = _head_sum(yc * yc, ones_bd) * (1.0 / HEAD_DIM)
    yn = yc * lax.rsqrt(var + GN_EPS) * lw_ref[...] + lb_ref[...]
    v = v_ref[...]
    bonus = _head_sum(r_ref[...] * k2_ref[...] * rk_ref[...], ones_bd) * v
    ya = (yn + bonus) * g_ref[...]
    d = d_ref[...]
    yb = jnp.concatenate(
        [_mm(d[:, POOL_GROUP * gi:POOL_GROUP * (gi + 1)], pw_ref[gi]) for gi in range(len(POOL_WINDOWS))],
        axis=1) * ps_ref[...]
    m_ref[...] = (jax.nn.sigmoid(zga_ref[...]) * _mm(ya, wa_ref[...])
                  + jax.nn.sigmoid(zgb_ref[...]) * _mm(yb, wb_ref[...]))


def _merge(y, r, k2, v, g, d, zga, zgb, row0, wts, tm):
    n = y.shape[0]
    assert row0 % tm == 0
    first = row0 // tm
    half = pl.BlockSpec((tm, D_RWKV), lambda i: (i, 0))
    full = pl.BlockSpec((tm, D_MODEL), lambda i: (i, 0))
    gate = pl.BlockSpec((tm, D_MODEL), lambda i: (first + i, 0))
    return pl.pallas_call(
        _merge_kernel,
        grid=(n // tm,),
        in_specs=[half] * 6 + [gate] * 2 + [_const_spec(w.shape) for w in wts],
        out_specs=full,
        out_shape=jax.ShapeDtypeStruct((n, D_MODEL), F32),
        compiler_params=_params("parallel"),
        name="merge",
    )(y, r, k2, v, g, d, zga, zgb, *wts)


def _mix_out_kernel(m_ref, x_ref, w_ref, g_ref, o_ref):
    mo = _mm(m_ref[...], w_ref[...])
    o_ref[...] = x_ref[...] + _rmsnorm(mo, g_ref[...])


def _mix_out(m, x2d, w_out, g, tm):
    n = m.shape[0]
    full = pl.BlockSpec((tm, D_MODEL), lambda i: (i, 0))
    return pl.pallas_call(
        _mix_out_kernel,
        grid=(n // tm,),
        in_specs=[full, full, _const_spec(w_out.shape), _const_spec(g.shape)],
        out_specs=full,
        out_shape=jax.ShapeDtypeStruct((n, D_MODEL), F32),
        compiler_params=_params("parallel"),
        name="mix_out",
    )(m, x2d, w_out, g)


def _gelu_tanh(x):
    return 0.5 * x * (1.0 + jnp.tanh(0.7978845608028654 * (x + 0.044715 * x * x * x)))


def _ffn_body(x_ref, gn_ref, wg_ref, wu_ref, cw_ref, cb_ref, wo_ref, gp_ref,
              o_ref, h_ref, acc_ref, prev1, prev2, emit_tail):
    c = pl.program_id(1)

    @pl.when(c == 0)
    def _():
        h_ref[...] = _rmsnorm(x_ref[...], gn_ref[...]).astype(BF16)
        acc_ref[...] = jnp.zeros_like(acc_ref)

    h = h_ref[...]
    gate = jnp.dot(h, wg_ref[...], preferred_element_type=F32)
    up = jnp.dot(h, wu_ref[...], preferred_element_type=F32)
    cw = cw_ref[...]
    cv = cb_ref[...] + cw[0:1, :] * prev2(gate) + cw[1:2, :] * prev1(gate) + cw[2:3, :] * gate
    emit_tail(gate)
    acc_ref[...] += _mm(_gelu_tanh(cv) * up, wo_ref[...])

    @pl.when(c == pl.num_programs(1) - 1)
    def _():
        o_ref[...] = x_ref[...] + _rmsnorm(acc_ref[...], gp_ref[...])


def _ffn_prompt_kernel(tiles_per_seq, x_ref, gn_ref, wg_ref, wu_ref, cw_ref, cb_ref, wo_ref, gp_ref,
                       o_ref, tail_ref, h_ref, acc_ref, carry_ref):
    i = pl.program_id(0)
    c = pl.program_id(1)

    @pl.when(i % tiles_per_seq == 0)
    def _():
        carry_ref[c] = jnp.zeros(carry_ref.shape[1:], F32)

    hist = carry_ref[c]

    def shifted(gate, s):
        row = lax.broadcasted_iota(jnp.int32, gate.shape, 0)
        rolled = pltpu.roll(gate, s, 0)
        out = rolled
        for j in range(s):
            out = jnp.where(row == j, hist[8 - s + j:9 - s + j, :], out)
        return out

    def emit_tail(gate):
        tail_ref[...] = gate[gate.shape[0] - tail_ref.shape[0]:, :]

    _ffn_body(x_ref, gn_ref, wg_ref, wu_ref, cw_ref, cb_ref, wo_ref, gp_ref, o_ref,
              h_ref, acc_ref, lambda g: shifted(g, 1), lambda g: shifted(g, 2), emit_tail)
    carry_ref[c] = tail_ref[...]


def _ffn_sample_kernel(seq, x_ref, st_ref, gn_ref, wg_ref, wu_ref, cw_ref, cb_ref, wo_ref, gp_ref,
                       o_ref, tail_ref, h_ref, acc_ref, g_ref, p1_ref, p2_ref):
    nb = st_ref.shape[0]
    st0 = st_ref[:, 0, :]
    st1 = st_ref[:, 1, :]
    prev1 = lambda gate: _replace_step_rows(p1_ref, pltpu.roll(gate, 1, 0), seq, [(0, st1)])
    prev2 = lambda gate: _replace_step_rows(p2_ref, pltpu.roll(gate, 2, 0), seq, [(0, st0), (1, st1)])

    def emit_tail(gate):
        tiles = gate.shape[1] // LANES_V7X
        for j in range(tiles):
            g_ref[j] = gate[:, LANES_V7X * j:LANES_V7X * (j + 1)]
        for s in range(CONV_W - 1):
            rows = pl.ds(seq - (CONV_W - 1) + s, nb, stride=seq)
            tail_ref[:, s, :] = jnp.concatenate([g_ref[j, rows, :] for j in range(tiles)], axis=1)

    _ffn_body(x_ref, gn_ref, wg_ref, wu_ref, cw_ref, cb_ref, wo_ref, gp_ref, o_ref,
              h_ref, acc_ref, prev1, prev2, emit_tail)


def _ffn_prompt(x2d, seq_len, wts, tm, fk):
    n = x2d.shape[0]
    nc = D_FF // fk
    gn, w_in, cw, cb, w_out, gp = wts
    full = pl.BlockSpec((tm, D_MODEL), lambda i, c: (i, 0))
    return pl.pallas_call(
        functools.partial(_ffn_prompt_kernel, seq_len // tm),
        grid=(n // tm, nc),
        in_specs=[full, _const_spec(gn.shape),
                  pl.BlockSpec((D_MODEL, fk), lambda i, c: (0, c)),
                  pl.BlockSpec((D_MODEL, fk), lambda i, c: (0, c + nc)),
                  pl.BlockSpec((CONV_W, fk), lambda i, c: (0, c)),
                  pl.BlockSpec((1, fk), lambda i, c: (0, c)),
                  pl.BlockSpec((fk, D_MODEL), lambda i, c: (c, 0)),
                  _const_spec(gp.shape)],
        out_specs=[full, pl.BlockSpec((None, 8, fk), lambda i, c: (i, 0, c))],
        out_shape=[jax.ShapeDtypeStruct((n, D_MODEL), F32),
                   jax.ShapeDtypeStruct((n // tm, 8, D_FF), F32)],
        scratch_shapes=[pltpu.VMEM((tm, D_MODEL), BF16), pltpu.VMEM((tm, D_MODEL), F32),
                        pltpu.VMEM((nc, 8, fk), F32)],
        compiler_params=_params("arbitrary", "arbitrary"),
        name="ffn_prompt",
    )(x2d, gn, w_in, w_in, cw, cb, w_out, gp)


def _ffn_sample(x2d, st, seq, wts, fk):
    n = x2d.shape[0]
    nb = n // seq
    assert seq >= CONV_W - 1 and st.shape == (nb, CONV_W - 1, D_FF)
    nc = D_FF // fk
    gn, w_in, cw, cb, w_out, gp = wts
    full = pl.BlockSpec((n, D_MODEL), lambda i, c: (0, 0))
    cols = pl.BlockSpec((nb, CONV_W - 1, fk), lambda i, c: (0, 0, c))
    return pl.pallas_call(
        functools.partial(_ffn_sample_kernel, seq),
        grid=(1, nc),
        in_specs=[full, cols, _const_spec(gn.shape),
                  pl.BlockSpec((D_MODEL, fk), lambda i, c: (0, c)),
                  pl.BlockSpec((D_MODEL, fk), lambda i, c: (0, c + nc)),
                  pl.BlockSpec((CONV_W, fk), lambda i, c: (0, c)),
                  pl.BlockSpec((1, fk), lambda i, c: (0, c)),
                  pl.BlockSpec((fk, D_MODEL), lambda i, c: (c, 0)),
                  _const_spec(gp.shape)],
        out_specs=[full, cols],
        out_shape=[jax.ShapeDtypeStruct((n, D_MODEL), F32), jax.ShapeDtypeStruct(st.shape, F32)],
        scratch_shapes=[pltpu.VMEM((n, D_MODEL), BF16), pltpu.VMEM((n, D_MODEL), F32)]
        + [pltpu.VMEM((fk // LANES_V7X, n, LANES_V7X), F32)] * 3,
        compiler_params=_params("arbitrary", "arbitrary"),
        name="ffn_sample",
    )(x2d, st, gn, w_in, w_in, cw, cb, w_out, gp)


def _row(v):
    return v.reshape(1, -1).astype(F32)


def _ones_bd():
    i = jnp.arange(MXU_DIM_V7X) // HEAD_DIM
    return (i[:, None] == i[None, :]).astype(BF16)


def _layer_weights(l, norm_pre_mix, w_in, mu_shift, w0, w2, a0, a2, g2, k_k, k_a, r_k, lnx_w, lnx_b,
                   w_branch_a, pool_w, pool_scale, w_branch_b, w_out, norm_post_mix,
                   norm_pre_ffn, w_ffn_in, conv_w, conv_b, w_ffn_out, norm_post_ffn):
    w_t = jnp.swapaxes(w_in[l], 0, 1)
    mu = jnp.pad(mu_shift[l], (0, D_SHIFT_PAD - D_SHIFT)).reshape(1, -1)
    w_lora = jnp.zeros((D_LORA_PAD, 3 * D_RWKV), F32)
    w_lora = w_lora.at[0:RANK_W, 0:D_RWKV].set(w2[l])
    w_lora = w_lora.at[RANK_W:RANK_W + RANK_A, D_RWKV:2 * D_RWKV].set(a2[l])
    w_lora = w_lora.at[RANK_W + RANK_A:D_LORA, 2 * D_RWKV:].set(g2[l])
    ones_bd = _ones_bd()
    return dict(
        in_proj=(_row(norm_pre_mix[l]), w_t),
        prep=(mu, w_lora.astype(BF16), _row(w0[l]), _row(a0[l]), _row(k_k[l]), _row(k_a[l]), ones_bd),
        merge=(_row(r_k[l]), _row(lnx_w[l]), _row(lnx_b[l]), ones_bd, w_branch_a[l].astype(BF16),
               pool_w[l].astype(BF16), _row(pool_scale[l]), w_branch_b[l].astype(BF16)),
        mix_out=(w_out[l].astype(BF16), _row(norm_post_mix[l])),
        ffn=(_row(norm_pre_ffn[l]), w_ffn_in[l].astype(BF16), conv_w[l].astype(F32), _row(conv_b[l]),
             w_ffn_out[l].astype(BF16), _row(norm_post_ffn[l])),
    )


def _largest_tile(n, cap, mult=16):
    best = None
    for d in range(mult, min(n, cap) + 1, mult):
        if n % d == 0:
            best = d
    assert best is not None, (n, cap)
    return best


def _project(xp2d, xs2d, wts):
    g, w_t = wts["in_proj"]
    tm_norm = _largest_tile(math.gcd(xp2d.shape[0], xs2d.shape[0]), 512)
    h = _pre_norm(xp2d, xs2d, g, tm_norm)
    tm = _largest_tile(h.shape[0], 1088)
    zrkv = _in_proj(h, w_t, 0, 3 * D_RWKV, D_RWKV, tm)
    zl = _in_proj(h, w_t, 3 * D_RWKV, D_LORA_PAD, D_LORA_PAD, tm)
    zp = _in_proj(h, w_t, D_SHIFT, D_POOL, D_POOL, tm)
    zga = _in_proj(h, w_t, D_SHIFT + D_POOL, D_MODEL, D_MODEL // 2, tm)
    zgb = _in_proj(h, w_t, D_SHIFT + D_POOL + D_MODEL, D_MODEL, D_MODEL // 2, tm)
    return zrkv, zl, zp, zga, zgb


def _last_shift_row(zrkv, zl, row0, b, t):
    if b <= 8:
        last = lambda z, w: jnp.concatenate(
            [lax.slice(z, (row0 + (i + 1) * t - 1, 0), (row0 + (i + 1) * t, w)) for i in range(b)])
    else:
        last = lambda z, w: lax.slice(z, (row0 + t - 1, 0), (row0 + b * t, w), (t, 1))
    return jnp.concatenate([last(zrkv, 3 * D_RWKV), last(zl, D_LORA)], axis=-1)[:, None, :]


def _prompt_layer(x, z, wts):
    b, t, _ = x.shape
    n = b * t
    x2d = x.reshape(n, D_MODEL)
    zrkv, zl, zp, zga, zgb = z
    r, lw, k2, v, a_s, b_s, g = _prep_prompt(zrkv, zl, b, t, wts["prep"], tm=256)
    y, wkv = _wkv_prompt(r, lw, k2, v, a_s, b_s)
    d = _pool_prompt(zp, b, t, tm=512)
    flat = lambda a: a.reshape(n, a.shape[-1])
    m = _merge(flat(y), flat(r), flat(k2), flat(v), flat(g), flat(d), zga, zgb, 0, wts["merge"], tm=256)
    x1 = _mix_out(m, x2d, *wts["mix_out"], tm=512)
    ffn_tm = _largest_tile(t, 512)
    out, tail = _ffn_prompt(x1, t, wts["ffn"], tm=ffn_tm, fk=512)
    shift = _last_shift_row(zrkv, zl, 0, b, t)
    pool =jnp.stack([zp[(i + 1) * t - POOL_HIST:(i + 1) * t] for i in range(b)])
    tiles = t // ffn_tm
    conv = tail.reshape(b, tiles, 8, D_FF)[:, tiles - 1, 8 - (CONV_W - 1):, :]
    return out.reshape(b, t, D_MODEL), shift, wkv, pool, conv


def _sample_layer(x, z, row0, st_shift, st_wkv, st_pool, st_conv, wts):
    b, t, _ = x.shape
    n = b * t
    x2d = x.reshape(n, D_MODEL)
    zrkv, zl, zp, zga, zgb = z
    st = jnp.pad(st_shift.reshape(b, D_SHIFT), ((0, 0), (0, D_SHIFT_PAD - D_SHIFT)))
    tm = _largest_tile(math.gcd(row0, n), 128, mult=SLAB * t)
    r, lw, k2, v, a_s, b_s, g = _prep_sample(zrkv, zl, row0, n, st, t, wts["prep"], tm=tm)
    y, wkv = _wkv_steps(r, lw, k2, v, a_s, b_s, jnp.transpose(st_wkv, (1, 2, 3, 0)), t)
    wkv = jnp.transpose(wkv, (3, 0, 1, 2))
    d, pool = _pool_sample(jnp.swapaxes(st_pool, 0, 1), zp, row0, t)
    pool = jnp.swapaxes(pool, 0, 1)
    m = _merge(y, r, k2, v, g, d, zga, zgb, row0, wts["merge"], tm=_largest_tile(math.gcd(row0, n), 256, mult=8))
    x1 = _mix_out(m, x2d, *wts["mix_out"], tm=n)
    out, conv = _ffn_sample(x1, st_conv, t, wts["ffn"], fk=512)
    shift = _last_shift_row(zrkv, zl, row0, b, t)
    return out.reshape(b, t, D_MODEL), shift, wkv, pool, conv


def kernel(x_prompt, x_sample, state_shift, state_wkv, state_pool, state_conv, norm_pre_mix, w_in, mu_shift, w0, w2, a0, a2, g2, k_k, k_a, r_k, lnx_w, lnx_b, w_branch_a, pool_w, pool_scale, w_branch_b, w_out, norm_post_mix, norm_pre_ffn, w_ffn_in, conv_w, conv_b, w_ffn_out, norm_post_ffn):
    weights = (norm_pre_mix, w_in, mu_shift, w0, w2, a0, a2, g2, k_k, k_a, r_k, lnx_w, lnx_b,
               w_branch_a, pool_w, pool_scale, w_branch_b, w_out, norm_post_mix,
               norm_pre_ffn, w_ffn_in, conv_w, conv_b, w_ffn_out, norm_post_ffn)
    depth = w_in.shape[0]
    yp, ys = x_prompt, x_sample
    p_states, s_states = [], []
    for l in range(depth):
        wts = _layer_weights(l, *weights)
        n_prompt = yp.shape[0] * yp.shape[1]
        z = _project(yp.reshape(n_prompt, D_MODEL), ys.reshape(-1, D_MODEL), wts)
        yp, *ps = _prompt_layer(yp, z, wts)
        ys, *ss = _sample_layer(ys, z, n_prompt, state_shift[l], state_wkv[l], state_pool[l], state_conv[l], wts)
        p_states.append(ps)
        s_states.append(ss)
    stack = lambda states, i: jnp.stack([s[i] for s in states])
    return (yp, ys,
            stack(p_states, 0), stack(p_states, 1), stack(p_states, 2), stack(p_states, 3),
            stack(s_states, 0), stack(s_states, 1), stack(s_states, 2), stack(s_states, 3))
```

```python
import functools
import math

import jax
import jax.numpy as jnp
from jax import lax
from jax.experimental import pallas as pl
from jax.experimental.pallas import tpu as pltpu

F32 = jnp.float32
BF16 = jnp.bfloat16

D_MODEL = 2048
HEAD_DIM = 64
D_RWKV = 1024
N_HEADS = 16
RANK_W, RANK_A, RANK_G = 64, 64, 160
D_LORA = RANK_W + RANK_A + RANK_G
D_LORA_PAD = 384
D_SHIFT = 3 * D_RWKV + D_LORA
D_SHIFT_PAD = 3 * D_RWKV + D_LORA_PAD
D_POOL = 1024
POOL_WINDOWS = (2, 4, 8, 16)
POOL_GROUP = 256
POOL_HIST = 15
D_FF = 5632
CONV_W = 3
NORM_EPS = 1e-6
GN_EPS = 64e-5
PAST_LEN = 16384

LANES_V7X = 128
SLAB = 8
MXU_DIM_V7X = 256
HEADS_PER_GROUP = MXU_DIM_V7X // HEAD_DIM
N_GROUPS = N_HEADS // HEADS_PER_GROUP
CHUNK = 64
VMEM_LIMIT_V7X = 56 * 1024 * 1024


def _params(*sem):
    return pltpu.CompilerParams(dimension_semantics=sem, vmem_limit_bytes=VMEM_LIMIT_V7X)


def _mm(a, b):
    return jnp.dot(a.astype(BF16), b.astype(BF16), preferred_element_type=F32)


def _mm_nt(a, b):
    return lax.dot_general(a.astype(BF16), b.astype(BF16), (((1,), (1,)), ((), ())),
                           preferred_element_type=F32)


def _split_hi_lo(x):
    hi = x.astype(BF16)
    lo = (x - hi.astype(F32)).astype(BF16)
    return hi, lo


def _head_sum(x, ones_bd):
    hi, lo = _split_hi_lo(x)
    outs = []
    for gi in range(x.shape[1] // MXU_DIM_V7X):
        sl = slice(MXU_DIM_V7X * gi, MXU_DIM_V7X * (gi + 1))
        outs.append(jnp.dot(hi[:, sl], ones_bd, preferred_element_type=F32)
                    + jnp.dot(lo[:, sl], ones_bd, preferred_element_type=F32))
    return jnp.concatenate(outs, axis=1)


def _rmsnorm(x, g):
    return x * lax.rsqrt(jnp.mean(x * x, axis=-1, keepdims=True) + NORM_EPS) * g


def _replace_step_rows(scr_ref, base, seq, steps):
    nb = base.shape[0] // seq
    for j in range(base.shape[1] // LANES_V7X):
        ls = slice(LANES_V7X * j, LANES_V7X * (j + 1))
        scr_ref[j] = base[:, ls]
        for t, vals in steps:
            scr_ref[j, pl.ds(t, nb, stride=seq), :] = vals[:, ls]
    return jnp.concatenate([scr_ref[j] for j in range(base.shape[1] // LANES_V7X)], axis=1)


def _pre_norm_kernel(prompt_tiles, xp_ref, xs_ref, g_ref, h_ref):
    i = pl.program_id(0)

    @pl.when(i < prompt_tiles)
    def _():
        h_ref[...] = _rmsnorm(xp_ref[...], g_ref[...]).astype(BF16)

    @pl.when(i >= prompt_tiles)
    def _():
        h_ref[...] = _rmsnorm(xs_ref[...], g_ref[...]).astype(BF16)


def _pre_norm(xp, xs, g, tm):
    pt, st = xp.shape[0] // tm, xs.shape[0] // tm
    return pl.pallas_call(
        functools.partial(_pre_norm_kernel, pt),
        grid=(pt + st,),
        in_specs=[pl.BlockSpec((tm, D_MODEL), lambda i: (jnp.minimum(i, pt - 1), 0)),
                  pl.BlockSpec((tm, D_MODEL), lambda i: (jnp.maximum(i - pt, 0), 0)),
                  _const_spec(g.shape)],
        out_specs=pl.BlockSpec((tm, D_MODEL), lambda i: (i, 0)),
        out_shape=jax.ShapeDtypeStruct((xp.shape[0] + xs.shape[0], D_MODEL), BF16),
        compiler_params=_params("arbitrary"),
        name="pre_norm",
    )(xp, xs, g)


def _in_proj_kernel(h_ref, w_ref, o_ref, wb_ref):
    @pl.when(pl.program_id(1) == 0)
    def _():
        wb_ref[...] = w_ref[...].astype(BF16)

    o_ref[...] = lax.dot_general(h_ref[...], wb_ref[...], (((1,), (1,)), ((), ())),
                                 preferred_element_type=F32)


def _in_proj(h, w_t, off, width, tn, tm):
    n = h.shape[0]
    assert width % tn == 0 and n % tm == 0
    return pl.pallas_call(
        _in_proj_kernel,
        grid=(width // tn, n // tm),
        in_specs=[pl.BlockSpec((tm, D_MODEL), lambda j, i: (i, 0)),
                  pl.BlockSpec((pl.Element(tn), pl.Element(D_MODEL)),
                               lambda j, i: (SLAB * (off // SLAB + j * (tn // SLAB)), 0))],
        out_specs=pl.BlockSpec((tm, tn), lambda j, i: (i, j)),
        out_shape=jax.ShapeDtypeStruct((n, width), F32),
        scratch_shapes=[pltpu.VMEM((tn, D_MODEL), BF16)],
        compiler_params=_params("arbitrary", "arbitrary"),
        name="in_proj",
    )(h, w_t)


def _prep_math(zs, prev, mu, w_lora, w0, a0, k_k, k_a, ones_bd):
    f = zs + (prev - zs) * mu
    r = f[:, 0:D_RWKV]
    k = f[:, D_RWKV:2 * D_RWKV]
    v = f[:, 2 * D_RWKV:3 * D_RWKV]
    low = f[:, 3 * D_RWKV:3 * D_RWKV + D_LORA_PAD]
    lane = lax.broadcasted_iota(jnp.int32, low.shape, 1)
    act = jnp.where(lane < RANK_W, jnp.tanh(low),
                    jnp.where(lane < RANK_W + RANK_A, low,
                              jnp.where(lane < D_LORA, jax.nn.sigmoid(low), 0.0)))
    lora = _mm(act, w_lora)
    u = w0 + lora[:, 0:D_RWKV]
    lw = -math.exp(-0.5) * jax.nn.sigmoid(u)
    a = jax.nn.sigmoid(a0 + lora[:, D_RWKV:2 * D_RWKV])
    g = lora[:, 2 * D_RWKV:3 * D_RWKV]
    kk = k * k_k
    kk = kk * jnp.minimum(lax.rsqrt(_head_sum(kk * kk, ones_bd)), 1e12)
    k2 = k * (1.0 + (a - 1.0) * k_a)
    return r, lw, k2, v, -kk, kk * a, g


def _prep_prompt_kernel(zrkv_ref, zl_ref, mu_ref, wl_ref, w0_ref, a0_ref, kk_ref, ka_ref, ones_ref,
                        r_ref, lw_ref, k2_ref, v_ref, as_ref, bs_ref, g_ref, carry_ref):
    i = pl.program_id(1)

    @pl.when(i == 0)
    def _():
        carry_ref[...] = jnp.zeros_like(carry_ref)

    zs = jnp.concatenate([zrkv_ref[...], zl_ref[...]], axis=1)
    rolled = pltpu.roll(zs, 1, 0)
    row = lax.broadcasted_iota(jnp.int32, zs.shape, 0)
    prev = jnp.where(row == 0, carry_ref[0:1, :], rolled)
    carry_ref[0:1, :] = zs[zs.shape[0] - 1:, :]
    outs = _prep_math(zs, prev, mu_ref[...], wl_ref[...], w0_ref[...], a0_ref[...],
                      kk_ref[...], ka_ref[...], ones_ref[...])
    for o_ref, o in zip((r_ref, lw_ref, k2_ref, v_ref, as_ref, bs_ref, g_ref), outs):
        o_ref[...] = o


def _prep_sample_kernel(seq, zrkv_ref, zl_ref, st_ref, mu_ref, wl_ref, w0_ref, a0_ref, kk_ref, ka_ref,
                        ones_ref, r_ref, lw_ref, k2_ref, v_ref, as_ref, bs_ref, g_ref, prev_ref):
    zs = jnp.concatenate([zrkv_ref[...], zl_ref[...]], axis=1)
    prev = _replace_step_rows(prev_ref, pltpu.roll(zs, 1, 0), seq, [(0, st_ref[...])])
    outs = _prep_math(zs, prev, mu_ref[...], wl_ref[...], w0_ref[...], a0_ref[...],
                      kk_ref[...], ka_ref[...], ones_ref[...])
    for o_ref, o in zip((r_ref, lw_ref, k2_ref, v_ref, as_ref, bs_ref, g_ref), outs):
        o_ref[...] = o


def _const_spec(shape):
    nd = len(shape)
    return pl.BlockSpec(shape, lambda *_: (0,) * nd)


def _prep_prompt(zrkv, zl, b, t, wts, tm):
    tiles = t // tm
    outs = [jax.ShapeDtypeStruct((b, t, D_RWKV), F32)] * 7
    row_spec = pl.BlockSpec((None, tm, D_RWKV), lambda bi, i: (bi, i, 0))
    return pl.pallas_call(
        _prep_prompt_kernel,
        grid=(b, tiles),
        in_specs=[pl.BlockSpec((tm, 3 * D_RWKV), lambda bi, i: (bi * tiles + i, 0)),
                  pl.BlockSpec((tm, D_LORA_PAD), lambda bi, i: (bi * tiles + i, 0))]
        + [_const_spec(w.shape) for w in wts],
        out_specs=[row_spec] * 7,
        out_shape=outs,
        scratch_shapes=[pltpu.VMEM((8, D_SHIFT_PAD), F32)],
        compiler_params=_params("parallel", "arbitrary"),
        name="prep_prompt",
    )(zrkv, zl, *wts)


def _prep_sample(zrkv, zl, row0, n, st, seq, wts, tm):
    assert tm % (SLAB * seq) == 0 and n % tm == 0 and row0 % tm == 0
    first = row0 // tm
    outs = [jax.ShapeDtypeStruct((n, D_RWKV), F32)] * 7
    row_spec = pl.BlockSpec((tm, D_RWKV), lambda i: (i, 0))
    return pl.pallas_call(
        functools.partial(_prep_sample_kernel, seq),
        grid=(n // tm,),
        in_specs=[pl.BlockSpec((tm, 3 * D_RWKV), lambda i: (first + i, 0)),
                  pl.BlockSpec((tm, D_LORA_PAD), lambda i: (first + i, 0)),
                  pl.BlockSpec((tm // seq, D_SHIFT_PAD), lambda i: (i, 0))]
        + [_const_spec(w.shape) for w in wts],
        out_specs=[row_spec] * 7,
        out_shape=outs,
        scratch_shapes=[pltpu.VMEM((D_SHIFT_PAD // LANES_V7X, tm, LANES_V7X), F32)],
        compiler_params=_params("parallel"),
        name="prep_sample",
    )(zrkv, zl, st, *wts)


def _expand_bd(a, bd_mask):
    return jnp.where(bd_mask, jnp.concatenate([a] * HEADS_PER_GROUP, axis=0), 0.0)


def _chunk_cumsum(x):
    row = lax.broadcasted_iota(jnp.int32, x.shape, 0)
    s = 1
    while s < x.shape[0]:
        x = x + jnp.where(row >= s, pltpu.roll(x, s, 0), 0.0)
        s *= 2
    return x


def _wkv_intra(xs, bts, kts, vs, pzs, masks, order):
    bd_mask, strict_mask, incl_mask, eye_c, head_masks = masks
    c = vs[0].shape[0]
    ystacks = [jnp.concatenate([jnp.where(m, bt, 0.0) for m in head_masks]
                               + [jnp.where(m, kt, 0.0) for m in head_masks], axis=0)
               for bt, kt in zip(bts, kts)]
    grams = [_mm_nt(x, ys) for x, ys in zip(xs, ystacks)]
    l_abs = [jnp.where(strict_mask, g[0:c, 0:4 * c], 0.0) for g in grams]
    l_aks = [jnp.where(strict_mask, g[0:c, 4 * c:8 * c], 0.0) for g in grams]
    m_rbs = [jnp.where(incl_mask, g[c:2 * c, 0:4 * c], 0.0) for g in grams]
    m_rks = [jnp.where(incl_mask, g[c:2 * c, 4 * c:8 * c], 0.0) for g in grams]

    tinvs = [eye_c + l for l in l_abs]
    lps = [_mm(l, _expand_bd(l, bd_mask)) for l in l_abs]
    p = 2
    while True:
        rhss = [_expand_bd(lp, bd_mask) for lp in lps]
        if 2 * p >= order:
            tinvs = [t + _mm(t, rhs) for t, rhs in zip(tinvs, rhss)]
            break
        ress = [_mm(jnp.concatenate([lp, t], axis=0), rhs) for lp, t, rhs in zip(lps, tinvs, rhss)]
        lps = [res[0:c] for res in ress]
        tinvs = [t + res[c:2 * c] for t, res in zip(tinvs, ress)]
        p *= 2

    v_bds = [_expand_bd(v, bd_mask) for v in vs]
    ws = [pz[0:c] + _mm(l_ak, v_bd) for pz, l_ak, v_bd in zip(pzs, l_aks, v_bds)]
    us = [_mm(t, _expand_bd(w, bd_mask)) for t, w in zip(tinvs, ws)]
    ys = [pz[c:2 * c] + _mm(jnp.concatenate([m_rb, m_rk], axis=1),
                            jnp.concatenate([_expand_bd(u, bd_mask), v_bd], axis=0))
          for pz, m_rb, m_rk, u, v_bd in zip(pzs, m_rbs, m_rks, us, v_bds)]
    return ys, us


def _wkv_chunk(groups, masks):
    c = groups[0][0].shape[0]
    cls = [_chunk_cumsum(g[1]) for g in groups]
    xs = [jnp.concatenate([g[4] * jnp.exp(cl - g[1]), g[0] * jnp.exp(cl)], axis=0)
          for g, cl in zip(groups, cls)]
    pzs = [_mm_nt(x, g[6]) for x, g in zip(xs, groups)]
    e_negs = [jnp.exp(-cl) for cl in cls]
    ys, us = _wkv_intra(xs, [g[5] * e for g, e in zip(groups, e_negs)],
                        [g[2] * e for g, e in zip(groups, e_negs)], [g[3] for g in groups], pzs, masks, c)
    dss = []
    for g, cl, u in zip(groups, cls, us):
        e_rem = jnp.exp(cl[c - 1:c, :] - cl)
        uv_t = jnp.concatenate([u, g[3]], axis=0).T
        bk = jnp.concatenate([g[5] * e_rem, g[2] * e_rem], axis=0)
        dss.append(_mm(uv_t, bk))
    return [(y, g[6] * jnp.exp(cl[c - 1:c, :]) + jnp.where(masks[0], ds, 0.0))
            for y, g, cl, ds in zip(ys, groups, cls, dss)]


def _wkv_masks(c, seq):
    n = HEADS_PER_GROUP * c
    rr = lax.broadcasted_iota(jnp.int32, (n, n), 0)
    cc = lax.broadcasted_iota(jnp.int32, (n, n), 1)
    bd_mask = (rr // c) == (cc // c)
    t = lax.broadcasted_iota(jnp.int32, (c, n), 0)
    s = lax.broadcasted_iota(jnp.int32, (c, n), 1) % c
    same = (t // seq) == (s // seq)
    strict_mask = same & (t > s)
    incl_mask = same & (t >= s)
    eye_c = jnp.where(t == s, 1.0, 0.0).astype(F32)
    lane = lax.broadcasted_iota(jnp.int32, (c, MXU_DIM_V7X), 1)
    head_masks = [(lane // HEAD_DIM) == h for h in range(HEADS_PER_GROUP)]
    return bd_mask, strict_mask, incl_mask, eye_c, head_masks


def _wkv_prompt_kernel(n_casts, r_ref, lw_ref, k2_ref, v_ref, as_ref, bs_ref, *refs):
    w_refs, (y_ref, sout_ref), wb_refs, (s_ref,) = (
        refs[:n_casts], refs[n_casts:n_casts + 2], refs[n_casts + 2:2 * n_casts + 2], refs[2 * n_casts + 2:])
    ci = pl.program_id(0)

    @pl.when(ci == 0)
    def _():
        s_ref[...] = jnp.zeros_like(s_ref)

    for w_ref, wb_ref in zip(w_refs, wb_refs):
        wb_ref[...] = w_ref[...].astype(BF16)

    masks = _wkv_masks(CHUNK, CHUNK)
    chains = [(bi, gi) for bi in range(r_ref.shape[0]) for gi in range(N_GROUPS)]
    lanes = lambda gi: slice(MXU_DIM_V7X * gi, MXU_DIM_V7X * (gi + 1))
    groups = [tuple(ref[bi, :, lanes(gi)] for ref in (r_ref, lw_ref, k2_ref, v_ref, as_ref, bs_ref))
              + (s_ref[bi, gi],) for bi, gi in chains]
    for (bi, gi), (y, s_new) in zip(chains, _wkv_chunk(groups, masks)):
        y_ref[bi, :, lanes(gi)] = y
        s_ref[bi, gi] = s_new

    @pl.when(ci == pl.num_programs(0) - 1)
    def _():
        for bi, gi in chains:
            s_bd = s_ref[bi, gi]
            for h in range(HEADS_PER_GROUP):
                hs = slice(HEAD_DIM * h, HEAD_DIM * (h + 1))
                sout_ref[bi, HEADS_PER_GROUP * gi + h] = s_bd[hs, hs]


def _wkv_prompt(r, lw, k2, v, a_s, b_s, weights):
    b, t, _ = r.shape
    steps = t // CHUNK
    spec = pl.BlockSpec((b, CHUNK, D_RWKV), lambda ci: (0, ci, 0))
    state = (b, N_GROUPS, MXU_DIM_V7X, MXU_DIM_V7X)
    out_state = (b, N_HEADS, HEAD_DIM, HEAD_DIM)
    bf16_rows = 2 * SLAB
    assert all(w.shape[0] % (steps * bf16_rows) == 0 for w in weights)
    slabs = [pl.BlockSpec((w.shape[0] // steps, w.shape[1]), lambda ci: (ci, 0)) for w in weights]
    y, s, *wb = pl.pallas_call(
        functools.partial(_wkv_prompt_kernel, len(weights)),
        grid=(steps,),
        in_specs=[spec] * 6 + slabs,
        out_specs=[spec, _const_spec(out_state)] + slabs,
        out_shape=[jax.ShapeDtypeStruct((b, t, D_RWKV), F32), jax.ShapeDtypeStruct(out_state, F32)]
        + [jax.ShapeDtypeStruct(w.shape, BF16) for w in weights],
        scratch_shapes=[pltpu.VMEM(state, F32)],
        compiler_params=_params("arbitrary"),
        name="wkv_prompt",
    )(r, lw, k2, v, a_s, b_s, *weights)
    return y, s, wb


HEAD_PAIR = 2 * HEAD_DIM


def _wkv_steps_kernel(seq, r_ref, lw_ref, k2_ref, v_ref, as_ref, bs_ref, s0_ref, y_ref, sout_ref,
                      op_ref, vt_ref, yt_ref):
    nb = r_ref.shape[0] // seq
    for t in range(seq):
        rows = pl.ds(t, nb, stride=seq)
        op_ref[0, t] = r_ref[rows, :].T
        op_ref[1, t] = jnp.exp(lw_ref[rows, :]).T
        op_ref[2, t] = k2_ref[rows, :].T
        op_ref[3, t] = as_ref[rows, :].T
        op_ref[4, t] = bs_ref[rows, :].T
        vt_ref[t] = v_ref[rows, :].T

    for hh in range(HEAD_PAIR // HEAD_DIM):
        ks = slice(HEAD_DIM * hh, HEAD_DIM * (hh + 1))

        def body(vi, carry):
            s = s0_ref[hh, vi]
            for t in range(seq):
                sa = jnp.sum(s * op_ref[3, t, ks, :], axis=0, keepdims=True)
                vrow = vt_ref[t, pl.ds(HEAD_DIM * hh + vi, 1), :]
                s = s * op_ref[1, t, ks, :] + sa * op_ref[4, t, ks, :] + vrow * op_ref[2, t, ks, :]
                yt_ref[t, pl.ds(HEAD_DIM * hh + vi, 1), :] = jnp.sum(s * op_ref[0, t, ks, :], axis=0,
                                                                      keepdims=True)
            sout_ref[hh, vi] = s
            return carry

        lax.fori_loop(0, HEAD_DIM, body, 0, unroll=4)

    for t in range(seq):
        y_ref[pl.ds(t, nb, stride=seq), :] = yt_ref[t].T


def _wkv_steps(r, lw, k2, v, a_s, b_s, s0_native, seq):
    n = r.shape[0]
    nb = n // seq
    assert s0_native.shape == (N_HEADS, HEAD_DIM, HEAD_DIM, nb) and nb == LANES_V7X
    vec = pl.BlockSpec((n, HEAD_PAIR), lambda i: (0, i))
    st = pl.BlockSpec((HEAD_PAIR // HEAD_DIM, HEAD_DIM, HEAD_DIM, nb), lambda i: (i, 0, 0, 0))
    return pl.pallas_call(
        functools.partial(_wkv_steps_kernel, seq),
        grid=(N_HEADS * HEAD_DIM // HEAD_PAIR,),
        in_specs=[vec] * 6 + [st],
        out_specs=[vec, st],
        out_shape=[jax.ShapeDtypeStruct((n, D_RWKV), F32), jax.ShapeDtypeStruct(s0_native.shape, F32)],
        scratch_shapes=[pltpu.VMEM((5, seq, HEAD_PAIR, nb), F32), pltpu.VMEM((seq, HEAD_PAIR, nb), F32),
                        pltpu.VMEM((seq, HEAD_PAIR, nb), F32)],
        compiler_params=_params("parallel"),
        name="wkv_steps",
    )(r, lw, k2, v, a_s, b_s, s0_native)


def _pool_prompt_kernel(zp_ref, d_ref, carry_ref):
    i = pl.program_id(1)
    tm = zp_ref.shape[0]

    @pl.when(i == 0)
    def _():
        carry_ref[...] = jnp.zeros_like(carry_ref)

    zp = zp_ref[...]
    buf = jnp.concatenate([carry_ref[...], zp], axis=0)
    carry_ref[...] = zp[tm - 16:, :]
    pos = (i * tm + lax.broadcasted_iota(jnp.int32, (tm, POOL_GROUP), 0) + 1).astype(F32)
    for gi, win in enumerate(POOL_WINDOWS):
        sl = slice(POOL_GROUP * gi, POOL_GROUP * (gi + 1))
        acc = buf[:, sl]
        s = 1
        while s < win:
            acc = acc + pltpu.roll(acc, s, 0)
            s *= 2
        cnt = jnp.minimum(float(win), pos)
        d_ref[:, sl] = acc[16:, :] / cnt - zp[:, sl]


def _pool_prompt(zp, b, t, tm):
    tiles = t // tm
    return pl.pallas_call(
        _pool_prompt_kernel,
        grid=(b, tiles),
        in_specs=[pl.BlockSpec((tm, D_POOL), lambda bi, i: (bi * tiles + i, 0))],
        out_specs=pl.BlockSpec((None, tm, D_POOL), lambda bi, i: (bi, i, 0)),
        out_shape=jax.ShapeDtypeStruct((b, t, D_POOL), F32),
        scratch_shapes=[pltpu.VMEM((16, D_POOL), F32)],
        compiler_params=_params("parallel", "arbitrary"),
        name="pool_prompt",
    )(zp)


def _pool_sample_kernel(seq, hist_ref, zp_ref, d_ref, new_ref):
    nb = hist_ref.shape[1]
    step = lambda t: zp_ref[pl.ds(t, nb, stride=seq), :]
    past = lambda j: step(j) if j >= 0 else hist_ref[POOL_HIST + j]
    tiles_per_group = POOL_GROUP // LANES_V7X
    for gi, win in enumerate(POOL_WINDOWS):
        @pl.when(pl.program_id(0) // tiles_per_group == gi)
        def _():
            for t in range(seq):
                acc = past(t)
                for j in range(1, win):
                    acc = acc + past(t - j)
                d_ref[pl.ds(t, nb, stride=seq), :] = acc / float(min(win, PAST_LEN + 1)) - past(t)
    for j in range(POOL_HIST):
        src = j + seq
        new_ref[j] = hist_ref[src] if src < POOL_HIST else step(src - POOL_HIST)


def _pool_sample(hist, zp, row0, seq):
    nb = hist.shape[1]
    n = nb * seq
    assert row0 % n == 0
    hist_spec = pl.BlockSpec((POOL_HIST, nb, LANES_V7X), lambda j: (0, 0, j))
    return pl.pallas_call(
        functools.partial(_pool_sample_kernel, seq),
        grid=(D_POOL // LANES_V7X,),
        in_specs=[hist_spec, pl.BlockSpec((n, LANES_V7X), lambda j: (row0 // n, j))],
        out_specs=[pl.BlockSpec((n, LANES_V7X), lambda j: (0, j)), hist_spec],
        out_shape=[jax.ShapeDtypeStruct((n, D_POOL), F32), jax.ShapeDtypeStruct(hist.shape, F32)],
        compiler_params=_params("parallel"),
        name="pool_sample",
    )(hist, zp)


def _merge_kernel(y_ref, r_ref, k2_ref, v_ref, g_ref, d_ref, zga_ref, zgb_ref,
                  rk_ref, lw_ref, lb_ref, ones_ref, wa_ref, pw_ref, ps_ref, wb_ref, m_ref):
    ones_bd = ones_ref[...]
    y = y_ref[...]
    mu = _head_sum(y, ones_bd) * (1.0 / HEAD_DIM)
    yc = y - mu
    var = _head_sum(yc * yc, ones_bd) * (1.0 / HEAD_DIM)
    yn = yc * lax.rsqrt(var + GN_EPS) * lw_ref[...] + lb_ref[...]
    v = v_ref[...]
    bonus = _head_sum(r_ref[...] * k2_ref[...] * rk_ref[...], ones_bd) * v
    ya = (yn + bonus) * g_ref[...]
    d = d_ref[...]
    yb = jnp.concatenate(
        [_mm(d[:, POOL_GROUP * gi:POOL_GROUP * (gi + 1)], pw_ref[gi]) for gi in range(len(POOL_WINDOWS))],
        axis=1) * ps_ref[...]
    m_ref[...] = (jax.nn.sigmoid(zga_ref[...]) * _mm(ya, wa_ref[...])
                  + jax.nn.sigmoid(zgb_ref[...]) * _mm(yb, wb_ref[...]))


def _merge(y, r, k2, v, g, d, zga, zgb, row0, wts, tm):
    n = y.shape[0]
    assert row0 % tm == 0
    first = row0 // tm
    half = pl.BlockSpec((tm, D_RWKV), lambda i: (i, 0))
    full = pl.BlockSpec((tm, D_MODEL), lambda i: (i, 0))
    gate = pl.BlockSpec((tm, D_MODEL), lambda i: (first + i, 0))
    return pl.pallas_call(
        _merge_kernel,
        grid=(n // tm,),
        in_specs=[half] * 6 + [gate] * 2 + [_const_spec(w.shape) for w in wts],
        out_specs=full,
        out_shape=jax.ShapeDtypeStruct((n, D_MODEL), F32),
        compiler_params=_params("parallel"),
        name="merge",
    )(y, r, k2, v, g, d, zga, zgb, *wts)


def _mix_out_kernel(m_ref, x_ref, w_ref, g_ref, o_ref):
    mo = _mm(m_ref[...], w_ref[...])
    o_ref[...] = x_ref[...] + _rmsnorm(mo, g_ref[...])


def _mix_out(m, x2d, w_out, g, tm):
    n = m.shape[0]
    full = pl.BlockSpec((tm, D_MODEL), lambda i: (i, 0))
    return pl.pallas_call(
        _mix_out_kernel,
        grid=(n // tm,),
        in_specs=[full, full, _const_spec(w_out.shape), _const_spec(g.shape)],
        out_specs=full,
        out_shape=jax.ShapeDtypeStruct((n, D_MODEL), F32),
        compiler_params=_params("parallel"),
        name="mix_out",
    )(m, x2d, w_out, g)


def _gelu_tanh(x):
    return 0.5 * x * (1.0 + jnp.tanh(0.7978845608028654 * (x + 0.044715 * x * x * x)))


def _ffn_body(x_ref, gn_ref, wg_ref, wu_ref, cw_ref, cb_ref, wo_ref, gp_ref,
              o_ref, h_ref, acc_ref, prev1, prev2, emit_tail):
    c = pl.program_id(1)

    @pl.when(c == 0)
    def _():
        h_ref[...] = _rmsnorm(x_ref[...], gn_ref[...]).astype(BF16)
        acc_ref[...] = jnp.zeros_like(acc_ref)

    h = h_ref[...]
    gate = jnp.dot(h, wg_ref[...], preferred_element_type=F32)
    up = jnp.dot(h, wu_ref[...], preferred_element_type=F32)
    cw = cw_ref[...]
    cv = cb_ref[...] + cw[0:1, :] * prev2(gate) + cw[1:2, :] * prev1(gate) + cw[2:3, :] * gate
    emit_tail(gate)
    acc_ref[...] += _mm(_gelu_tanh(cv) * up, wo_ref[...])

    @pl.when(c == pl.num_programs(1) - 1)
    def _():
        o_ref[...] = x_ref[...] + _rmsnorm(acc_ref[...], gp_ref[...])


def _ffn_prompt_kernel(tiles_per_seq, x_ref, gn_ref, wg_ref, wu_ref, cw_ref, cb_ref, wo_ref, gp_ref,
                       o_ref, tail_ref, h_ref, acc_ref, carry_ref):
    i = pl.program_id(0)
    c = pl.program_id(1)

    @pl.when(i % tiles_per_seq == 0)
    def _():
        carry_ref[c] = jnp.zeros(carry_ref.shape[1:], F32)

    hist = carry_ref[c]

    def shifted(gate, s):
        row = lax.broadcasted_iota(jnp.int32, gate.shape, 0)
        rolled = pltpu.roll(gate, s, 0)
        out = rolled
        for j in range(s):
            out = jnp.where(row == j, hist[8 - s + j:9 - s + j, :], out)
        return out

    def emit_tail(gate):
        tail_ref[...] = gate[gate.shape[0] - tail_ref.shape[0]:, :]

    _ffn_body(x_ref, gn_ref, wg_ref, wu_ref, cw_ref, cb_ref, wo_ref, gp_ref, o_ref,
              h_ref, acc_ref, lambda g: shifted(g, 1), lambda g: shifted(g, 2), emit_tail)
    carry_ref[c] = tail_ref[...]


def _ffn_sample_kernel(seq, x_ref, st_ref, gn_ref, wg_ref, wu_ref, cw_ref, cb_ref, wo_ref, gp_ref,
                       o_ref, tail_ref, h_ref, acc_ref, g_ref, p1_ref, p2_ref):
    nb = st_ref.shape[0]
    st0 = st_ref[:, 0, :]
    st1 = st_ref[:, 1, :]
    prev1 = lambda gate: _replace_step_rows(p1_ref, pltpu.roll(gate, 1, 0), seq, [(0, st1)])
    prev2 = lambda gate: _replace_step_rows(p2_ref, pltpu.roll(gate, 2, 0), seq, [(0, st0), (1, st1)])

    def emit_tail(gate):
        tiles = gate.shape[1] // LANES_V7X
        for j in range(tiles):
            g_ref[j] = gate[:, LANES_V7X * j:LANES_V7X * (j + 1)]
        for s in range(CONV_W - 1):
            rows = pl.ds(seq - (CONV_W - 1) + s, nb, stride=seq)
            tail_ref[:, s, :] = jnp.concatenate([g_ref[j, rows, :] for j in range(tiles)], axis=1)

    _ffn_body(x_ref, gn_ref, wg_ref, wu_ref, cw_ref, cb_ref, wo_ref, gp_ref, o_ref,
              h_ref, acc_ref, prev1, prev2, emit_tail)


def _ffn_prompt(x2d, seq_len, wts, tm, fk):
    n = x2d.shape[0]
    nc = D_FF // fk
    gn, w_in, cw, cb, w_out, gp = wts
    full = pl.BlockSpec((tm, D_MODEL), lambda i, c: (i, 0))
    return pl.pallas_call(
        functools.partial(_ffn_prompt_kernel, seq_len // tm),
        grid=(n // tm, nc),
        in_specs=[full, _const_spec(gn.shape),
                  pl.BlockSpec((D_MODEL, fk), lambda i, c: (0, c)),
                  pl.BlockSpec((D_MODEL, fk), lambda i, c: (0, c + nc)),
                  pl.BlockSpec((CONV_W, fk), lambda i, c: (0, c)),
                  pl.BlockSpec((1, fk), lambda i, c: (0, c)),
                  pl.BlockSpec((fk, D_MODEL), lambda i, c: (c, 0)),
                  _const_spec(gp.shape)],
        out_specs=[full, pl.BlockSpec((None, 8, fk), lambda i, c: (i, 0, c))],
        out_shape=[jax.ShapeDtypeStruct((n, D_MODEL), F32),
                   jax.ShapeDtypeStruct((n // tm, 8, D_FF), F32)],
        scratch_shapes=[pltpu.VMEM((tm, D_MODEL), BF16), pltpu.VMEM((tm, D_MODEL), F32),
                        pltpu.VMEM((nc, 8, fk), F32)],
        compiler_params=_params("arbitrary", "arbitrary"),
        name="ffn_prompt",
    )(x2d, gn, w_in, w_in, cw, cb, w_out, gp)


def _ffn_sample(x2d, st, seq, wts, fk):
    n = x2d.shape[0]
    nb = n // seq
    assert seq >= CONV_W - 1 and st.shape == (nb, CONV_W - 1, D_FF)
    nc = D_FF // fk
    gn, w_in, cw, cb, w_out, gp = wts
    full = pl.BlockSpec((n, D_MODEL), lambda i, c: (0, 0))
    cols = pl.BlockSpec((nb, CONV_W - 1, fk), lambda i, c: (0, 0, c))
    return pl.pallas_call(
        functools.partial(_ffn_sample_kernel, seq),
        grid=(1, nc),
        in_specs=[full, cols, _const_spec(gn.shape),
                  pl.BlockSpec((D_MODEL, fk), lambda i, c: (0, c)),
                  pl.BlockSpec((D_MODEL, fk), lambda i, c: (0, c + nc)),
                  pl.BlockSpec((CONV_W, fk), lambda i, c: (0, c)),
                  pl.BlockSpec((1, fk), lambda i, c: (0, c)),
                  pl.BlockSpec((fk, D_MODEL), lambda i, c: (c, 0)),
                  _const_spec(gp.shape)],
        out_specs=[full, cols],
        out_shape=[jax.ShapeDtypeStruct((n, D_MODEL), F32), jax.ShapeDtypeStruct(st.shape, F32)],
        scratch_shapes=[pltpu.VMEM((n, D_MODEL), BF16), pltpu.VMEM((n, D_MODEL), F32)]
        + [pltpu.VMEM((fk // LANES_V7X, n, LANES_V7X), F32)] * 3,
        compiler_params=_params("arbitrary", "arbitrary"),
        name="ffn_sample",
    )(x2d, st, gn, w_in, w_in, cw, cb, w_out, gp)


def _row(v):
    return v.reshape(1, -1).astype(F32)


def _ones_bd():
    i = jnp.arange(MXU_DIM_V7X) // HEAD_DIM
    return (i[:, None] == i[None, :]).astype(BF16)


def _layer_weights(l, norm_pre_mix, w_in, mu_shift, w0, w2, a0, a2, g2, k_k, k_a, r_k, lnx_w, lnx_b,
                   w_branch_a, pool_w, pool_scale, w_branch_b, w_out, norm_post_mix,
                   norm_pre_ffn, w_ffn_in, conv_w, conv_b, w_ffn_out, norm_post_ffn):
    w_t = jnp.swapaxes(w_in[l], 0, 1)
    mu = jnp.pad(mu_shift[l], (0, D_SHIFT_PAD - D_SHIFT)).reshape(1, -1)
    w_lora = jnp.zeros((D_LORA_PAD, 3 * D_RWKV), F32)
    w_lora = w_lora.at[0:RANK_W, 0:D_RWKV].set(w2[l])
    w_lora = w_lora.at[RANK_W:RANK_W + RANK_A, D_RWKV:2 * D_RWKV].set(a2[l])
    w_lora = w_lora.at[RANK_W + RANK_A:D_LORA, 2 * D_RWKV:].set(g2[l])
    ones_bd = _ones_bd()

    def with_bf16(wa, pw, wb, wo, wfi, wfo):
        return dict(
            merge=(_row(r_k[l]), _row(lnx_w[l]), _row(lnx_b[l]), ones_bd, wa,
                   pw.reshape(pool_w.shape[1:]), _row(pool_scale[l]), wb),
            mix_out=(wo, _row(norm_post_mix[l])),
            ffn=(_row(norm_pre_ffn[l]), wfi, conv_w[l].astype(F32), _row(conv_b[l]), wfo,
                 _row(norm_post_ffn[l])),
        )

    return dict(
        in_proj=(_row(norm_pre_mix[l]), w_t),
        prep=(mu, w_lora.astype(BF16), _row(w0[l]), _row(a0[l]), _row(k_k[l]), _row(k_a[l]), ones_bd),
        f32_matmul_weights=[w_branch_a[l], pool_w[l].reshape(-1, POOL_GROUP), w_branch_b[l], w_out[l],
                            w_ffn_in[l], w_ffn_out[l]],
        with_bf16=with_bf16,
    )


def _largest_tile(n, cap, mult=16):
    best = None
    for d in range(mult, min(n, cap) + 1, mult):
        if n % d == 0:
            best = d
    assert best is not None, (n, cap)
    return best


def _project(xp2d, xs2d, wts):
    g, w_t = wts["in_proj"]
    tm_norm = _largest_tile(math.gcd(xp2d.shape[0], xs2d.shape[0]), 512)
    h = _pre_norm(xp2d, xs2d, g, tm_norm)
    tm = _largest_tile(h.shape[0], 1088)
    zrkv = _in_proj(h, w_t, 0, 3 * D_RWKV, D_RWKV, tm)
    zl = _in_proj(h, w_t, 3 * D_RWKV, D_LORA_PAD, D_LORA_PAD, tm)
    zp = _in_proj(h, w_t, D_SHIFT, D_POOL, D_POOL, tm)
    zga = _in_proj(h, w_t, D_SHIFT + D_POOL, D_MODEL, D_MODEL // 2, tm)
    zgb = _in_proj(h, w_t, D_SHIFT + D_POOL + D_MODEL, D_MODEL, D_MODEL // 2, tm)
    return zrkv, zl, zp, zga, zgb


def _last_shift_row(zrkv, zl, row0, b, t):
    if b <= 8:
        last = lambda z, w: jnp.concatenate(
            [lax.slice(z, (row0 + (i + 1) * t - 1, 0), (row0 + (i + 1) * t, w)) for i in range(b)])
    else:
        last = lambda z, w: lax.slice(z, (row0 + t - 1, 0), (row0 + b * t, w), (t, 1))
    return jnp.concatenate([last(zrkv, 3 * D_RWKV), last(zl, D_LORA)], axis=-1)[:, None, :]


def _prompt_layer(x, z, wts):
    b, t, _ = x.shape
    n = b * t
    x2d = x.reshape(n, D_MODEL)
    zrkv, zl, zp, zga, zgb = z
    r, lw, k2, v, a_s, b_s, g = _prep_prompt(zrkv, zl, b, t, wts["prep"], tm=256)
    y, wkv, bf16_weights = _wkv_prompt(r, lw, k2, v, a_s, b_s, wts["f32_matmul_weights"])
    wts = wts["with_bf16"](*bf16_weights)
    d = _pool_prompt(zp, b, t, tm=512)
    flat = lambda a: a.reshape(n, a.shape[-1])
    m = _merge(flat(y), flat(r), flat(k2), flat(v), flat(g), flat(d), zga, zgb, 0, wts["merge"], tm=256)
    x1 = _mix_out(m, x2d, *wts["mix_out"], tm=512)
    ffn_tm = _largest_tile(t, 512)
    out, tail = _ffn_prompt(x1, t, wts["ffn"], tm=ffn_tm, fk=512)
    shift = _last_shift_row(zrkv, zl, 0, b, t)
    pool =jnp.stack([zp[(i + 1) * t - POOL_HIST:(i + 1) * t] for i in range(b)])
    tiles = t // ffn_tm
    conv = tail.reshape(b, tiles, 8, D_FF)[:, tiles - 1, 8 - (CONV_W - 1):, :]
    return wts, (out.reshape(b, t, D_MODEL), shift, wkv, pool, conv)


def _sample_layer(x, z, row0, st_shift, st_wkv, st_pool, st_conv, prep_wts, wts):
    b, t, _ = x.shape
    n = b * t
    x2d = x.reshape(n, D_MODEL)
    zrkv, zl, zp, zga, zgb = z
    st = jnp.pad(st_shift.reshape(b, D_SHIFT), ((0, 0), (0, D_SHIFT_PAD - D_SHIFT)))
    tm = _largest_tile(math.gcd(row0, n), 128, mult=SLAB * t)
    r, lw, k2, v, a_s, b_s, g = _prep_sample(zrkv, zl, row0, n, st, t, prep_wts, tm=tm)
    y, wkv = _wkv_steps(r, lw, k2, v, a_s, b_s, jnp.transpose(st_wkv, (1, 2, 3, 0)), t)
    wkv = jnp.transpose(wkv, (3, 0, 1, 2))
    d, pool = _pool_sample(jnp.swapaxes(st_pool, 0, 1), zp, row0, t)
    pool = jnp.swapaxes(pool, 0, 1)
    m = _merge(y, r, k2, v, g, d, zga, zgb, row0, wts["merge"], tm=_largest_tile(math.gcd(row0, n), 256, mult=8))
    x1 = _mix_out(m, x2d, *wts["mix_out"], tm=n)
    out, conv = _ffn_sample(x1, st_conv, t, wts["ffn"], fk=512)
    shift = _last_shift_row(zrkv, zl, row0, b, t)
    return out.reshape(b, t, D_MODEL), shift, wkv, pool, conv


def kernel(x_prompt, x_sample, state_shift, state_wkv, state_pool, state_conv, norm_pre_mix, w_in, mu_shift, w0, w2, a0, a2, g2, k_k, k_a, r_k, lnx_w, lnx_b, w_branch_a, pool_w, pool_scale, w_branch_b, w_out, norm_post_mix, norm_pre_ffn, w_ffn_in, conv_w, conv_b, w_ffn_out, norm_post_ffn):
    weights = (norm_pre_mix, w_in, mu_shift, w0, w2, a0, a2, g2, k_k, k_a, r_k, lnx_w, lnx_b,
               w_branch_a, pool_w, pool_scale, w_branch_b, w_out, norm_post_mix,
               norm_pre_ffn, w_ffn_in, conv_w, conv_b, w_ffn_out, norm_post_ffn)
    depth = w_in.shape[0]
    yp, ys = x_prompt, x_sample
    p_states, s_states = [], []
    for l in range(depth):
        wts = _layer_weights(l, *weights)
        n_prompt = yp.shape[0] * yp.shape[1]
        z = _project(yp.reshape(n_prompt, D_MODEL), ys.reshape(-1, D_MODEL), wts)
        bf16_wts, (yp, *ps) = _prompt_layer(yp, z, wts)
        ys, *ss = _sample_layer(ys, z, n_prompt, state_shift[l], state_wkv[l], state_pool[l], state_conv[l],
                                wts["prep"], bf16_wts)
        p_states.append(ps)
        s_states.append(ss)
    stack = lambda states, i: jnp.stack([s[i] for s in states])
    return (yp, ys,
            stack(p_states, 0), stack(p_states, 1), stack(p_states, 2), stack(p_states, 3),
            stack(s_states, 0), stack(s_states, 1), stack(s_states, 2), stack(s_states, 3))
```

```python
import functools
import math

import jax
import jax.numpy as jnp
from jax import lax
from jax.experimental import pallas as pl
from jax.experimental.pallas import tpu as pltpu

F32 = jnp.float32
BF16 = jnp.bfloat16

D_MODEL = 2048
HEAD_DIM = 64
D_RWKV = 1024
N_HEADS = 16
RANK_W, RANK_A, RANK_G = 64, 64, 160
D_LORA = RANK_W + RANK_A + RANK_G
D_LORA_PAD = 384
D_SHIFT = 3 * D_RWKV + D_LORA
D_SHIFT_PAD = 3 * D_RWKV + D_LORA_PAD
D_POOL = 1024
POOL_WINDOWS = (2, 4, 8, 16)
POOL_GROUP = 256
POOL_HIST = 15
D_FF = 5632
CONV_W = 3
NORM_EPS = 1e-6
GN_EPS = 64e-5
PAST_LEN = 16384

LANES_V7X = 128
SLAB = 8
MXU_DIM_V7X = 256
HEADS_PER_GROUP = MXU_DIM_V7X // HEAD_DIM
N_GROUPS = N_HEADS // HEADS_PER_GROUP
CHUNK = 64
VMEM_LIMIT_V7X = 56 * 1024 * 1024


def _params(*sem):
    return pltpu.CompilerParams(dimension_semantics=sem, vmem_limit_bytes=VMEM_LIMIT_V7X)


def _mm(a, b):
    return jnp.dot(a.astype(BF16), b.astype(BF16), preferred_element_type=F32)


def _mm_nt(a, b):
    return lax.dot_general(a.astype(BF16), b.astype(BF16), (((1,), (1,)), ((), ())),
                           preferred_element_type=F32)


def _split_hi_lo(x):
    hi = x.astype(BF16)
    lo = (x - hi.astype(F32)).astype(BF16)
    return hi, lo


def _head_sum(x, ones_bd):
    hi, lo = _split_hi_lo(x)
    outs = []
    for gi in range(x.shape[1] // MXU_DIM_V7X):
        sl = slice(MXU_DIM_V7X * gi, MXU_DIM_V7X * (gi + 1))
        outs.append(jnp.dot(hi[:, sl], ones_bd, preferred_element_type=F32)
                    + jnp.dot(lo[:, sl], ones_bd, preferred_element_type=F32))
    return jnp.concatenate(outs, axis=1)


def _rmsnorm(x, g):
    return x * lax.rsqrt(jnp.mean(x * x, axis=-1, keepdims=True) + NORM_EPS) * g


def _replace_step_rows(scr_ref, base, seq, steps):
    nb = base.shape[0] // seq
    for j in range(base.shape[1] // LANES_V7X):
        ls = slice(LANES_V7X * j, LANES_V7X * (j + 1))
        scr_ref[j] = base[:, ls]
        for t, vals in steps:
            scr_ref[j, pl.ds(t, nb, stride=seq), :] = vals[:, ls]
    return jnp.concatenate([scr_ref[j] for j in range(base.shape[1] // LANES_V7X)], axis=1)


def _pre_norm_kernel(prompt_tiles, xp_ref, xs_ref, g_ref, h_ref):
    i = pl.program_id(0)

    @pl.when(i < prompt_tiles)
    def _():
        h_ref[...] = _rmsnorm(xp_ref[...], g_ref[...]).astype(BF16)

    @pl.when(i >= prompt_tiles)
    def _():
        h_ref[...] = _rmsnorm(xs_ref[...], g_ref[...]).astype(BF16)


def _pre_norm(xp, xs, g, tm):
    pt, st = xp.shape[0] // tm, xs.shape[0] // tm
    return pl.pallas_call(
        functools.partial(_pre_norm_kernel, pt),
        grid=(pt + st,),
        in_specs=[pl.BlockSpec((tm, D_MODEL), lambda i: (jnp.minimum(i, pt - 1), 0)),
                  pl.BlockSpec((tm, D_MODEL), lambda i: (jnp.maximum(i - pt, 0), 0)),
                  _const_spec(g.shape)],
        out_specs=pl.BlockSpec((tm, D_MODEL), lambda i: (i, 0)),
        out_shape=jax.ShapeDtypeStruct((xp.shape[0] + xs.shape[0], D_MODEL), BF16),
        compiler_params=_params("arbitrary"),
        name="pre_norm",
    )(xp, xs, g)


def _in_proj_kernel(h_ref, w_ref, o_ref, wb_ref):
    @pl.when(pl.program_id(1) == 0)
    def _():
        wb_ref[...] = w_ref[...].astype(BF16)

    o_ref[...] = lax.dot_general(h_ref[...], wb_ref[...], (((1,), (1,)), ((), ())),
                                 preferred_element_type=F32)


def _in_proj(h, w_t, off, width, tn, tm):
    n = h.shape[0]
    assert width % tn == 0 and n % tm == 0
    return pl.pallas_call(
        _in_proj_kernel,
        grid=(width // tn, n // tm),
        in_specs=[pl.BlockSpec((tm, D_MODEL), lambda j, i: (i, 0)),
                  pl.BlockSpec((pl.Element(tn), pl.Element(D_MODEL)),
                               lambda j, i: (SLAB * (off // SLAB + j * (tn // SLAB)), 0))],
        out_specs=pl.BlockSpec((tm, tn), lambda j, i: (i, j)),
        out_shape=jax.ShapeDtypeStruct((n, width), F32),
        scratch_shapes=[pltpu.VMEM((tn, D_MODEL), BF16)],
        compiler_params=_params("arbitrary", "arbitrary"),
        name="in_proj",
    )(h, w_t)


def _prep_math(zs, prev, mu, w_lora, w0, a0, k_k, k_a, ones_bd):
    f = zs + (prev - zs) * mu
    r = f[:, 0:D_RWKV]
    k = f[:, D_RWKV:2 * D_RWKV]
    v = f[:, 2 * D_RWKV:3 * D_RWKV]
    low = f[:, 3 * D_RWKV:3 * D_RWKV + D_LORA_PAD]
    lane = lax.broadcasted_iota(jnp.int32, low.shape, 1)
    act = jnp.where(lane < RANK_W, jnp.tanh(low),
                    jnp.where(lane < RANK_W + RANK_A, low,
                              jnp.where(lane < D_LORA, jax.nn.sigmoid(low), 0.0)))
    lora = _mm(act, w_lora)
    u = w0 + lora[:, 0:D_RWKV]
    lw = -math.exp(-0.5) * jax.nn.sigmoid(u)
    a = jax.nn.sigmoid(a0 + lora[:, D_RWKV:2 * D_RWKV])
    g = lora[:, 2 * D_RWKV:3 * D_RWKV]
    kk = k * k_k
    kk = kk * jnp.minimum(lax.rsqrt(_head_sum(kk * kk, ones_bd)), 1e12)
    k2 = k * (1.0 + (a - 1.0) * k_a)
    return r, lw, k2, v, -kk, kk * a, g


def _prep_prompt_kernel(zrkv_ref, zl_ref, mu_ref, wl_ref, w0_ref, a0_ref, kk_ref, ka_ref, ones_ref,
                        r_ref, lw_ref, k2_ref, v_ref, as_ref, bs_ref, g_ref, carry_ref):
    i = pl.program_id(1)

    @pl.when(i == 0)
    def _():
        carry_ref[...] = jnp.zeros_like(carry_ref)

    zs = jnp.concatenate([zrkv_ref[...], zl_ref[...]], axis=1)
    rolled = pltpu.roll(zs, 1, 0)
    row = lax.broadcasted_iota(jnp.int32, zs.shape, 0)
    prev = jnp.where(row == 0, carry_ref[0:1, :], rolled)
    carry_ref[0:1, :] = zs[zs.shape[0] - 1:, :]
    outs = _prep_math(zs, prev, mu_ref[...], wl_ref[...], w0_ref[...], a0_ref[...],
                      kk_ref[...], ka_ref[...], ones_ref[...])
    for o_ref, o in zip((r_ref, lw_ref, k2_ref, v_ref, as_ref, bs_ref, g_ref), outs):
        o_ref[...] = o


def _prep_sample_kernel(seq, zrkv_ref, zl_ref, st_ref, mu_ref, wl_ref, w0_ref, a0_ref, kk_ref, ka_ref,
                        ones_ref, r_ref, lw_ref, k2_ref, v_ref, as_ref, bs_ref, g_ref, prev_ref):
    zs = jnp.concatenate([zrkv_ref[...], zl_ref[...]], axis=1)
    prev = _replace_step_rows(prev_ref, pltpu.roll(zs, 1, 0), seq, [(0, st_ref[...])])
    outs = _prep_math(zs, prev, mu_ref[...], wl_ref[...], w0_ref[...], a0_ref[...],
                      kk_ref[...], ka_ref[...], ones_ref[...])
    for o_ref, o in zip((r_ref, lw_ref, k2_ref, v_ref, as_ref, bs_ref, g_ref), outs):
        o_ref[...] = o


def _const_spec(shape):
    nd = len(shape)
    return pl.BlockSpec(shape, lambda *_: (0,) * nd)


def _prep_prompt(zrkv, zl, b, t, wts, tm):
    tiles = t // tm
    outs = [jax.ShapeDtypeStruct((b, t, D_RWKV), F32)] * 7
    row_spec = pl.BlockSpec((None, tm, D_RWKV), lambda bi, i: (bi, i, 0))
    return pl.pallas_call(
        _prep_prompt_kernel,
        grid=(b, tiles),
        in_specs=[pl.BlockSpec((tm, 3 * D_RWKV), lambda bi, i: (bi * tiles + i, 0)),
                  pl.BlockSpec((tm, D_LORA_PAD), lambda bi, i: (bi * tiles + i, 0))]
        + [_const_spec(w.shape) for w in wts],
        out_specs=[row_spec] * 7,
        out_shape=outs,
        scratch_shapes=[pltpu.VMEM((8, D_SHIFT_PAD), F32)],
        compiler_params=_params("parallel", "arbitrary"),
        name="prep_prompt",
    )(zrkv, zl, *wts)


def _prep_sample(zrkv, zl, row0, n, st, seq, wts, tm):
    assert tm % (SLAB * seq) == 0 and n % tm == 0 and row0 % tm == 0
    first = row0 // tm
    outs = [jax.ShapeDtypeStruct((n, D_RWKV), F32)] * 7
    row_spec = pl.BlockSpec((tm, D_RWKV), lambda i: (i, 0))
    return pl.pallas_call(
        functools.partial(_prep_sample_kernel, seq),
        grid=(n // tm,),
        in_specs=[pl.BlockSpec((tm, 3 * D_RWKV), lambda i: (first + i, 0)),
                  pl.BlockSpec((tm, D_LORA_PAD), lambda i: (first + i, 0)),
                  pl.BlockSpec((tm // seq, D_SHIFT_PAD), lambda i: (i, 0))]
        + [_const_spec(w.shape) for w in wts],
        out_specs=[row_spec] * 7,
        out_shape=outs,
        scratch_shapes=[pltpu.VMEM((D_SHIFT_PAD // LANES_V7X, tm, LANES_V7X), F32)],
        compiler_params=_params("parallel"),
        name="prep_sample",
    )(zrkv, zl, st, *wts)


def _expand_bd(a, bd_mask):
    return jnp.where(bd_mask, jnp.concatenate([a] * HEADS_PER_GROUP, axis=0), 0.0)


def _chunk_cumsum(x):
    row = lax.broadcasted_iota(jnp.int32, x.shape, 0)
    s = 1
    while s < x.shape[0]:
        x = x + jnp.where(row >= s, pltpu.roll(x, s, 0), 0.0)
        s *= 2
    return x


def _wkv_intra(xs, bts, kts, vs, pzs, masks, order):
    bd_mask, strict_mask, incl_mask, eye_c, head_masks = masks
    c = vs[0].shape[0]
    ystacks = [jnp.concatenate([jnp.where(m, bt, 0.0) for m in head_masks]
                               + [jnp.where(m, kt, 0.0) for m in head_masks], axis=0)
               for bt, kt in zip(bts, kts)]
    grams = [_mm_nt(x, ys) for x, ys in zip(xs, ystacks)]
    l_abs = [jnp.where(strict_mask, g[0:c, 0:4 * c], 0.0) for g in grams]
    l_aks = [jnp.where(strict_mask, g[0:c, 4 * c:8 * c], 0.0) for g in grams]
    m_rbs = [jnp.where(incl_mask, g[c:2 * c, 0:4 * c], 0.0) for g in grams]
    m_rks = [jnp.where(incl_mask, g[c:2 * c, 4 * c:8 * c], 0.0) for g in grams]

    tinvs = [eye_c + l for l in l_abs]
    lps = [_mm(l, _expand_bd(l, bd_mask)) for l in l_abs]
    p = 2
    while True:
        rhss = [_expand_bd(lp, bd_mask) for lp in lps]
        if 2 * p >= order:
            tinvs = [t + _mm(t, rhs) for t, rhs in zip(tinvs, rhss)]
            break
        ress = [_mm(jnp.concatenate([lp, t], axis=0), rhs) for lp, t, rhs in zip(lps, tinvs, rhss)]
        lps = [res[0:c] for res in ress]
        tinvs = [t + res[c:2 * c] for t, res in zip(tinvs, ress)]
        p *= 2

    v_bds = [_expand_bd(v, bd_mask) for v in vs]
    ws = [pz[0:c] + _mm(l_ak, v_bd) for pz, l_ak, v_bd in zip(pzs, l_aks, v_bds)]
    us = [_mm(t, _expand_bd(w, bd_mask)) for t, w in zip(tinvs, ws)]
    ys = [pz[c:2 * c] + _mm(jnp.concatenate([m_rb, m_rk], axis=1),
                            jnp.concatenate([_expand_bd(u, bd_mask), v_bd], axis=0))
          for pz, m_rb, m_rk, u, v_bd in zip(pzs, m_rbs, m_rks, us, v_bds)]
    return ys, us


def _wkv_chunk(groups, masks):
    c = groups[0][0].shape[0]
    cls = [_chunk_cumsum(g[1]) for g in groups]
    xs = [jnp.concatenate([g[4] * jnp.exp(cl - g[1]), g[0] * jnp.exp(cl)], axis=0)
          for g, cl in zip(groups, cls)]
    pzs = [_mm_nt(x, g[6]) for x, g in zip(xs, groups)]
    e_negs = [jnp.exp(-cl) for cl in cls]
    ys, us = _wkv_intra(xs, [g[5] * e for g, e in zip(groups, e_negs)],
                        [g[2] * e for g, e in zip(groups, e_negs)], [g[3] for g in groups], pzs, masks, c)
    dss = []
    for g, cl, u in zip(groups, cls, us):
        e_rem = jnp.exp(cl[c - 1:c, :] - cl)
        uv_t = jnp.concatenate([u, g[3]], axis=0).T
        bk = jnp.concatenate([g[5] * e_rem, g[2] * e_rem], axis=0)
        dss.append(_mm(uv_t, bk))
    return [(y, g[6] * jnp.exp(cl[c - 1:c, :]) + jnp.where(masks[0], ds, 0.0))
            for y, g, cl, ds in zip(ys, groups, cls, dss)]


def _wkv_masks(c, seq):
    n = HEADS_PER_GROUP * c
    rr = lax.broadcasted_iota(jnp.int32, (n, n), 0)
    cc = lax.broadcasted_iota(jnp.int32, (n, n), 1)
    bd_mask = (rr // c) == (cc // c)
    t = lax.broadcasted_iota(jnp.int32, (c, n), 0)
    s = lax.broadcasted_iota(jnp.int32, (c, n), 1) % c
    same = (t // seq) == (s // seq)
    strict_mask = same & (t > s)
    incl_mask = same & (t >= s)
    eye_c = jnp.where(t == s, 1.0, 0.0).astype(F32)
    lane = lax.broadcasted_iota(jnp.int32, (c, MXU_DIM_V7X), 1)
    head_masks = [(lane // HEAD_DIM) == h for h in range(HEADS_PER_GROUP)]
    return bd_mask, strict_mask, incl_mask, eye_c, head_masks


def _wkv_prompt_kernel(n_casts, r_ref, lw_ref, k2_ref, v_ref, as_ref, bs_ref, *refs):
    w_refs, (y_ref, sout_ref), wb_refs, (s_ref,) = (
        refs[:n_casts], refs[n_casts:n_casts + 2], refs[n_casts + 2:2 * n_casts + 2], refs[2 * n_casts + 2:])
    ci = pl.program_id(0)

    @pl.when(ci == 0)
    def _():
        s_ref[...] = jnp.zeros_like(s_ref)

    for w_ref, wb_ref in zip(w_refs, wb_refs):
        wb_ref[...] = w_ref[...].astype(BF16)

    masks = _wkv_masks(CHUNK, CHUNK)
    chains = [(bi, gi) for bi in range(r_ref.shape[0]) for gi in range(N_GROUPS)]
    lanes = lambda gi: slice(MXU_DIM_V7X * gi, MXU_DIM_V7X * (gi + 1))
    groups = [tuple(ref[bi, :, lanes(gi)] for ref in (r_ref, lw_ref, k2_ref, v_ref, as_ref, bs_ref))
              + (s_ref[bi, gi],) for bi, gi in chains]
    for (bi, gi), (y, s_new) in zip(chains, _wkv_chunk(groups, masks)):
        y_ref[bi, :, lanes(gi)] = y
        s_ref[bi, gi] = s_new

    @pl.when(ci == pl.num_programs(0) - 1)
    def _():
        for bi, gi in chains:
            s_bd = s_ref[bi, gi]
            for h in range(HEADS_PER_GROUP):
                hs = slice(HEAD_DIM * h, HEAD_DIM * (h + 1))
                sout_ref[bi, HEADS_PER_GROUP * gi + h] = s_bd[hs, hs]


def _wkv_prompt(r, lw, k2, v, a_s, b_s, weights):
    b, t, _ = r.shape
    steps = t // CHUNK
    spec = pl.BlockSpec((b, CHUNK, D_RWKV), lambda ci: (0, ci, 0))
    state = (b, N_GROUPS, MXU_DIM_V7X, MXU_DIM_V7X)
    out_state = (b, N_HEADS, HEAD_DIM, HEAD_DIM)
    bf16_rows = 2 * SLAB
    assert all(w.shape[0] % (steps * bf16_rows) == 0 for w in weights)
    slabs = [pl.BlockSpec((w.shape[0] // steps, w.shape[1]), lambda ci: (ci, 0)) for w in weights]
    y, s, *wb = pl.pallas_call(
        functools.partial(_wkv_prompt_kernel, len(weights)),
        grid=(steps,),
        in_specs=[spec] * 6 + slabs,
        out_specs=[spec, _const_spec(out_state)] + slabs,
        out_shape=[jax.ShapeDtypeStruct((b, t, D_RWKV), F32), jax.ShapeDtypeStruct(out_state, F32)]
        + [jax.ShapeDtypeStruct(w.shape, BF16) for w in weights],
        scratch_shapes=[pltpu.VMEM(state, F32)],
        compiler_params=_params("arbitrary"),
        name="wkv_prompt",
    )(r, lw, k2, v, a_s, b_s, *weights)
    return y, s, wb


HEAD_PAIR = 2 * HEAD_DIM


def _wkv_steps_kernel(seq, r_ref, lw_ref, k2_ref, v_ref, as_ref, bs_ref, s0_ref, y_ref, sout_ref,
                      op_ref, vt_ref, yt_ref):
    nb = r_ref.shape[0] // seq
    for t in range(seq):
        rows = pl.ds(t, nb, stride=seq)
        op_ref[0, t] = r_ref[rows, :].T
        op_ref[1, t] = jnp.exp(lw_ref[rows, :]).T
        op_ref[2, t] = k2_ref[rows, :].T
        op_ref[3, t] = as_ref[rows, :].T
        op_ref[4, t] = bs_ref[rows, :].T
        vt_ref[t] = v_ref[rows, :].T

    for hh in range(HEAD_PAIR // HEAD_DIM):
        ks = slice(HEAD_DIM * hh, HEAD_DIM * (hh + 1))

        def body(vi, carry):
            s = s0_ref[hh, vi]
            for t in range(seq):
                sa = jnp.sum(s * op_ref[3, t, ks, :], axis=0, keepdims=True)
                vrow = vt_ref[t, pl.ds(HEAD_DIM * hh + vi, 1), :]
                s = s * op_ref[1, t, ks, :] + sa * op_ref[4, t, ks, :] + vrow * op_ref[2, t, ks, :]
                yt_ref[t, pl.ds(HEAD_DIM * hh + vi, 1), :] = jnp.sum(s * op_ref[0, t, ks, :], axis=0,
                                                                      keepdims=True)
            sout_ref[hh, vi] = s
            return carry

        lax.fori_loop(0, HEAD_DIM, body, 0, unroll=4)

    for t in range(seq):
        y_ref[pl.ds(t, nb, stride=seq), :] = yt_ref[t].T


def _wkv_steps(r, lw, k2, v, a_s, b_s, s0_native, seq):
    n = r.shape[0]
    nb = n // seq
    assert s0_native.shape == (N_HEADS, HEAD_DIM, HEAD_DIM, nb) and nb == LANES_V7X
    vec = pl.BlockSpec((n, HEAD_PAIR), lambda i: (0, i))
    st = pl.BlockSpec((HEAD_PAIR // HEAD_DIM, HEAD_DIM, HEAD_DIM, nb), lambda i: (i, 0, 0, 0))
    return pl.pallas_call(
        functools.partial(_wkv_steps_kernel, seq),
        grid=(N_HEADS * HEAD_DIM // HEAD_PAIR,),
        in_specs=[vec] * 6 + [st],
        out_specs=[vec, st],
        out_shape=[jax.ShapeDtypeStruct((n, D_RWKV), F32), jax.ShapeDtypeStruct(s0_native.shape, F32)],
        scratch_shapes=[pltpu.VMEM((5, seq, HEAD_PAIR, nb), F32), pltpu.VMEM((seq, HEAD_PAIR, nb), F32),
                        pltpu.VMEM((seq, HEAD_PAIR, nb), F32)],
        compiler_params=_params("parallel"),
        name="wkv_steps",
    )(r, lw, k2, v, a_s, b_s, s0_native)


POOL_CARRY = 16


def _pool_prompt_tile(zp, carry_ref, tile_in_seq):
    tm = zp.shape[0]

    @pl.when(tile_in_seq == 0)
    def _():
        carry_ref[...] = jnp.zeros_like(carry_ref)

    buf = jnp.concatenate([carry_ref[...], zp], axis=0)
    carry_ref[...] = zp[tm - POOL_CARRY:, :]
    pos = (tile_in_seq * tm + lax.broadcasted_iota(jnp.int32, (tm, POOL_GROUP), 0) + 1).astype(F32)
    ds = []
    for gi, win in enumerate(POOL_WINDOWS):
        sl = slice(POOL_GROUP * gi, POOL_GROUP * (gi + 1))
        acc = buf[:, sl]
        s = 1
        while s < win:
            acc = acc + pltpu.roll(acc, s, 0)
            s *= 2
        ds.append(acc[POOL_CARRY:, :] / jnp.minimum(float(win), pos) - zp[:, sl])
    return jnp.concatenate(ds, axis=1)


def _pool_sample_kernel(seq, hist_ref, zp_ref, d_ref, new_ref):
    nb = hist_ref.shape[1]
    step = lambda t: zp_ref[pl.ds(t, nb, stride=seq), :]
    past = lambda j: step(j) if j >= 0 else hist_ref[POOL_HIST + j]
    tiles_per_group = POOL_GROUP // LANES_V7X
    for gi, win in enumerate(POOL_WINDOWS):
        @pl.when(pl.program_id(0) // tiles_per_group == gi)
        def _():
            for t in range(seq):
                acc = past(t)
                for j in range(1, win):
                    acc = acc + past(t - j)
                d_ref[pl.ds(t, nb, stride=seq), :] = acc / float(min(win, PAST_LEN + 1)) - past(t)
    for j in range(POOL_HIST):
        src = j + seq
        new_ref[j] = hist_ref[src] if src < POOL_HIST else step(src - POOL_HIST)


def _pool_sample(hist, zp, row0, seq):
    nb = hist.shape[1]
    n = nb * seq
    assert row0 % n == 0
    hist_spec = pl.BlockSpec((POOL_HIST, nb, LANES_V7X), lambda j: (0, 0, j))
    return pl.pallas_call(
        functools.partial(_pool_sample_kernel, seq),
        grid=(D_POOL // LANES_V7X,),
        in_specs=[hist_spec, pl.BlockSpec((n, LANES_V7X), lambda j: (row0 // n, j))],
        out_specs=[pl.BlockSpec((n, LANES_V7X), lambda j: (0, j)), hist_spec],
        out_shape=[jax.ShapeDtypeStruct((n, D_POOL), F32), jax.ShapeDtypeStruct(hist.shape, F32)],
        compiler_params=_params("parallel"),
        name="pool_sample",
    )(hist, zp)


def _merge_kernel(seq_tiles, y_ref, r_ref, k2_ref, v_ref, g_ref, d_ref, zga_ref, zgb_ref,
                  rk_ref, lw_ref, lb_ref, ones_ref, wa_ref, pw_ref, ps_ref, wb_ref, m_ref, *carry):
    if seq_tiles is None:
        d = d_ref[...]
    else:
        d = _pool_prompt_tile(d_ref[...], carry[0], pl.program_id(0) % seq_tiles)
    ones_bd = ones_ref[...]
    y = y_ref[...]
    mu = _head_sum(y, ones_bd) * (1.0 / HEAD_DIM)
    yc = y - mu
    var = _head_sum(yc * yc, ones_bd) * (1.0 / HEAD_DIM)
    yn = yc * lax.rsqrt(var + GN_EPS) * lw_ref[...] + lb_ref[...]
    v = v_ref[...]
    bonus = _head_sum(r_ref[...] * k2_ref[...] * rk_ref[...], ones_bd) * v
    ya = (yn + bonus) * g_ref[...]
    yb = jnp.concatenate(
        [_mm(d[:, POOL_GROUP * gi:POOL_GROUP * (gi + 1)], pw_ref[gi]) for gi in range(len(POOL_WINDOWS))],
        axis=1) * ps_ref[...]
    m_ref[...] = (jax.nn.sigmoid(zga_ref[...]) * _mm(ya, wa_ref[...])
                  + jax.nn.sigmoid(zgb_ref[...]) * _mm(yb, wb_ref[...]))


def _merge(y, r, k2, v, g, d, zga, zgb, row0, wts, tm, seq_len=None):
    n = y.shape[0]
    assert row0 % tm == 0 and (seq_len is None or (seq_len % tm == 0 and tm >= POOL_CARRY))
    first = row0 // tm
    half = pl.BlockSpec((tm, D_RWKV), lambda i: (i, 0))
    full = pl.BlockSpec((tm, D_MODEL), lambda i: (i, 0))
    gate = pl.BlockSpec((tm, D_MODEL), lambda i: (first + i, 0))
    return pl.pallas_call(
        functools.partial(_merge_kernel, None if seq_len is None else seq_len // tm),
        grid=(n // tm,),
        in_specs=[half] * 6 + [gate] * 2 + [_const_spec(w.shape) for w in wts],
        out_specs=full,
        out_shape=jax.ShapeDtypeStruct((n, D_MODEL), F32),
        scratch_shapes=[] if seq_len is None else [pltpu.VMEM((POOL_CARRY, D_POOL), F32)],
        compiler_params=_params("parallel" if seq_len is None else "arbitrary"),
        name="merge",
    )(y, r, k2, v, g, d, zga, zgb, *wts)


def _mix_out_kernel(m_ref, x_ref, w_ref, g_ref, gn_ref, o_ref, h_ref):
    mo = _mm(m_ref[...], w_ref[...])
    x1 = x_ref[...] + _rmsnorm(mo, g_ref[...])
    o_ref[...] = x1
    h_ref[...] = _rmsnorm(x1, gn_ref[...]).astype(BF16)


def _mix_out(m, x2d, w_out, g, g_next, tm):
    n = m.shape[0]
    full = pl.BlockSpec((tm, D_MODEL), lambda i: (i, 0))
    return pl.pallas_call(
        _mix_out_kernel,
        grid=(n // tm,),
        in_specs=[full, full, _const_spec(w_out.shape), _const_spec(g.shape), _const_spec(g_next.shape)],
        out_specs=[full, full],
        out_shape=[jax.ShapeDtypeStruct((n, D_MODEL), F32), jax.ShapeDtypeStruct((n, D_MODEL), BF16)],
        compiler_params=_params("parallel"),
        name="mix_out",
    )(m, x2d, w_out, g, g_next)


def _gelu_tanh(x):
    return 0.5 * x * (1.0 + jnp.tanh(0.7978845608028654 * (x + 0.044715 * x * x * x)))


def _ffn_body(x_ref, h_ref, wg_ref, wu_ref, cw_ref, cb_ref, wo_ref, gp_ref,
              o_ref, acc_ref, prev1, prev2, emit_tail):
    c = pl.program_id(1)

    @pl.when(c == 0)
    def _():
        acc_ref[...] = jnp.zeros_like(acc_ref)

    h = h_ref[...]
    gate = jnp.dot(h, wg_ref[...], preferred_element_type=F32)
    up = jnp.dot(h, wu_ref[...], preferred_element_type=F32)
    cw = cw_ref[...]
    cv = cb_ref[...] + cw[0:1, :] * prev2(gate) + cw[1:2, :] * prev1(gate) + cw[2:3, :] * gate
    emit_tail(gate)
    acc_ref[...] += _mm(_gelu_tanh(cv) * up, wo_ref[...])

    @pl.when(c == pl.num_programs(1) - 1)
    def _():
        o_ref[...] = x_ref[...] + _rmsnorm(acc_ref[...], gp_ref[...])


def _ffn_prompt_kernel(tiles_per_seq, x_ref, h_ref, wg_ref, wu_ref, cw_ref, cb_ref, wo_ref, gp_ref,
                       o_ref, tail_ref, acc_ref, carry_ref):
    i = pl.program_id(0)
    c = pl.program_id(1)

    @pl.when(i % tiles_per_seq == 0)
    def _():
        carry_ref[c] = jnp.zeros(carry_ref.shape[1:], F32)

    hist = carry_ref[c]

    def shifted(gate, s):
        row = lax.broadcasted_iota(jnp.int32, gate.shape, 0)
        rolled = pltpu.roll(gate, s, 0)
        out = rolled
        for j in range(s):
            out = jnp.where(row == j, hist[8 - s + j:9 - s + j, :], out)
        return out

    def emit_tail(gate):
        tail_ref[...] = gate[gate.shape[0] - tail_ref.shape[0]:, :]

    _ffn_body(x_ref, h_ref, wg_ref, wu_ref, cw_ref, cb_ref, wo_ref, gp_ref, o_ref,
              acc_ref, lambda g: shifted(g, 1), lambda g: shifted(g, 2), emit_tail)
    carry_ref[c] = tail_ref[...]


def _ffn_sample_kernel(seq, x_ref, st_ref, h_ref, wg_ref, wu_ref, cw_ref, cb_ref, wo_ref, gp_ref,
                       o_ref, tail_ref, acc_ref, g_ref, p1_ref, p2_ref):
    nb = st_ref.shape[0]
    st0 = st_ref[:, 0, :]
    st1 = st_ref[:, 1, :]
    prev1 = lambda gate: _replace_step_rows(p1_ref, pltpu.roll(gate, 1, 0), seq, [(0, st1)])
    prev2 = lambda gate: _replace_step_rows(p2_ref, pltpu.roll(gate, 2, 0), seq, [(0, st0), (1, st1)])

    def emit_tail(gate):
        tiles = gate.shape[1] // LANES_V7X
        for j in range(tiles):
            g_ref[j] = gate[:, LANES_V7X * j:LANES_V7X * (j + 1)]
        for s in range(CONV_W - 1):
            rows = pl.ds(seq - (CONV_W - 1) + s, nb, stride=seq)
            tail_ref[:, s, :] = jnp.concatenate([g_ref[j, rows, :] for j in range(tiles)], axis=1)

    _ffn_body(x_ref, h_ref, wg_ref, wu_ref, cw_ref, cb_ref, wo_ref, gp_ref, o_ref,
              acc_ref, prev1, prev2, emit_tail)


def _ffn_prompt(x2d, h, seq_len, wts, tm, fk):
    n = x2d.shape[0]
    nc = D_FF // fk
    _, w_in, cw, cb, w_out, gp = wts
    full = pl.BlockSpec((tm, D_MODEL), lambda i, c: (i, 0))
    return pl.pallas_call(
        functools.partial(_ffn_prompt_kernel, seq_len // tm),
        grid=(n // tm, nc),
        in_specs=[full, full,
                  pl.BlockSpec((D_MODEL, fk), lambda i, c: (0, c)),
                  pl.BlockSpec((D_MODEL, fk), lambda i, c: (0, c + nc)),
                  pl.BlockSpec((CONV_W, fk), lambda i, c: (0, c)),
                  pl.BlockSpec((1, fk), lambda i, c: (0, c)),
                  pl.BlockSpec((fk, D_MODEL), lambda i, c: (c, 0)),
                  _const_spec(gp.shape)],
        out_specs=[full, pl.BlockSpec((None, 8, fk), lambda i, c: (i, 0, c))],
        out_shape=[jax.ShapeDtypeStruct((n, D_MODEL), F32),
                   jax.ShapeDtypeStruct((n // tm, 8, D_FF), F32)],
        scratch_shapes=[pltpu.VMEM((tm, D_MODEL), F32), pltpu.VMEM((nc, 8, fk), F32)],
        compiler_params=_params("arbitrary", "arbitrary"),
        name="ffn_prompt",
    )(x2d, h, w_in, w_in, cw, cb, w_out, gp)


def _ffn_sample(x2d, h, st, seq, wts, fk):
    n = x2d.shape[0]
    nb = n // seq
    assert seq >= CONV_W - 1 and st.shape == (nb, CONV_W - 1, D_FF)
    nc = D_FF // fk
    _, w_in, cw, cb, w_out, gp = wts
    full = pl.BlockSpec((n, D_MODEL), lambda i, c: (0, 0))
    cols = pl.BlockSpec((nb, CONV_W - 1, fk), lambda i, c: (0, 0, c))
    return pl.pallas_call(
        functools.partial(_ffn_sample_kernel, seq),
        grid=(1, nc),
        in_specs=[full, cols, full,
                  pl.BlockSpec((D_MODEL, fk), lambda i, c: (0, c)),
                  pl.BlockSpec((D_MODEL, fk), lambda i, c: (0, c + nc)),
                  pl.BlockSpec((CONV_W, fk), lambda i, c: (0, c)),
                  pl.BlockSpec((1, fk), lambda i, c: (0, c)),
                  pl.BlockSpec((fk, D_MODEL), lambda i, c: (c, 0)),
                  _const_spec(gp.shape)],
        out_specs=[full, cols],
        out_shape=[jax.ShapeDtypeStruct((n, D_MODEL), F32), jax.ShapeDtypeStruct(st.shape, F32)],
        scratch_shapes=[pltpu.VMEM((n, D_MODEL), F32)]
        + [pltpu.VMEM((fk // LANES_V7X, n, LANES_V7X), F32)] * 3,
        compiler_params=_params("arbitrary", "arbitrary"),
        name="ffn_sample",
    )(x2d, st, h, w_in, w_in, cw, cb, w_out, gp)


def _row(v):
    return v.reshape(1, -1).astype(F32)


def _ones_bd():
    i = jnp.arange(MXU_DIM_V7X) // HEAD_DIM
    return (i[:, None] == i[None, :]).astype(BF16)


def _layer_weights(l, norm_pre_mix, w_in, mu_shift, w0, w2, a0, a2, g2, k_k, k_a, r_k, lnx_w, lnx_b,
                   w_branch_a, pool_w, pool_scale, w_branch_b, w_out, norm_post_mix,
                   norm_pre_ffn, w_ffn_in, conv_w, conv_b, w_ffn_out, norm_post_ffn):
    w_t = jnp.swapaxes(w_in[l], 0, 1)
    mu = jnp.pad(mu_shift[l], (0, D_SHIFT_PAD - D_SHIFT)).reshape(1, -1)
    w_lora = jnp.zeros((D_LORA_PAD, 3 * D_RWKV), F32)
    w_lora = w_lora.at[0:RANK_W, 0:D_RWKV].set(w2[l])
    w_lora = w_lora.at[RANK_W:RANK_W + RANK_A, D_RWKV:2 * D_RWKV].set(a2[l])
    w_lora = w_lora.at[RANK_W + RANK_A:D_LORA, 2 * D_RWKV:].set(g2[l])
    ones_bd = _ones_bd()

    def with_bf16(wa, pw, wb, wo, wfi, wfo):
        return dict(
            merge=(_row(r_k[l]), _row(lnx_w[l]), _row(lnx_b[l]), ones_bd, wa,
                   pw.reshape(pool_w.shape[1:]), _row(pool_scale[l]), wb),
            mix_out=(wo, _row(norm_post_mix[l])),
            ffn=(_row(norm_pre_ffn[l]), wfi, conv_w[l].astype(F32), _row(conv_b[l]), wfo,
                 _row(norm_post_ffn[l])),
        )

    return dict(
        in_proj=(_row(norm_pre_mix[l]), w_t),
        prep=(mu, w_lora.astype(BF16), _row(w0[l]), _row(a0[l]), _row(k_k[l]), _row(k_a[l]), ones_bd),
        f32_matmul_weights=[w_branch_a[l], pool_w[l].reshape(-1, POOL_GROUP), w_branch_b[l], w_out[l],
                            w_ffn_in[l], w_ffn_out[l]],
        with_bf16=with_bf16,
    )


def _largest_tile(n, cap, mult=16):
    best = None
    for d in range(mult, min(n, cap) + 1, mult):
        if n % d == 0:
            best = d
    assert best is not None, (n, cap)
    return best


def _project(xp2d, xs2d, wts):
    g, w_t = wts["in_proj"]
    tm_norm = _largest_tile(math.gcd(xp2d.shape[0], xs2d.shape[0]), 512)
    h = _pre_norm(xp2d, xs2d, g, tm_norm)
    tm = _largest_tile(h.shape[0], 1088)
    zrkv = _in_proj(h, w_t, 0, 3 * D_RWKV, D_RWKV, tm)
    zl = _in_proj(h, w_t, 3 * D_RWKV, D_LORA_PAD, D_LORA_PAD, tm)
    zp = _in_proj(h, w_t, D_SHIFT, D_POOL, D_POOL, tm)
    zga = _in_proj(h, w_t, D_SHIFT + D_POOL, D_MODEL, D_MODEL // 2, tm)
    zgb = _in_proj(h, w_t, D_SHIFT + D_POOL + D_MODEL, D_MODEL, D_MODEL // 2, tm)
    return zrkv, zl, zp, zga, zgb


def _last_shift_row(zrkv, zl, row0, b, t):
    if b <= 8:
        last = lambda z, w: jnp.concatenate(
            [lax.slice(z, (row0 + (i + 1) * t - 1, 0), (row0 + (i + 1) * t, w)) for i in range(b)])
    else:
        last = lambda z, w: lax.slice(z, (row0 + t - 1, 0), (row0 + b * t, w), (t, 1))
    return jnp.concatenate([last(zrkv, 3 * D_RWKV), last(zl, D_LORA)], axis=-1)[:, None, :]


def _prompt_layer(x, z, wts):
    b, t, _ = x.shape
    n = b * t
    x2d = x.reshape(n, D_MODEL)
    zrkv, zl, zp, zga, zgb = z
    r, lw, k2, v, a_s, b_s, g = _prep_prompt(zrkv, zl, b, t, wts["prep"], tm=256)
    y, wkv, bf16_weights = _wkv_prompt(r, lw, k2, v, a_s, b_s, wts["f32_matmul_weights"])
    wts = wts["with_bf16"](*bf16_weights)
    flat = lambda a: a.reshape(n, a.shape[-1])
    m = _merge(flat(y), flat(r), flat(k2), flat(v), flat(g), zp, zga, zgb, 0, wts["merge"], tm=256, seq_len=t)
    x1, h1 = _mix_out(m, x2d, *wts["mix_out"], wts["ffn"][0], tm=512)
    ffn_tm = _largest_tile(t, 512)
    out, tail = _ffn_prompt(x1, h1, t, wts["ffn"], tm=ffn_tm, fk=512)
    shift = _last_shift_row(zrkv, zl, 0, b, t)
    pool =jnp.stack([zp[(i + 1) * t - POOL_HIST:(i + 1) * t] for i in range(b)])
    tiles = t // ffn_tm
    conv = tail.reshape(b, tiles, 8, D_FF)[:, tiles - 1, 8 - (CONV_W - 1):, :]
    return wts, (out.reshape(b, t, D_MODEL), shift, wkv, pool, conv)


def _sample_layer(x, z, row0, st_shift, st_wkv, st_pool, st_conv, prep_wts, wts):
    b, t, _ = x.shape
    n = b * t
    x2d = x.reshape(n, D_MODEL)
    zrkv, zl, zp, zga, zgb = z
    st = jnp.pad(st_shift.reshape(b, D_SHIFT), ((0, 0), (0, D_SHIFT_PAD - D_SHIFT)))
    tm = _largest_tile(math.gcd(row0, n), 128, mult=SLAB * t)
    r, lw, k2, v, a_s, b_s, g = _prep_sample(zrkv, zl, row0, n, st, t, prep_wts, tm=tm)
    y, wkv = _wkv_steps(r, lw, k2, v, a_s, b_s, jnp.transpose(st_wkv, (1, 2, 3, 0)), t)
    wkv = jnp.transpose(wkv, (3, 0, 1, 2))
    d, pool = _pool_sample(jnp.swapaxes(st_pool, 0, 1), zp, row0, t)
    pool = jnp.swapaxes(pool, 0, 1)
    m = _merge(y, r, k2, v, g, d, zga, zgb, row0, wts["merge"], tm=_largest_tile(math.gcd(row0, n), 256, mult=8))
    x1, h1 = _mix_out(m, x2d, *wts["mix_out"], wts["ffn"][0], tm=n)
    out, conv = _ffn_sample(x1, h1, st_conv, t, wts["ffn"], fk=512)
    shift = _last_shift_row(zrkv, zl, row0, b, t)
    return out.reshape(b, t, D_MODEL), shift, wkv, pool, conv


def kernel(x_prompt, x_sample, state_shift, state_wkv, state_pool, state_conv, norm_pre_mix, w_in, mu_shift, w0, w2, a0, a2, g2, k_k, k_a, r_k, lnx_w, lnx_b, w_branch_a, pool_w, pool_scale, w_branch_b, w_out, norm_post_mix, norm_pre_ffn, w_ffn_in, conv_w, conv_b, w_ffn_out, norm_post_ffn):
    weights = (norm_pre_mix, w_in, mu_shift, w0, w2, a0, a2, g2, k_k, k_a, r_k, lnx_w, lnx_b,
               w_branch_a, pool_w, pool_scale, w_branch_b, w_out, norm_post_mix,
               norm_pre_ffn, w_ffn_in, conv_w, conv_b, w_ffn_out, norm_post_ffn)
    depth = w_in.shape[0]
    yp, ys = x_prompt, x_sample
    p_states, s_states = [], []
    for l in range(depth):
        wts = _layer_weights(l, *weights)
        n_prompt = yp.shape[0] * yp.shape[1]
        z = _project(yp.reshape(n_prompt, D_MODEL), ys.reshape(-1, D_MODEL), wts)
        bf16_wts, (yp, *ps) = _prompt_layer(yp, z, wts)
        ys, *ss = _sample_layer(ys, z, n_prompt, state_shift[l], state_wkv[l], state_pool[l], state_conv[l],
                                wts["prep"], bf16_wts)
        p_states.append(ps)
        s_states.append(ss)
    stack = lambda states, i: jnp.stack([s[i] for s in states])
    return (yp, ys,
            stack(p_states, 0), stack(p_states, 1), stack(p_states, 2), stack(p_states, 3),
            stack(s_states, 0), stack(s_states, 1), stack(s_states, 2), stack(s_states, 3))
```

```python
import functools
import math

import jax
import jax.numpy as jnp
from jax import lax
from jax.experimental import pallas as pl
from jax.experimental.pallas import tpu as pltpu

F32 = jnp.float32
BF16 = jnp.bfloat16

D_MODEL = 2048
HEAD_DIM = 64
D_RWKV = 1024
N_HEADS = 16
RANK_W, RANK_A, RANK_G = 64, 64, 160
D_LORA = RANK_W + RANK_A + RANK_G
D_LORA_PAD = 384
D_SHIFT = 3 * D_RWKV + D_LORA
D_SHIFT_PAD = 3 * D_RWKV + D_LORA_PAD
D_POOL = 1024
POOL_WINDOWS = (2, 4, 8, 16)
POOL_GROUP = 256
POOL_HIST = 15
D_FF = 5632
CONV_W = 3
NORM_EPS = 1e-6
GN_EPS = 64e-5
PAST_LEN = 16384

LANES_V7X = 128
SLAB = 8
MXU_DIM_V7X = 256
HEADS_PER_GROUP = MXU_DIM_V7X // HEAD_DIM
N_GROUPS = N_HEADS // HEADS_PER_GROUP
CHUNK = 64
VMEM_LIMIT_V7X = 56 * 1024 * 1024


def _params(*sem):
    return pltpu.CompilerParams(dimension_semantics=sem, vmem_limit_bytes=VMEM_LIMIT_V7X)


def _mm(a, b):
    return jnp.dot(a.astype(BF16), b.astype(BF16), preferred_element_type=F32)


def _mm_nt(a, b):
    return lax.dot_general(a.astype(BF16), b.astype(BF16), (((1,), (1,)), ((), ())),
                           preferred_element_type=F32)


def _split_hi_lo(x):
    hi = x.astype(BF16)
    lo = (x - hi.astype(F32)).astype(BF16)
    return hi, lo


def _head_sum(x, ones_bd):
    hi, lo = _split_hi_lo(x)
    outs = []
    for gi in range(x.shape[1] // MXU_DIM_V7X):
        sl = slice(MXU_DIM_V7X * gi, MXU_DIM_V7X * (gi + 1))
        outs.append(jnp.dot(hi[:, sl], ones_bd, preferred_element_type=F32)
                    + jnp.dot(lo[:, sl], ones_bd, preferred_element_type=F32))
    return jnp.concatenate(outs, axis=1)


def _rmsnorm(x, g):
    return x * lax.rsqrt(jnp.mean(x * x, axis=-1, keepdims=True) + NORM_EPS) * g


def _replace_step_rows(scr_ref, base, seq, steps):
    nb = base.shape[0] // seq
    for j in range(base.shape[1] // LANES_V7X):
        ls = slice(LANES_V7X * j, LANES_V7X * (j + 1))
        scr_ref[j] = base[:, ls]
        for t, vals in steps:
            scr_ref[j, pl.ds(t, nb, stride=seq), :] = vals[:, ls]
    return jnp.concatenate([scr_ref[j] for j in range(base.shape[1] // LANES_V7X)], axis=1)


def _pre_norm_kernel(prompt_tiles, xp_ref, xs_ref, g_ref, h_ref):
    i = pl.program_id(0)

    @pl.when(i < prompt_tiles)
    def _():
        h_ref[...] = _rmsnorm(xp_ref[...], g_ref[...]).astype(BF16)

    @pl.when(i >= prompt_tiles)
    def _():
        h_ref[...] = _rmsnorm(xs_ref[...], g_ref[...]).astype(BF16)


def _pre_norm(xp, xs, g, tm):
    pt, st = xp.shape[0] // tm, xs.shape[0] // tm
    return pl.pallas_call(
        functools.partial(_pre_norm_kernel, pt),
        grid=(pt + st,),
        in_specs=[pl.BlockSpec((tm, D_MODEL), lambda i: (jnp.minimum(i, pt - 1), 0)),
                  pl.BlockSpec((tm, D_MODEL), lambda i: (jnp.maximum(i - pt, 0), 0)),
                  _const_spec(g.shape)],
        out_specs=pl.BlockSpec((tm, D_MODEL), lambda i: (i, 0)),
        out_shape=jax.ShapeDtypeStruct((xp.shape[0] + xs.shape[0], D_MODEL), BF16),
        compiler_params=_params("arbitrary"),
        name="pre_norm",
    )(xp, xs, g)


def _in_proj_kernel(h_ref, w_ref, o_ref, wb_ref):
    @pl.when(pl.program_id(1) == 0)
    def _():
        wb_ref[...] = w_ref[...].astype(BF16)

    o_ref[...] = lax.dot_general(h_ref[...], wb_ref[...], (((1,), (1,)), ((), ())),
                                 preferred_element_type=F32)


def _in_proj(h, w_t, off, width, tn, tm):
    n = h.shape[0]
    assert width % tn == 0 and n % tm == 0
    return pl.pallas_call(
        _in_proj_kernel,
        grid=(width // tn, n // tm),
        in_specs=[pl.BlockSpec((tm, D_MODEL), lambda j, i: (i, 0)),
                  pl.BlockSpec((pl.Element(tn), pl.Element(D_MODEL)),
                               lambda j, i: (SLAB * (off // SLAB + j * (tn // SLAB)), 0))],
        out_specs=pl.BlockSpec((tm, tn), lambda j, i: (i, j)),
        out_shape=jax.ShapeDtypeStruct((n, width), F32),
        scratch_shapes=[pltpu.VMEM((tn, D_MODEL), BF16)],
        compiler_params=_params("arbitrary", "arbitrary"),
        name="in_proj",
    )(h, w_t)


def _prep_math(zs, prev, mu, w_lora, w0, a0, k_k, k_a, ones_bd):
    f = zs + (prev - zs) * mu
    r = f[:, 0:D_RWKV]
    k = f[:, D_RWKV:2 * D_RWKV]
    v = f[:, 2 * D_RWKV:3 * D_RWKV]
    low = f[:, 3 * D_RWKV:3 * D_RWKV + D_LORA_PAD]
    lane = lax.broadcasted_iota(jnp.int32, low.shape, 1)
    act = jnp.where(lane < RANK_W, jnp.tanh(low),
                    jnp.where(lane < RANK_W + RANK_A, low,
                              jnp.where(lane < D_LORA, jax.nn.sigmoid(low), 0.0)))
    lora = _mm(act, w_lora)
    u = w0 + lora[:, 0:D_RWKV]
    lw = -math.exp(-0.5) * jax.nn.sigmoid(u)
    a = jax.nn.sigmoid(a0 + lora[:, D_RWKV:2 * D_RWKV])
    g = lora[:, 2 * D_RWKV:3 * D_RWKV]
    kk = k * k_k
    kk = kk * jnp.minimum(lax.rsqrt(_head_sum(kk * kk, ones_bd)), 1e12)
    k2 = k * (1.0 + (a - 1.0) * k_a)
    return r, lw, k2, v, -kk, kk * a, g


def _prep_prompt_kernel(zrkv_ref, zl_ref, mu_ref, wl_ref, w0_ref, a0_ref, kk_ref, ka_ref, ones_ref,
                        r_ref, lw_ref, k2_ref, v_ref, as_ref, bs_ref, g_ref, carry_ref):
    i = pl.program_id(1)

    @pl.when(i == 0)
    def _():
        carry_ref[...] = jnp.zeros_like(carry_ref)

    zs = jnp.concatenate([zrkv_ref[...], zl_ref[...]], axis=1)
    rolled = pltpu.roll(zs, 1, 0)
    row = lax.broadcasted_iota(jnp.int32, zs.shape, 0)
    prev = jnp.where(row == 0, carry_ref[0:1, :], rolled)
    carry_ref[0:1, :] = zs[zs.shape[0] - 1:, :]
    outs = _prep_math(zs, prev, mu_ref[...], wl_ref[...], w0_ref[...], a0_ref[...],
                      kk_ref[...], ka_ref[...], ones_ref[...])
    for o_ref, o in zip((r_ref, lw_ref, k2_ref, v_ref, as_ref, bs_ref, g_ref), outs):
        o_ref[...] = o


def _prep_sample_kernel(seq, zrkv_ref, zl_ref, st_ref, mu_ref, wl_ref, w0_ref, a0_ref, kk_ref, ka_ref,
                        ones_ref, r_ref, lw_ref, k2_ref, v_ref, as_ref, bs_ref, g_ref, prev_ref):
    zs = jnp.concatenate([zrkv_ref[...], zl_ref[...]], axis=1)
    prev = _replace_step_rows(prev_ref, pltpu.roll(zs, 1, 0), seq, [(0, st_ref[...])])
    outs = _prep_math(zs, prev, mu_ref[...], wl_ref[...], w0_ref[...], a0_ref[...],
                      kk_ref[...], ka_ref[...], ones_ref[...])
    for o_ref, o in zip((r_ref, lw_ref, k2_ref, v_ref, as_ref, bs_ref, g_ref), outs):
        o_ref[...] = o


def _const_spec(shape):
    nd = len(shape)
    return pl.BlockSpec(shape, lambda *_: (0,) * nd)


def _prep_prompt(zrkv, zl, b, t, wts, tm):
    tiles = t // tm
    outs = [jax.ShapeDtypeStruct((b, t, D_RWKV), F32)] * 7
    row_spec = pl.BlockSpec((None, tm, D_RWKV), lambda bi, i: (bi, i, 0))
    return pl.pallas_call(
        _prep_prompt_kernel,
        grid=(b, tiles),
        in_specs=[pl.BlockSpec((tm, 3 * D_RWKV), lambda bi, i: (bi * tiles + i, 0)),
                  pl.BlockSpec((tm, D_LORA_PAD), lambda bi, i: (bi * tiles + i, 0))]
        + [_const_spec(w.shape) for w in wts],
        out_specs=[row_spec] * 7,
        out_shape=outs,
        scratch_shapes=[pltpu.VMEM((8, D_SHIFT_PAD), F32)],
        compiler_params=_params("parallel", "arbitrary"),
        name="prep_prompt",
    )(zrkv, zl, *wts)


def _prep_sample(zrkv, zl, row0, n, st, seq, wts, tm):
    assert tm % (SLAB * seq) == 0 and n % tm == 0 and row0 % tm == 0
    first = row0 // tm
    outs = [jax.ShapeDtypeStruct((n, D_RWKV), F32)] * 7
    row_spec = pl.BlockSpec((tm, D_RWKV), lambda i: (i, 0))
    return pl.pallas_call(
        functools.partial(_prep_sample_kernel, seq),
        grid=(n // tm,),
        in_specs=[pl.BlockSpec((tm, 3 * D_RWKV), lambda i: (first + i, 0)),
                  pl.BlockSpec((tm, D_LORA_PAD), lambda i: (first + i, 0)),
                  pl.BlockSpec((tm // seq, D_SHIFT_PAD), lambda i: (i, 0))]
        + [_const_spec(w.shape) for w in wts],
        out_specs=[row_spec] * 7,
        out_shape=outs,
        scratch_shapes=[pltpu.VMEM((D_SHIFT_PAD // LANES_V7X, tm, LANES_V7X), F32)],
        compiler_params=_params("parallel"),
        name="prep_sample",
    )(zrkv, zl, st, *wts)


def _expand_bd(a, bd_mask):
    return jnp.where(bd_mask, jnp.concatenate([a] * HEADS_PER_GROUP, axis=0), 0.0)


def _chunk_cumsum(x):
    row = lax.broadcasted_iota(jnp.int32, x.shape, 0)
    s = 1
    while s < x.shape[0]:
        x = x + jnp.where(row >= s, pltpu.roll(x, s, 0), 0.0)
        s *= 2
    return x


def _wkv_intra(xs, bts, kts, vs, pzs, masks, order):
    bd_mask, strict_mask, incl_mask, eye_c, head_masks = masks
    c = vs[0].shape[0]
    ystacks = [jnp.concatenate([jnp.where(m, bt, 0.0) for m in head_masks]
                               + [jnp.where(m, kt, 0.0) for m in head_masks], axis=0)
               for bt, kt in zip(bts, kts)]
    grams = [_mm_nt(x, ys) for x, ys in zip(xs, ystacks)]
    l_abs = [jnp.where(strict_mask, g[0:c, 0:4 * c], 0.0) for g in grams]
    l_aks = [jnp.where(strict_mask, g[0:c, 4 * c:8 * c], 0.0) for g in grams]
    m_rbs = [jnp.where(incl_mask, g[c:2 * c, 0:4 * c], 0.0) for g in grams]
    m_rks = [jnp.where(incl_mask, g[c:2 * c, 4 * c:8 * c], 0.0) for g in grams]

    tinvs = [eye_c + l for l in l_abs]
    lps = [_mm(l, _expand_bd(l, bd_mask)) for l in l_abs]
    p = 2
    while True:
        rhss = [_expand_bd(lp, bd_mask) for lp in lps]
        if 2 * p >= order:
            tinvs = [t + _mm(t, rhs) for t, rhs in zip(tinvs, rhss)]
            break
        ress = [_mm(jnp.concatenate([lp, t], axis=0), rhs) for lp, t, rhs in zip(lps, tinvs, rhss)]
        lps = [res[0:c] for res in ress]
        tinvs = [t + res[c:2 * c] for t, res in zip(tinvs, ress)]
        p *= 2

    v_bds = [_expand_bd(v, bd_mask) for v in vs]
    ws = [pz[0:c] + _mm(l_ak, v_bd) for pz, l_ak, v_bd in zip(pzs, l_aks, v_bds)]
    us = [_mm(t, _expand_bd(w, bd_mask)) for t, w in zip(tinvs, ws)]
    ys = [pz[c:2 * c] + _mm(jnp.concatenate([m_rb, m_rk], axis=1),
                            jnp.concatenate([_expand_bd(u, bd_mask), v_bd], axis=0))
          for pz, m_rb, m_rk, u, v_bd in zip(pzs, m_rbs, m_rks, us, v_bds)]
    return ys, us


def _wkv_chunk(groups, masks):
    c = groups[0][0].shape[0]
    cls = [_chunk_cumsum(g[1]) for g in groups]
    xs = [jnp.concatenate([g[4] * jnp.exp(cl - g[1]), g[0] * jnp.exp(cl)], axis=0)
          for g, cl in zip(groups, cls)]
    pzs = [_mm_nt(x, g[6]) for x, g in zip(xs, groups)]
    e_negs = [jnp.exp(-cl) for cl in cls]
    ys, us = _wkv_intra(xs, [g[5] * e for g, e in zip(groups, e_negs)],
                        [g[2] * e for g, e in zip(groups, e_negs)], [g[3] for g in groups], pzs, masks, c)
    dss = []
    for g, cl, u in zip(groups, cls, us):
        e_rem = jnp.exp(cl[c - 1:c, :] - cl)
        uv_t = jnp.concatenate([u, g[3]], axis=0).T
        bk = jnp.concatenate([g[5] * e_rem, g[2] * e_rem], axis=0)
        dss.append(_mm(uv_t, bk))
    return [(y, g[6] * jnp.exp(cl[c - 1:c, :]) + jnp.where(masks[0], ds, 0.0))
            for y, g, cl, ds in zip(ys, groups, cls, dss)]


def _wkv_masks(c, seq):
    n = HEADS_PER_GROUP * c
    rr = lax.broadcasted_iota(jnp.int32, (n, n), 0)
    cc = lax.broadcasted_iota(jnp.int32, (n, n), 1)
    bd_mask = (rr // c) == (cc // c)
    t = lax.broadcasted_iota(jnp.int32, (c, n), 0)
    s = lax.broadcasted_iota(jnp.int32, (c, n), 1) % c
    same = (t // seq) == (s // seq)
    strict_mask = same & (t > s)
    incl_mask = same & (t >= s)
    eye_c = jnp.where(t == s, 1.0, 0.0).astype(F32)
    lane = lax.broadcasted_iota(jnp.int32, (c, MXU_DIM_V7X), 1)
    head_masks = [(lane // HEAD_DIM) == h for h in range(HEADS_PER_GROUP)]
    return bd_mask, strict_mask, incl_mask, eye_c, head_masks


def _wkv_prompt_kernel(n_casts, r_ref, lw_ref, k2_ref, v_ref, as_ref, bs_ref, *refs):
    w_refs, (y_ref, sout_ref), wb_refs, (s_ref,) = (
        refs[:n_casts], refs[n_casts:n_casts + 2], refs[n_casts + 2:2 * n_casts + 2], refs[2 * n_casts + 2:])
    ci = pl.program_id(0)

    @pl.when(ci == 0)
    def _():
        s_ref[...] = jnp.zeros_like(s_ref)

    for w_ref, wb_ref in zip(w_refs, wb_refs):
        wb_ref[...] = w_ref[...].astype(BF16)

    masks = _wkv_masks(CHUNK, CHUNK)
    chains = [(bi, gi) for bi in range(r_ref.shape[0]) for gi in range(N_GROUPS)]
    lanes = lambda gi: slice(MXU_DIM_V7X * gi, MXU_DIM_V7X * (gi + 1))
    groups = [tuple(ref[bi, :, lanes(gi)] for ref in (r_ref, lw_ref, k2_ref, v_ref, as_ref, bs_ref))
              + (s_ref[bi, gi],) for bi, gi in chains]
    for (bi, gi), (y, s_new) in zip(chains, _wkv_chunk(groups, masks)):
        y_ref[bi, :, lanes(gi)] = y
        s_ref[bi, gi] = s_new

    @pl.when(ci == pl.num_programs(0) - 1)
    def _():
        for bi, gi in chains:
            s_bd = s_ref[bi, gi]
            for h in range(HEADS_PER_GROUP):
                hs = slice(HEAD_DIM * h, HEAD_DIM * (h + 1))
                sout_ref[bi, HEADS_PER_GROUP * gi + h] = s_bd[hs, hs]


def _wkv_prompt(r, lw, k2, v, a_s, b_s, weights):
    b, t, _ = r.shape
    steps = t // CHUNK
    spec = pl.BlockSpec((b, CHUNK, D_RWKV), lambda ci: (0, ci, 0))
    state = (b, N_GROUPS, MXU_DIM_V7X, MXU_DIM_V7X)
    out_state = (b, N_HEADS, HEAD_DIM, HEAD_DIM)
    bf16_rows = 2 * SLAB
    assert all(w.shape[0] % (steps * bf16_rows) == 0 for w in weights)
    slabs = [pl.BlockSpec((w.shape[0] // steps, w.shape[1]), lambda ci: (ci, 0)) for w in weights]
    y, s, *wb = pl.pallas_call(
        functools.partial(_wkv_prompt_kernel, len(weights)),
        grid=(steps,),
        in_specs=[spec] * 6 + slabs,
        out_specs=[spec, _const_spec(out_state)] + slabs,
        out_shape=[jax.ShapeDtypeStruct((b, t, D_RWKV), F32), jax.ShapeDtypeStruct(out_state, F32)]
        + [jax.ShapeDtypeStruct(w.shape, BF16) for w in weights],
        scratch_shapes=[pltpu.VMEM(state, F32)],
        compiler_params=_params("arbitrary"),
        name="wkv_prompt",
    )(r, lw, k2, v, a_s, b_s, *weights)
    return y, s, wb


HEAD_PAIR = 2 * HEAD_DIM


def _wkv_steps_kernel(seq, r_ref, lw_ref, k2_ref, v_ref, as_ref, bs_ref, s0_ref, y_ref, sout_ref,
                      op_ref, vt_ref, yt_ref):
    nb = r_ref.shape[0] // seq
    for t in range(seq):
        rows = pl.ds(t, nb, stride=seq)
        op_ref[0, t] = r_ref[rows, :].T
        op_ref[1, t] = jnp.exp(lw_ref[rows, :]).T
        op_ref[2, t] = k2_ref[rows, :].T
        op_ref[3, t] = as_ref[rows, :].T
        op_ref[4, t] = bs_ref[rows, :].T
        vt_ref[t] = v_ref[rows, :].T

    for hh in range(HEAD_PAIR // HEAD_DIM):
        ks = slice(HEAD_DIM * hh, HEAD_DIM * (hh + 1))

        def body(vi, carry):
            s = s0_ref[hh, vi]
            for t in range(seq):
                sa = jnp.sum(s * op_ref[3, t, ks, :], axis=0, keepdims=True)
                vrow = vt_ref[t, pl.ds(HEAD_DIM * hh + vi, 1), :]
                s = s * op_ref[1, t, ks, :] + sa * op_ref[4, t, ks, :] + vrow * op_ref[2, t, ks, :]
                yt_ref[t, pl.ds(HEAD_DIM * hh + vi, 1), :] = jnp.sum(s * op_ref[0, t, ks, :], axis=0,
                                                                      keepdims=True)
            sout_ref[hh, vi] = s
            return carry

        lax.fori_loop(0, HEAD_DIM, body, 0, unroll=4)

    for t in range(seq):
        y_ref[pl.ds(t, nb, stride=seq), :] = yt_ref[t].T


def _wkv_steps(r, lw, k2, v, a_s, b_s, s0_native, seq):
    n = r.shape[0]
    nb = n // seq
    assert s0_native.shape == (N_HEADS, HEAD_DIM, HEAD_DIM, nb) and nb == LANES_V7X
    vec = pl.BlockSpec((n, HEAD_PAIR), lambda i: (0, i))
    st = pl.BlockSpec((HEAD_PAIR // HEAD_DIM, HEAD_DIM, HEAD_DIM, nb), lambda i: (i, 0, 0, 0))
    return pl.pallas_call(
        functools.partial(_wkv_steps_kernel, seq),
        grid=(N_HEADS * HEAD_DIM // HEAD_PAIR,),
        in_specs=[vec] * 6 + [st],
        out_specs=[vec, st],
        out_shape=[jax.ShapeDtypeStruct((n, D_RWKV), F32), jax.ShapeDtypeStruct(s0_native.shape, F32)],
        scratch_shapes=[pltpu.VMEM((5, seq, HEAD_PAIR, nb), F32), pltpu.VMEM((seq, HEAD_PAIR, nb), F32),
                        pltpu.VMEM((seq, HEAD_PAIR, nb), F32)],
        compiler_params=_params("parallel"),
        name="wkv_steps",
    )(r, lw, k2, v, a_s, b_s, s0_native)


POOL_CARRY = 16


def _pool_prompt_tile(zp, carry_ref, tile_in_seq):
    tm = zp.shape[0]

    @pl.when(tile_in_seq == 0)
    def _():
        carry_ref[...] = jnp.zeros_like(carry_ref)

    buf = jnp.concatenate([carry_ref[...], zp], axis=0)
    carry_ref[...] = zp[tm - POOL_CARRY:, :]
    pos = (tile_in_seq * tm + lax.broadcasted_iota(jnp.int32, (tm, POOL_GROUP), 0) + 1).astype(F32)
    ds = []
    for gi, win in enumerate(POOL_WINDOWS):
        sl = slice(POOL_GROUP * gi, POOL_GROUP * (gi + 1))
        acc = buf[:, sl]
        s = 1
        while s < win:
            acc = acc + pltpu.roll(acc, s, 0)
            s *= 2
        ds.append(acc[POOL_CARRY:, :] / jnp.minimum(float(win), pos) - zp[:, sl])
    return jnp.concatenate(ds, axis=1)


def _pool_sample_kernel(seq, hist_ref, zp_ref, d_ref, new_ref):
    nb = hist_ref.shape[1]
    step = lambda t: zp_ref[pl.ds(t, nb, stride=seq), :]
    past = lambda j: step(j) if j >= 0 else hist_ref[POOL_HIST + j]
    tiles_per_group = POOL_GROUP // LANES_V7X
    for gi, win in enumerate(POOL_WINDOWS):
        @pl.when(pl.program_id(0) // tiles_per_group == gi)
        def _():
            for t in range(seq):
                acc = past(t)
                for j in range(1, win):
                    acc = acc + past(t - j)
                d_ref[pl.ds(t, nb, stride=seq), :] = acc / float(min(win, PAST_LEN + 1)) - past(t)
    for j in range(POOL_HIST):
        src = j + seq
        new_ref[j] = hist_ref[src] if src < POOL_HIST else step(src - POOL_HIST)


def _pool_sample(hist, zp, row0, seq):
    nb = hist.shape[1]
    n = nb * seq
    assert row0 % n == 0
    hist_spec = pl.BlockSpec((POOL_HIST, nb, LANES_V7X), lambda j: (0, 0, j))
    return pl.pallas_call(
        functools.partial(_pool_sample_kernel, seq),
        grid=(D_POOL // LANES_V7X,),
        in_specs=[hist_spec, pl.BlockSpec((n, LANES_V7X), lambda j: (row0 // n, j))],
        out_specs=[pl.BlockSpec((n, LANES_V7X), lambda j: (0, j)), hist_spec],
        out_shape=[jax.ShapeDtypeStruct((n, D_POOL), F32), jax.ShapeDtypeStruct(hist.shape, F32)],
        compiler_params=_params("parallel"),
        name="pool_sample",
    )(hist, zp)


def _merge_kernel(seq_tiles, y_ref, r_ref, k2_ref, v_ref, g_ref, d_ref, zga_ref, zgb_ref, x_ref,
                  rk_ref, lw_ref, lb_ref, ones_ref, wa_ref, pw_ref, ps_ref, wb_ref,
                  wo_ref, gpost_ref, gnext_ref, o_ref, h_ref, *carry):
    if seq_tiles is None:
        d = d_ref[...]
    else:
        d = _pool_prompt_tile(d_ref[...], carry[0], pl.program_id(0) % seq_tiles)
    ones_bd = ones_ref[...]
    y = y_ref[...]
    mu = _head_sum(y, ones_bd) * (1.0 / HEAD_DIM)
    yc = y - mu
    var = _head_sum(yc * yc, ones_bd) * (1.0 / HEAD_DIM)
    yn = yc * lax.rsqrt(var + GN_EPS) * lw_ref[...] + lb_ref[...]
    v = v_ref[...]
    bonus = _head_sum(r_ref[...] * k2_ref[...] * rk_ref[...], ones_bd) * v
    ya = (yn + bonus) * g_ref[...]
    yb = jnp.concatenate(
        [_mm(d[:, POOL_GROUP * gi:POOL_GROUP * (gi + 1)], pw_ref[gi]) for gi in range(len(POOL_WINDOWS))],
        axis=1) * ps_ref[...]
    m = (jax.nn.sigmoid(zga_ref[...]) * _mm(ya, wa_ref[...])
         + jax.nn.sigmoid(zgb_ref[...]) * _mm(yb, wb_ref[...]))
    x1 = x_ref[...] + _rmsnorm(_mm(m, wo_ref[...]), gpost_ref[...])
    o_ref[...] = x1
    h_ref[...] = _rmsnorm(x1, gnext_ref[...]).astype(BF16)


def _merge(y, r, k2, v, g, d, zga, zgb, x2d, row0, wts, tm, seq_len=None):
    n = y.shape[0]
    assert row0 % tm == 0 and (seq_len is None or (seq_len % tm == 0 and tm >= POOL_CARRY))
    first = row0 // tm
    half = pl.BlockSpec((tm, D_RWKV), lambda i: (i, 0))
    full = pl.BlockSpec((tm, D_MODEL), lambda i: (i, 0))
    gate = pl.BlockSpec((tm, D_MODEL), lambda i: (first + i, 0))
    resident = [pl.BlockSpec(w.shape, functools.partial(lambda nd, i: (0,) * nd, w.ndim),
                             pipeline_mode=pl.Buffered(1)) for w in wts]
    return pl.pallas_call(
        functools.partial(_merge_kernel, None if seq_len is None else seq_len // tm),
        grid=(n // tm,),
        in_specs=[half] * 6 + [gate] * 2 + [full] + resident,
        out_specs=[full, full],
        out_shape=[jax.ShapeDtypeStruct((n, D_MODEL), F32), jax.ShapeDtypeStruct((n, D_MODEL), BF16)],
        scratch_shapes=[] if seq_len is None else [pltpu.VMEM((POOL_CARRY, D_POOL), F32)],
        compiler_params=_params("parallel" if seq_len is None else "arbitrary"),
        name="merge",
    )(y, r, k2, v, g, d, zga, zgb, x2d, *wts)


def _gelu_tanh(x):
    return 0.5 * x * (1.0 + jnp.tanh(0.7978845608028654 * (x + 0.044715 * x * x * x)))


def _ffn_body(x_ref, h_ref, wg_ref, wu_ref, cw_ref, cb_ref, wo_ref, gp_ref,
              o_ref, acc_ref, prev1, prev2, emit_tail):
    c = pl.program_id(1)

    @pl.when(c == 0)
    def _():
        acc_ref[...] = jnp.zeros_like(acc_ref)

    h = h_ref[...]
    gate = jnp.dot(h, wg_ref[...], preferred_element_type=F32)
    up = jnp.dot(h, wu_ref[...], preferred_element_type=F32)
    cw = cw_ref[...]
    cv = cb_ref[...] + cw[0:1, :] * prev2(gate) + cw[1:2, :] * prev1(gate) + cw[2:3, :] * gate
    emit_tail(gate)
    acc_ref[...] += _mm(_gelu_tanh(cv) * up, wo_ref[...])

    @pl.when(c == pl.num_programs(1) - 1)
    def _():
        o_ref[...] = x_ref[...] + _rmsnorm(acc_ref[...], gp_ref[...])


def _ffn_prompt_kernel(tiles_per_seq, x_ref, h_ref, wg_ref, wu_ref, cw_ref, cb_ref, wo_ref, gp_ref,
                       o_ref, tail_ref, acc_ref, carry_ref):
    i = pl.program_id(0)
    c = pl.program_id(1)

    @pl.when(i % tiles_per_seq == 0)
    def _():
        carry_ref[c] = jnp.zeros(carry_ref.shape[1:], F32)

    hist = carry_ref[c]

    def shifted(gate, s):
        row = lax.broadcasted_iota(jnp.int32, gate.shape, 0)
        rolled = pltpu.roll(gate, s, 0)
        out = rolled
        for j in range(s):
            out = jnp.where(row == j, hist[8 - s + j:9 - s + j, :], out)
        return out

    def emit_tail(gate):
        tail_ref[...] = gate[gate.shape[0] - tail_ref.shape[0]:, :]

    _ffn_body(x_ref, h_ref, wg_ref, wu_ref, cw_ref, cb_ref, wo_ref, gp_ref, o_ref,
              acc_ref, lambda g: shifted(g, 1), lambda g: shifted(g, 2), emit_tail)
    carry_ref[c] = tail_ref[...]


def _ffn_sample_kernel(seq, x_ref, st_ref, h_ref, wg_ref, wu_ref, cw_ref, cb_ref, wo_ref, gp_ref,
                       o_ref, tail_ref, acc_ref, g_ref, p1_ref, p2_ref):
    nb = st_ref.shape[0]
    st0 = st_ref[:, 0, :]
    st1 = st_ref[:, 1, :]
    prev1 = lambda gate: _replace_step_rows(p1_ref, pltpu.roll(gate, 1, 0), seq, [(0, st1)])
    prev2 = lambda gate: _replace_step_rows(p2_ref, pltpu.roll(gate, 2, 0), seq, [(0, st0), (1, st1)])

    def emit_tail(gate):
        tiles = gate.shape[1] // LANES_V7X
        for j in range(tiles):
            g_ref[j] = gate[:, LANES_V7X * j:LANES_V7X * (j + 1)]
        for s in range(CONV_W - 1):
            rows = pl.ds(seq - (CONV_W - 1) + s, nb, stride=seq)
            tail_ref[:, s, :] = jnp.concatenate([g_ref[j, rows, :] for j in range(tiles)], axis=1)

    _ffn_body(x_ref, h_ref, wg_ref, wu_ref, cw_ref, cb_ref, wo_ref, gp_ref, o_ref,
              acc_ref, prev1, prev2, emit_tail)


def _ffn_prompt(x2d, h, seq_len, wts, tm, fk):
    n = x2d.shape[0]
    nc = D_FF // fk
    w_in, cw, cb, w_out, gp = wts
    full = pl.BlockSpec((tm, D_MODEL), lambda i, c: (i, 0))
    return pl.pallas_call(
        functools.partial(_ffn_prompt_kernel, seq_len // tm),
        grid=(n // tm, nc),
        in_specs=[full, full,
                  pl.BlockSpec((D_MODEL, fk), lambda i, c: (0, c)),
                  pl.BlockSpec((D_MODEL, fk), lambda i, c: (0, c + nc)),
                  pl.BlockSpec((CONV_W, fk), lambda i, c: (0, c)),
                  pl.BlockSpec((1, fk), lambda i, c: (0, c)),
                  pl.BlockSpec((fk, D_MODEL), lambda i, c: (c, 0)),
                  _const_spec(gp.shape)],
        out_specs=[full, pl.BlockSpec((None, 8, fk), lambda i, c: (i, 0, c))],
        out_shape=[jax.ShapeDtypeStruct((n, D_MODEL), F32),
                   jax.ShapeDtypeStruct((n // tm, 8, D_FF), F32)],
        scratch_shapes=[pltpu.VMEM((tm, D_MODEL), F32), pltpu.VMEM((nc, 8, fk), F32)],
        compiler_params=_params("arbitrary", "arbitrary"),
        name="ffn_prompt",
    )(x2d, h, w_in, w_in, cw, cb, w_out, gp)


def _ffn_sample(x2d, h, st, seq, wts, fk):
    n = x2d.shape[0]
    nb = n // seq
    assert seq >= CONV_W - 1 and st.shape == (nb, CONV_W - 1, D_FF)
    nc = D_FF // fk
    w_in, cw, cb, w_out, gp = wts
    full = pl.BlockSpec((n, D_MODEL), lambda i, c: (0, 0))
    cols = pl.BlockSpec((nb, CONV_W - 1, fk), lambda i, c: (0, 0, c))
    return pl.pallas_call(
        functools.partial(_ffn_sample_kernel, seq),
        grid=(1, nc),
        in_specs=[full, cols, full,
                  pl.BlockSpec((D_MODEL, fk), lambda i, c: (0, c)),
                  pl.BlockSpec((D_MODEL, fk), lambda i, c: (0, c + nc)),
                  pl.BlockSpec((CONV_W, fk), lambda i, c: (0, c)),
                  pl.BlockSpec((1, fk), lambda i, c: (0, c)),
                  pl.BlockSpec((fk, D_MODEL), lambda i, c: (c, 0)),
                  _const_spec(gp.shape)],
        out_specs=[full, cols],
        out_shape=[jax.ShapeDtypeStruct((n, D_MODEL), F32), jax.ShapeDtypeStruct(st.shape, F32)],
        scratch_shapes=[pltpu.VMEM((n, D_MODEL), F32)]
        + [pltpu.VMEM((fk // LANES_V7X, n, LANES_V7X), F32)] * 3,
        compiler_params=_params("arbitrary", "arbitrary"),
        name="ffn_sample",
    )(x2d, st, h, w_in, w_in, cw, cb, w_out, gp)


def _row(v):
    return v.reshape(1, -1).astype(F32)


def _ones_bd():
    i = jnp.arange(MXU_DIM_V7X) // HEAD_DIM
    return (i[:, None] == i[None, :]).astype(BF16)


def _layer_weights(l, norm_pre_mix, w_in, mu_shift, w0, w2, a0, a2, g2, k_k, k_a, r_k, lnx_w, lnx_b,
                   w_branch_a, pool_w, pool_scale, w_branch_b, w_out, norm_post_mix,
                   norm_pre_ffn, w_ffn_in, conv_w, conv_b, w_ffn_out, norm_post_ffn):
    w_t = jnp.swapaxes(w_in[l], 0, 1)
    mu = jnp.pad(mu_shift[l], (0, D_SHIFT_PAD - D_SHIFT)).reshape(1, -1)
    w_lora = jnp.zeros((D_LORA_PAD, 3 * D_RWKV), F32)
    w_lora = w_lora.at[0:RANK_W, 0:D_RWKV].set(w2[l])
    w_lora = w_lora.at[RANK_W:RANK_W + RANK_A, D_RWKV:2 * D_RWKV].set(a2[l])
    w_lora = w_lora.at[RANK_W + RANK_A:D_LORA, 2 * D_RWKV:].set(g2[l])
    ones_bd = _ones_bd()

    def with_bf16(wa, pw, wb, wo, wfi, wfo):
        return dict(
            merge=(_row(r_k[l]), _row(lnx_w[l]), _row(lnx_b[l]), ones_bd, wa,
                   pw.reshape(pool_w.shape[1:]), _row(pool_scale[l]), wb,
                   wo, _row(norm_post_mix[l]), _row(norm_pre_ffn[l])),
            ffn=(wfi, conv_w[l].astype(F32), _row(conv_b[l]), wfo, _row(norm_post_ffn[l])),
        )

    return dict(
        in_proj=(_row(norm_pre_mix[l]), w_t),
        prep=(mu, w_lora.astype(BF16), _row(w0[l]), _row(a0[l]), _row(k_k[l]), _row(k_a[l]), ones_bd),
        f32_matmul_weights=[w_branch_a[l], pool_w[l].reshape(-1, POOL_GROUP), w_branch_b[l], w_out[l],
                            w_ffn_in[l], w_ffn_out[l]],
        with_bf16=with_bf16,
    )


def _largest_tile(n, cap, mult=16):
    best = None
    for d in range(mult, min(n, cap) + 1, mult):
        if n % d == 0:
            best = d
    assert best is not None, (n, cap)
    return best


def _project(xp2d, xs2d, wts):
    g, w_t = wts["in_proj"]
    tm_norm = _largest_tile(math.gcd(xp2d.shape[0], xs2d.shape[0]), 512)
    h = _pre_norm(xp2d, xs2d, g, tm_norm)
    tm = _largest_tile(h.shape[0], 1088)
    zrkv = _in_proj(h, w_t, 0, 3 * D_RWKV, D_RWKV, tm)
    zl = _in_proj(h, w_t, 3 * D_RWKV, D_LORA_PAD, D_LORA_PAD, tm)
    zp = _in_proj(h, w_t, D_SHIFT, D_POOL, D_POOL, tm)
    zga = _in_proj(h, w_t, D_SHIFT + D_POOL, D_MODEL, D_MODEL // 2, tm)
    zgb = _in_proj(h, w_t, D_SHIFT + D_POOL + D_MODEL, D_MODEL, D_MODEL // 2, tm)
    return zrkv, zl, zp, zga, zgb


def _last_shift_row(zrkv, zl, row0, b, t):
    if b <= 8:
        last = lambda z, w: jnp.concatenate(
            [lax.slice(z, (row0 + (i + 1) * t - 1, 0), (row0 + (i + 1) * t, w)) for i in range(b)])
    else:
        last = lambda z, w: lax.slice(z, (row0 + t - 1, 0), (row0 + b * t, w), (t, 1))
    return jnp.concatenate([last(zrkv, 3 * D_RWKV), last(zl, D_LORA)], axis=-1)[:, None, :]


def _prompt_layer(x, z, wts):
    b, t, _ = x.shape
    n = b * t
    x2d = x.reshape(n, D_MODEL)
    zrkv, zl, zp, zga, zgb = z
    r, lw, k2, v, a_s, b_s, g = _prep_prompt(zrkv, zl, b, t, wts["prep"], tm=256)
    y, wkv, bf16_weights = _wkv_prompt(r, lw, k2, v, a_s, b_s, wts["f32_matmul_weights"])
    wts = wts["with_bf16"](*bf16_weights)
    flat = lambda a: a.reshape(n, a.shape[-1])
    x1, h1 = _merge(flat(y), flat(r), flat(k2), flat(v), flat(g), zp, zga, zgb, x2d, 0, wts["merge"],
                    tm=256, seq_len=t)
    ffn_tm = _largest_tile(t, 512)
    out, tail = _ffn_prompt(x1, h1, t, wts["ffn"], tm=ffn_tm, fk=512)
    shift = _last_shift_row(zrkv, zl, 0, b, t)
    pool =jnp.stack([zp[(i + 1) * t - POOL_HIST:(i + 1) * t] for i in range(b)])
    tiles = t // ffn_tm
    conv = tail.reshape(b, tiles, 8, D_FF)[:, tiles - 1, 8 - (CONV_W - 1):, :]
    return wts, (out.reshape(b, t, D_MODEL), shift, wkv, pool, conv)


def _sample_layer(x, z, row0, st_shift, st_wkv, st_pool, st_conv, prep_wts, wts):
    b, t, _ = x.shape
    n = b * t
    x2d = x.reshape(n, D_MODEL)
    zrkv, zl, zp, zga, zgb = z
    st = jnp.pad(st_shift.reshape(b, D_SHIFT), ((0, 0), (0, D_SHIFT_PAD - D_SHIFT)))
    tm = _largest_tile(math.gcd(row0, n), 128, mult=SLAB * t)
    r, lw, k2, v, a_s, b_s, g = _prep_sample(zrkv, zl, row0, n, st, t, prep_wts, tm=tm)
    y, wkv = _wkv_steps(r, lw, k2, v, a_s, b_s, jnp.transpose(st_wkv, (1, 2, 3, 0)), t)
    wkv = jnp.transpose(wkv, (3, 0, 1, 2))
    d, pool = _pool_sample(jnp.swapaxes(st_pool, 0, 1), zp, row0, t)
    pool = jnp.swapaxes(pool, 0, 1)
    x1, h1 = _merge(y, r, k2, v, g, d, zga, zgb, x2d, row0, wts["merge"],
                    tm=_largest_tile(math.gcd(row0, n), 256, mult=16))
    out, conv = _ffn_sample(x1, h1, st_conv, t, wts["ffn"], fk=512)
    shift = _last_shift_row(zrkv, zl, row0, b, t)
    return out.reshape(b, t, D_MODEL), shift, wkv, pool, conv


def kernel(x_prompt, x_sample, state_shift, state_wkv, state_pool, state_conv, norm_pre_mix, w_in, mu_shift, w0, w2, a0, a2, g2, k_k, k_a, r_k, lnx_w, lnx_b, w_branch_a, pool_w, pool_scale, w_branch_b, w_out, norm_post_mix, norm_pre_ffn, w_ffn_in, conv_w, conv_b, w_ffn_out, norm_post_ffn):
    weights = (norm_pre_mix, w_in, mu_shift, w0, w2, a0, a2, g2, k_k, k_a, r_k, lnx_w, lnx_b,
               w_branch_a, pool_w, pool_scale, w_branch_b, w_out, norm_post_mix,
               norm_pre_ffn, w_ffn_in, conv_w, conv_b, w_ffn_out, norm_post_ffn)
    depth = w_in.shape[0]
    yp, ys = x_prompt, x_sample
    p_states, s_states = [], []
    for l in range(depth):
        wts = _layer_weights(l, *weights)
        n_prompt = yp.shape[0] * yp.shape[1]
        z = _project(yp.reshape(n_prompt, D_MODEL), ys.reshape(-1, D_MODEL), wts)
        bf16_wts, (yp, *ps) = _prompt_layer(yp, z, wts)
        ys, *ss = _sample_layer(ys, z, n_prompt, state_shift[l], state_wkv[l], state_pool[l], state_conv[l],
                                wts["prep"], bf16_wts)
        p_states.append(ps)
        s_states.append(ss)
    stack = lambda states, i: jnp.stack([s[i] for s in states])
    return (yp, ys,
            stack(p_states, 0), stack(p_states, 1), stack(p_states, 2), stack(p_states, 3),
            stack(s_states, 0), stack(s_states, 1), stack(s_states, 2), stack(s_states, 3))
```

```python
import functools
import math

import jax
import jax.numpy as jnp
from jax import lax
from jax.experimental import pallas as pl
from jax.experimental.pallas import tpu as pltpu

F32 = jnp.float32
BF16 = jnp.bfloat16

D_MODEL = 2048
HEAD_DIM = 64
D_RWKV = 1024
N_HEADS = 16
RANK_W, RANK_A, RANK_G = 64, 64, 160
D_LORA = RANK_W + RANK_A + RANK_G
D_LORA_PAD = 384
D_SHIFT = 3 * D_RWKV + D_LORA
D_SHIFT_PAD = 3 * D_RWKV + D_LORA_PAD
D_POOL = 1024
POOL_WINDOWS = (2, 4, 8, 16)
POOL_GROUP = 256
POOL_HIST = 15
D_FF = 5632
CONV_W = 3
NORM_EPS = 1e-6
GN_EPS = 64e-5
PAST_LEN = 16384

LANES_V7X = 128
SLAB = 8
MXU_DIM_V7X = 256
HEADS_PER_GROUP = MXU_DIM_V7X // HEAD_DIM
N_GROUPS = N_HEADS // HEADS_PER_GROUP
CHUNK = 64
VMEM_LIMIT_V7X = 56 * 1024 * 1024


def _params(*sem):
    return pltpu.CompilerParams(dimension_semantics=sem, vmem_limit_bytes=VMEM_LIMIT_V7X)


def _mm(a, b):
    return jnp.dot(a.astype(BF16), b.astype(BF16), preferred_element_type=F32)


def _mm_nt(a, b):
    return lax.dot_general(a.astype(BF16), b.astype(BF16), (((1,), (1,)), ((), ())),
                           preferred_element_type=F32)


def _split_hi_lo(x):
    hi = x.astype(BF16)
    lo = (x - hi.astype(F32)).astype(BF16)
    return hi, lo


def _head_sum(x, ones_bd):
    hi, lo = _split_hi_lo(x)
    outs = []
    for gi in range(x.shape[1] // MXU_DIM_V7X):
        sl = slice(MXU_DIM_V7X * gi, MXU_DIM_V7X * (gi + 1))
        outs.append(jnp.dot(hi[:, sl], ones_bd, preferred_element_type=F32)
                    + jnp.dot(lo[:, sl], ones_bd, preferred_element_type=F32))
    return jnp.concatenate(outs, axis=1)


def _rmsnorm(x, g):
    return x * lax.rsqrt(jnp.mean(x * x, axis=-1, keepdims=True) + NORM_EPS) * g


def _replace_step_rows(scr_ref, base, seq, steps):
    nb = base.shape[0] // seq
    for j in range(base.shape[1] // LANES_V7X):
        ls = slice(LANES_V7X * j, LANES_V7X * (j + 1))
        scr_ref[j] = base[:, ls]
        for t, vals in steps:
            scr_ref[j, pl.ds(t, nb, stride=seq), :] = vals[:, ls]
    return jnp.concatenate([scr_ref[j] for j in range(base.shape[1] // LANES_V7X)], axis=1)


def _pre_norm_kernel(prompt_tiles, xp_ref, xs_ref, g_ref, h_ref):
    i = pl.program_id(0)

    @pl.when(i < prompt_tiles)
    def _():
        h_ref[...] = _rmsnorm(xp_ref[...], g_ref[...]).astype(BF16)

    @pl.when(i >= prompt_tiles)
    def _():
        h_ref[...] = _rmsnorm(xs_ref[...], g_ref[...]).astype(BF16)


def _pre_norm(xp, xs, g, tm):
    pt, st = xp.shape[0] // tm, xs.shape[0] // tm
    return pl.pallas_call(
        functools.partial(_pre_norm_kernel, pt),
        grid=(pt + st,),
        in_specs=[pl.BlockSpec((tm, D_MODEL), lambda i: (jnp.minimum(i, pt - 1), 0)),
                  pl.BlockSpec((tm, D_MODEL), lambda i: (jnp.maximum(i - pt, 0), 0)),
                  _const_spec(g.shape)],
        out_specs=pl.BlockSpec((tm, D_MODEL), lambda i: (i, 0)),
        out_shape=jax.ShapeDtypeStruct((xp.shape[0] + xs.shape[0], D_MODEL), BF16),
        compiler_params=_params("arbitrary"),
        name="pre_norm",
    )(xp, xs, g)


def _in_proj_kernel(h_ref, w_ref, o_ref, wb_ref):
    @pl.when(pl.program_id(1) == 0)
    def _():
        wb_ref[...] = w_ref[...].astype(BF16)

    o_ref[...] = lax.dot_general(h_ref[...], wb_ref[...], (((1,), (1,)), ((), ())),
                                 preferred_element_type=F32)


def _in_proj(h, w_t, off, width, tn, tm):
    n = h.shape[0]
    assert width % tn == 0 and n % tm == 0
    return pl.pallas_call(
        _in_proj_kernel,
        grid=(width // tn, n // tm),
        in_specs=[pl.BlockSpec((tm, D_MODEL), lambda j, i: (i, 0)),
                  pl.BlockSpec((pl.Element(tn), pl.Element(D_MODEL)),
                               lambda j, i: (SLAB * (off // SLAB + j * (tn // SLAB)), 0))],
        out_specs=pl.BlockSpec((tm, tn), lambda j, i: (i, j)),
        out_shape=jax.ShapeDtypeStruct((n, width), F32),
        scratch_shapes=[pltpu.VMEM((tn, D_MODEL), BF16)],
        compiler_params=_params("arbitrary", "arbitrary"),
        name="in_proj",
    )(h, w_t)


def _prep_math(zs, prev, mu, w_lora, w0, a0, k_k, k_a, r_k, ones_bd):
    f = zs + (prev - zs) * mu
    r = f[:, 0:D_RWKV]
    k = f[:, D_RWKV:2 * D_RWKV]
    v = f[:, 2 * D_RWKV:3 * D_RWKV]
    low = f[:, 3 * D_RWKV:3 * D_RWKV + D_LORA_PAD]
    lane = lax.broadcasted_iota(jnp.int32, low.shape, 1)
    act = jnp.where(lane < RANK_W, jnp.tanh(low),
                    jnp.where(lane < RANK_W + RANK_A, low,
                              jnp.where(lane < D_LORA, jax.nn.sigmoid(low), 0.0)))
    lora = _mm(act, w_lora)
    u = w0 + lora[:, 0:D_RWKV]
    lw = -math.exp(-0.5) * jax.nn.sigmoid(u)
    a = jax.nn.sigmoid(a0 + lora[:, D_RWKV:2 * D_RWKV])
    g = lora[:, 2 * D_RWKV:3 * D_RWKV]
    kk = k * k_k
    kk = kk * jnp.minimum(lax.rsqrt(_head_sum(kk * kk, ones_bd)), 1e12)
    k2 = k * (1.0 + (a - 1.0) * k_a)
    bonus = _head_sum(r * k2 * r_k, ones_bd) * v
    return r, lw, k2, v, -kk, kk * a, g, bonus


N_PREP_WEIGHTS = 8


def _prep_sample_kernel(seq, zrkv_ref, zl_ref, st_ref, *refs):
    w_refs, o_refs, prev_ref = refs[:N_PREP_WEIGHTS], refs[N_PREP_WEIGHTS:-1], refs[-1]
    zs = jnp.concatenate([zrkv_ref[...], zl_ref[...]], axis=1)
    prev = _replace_step_rows(prev_ref, pltpu.roll(zs, 1, 0), seq, [(0, st_ref[...])])
    for o_ref, o in zip(o_refs, _prep_math(zs, prev, *[w[...] for w in w_refs])):
        o_ref[...] = o


def _const_spec(shape):
    nd = len(shape)
    return pl.BlockSpec(shape, lambda *_: (0,) * nd)


def _prep_sample(zrkv, zl, row0, n, st, seq, wts, tm):
    assert tm % (SLAB * seq) == 0 and n % tm == 0 and row0 % tm == 0 and len(wts) == N_PREP_WEIGHTS
    first = row0 // tm
    outs = [jax.ShapeDtypeStruct((n, D_RWKV), F32)] * 8
    row_spec = pl.BlockSpec((tm, D_RWKV), lambda i: (i, 0))
    return pl.pallas_call(
        functools.partial(_prep_sample_kernel, seq),
        grid=(n // tm,),
        in_specs=[pl.BlockSpec((tm, 3 * D_RWKV), lambda i: (first + i, 0)),
                  pl.BlockSpec((tm, D_LORA_PAD), lambda i: (first + i, 0)),
                  pl.BlockSpec((tm // seq, D_SHIFT_PAD), lambda i: (i, 0))]
        + [_const_spec(w.shape) for w in wts],
        out_specs=[row_spec] * 8,
        out_shape=outs,
        scratch_shapes=[pltpu.VMEM((D_SHIFT_PAD // LANES_V7X, tm, LANES_V7X), F32)],
        compiler_params=_params("parallel"),
        name="prep_sample",
    )(zrkv, zl, st, *wts)


def _expand_bd(a, bd_mask):
    return jnp.where(bd_mask, jnp.concatenate([a] * HEADS_PER_GROUP, axis=0), 0.0)


def _chunk_cumsum(x):
    row = lax.broadcasted_iota(jnp.int32, x.shape, 0)
    s = 1
    while s < x.shape[0]:
        x = x + jnp.where(row >= s, pltpu.roll(x, s, 0), 0.0)
        s *= 2
    return x


def _wkv_intra(xs, bts, kts, vs, pzs, masks, order):
    bd_mask, strict_mask, incl_mask, eye_c, head_masks = masks
    c = vs[0].shape[0]
    ystacks = [jnp.concatenate([jnp.where(m, bt, 0.0) for m in head_masks]
                               + [jnp.where(m, kt, 0.0) for m in head_masks], axis=0)
               for bt, kt in zip(bts, kts)]
    grams = [_mm_nt(x, ys) for x, ys in zip(xs, ystacks)]
    l_abs = [jnp.where(strict_mask, g[0:c, 0:4 * c], 0.0) for g in grams]
    l_aks = [jnp.where(strict_mask, g[0:c, 4 * c:8 * c], 0.0) for g in grams]
    m_rbs = [jnp.where(incl_mask, g[c:2 * c, 0:4 * c], 0.0) for g in grams]
    m_rks = [jnp.where(incl_mask, g[c:2 * c, 4 * c:8 * c], 0.0) for g in grams]

    tinvs = [eye_c + l for l in l_abs]
    lps = [_mm(l, _expand_bd(l, bd_mask)) for l in l_abs]
    p = 2
    while True:
        rhss = [_expand_bd(lp, bd_mask) for lp in lps]
        if 2 * p >= order:
            tinvs = [t + _mm(t, rhs) for t, rhs in zip(tinvs, rhss)]
            break
        ress = [_mm(jnp.concatenate([lp, t], axis=0), rhs) for lp, t, rhs in zip(lps, tinvs, rhss)]
        lps = [res[0:c] for res in ress]
        tinvs = [t + res[c:2 * c] for t, res in zip(tinvs, ress)]
        p *= 2

    v_bds = [_expand_bd(v, bd_mask) for v in vs]
    ws = [pz[0:c] + _mm(l_ak, v_bd) for pz, l_ak, v_bd in zip(pzs, l_aks, v_bds)]
    us = [_mm(t, _expand_bd(w, bd_mask)) for t, w in zip(tinvs, ws)]
    ys = [pz[c:2 * c] + _mm(jnp.concatenate([m_rb, m_rk], axis=1),
                            jnp.concatenate([_expand_bd(u, bd_mask), v_bd], axis=0))
          for pz, m_rb, m_rk, u, v_bd in zip(pzs, m_rbs, m_rks, us, v_bds)]
    return ys, us


def _wkv_chunk(groups, masks):
    c = groups[0][0].shape[0]
    cls = [_chunk_cumsum(g[1]) for g in groups]
    xs = [jnp.concatenate([g[4] * jnp.exp(cl - g[1]), g[0] * jnp.exp(cl)], axis=0)
          for g, cl in zip(groups, cls)]
    pzs = [_mm_nt(x, g[6]) for x, g in zip(xs, groups)]
    e_negs = [jnp.exp(-cl) for cl in cls]
    ys, us = _wkv_intra(xs, [g[5] * e for g, e in zip(groups, e_negs)],
                        [g[2] * e for g, e in zip(groups, e_negs)], [g[3] for g in groups], pzs, masks, c)
    dss = []
    for g, cl, u in zip(groups, cls, us):
        e_rem = jnp.exp(cl[c - 1:c, :] - cl)
        uv_t = jnp.concatenate([u, g[3]], axis=0).T
        bk = jnp.concatenate([g[5] * e_rem, g[2] * e_rem], axis=0)
        dss.append(_mm(uv_t, bk))
    return [(y, g[6] * jnp.exp(cl[c - 1:c, :]) + jnp.where(masks[0], ds, 0.0))
            for y, g, cl, ds in zip(ys, groups, cls, dss)]


def _wkv_masks(c, seq):
    n = HEADS_PER_GROUP * c
    rr = lax.broadcasted_iota(jnp.int32, (n, n), 0)
    cc = lax.broadcasted_iota(jnp.int32, (n, n), 1)
    bd_mask = (rr // c) == (cc // c)
    t = lax.broadcasted_iota(jnp.int32, (c, n), 0)
    s = lax.broadcasted_iota(jnp.int32, (c, n), 1) % c
    same = (t // seq) == (s // seq)
    strict_mask = same & (t > s)
    incl_mask = same & (t >= s)
    eye_c = jnp.where(t == s, 1.0, 0.0).astype(F32)
    lane = lax.broadcasted_iota(jnp.int32, (c, MXU_DIM_V7X), 1)
    head_masks = [(lane // HEAD_DIM) == h for h in range(HEADS_PER_GROUP)]
    return bd_mask, strict_mask, incl_mask, eye_c, head_masks


def _wkv_prompt_kernel(nb, n_casts, *refs):
    it = iter(refs)
    take = lambda k: [next(it) for _ in range(k)]
    zrkv_refs, zl_refs, pw_refs, w_refs = take(nb), take(nb), take(N_PREP_WEIGHTS), take(n_casts)
    y_ref, bonus_ref, g_ref, sout_ref = take(4)
    wb_refs = take(n_casts)
    s_ref, carry_ref = take(2)
    ci = pl.program_id(0)

    @pl.when(ci == 0)
    def _():
        s_ref[...] = jnp.zeros_like(s_ref)
        carry_ref[...] = jnp.zeros_like(carry_ref)

    for w_ref, wb_ref in zip(w_refs, wb_refs):
        wb_ref[...] = w_ref[...].astype(BF16)

    zs_parts, prev_parts = [], []
    for bi in range(nb):
        zs = jnp.concatenate([zrkv_refs[bi][...], zl_refs[bi][...]], axis=1)
        row = lax.broadcasted_iota(jnp.int32, zs.shape, 0)
        prev_parts.append(jnp.where(row == 0, carry_ref[bi, 0:1, :], pltpu.roll(zs, 1, 0)))
        carry_ref[bi, 0:1, :] = zs[CHUNK - 1:, :]
        zs_parts.append(zs)
    *scan_ops, g, bonus = _prep_math(
        jnp.concatenate(zs_parts, axis=0), jnp.concatenate(prev_parts, axis=0), *[w[...] for w in pw_refs])
    for bi in range(nb):
        rows = slice(CHUNK * bi, CHUNK * (bi + 1))
        g_ref[bi] = g[rows]
        bonus_ref[bi] = bonus[rows]

    masks = _wkv_masks(CHUNK, CHUNK)
    chains = [(bi, gi) for bi in range(nb) for gi in range(N_GROUPS)]
    lanes = lambda gi: slice(MXU_DIM_V7X * gi, MXU_DIM_V7X * (gi + 1))
    groups = [tuple(x[CHUNK * bi:CHUNK * (bi + 1), lanes(gi)] for x in scan_ops)
              + (s_ref[bi, gi],) for bi, gi in chains]
    for (bi, gi), (y, s_new) in zip(chains, _wkv_chunk(groups, masks)):
        y_ref[bi, :, lanes(gi)] = y
        s_ref[bi, gi] = s_new

    @pl.when(ci == pl.num_programs(0) - 1)
    def _():
        for bi, gi in chains:
            s_bd = s_ref[bi, gi]
            for h in range(HEADS_PER_GROUP):
                hs = slice(HEAD_DIM * h, HEAD_DIM * (h + 1))
                sout_ref[bi, HEADS_PER_GROUP * gi + h] = s_bd[hs, hs]


def _wkv_prompt(zrkv, zl, b, t, prep_wts, weights):
    steps = t // CHUNK
    assert len(prep_wts) == N_PREP_WEIGHTS
    seq_rows = lambda w, bi: pl.BlockSpec((CHUNK, w), lambda ci: (bi * steps + ci, 0))
    spec = pl.BlockSpec((b, CHUNK, D_RWKV), lambda ci: (0, ci, 0))
    state = (b, N_GROUPS, MXU_DIM_V7X, MXU_DIM_V7X)
    out_state = (b, N_HEADS, HEAD_DIM, HEAD_DIM)
    bf16_rows = 2 * SLAB
    assert all(w.shape[0] % (steps * bf16_rows) == 0 for w in weights)
    slabs = [pl.BlockSpec((w.shape[0] // steps, w.shape[1]), lambda ci: (ci, 0)) for w in weights]
    rows = jax.ShapeDtypeStruct((b, t, D_RWKV), F32)
    y, bonus, g, s, *wb = pl.pallas_call(
        functools.partial(_wkv_prompt_kernel, b, len(weights)),
        grid=(steps,),
        in_specs=[seq_rows(3 * D_RWKV, bi) for bi in range(b)] + [seq_rows(D_LORA_PAD, bi) for bi in range(b)]
        + [_const_spec(w.shape) for w in prep_wts] + slabs,
        out_specs=[spec, spec, spec, _const_spec(out_state)] + slabs,
        out_shape=[rows, rows, rows, jax.ShapeDtypeStruct(out_state, F32)]
        + [jax.ShapeDtypeStruct(w.shape, BF16) for w in weights],
        scratch_shapes=[pltpu.VMEM(state, F32), pltpu.VMEM((b, SLAB, D_SHIFT_PAD), F32)],
        compiler_params=_params("arbitrary"),
        name="wkv_prompt",
    )(*([zrkv] * b), *([zl] * b), *prep_wts, *weights)
    return y, bonus, g, s, wb


HEAD_PAIR = 2 * HEAD_DIM


def _wkv_steps_kernel(seq, r_ref, lw_ref, k2_ref, v_ref, as_ref, bs_ref, s0_ref, y_ref, sout_ref,
                      op_ref, vt_ref, yt_ref):
    nb = r_ref.shape[0] // seq
    for t in range(seq):
        rows = pl.ds(t, nb, stride=seq)
        op_ref[0, t] = r_ref[rows, :].T
        op_ref[1, t] = jnp.exp(lw_ref[rows, :]).T
        op_ref[2, t] = k2_ref[rows, :].T
        op_ref[3, t] = as_ref[rows, :].T
        op_ref[4, t] = bs_ref[rows, :].T
        vt_ref[t] = v_ref[rows, :].T

    for hh in range(HEAD_PAIR // HEAD_DIM):
        ks = slice(HEAD_DIM * hh, HEAD_DIM * (hh + 1))

        def body(vi, carry):
            s = s0_ref[hh, vi]
            for t in range(seq):
                sa = jnp.sum(s * op_ref[3, t, ks, :], axis=0, keepdims=True)
                vrow = vt_ref[t, pl.ds(HEAD_DIM * hh + vi, 1), :]
                s = s * op_ref[1, t, ks, :] + sa * op_ref[4, t, ks, :] + vrow * op_ref[2, t, ks, :]
                yt_ref[t, pl.ds(HEAD_DIM * hh + vi, 1), :] = jnp.sum(s * op_ref[0, t, ks, :], axis=0,
                                                                      keepdims=True)
            sout_ref[hh, vi] = s
            return carry

        lax.fori_loop(0, HEAD_DIM, body, 0, unroll=4)

    for t in range(seq):
        y_ref[pl.ds(t, nb, stride=seq), :] = yt_ref[t].T


def _wkv_steps(r, lw, k2, v, a_s, b_s, s0_native, seq):
    n = r.shape[0]
    nb = n // seq
    assert s0_native.shape == (N_HEADS, HEAD_DIM, HEAD_DIM, nb) and nb == LANES_V7X
    vec = pl.BlockSpec((n, HEAD_PAIR), lambda i: (0, i))
    st = pl.BlockSpec((HEAD_PAIR // HEAD_DIM, HEAD_DIM, HEAD_DIM, nb), lambda i: (i, 0, 0, 0))
    return pl.pallas_call(
        functools.partial(_wkv_steps_kernel, seq),
        grid=(N_HEADS * HEAD_DIM // HEAD_PAIR,),
        in_specs=[vec] * 6 + [st],
        out_specs=[vec, st],
        out_shape=[jax.ShapeDtypeStruct((n, D_RWKV), F32), jax.ShapeDtypeStruct(s0_native.shape, F32)],
        scratch_shapes=[pltpu.VMEM((5, seq, HEAD_PAIR, nb), F32), pltpu.VMEM((seq, HEAD_PAIR, nb), F32),
                        pltpu.VMEM((seq, HEAD_PAIR, nb), F32)],
        compiler_params=_params("parallel"),
        name="wkv_steps",
    )(r, lw, k2, v, a_s, b_s, s0_native)


POOL_CARRY = 16


def _pool_prompt_tile(zp, carry_ref, tile_in_seq):
    tm = zp.shape[0]

    @pl.when(tile_in_seq == 0)
    def _():
        carry_ref[...] = jnp.zeros_like(carry_ref)

    buf = jnp.concatenate([carry_ref[...], zp], axis=0)
    carry_ref[...] = zp[tm - POOL_CARRY:, :]
    pos = (tile_in_seq * tm + lax.broadcasted_iota(jnp.int32, (tm, POOL_GROUP), 0) + 1).astype(F32)
    ds = []
    for gi, win in enumerate(POOL_WINDOWS):
        sl = slice(POOL_GROUP * gi, POOL_GROUP * (gi + 1))
        acc = buf[:, sl]
        s = 1
        while s < win:
            acc = acc + pltpu.roll(acc, s, 0)
            s *= 2
        ds.append(acc[POOL_CARRY:, :] / jnp.minimum(float(win), pos) - zp[:, sl])
    return jnp.concatenate(ds, axis=1)


def _pool_sample_kernel(seq, hist_ref, zp_ref, d_ref, new_ref):
    nb = hist_ref.shape[1]
    step = lambda t: zp_ref[pl.ds(t, nb, stride=seq), :]
    past = lambda j: step(j) if j >= 0 else hist_ref[POOL_HIST + j]
    tiles_per_group = POOL_GROUP // LANES_V7X
    for gi, win in enumerate(POOL_WINDOWS):
        @pl.when(pl.program_id(0) // tiles_per_group == gi)
        def _():
            for t in range(seq):
                acc = past(t)
                for j in range(1, win):
                    acc = acc + past(t - j)
                d_ref[pl.ds(t, nb, stride=seq), :] = acc / float(min(win, PAST_LEN + 1)) - past(t)
    for j in range(POOL_HIST):
        src = j + seq
        new_ref[j] = hist_ref[src] if src < POOL_HIST else step(src - POOL_HIST)


def _pool_sample(hist, zp, row0, seq):
    nb = hist.shape[1]
    n = nb * seq
    assert row0 % n == 0
    hist_spec = pl.BlockSpec((POOL_HIST, nb, LANES_V7X), lambda j: (0, 0, j))
    return pl.pallas_call(
        functools.partial(_pool_sample_kernel, seq),
        grid=(D_POOL // LANES_V7X,),
        in_specs=[hist_spec, pl.BlockSpec((n, LANES_V7X), lambda j: (row0 // n, j))],
        out_specs=[pl.BlockSpec((n, LANES_V7X), lambda j: (0, j)), hist_spec],
        out_shape=[jax.ShapeDtypeStruct((n, D_POOL), F32), jax.ShapeDtypeStruct(hist.shape, F32)],
        compiler_params=_params("parallel"),
        name="pool_sample",
    )(hist, zp)


def _merge_kernel(seq_tiles, y_ref, bonus_ref, g_ref, d_ref, zga_ref, zgb_ref, x_ref,
                  lw_ref, lb_ref, ones_ref, wa_ref, pw_ref, ps_ref, wb_ref,
                  wo_ref, gpost_ref, gnext_ref, o_ref, h_ref, *carry):
    if seq_tiles is None:
        d = d_ref[...]
    else:
        d = _pool_prompt_tile(d_ref[...], carry[0], pl.program_id(0) % seq_tiles)
    ones_bd = ones_ref[...]
    y = y_ref[...]
    mu = _head_sum(y, ones_bd) * (1.0 / HEAD_DIM)
    yc = y - mu
    var = _head_sum(yc * yc, ones_bd) * (1.0 / HEAD_DIM)
    yn = yc * lax.rsqrt(var + GN_EPS) * lw_ref[...] + lb_ref[...]
    ya = (yn + bonus_ref[...]) * g_ref[...]
    yb = jnp.concatenate(
        [_mm(d[:, POOL_GROUP * gi:POOL_GROUP * (gi + 1)], pw_ref[gi]) for gi in range(len(POOL_WINDOWS))],
        axis=1) * ps_ref[...]
    m = (jax.nn.sigmoid(zga_ref[...]) * _mm(ya, wa_ref[...])
         + jax.nn.sigmoid(zgb_ref[...]) * _mm(yb, wb_ref[...]))
    x1 = x_ref[...] + _rmsnorm(_mm(m, wo_ref[...]), gpost_ref[...])
    o_ref[...] = x1
    h_ref[...] = _rmsnorm(x1, gnext_ref[...]).astype(BF16)


def _merge(y, bonus, g, d, zga, zgb, x2d, row0, wts, tm, seq_len=None):
    n = y.shape[0]
    assert row0 % tm == 0 and (seq_len is None or (seq_len % tm == 0 and tm >= POOL_CARRY))
    first = row0 // tm
    half = pl.BlockSpec((tm, D_RWKV), lambda i: (i, 0))
    full = pl.BlockSpec((tm, D_MODEL), lambda i: (i, 0))
    gate = pl.BlockSpec((tm, D_MODEL), lambda i: (first + i, 0))
    resident = [pl.BlockSpec(w.shape, functools.partial(lambda nd, i: (0,) * nd, w.ndim),
                             pipeline_mode=pl.Buffered(1)) for w in wts]
    return pl.pallas_call(
        functools.partial(_merge_kernel, None if seq_len is None else seq_len // tm),
        grid=(n // tm,),
        in_specs=[half] * 4 + [gate] * 2 + [full] + resident,
        out_specs=[full, full],
        out_shape=[jax.ShapeDtypeStruct((n, D_MODEL), F32), jax.ShapeDtypeStruct((n, D_MODEL), BF16)],
        scratch_shapes=[] if seq_len is None else [pltpu.VMEM((POOL_CARRY, D_POOL), F32)],
        compiler_params=_params("parallel" if seq_len is None else "arbitrary"),
        name="merge",
    )(y, bonus, g, d, zga, zgb, x2d, *wts)


def _gelu_tanh(x):
    return 0.5 * x * (1.0 + jnp.tanh(0.7978845608028654 * (x + 0.044715 * x * x * x)))


def _ffn_body(x_ref, h_ref, wg_ref, wu_ref, cw_ref, cb_ref, wo_ref, gp_ref,
              o_ref, acc_ref, prev1, prev2, emit_tail):
    c = pl.program_id(1)

    @pl.when(c == 0)
    def _():
        acc_ref[...] = jnp.zeros_like(acc_ref)

    h = h_ref[...]
    gate = jnp.dot(h, wg_ref[...], preferred_element_type=F32)
    up = jnp.dot(h, wu_ref[...], preferred_element_type=F32)
    cw = cw_ref[...]
    cv = cb_ref[...] + cw[0:1, :] * prev2(gate) + cw[1:2, :] * prev1(gate) + cw[2:3, :] * gate
    emit_tail(gate)
    acc_ref[...] += _mm(_gelu_tanh(cv) * up, wo_ref[...])

    @pl.when(c == pl.num_programs(1) - 1)
    def _():
        o_ref[...] = x_ref[...] + _rmsnorm(acc_ref[...], gp_ref[...])


def _ffn_prompt_kernel(tiles_per_seq, x_ref, h_ref, wg_ref, wu_ref, cw_ref, cb_ref, wo_ref, gp_ref,
                       o_ref, tail_ref, acc_ref, carry_ref):
    i = pl.program_id(0)
    c = pl.program_id(1)

    @pl.when(i % tiles_per_seq == 0)
    def _():
        carry_ref[c] = jnp.zeros(carry_ref.shape[1:], F32)

    hist = carry_ref[c]

    def shifted(gate, s):
        row = lax.broadcasted_iota(jnp.int32, gate.shape, 0)
        rolled = pltpu.roll(gate, s, 0)
        out = rolled
        for j in range(s):
            out = jnp.where(row == j, hist[8 - s + j:9 - s + j, :], out)
        return out

    def emit_tail(gate):
        tail_ref[...] = gate[gate.shape[0] - tail_ref.shape[0]:, :]

    _ffn_body(x_ref, h_ref, wg_ref, wu_ref, cw_ref, cb_ref, wo_ref, gp_ref, o_ref,
              acc_ref, lambda g: shifted(g, 1), lambda g: shifted(g, 2), emit_tail)
    carry_ref[c] = tail_ref[...]


def _ffn_sample_kernel(seq, x_ref, st_ref, h_ref, wg_ref, wu_ref, cw_ref, cb_ref, wo_ref, gp_ref,
                       o_ref, tail_ref, acc_ref, g_ref, p1_ref, p2_ref):
    nb = st_ref.shape[0]
    st0 = st_ref[:, 0, :]
    st1 = st_ref[:, 1, :]
    prev1 = lambda gate: _replace_step_rows(p1_ref, pltpu.roll(gate, 1, 0), seq, [(0, st1)])
    prev2 = lambda gate: _replace_step_rows(p2_ref, pltpu.roll(gate, 2, 0), seq, [(0, st0), (1, st1)])

    def emit_tail(gate):
        tiles = gate.shape[1] // LANES_V7X
        for j in range(tiles):
            g_ref[j] = gate[:, LANES_V7X * j:LANES_V7X * (j + 1)]
        for s in range(CONV_W - 1):
            rows = pl.ds(seq - (CONV_W - 1) + s, nb, stride=seq)
            tail_ref[:, s, :] = jnp.concatenate([g_ref[j, rows, :] for j in range(tiles)], axis=1)

    _ffn_body(x_ref, h_ref, wg_ref, wu_ref, cw_ref, cb_ref, wo_ref, gp_ref, o_ref,
              acc_ref, prev1, prev2, emit_tail)


def _ffn_prompt(x2d, h, seq_len, wts, tm, fk):
    n = x2d.shape[0]
    nc = D_FF // fk
    w_in, cw, cb, w_out, gp = wts
    full = pl.BlockSpec((tm, D_MODEL), lambda i, c: (i, 0))
    return pl.pallas_call(
        functools.partial(_ffn_prompt_kernel, seq_len // tm),
        grid=(n // tm, nc),
        in_specs=[full, full,
                  pl.BlockSpec((D_MODEL, fk), lambda i, c: (0, c)),
                  pl.BlockSpec((D_MODEL, fk), lambda i, c: (0, c + nc)),
                  pl.BlockSpec((CONV_W, fk), lambda i, c: (0, c)),
                  pl.BlockSpec((1, fk), lambda i, c: (0, c)),
                  pl.BlockSpec((fk, D_MODEL), lambda i, c: (c, 0)),
                  _const_spec(gp.shape)],
        out_specs=[full, pl.BlockSpec((None, 8, fk), lambda i, c: (i, 0, c))],
        out_shape=[jax.ShapeDtypeStruct((n, D_MODEL), F32),
                   jax.ShapeDtypeStruct((n // tm, 8, D_FF), F32)],
        scratch_shapes=[pltpu.VMEM((tm, D_MODEL), F32), pltpu.VMEM((nc, 8, fk), F32)],
        compiler_params=_params("arbitrary", "arbitrary"),
        name="ffn_prompt",
    )(x2d, h, w_in, w_in, cw, cb, w_out, gp)


def _ffn_sample(x2d, h, st, seq, wts, fk):
    n = x2d.shape[0]
    nb = n // seq
    assert seq >= CONV_W - 1 and st.shape == (nb, CONV_W - 1, D_FF)
    nc = D_FF // fk
    w_in, cw, cb, w_out, gp = wts
    full = pl.BlockSpec((n, D_MODEL), lambda i, c: (0, 0))
    cols = pl.BlockSpec((nb, CONV_W - 1, fk), lambda i, c: (0, 0, c))
    return pl.pallas_call(
        functools.partial(_ffn_sample_kernel, seq),
        grid=(1, nc),
        in_specs=[full, cols, full,
                  pl.BlockSpec((D_MODEL, fk), lambda i, c: (0, c)),
                  pl.BlockSpec((D_MODEL, fk), lambda i, c: (0, c + nc)),
                  pl.BlockSpec((CONV_W, fk), lambda i, c: (0, c)),
                  pl.BlockSpec((1, fk), lambda i, c: (0, c)),
                  pl.BlockSpec((fk, D_MODEL), lambda i, c: (c, 0)),
                  _const_spec(gp.shape)],
        out_specs=[full, cols],
        out_shape=[jax.ShapeDtypeStruct((n, D_MODEL), F32), jax.ShapeDtypeStruct(st.shape, F32)],
        scratch_shapes=[pltpu.VMEM((n, D_MODEL), F32)]
        + [pltpu.VMEM((fk // LANES_V7X, n, LANES_V7X), F32)] * 3,
        compiler_params=_params("arbitrary", "arbitrary"),
        name="ffn_sample",
    )(x2d, st, h, w_in, w_in, cw, cb, w_out, gp)


def _row(v):
    return v.reshape(1, -1).astype(F32)


def _ones_bd():
    i = jnp.arange(MXU_DIM_V7X) // HEAD_DIM
    return (i[:, None] == i[None, :]).astype(BF16)


def _layer_weights(l, norm_pre_mix, w_in, mu_shift, w0, w2, a0, a2, g2, k_k, k_a, r_k, lnx_w, lnx_b,
                   w_branch_a, pool_w, pool_scale, w_branch_b, w_out, norm_post_mix,
                   norm_pre_ffn, w_ffn_in, conv_w, conv_b, w_ffn_out, norm_post_ffn):
    w_t = jnp.swapaxes(w_in[l], 0, 1)
    mu = jnp.pad(mu_shift[l], (0, D_SHIFT_PAD - D_SHIFT)).reshape(1, -1)
    w_lora = jnp.zeros((D_LORA_PAD, 3 * D_RWKV), F32)
    w_lora = w_lora.at[0:RANK_W, 0:D_RWKV].set(w2[l])
    w_lora = w_lora.at[RANK_W:RANK_W + RANK_A, D_RWKV:2 * D_RWKV].set(a2[l])
    w_lora = w_lora.at[RANK_W + RANK_A:D_LORA, 2 * D_RWKV:].set(g2[l])
    ones_bd = _ones_bd()

    def with_bf16(wa, pw, wb, wo, wfi, wfo):
        return dict(
            merge=(_row(lnx_w[l]), _row(lnx_b[l]), ones_bd, wa,
                   pw.reshape(pool_w.shape[1:]), _row(pool_scale[l]), wb,
                   wo, _row(norm_post_mix[l]), _row(norm_pre_ffn[l])),
            ffn=(wfi, conv_w[l].astype(F32), _row(conv_b[l]), wfo, _row(norm_post_ffn[l])),
        )

    return dict(
        in_proj=(_row(norm_pre_mix[l]), w_t),
        prep=(mu, w_lora.astype(BF16), _row(w0[l]), _row(a0[l]), _row(k_k[l]), _row(k_a[l]), _row(r_k[l]),
              ones_bd),
        f32_matmul_weights=[w_branch_a[l], pool_w[l].reshape(-1, POOL_GROUP), w_branch_b[l], w_out[l],
                            w_ffn_in[l], w_ffn_out[l]],
        with_bf16=with_bf16,
    )


def _largest_tile(n, cap, mult=16):
    best = None
    for d in range(mult, min(n, cap) + 1, mult):
        if n % d == 0:
            best = d
    assert best is not None, (n, cap)
    return best


def _project(xp2d, xs2d, wts):
    g, w_t = wts["in_proj"]
    tm_norm = _largest_tile(math.gcd(xp2d.shape[0], xs2d.shape[0]), 512)
    h = _pre_norm(xp2d, xs2d, g, tm_norm)
    tm = _largest_tile(h.shape[0], 1088)
    zrkv = _in_proj(h, w_t, 0, 3 * D_RWKV, D_RWKV, tm)
    zl = _in_proj(h, w_t, 3 * D_RWKV, D_LORA_PAD, D_LORA_PAD, tm)
    zp = _in_proj(h, w_t, D_SHIFT, D_POOL, D_POOL, tm)
    zga = _in_proj(h, w_t, D_SHIFT + D_POOL, D_MODEL, D_MODEL // 2, tm)
    zgb = _in_proj(h, w_t, D_SHIFT + D_POOL + D_MODEL, D_MODEL, D_MODEL // 2, tm)
    return zrkv, zl, zp, zga, zgb


def _last_shift_row(zrkv, zl, row0, b, t):
    if b <= 8:
        last = lambda z, w: jnp.concatenate(
            [lax.slice(z, (row0 + (i + 1) * t - 1, 0), (row0 + (i + 1) * t, w)) for i in range(b)])
    else:
        last = lambda z, w: lax.slice(z, (row0 + t - 1, 0), (row0 + b * t, w), (t, 1))
    return jnp.concatenate([last(zrkv, 3 * D_RWKV), last(zl, D_LORA)], axis=-1)[:, None, :]


def _prompt_layer(x, z, wts):
    b, t, _ = x.shape
    n = b * t
    x2d = x.reshape(n, D_MODEL)
    zrkv, zl, zp, zga, zgb = z
    y, bonus, g, wkv, bf16_weights = _wkv_prompt(zrkv, zl, b, t, wts["prep"], wts["f32_matmul_weights"])
    wts = wts["with_bf16"](*bf16_weights)
    flat = lambda a: a.reshape(n, a.shape[-1])
    x1, h1 = _merge(flat(y), flat(bonus), flat(g), zp, zga, zgb, x2d, 0, wts["merge"], tm=256, seq_len=t)
    ffn_tm = _largest_tile(t, 512)
    out, tail = _ffn_prompt(x1, h1, t, wts["ffn"], tm=ffn_tm, fk=512)
    shift = _last_shift_row(zrkv, zl, 0, b, t)
    pool =jnp.stack([zp[(i + 1) * t - POOL_HIST:(i + 1) * t] for i in range(b)])
    tiles = t // ffn_tm
    conv = tail.reshape(b, tiles, 8, D_FF)[:, tiles - 1, 8 - (CONV_W - 1):, :]
    return wts, (out.reshape(b, t, D_MODEL), shift, wkv, pool, conv)


def _sample_layer(x, z, row0, st_shift, st_wkv, st_pool, st_conv, prep_wts, wts):
    b, t, _ = x.shape
    n = b * t
    x2d = x.reshape(n, D_MODEL)
    zrkv, zl, zp, zga, zgb = z
    st = jnp.pad(st_shift.reshape(b, D_SHIFT), ((0, 0), (0, D_SHIFT_PAD - D_SHIFT)))
    tm = _largest_tile(math.gcd(row0, n), 128, mult=SLAB * t)
    r, lw, k2, v, a_s, b_s, g, bonus = _prep_sample(zrkv, zl, row0, n, st, t, prep_wts, tm=tm)
    y, wkv = _wkv_steps(r, lw, k2, v, a_s, b_s, jnp.transpose(st_wkv, (1, 2, 3, 0)), t)
    wkv = jnp.transpose(wkv, (3, 0, 1, 2))
    d, pool = _pool_sample(jnp.swapaxes(st_pool, 0, 1), zp, row0, t)
    pool = jnp.swapaxes(pool, 0, 1)
    x1, h1 = _merge(y, bonus, g, d, zga, zgb, x2d, row0, wts["merge"],
                    tm=_largest_tile(math.gcd(row0, n), 256, mult=16))
    out, conv = _ffn_sample(x1, h1, st_conv, t, wts["ffn"], fk=512)
    shift = _last_shift_row(zrkv, zl, row0, b, t)
    return out.reshape(b, t, D_MODEL), shift, wkv, pool, conv


def kernel(x_prompt, x_sample, state_shift, state_wkv, state_pool, state_conv, norm_pre_mix, w_in, mu_shift, w0, w2, a0, a2, g2, k_k, k_a, r_k, lnx_w, lnx_b, w_branch_a, pool_w, pool_scale, w_branch_b, w_out, norm_post_mix, norm_pre_ffn, w_ffn_in, conv_w, conv_b, w_ffn_out, norm_post_ffn):
    weights = (norm_pre_mix, w_in, mu_shift, w0, w2, a0, a2, g2, k_k, k_a, r_k, lnx_w, lnx_b,
               w_branch_a, pool_w, pool_scale, w_branch_b, w_out, norm_post_mix,
               norm_pre_ffn, w_ffn_in, conv_w, conv_b, w_ffn_out, norm_post_ffn)
    depth = w_in.shape[0]
    yp, ys = x_prompt, x_sample
    p_states, s_states = [], []
    for l in range(depth):
        wts = _layer_weights(l, *weights)
        n_prompt = yp.shape[0] * yp.shape[1]
        z = _project(yp.reshape(n_prompt, D_MODEL), ys.reshape(-1, D_MODEL), wts)
        bf16_wts, (yp, *ps) = _prompt_layer(yp, z, wts)
        ys, *ss = _sample_layer(ys, z, n_prompt, state_shift[l], state_wkv[l], state_pool[l], state_conv[l],
                                wts["prep"], bf16_wts)
        p_states.append(ps)
        s_states.append(ss)
    stack = lambda states, i: jnp.stack([s[i] for s in states])
    return (yp, ys,
            stack(p_states, 0), stack(p_states, 1), stack(p_states, 2), stack(p_states, 3),
            stack(s_states, 0), stack(s_states, 1), stack(s_states, 2), stack(s_states, 3))
```

```python
import functools
import math

import jax
import jax.numpy as jnp
from jax import lax
from jax.experimental import pallas as pl
from jax.experimental.pallas import tpu as pltpu

F32 = jnp.float32
BF16 = jnp.bfloat16

D_MODEL = 2048
HEAD_DIM = 64
D_RWKV = 1024
N_HEADS = 16
RANK_W, RANK_A, RANK_G = 64, 64, 160
D_LORA = RANK_W + RANK_A + RANK_G
D_LORA_PAD = 384
D_SHIFT = 3 * D_RWKV + D_LORA
D_SHIFT_PAD = 3 * D_RWKV + D_LORA_PAD
D_POOL = 1024
POOL_WINDOWS = (2, 4, 8, 16)
POOL_GROUP = 256
POOL_HIST = 15
D_FF = 5632
CONV_W = 3
NORM_EPS = 1e-6
GN_EPS = 64e-5
PAST_LEN = 16384

LANES_V7X = 128
SLAB = 8
MXU_DIM_V7X = 256
HEADS_PER_GROUP = MXU_DIM_V7X // HEAD_DIM
N_GROUPS = N_HEADS // HEADS_PER_GROUP
CHUNK = 64
VMEM_LIMIT_V7X = 56 * 1024 * 1024


def _params(*sem):
    return pltpu.CompilerParams(dimension_semantics=sem, vmem_limit_bytes=VMEM_LIMIT_V7X)


def _mm(a, b):
    return jnp.dot(a.astype(BF16), b.astype(BF16), preferred_element_type=F32)


def _mm_nt(a, b):
    return lax.dot_general(a.astype(BF16), b.astype(BF16), (((1,), (1,)), ((), ())),
                           preferred_element_type=F32)


def _split_hi_lo(x):
    hi = x.astype(BF16)
    lo = (x - hi.astype(F32)).astype(BF16)
    return hi, lo


def _head_sum(x, ones_bd):
    hi, lo = _split_hi_lo(x)
    outs = []
    for gi in range(x.shape[1] // MXU_DIM_V7X):
        sl = slice(MXU_DIM_V7X * gi, MXU_DIM_V7X * (gi + 1))
        outs.append(jnp.dot(hi[:, sl], ones_bd, preferred_element_type=F32)
                    + jnp.dot(lo[:, sl], ones_bd, preferred_element_type=F32))
    return jnp.concatenate(outs, axis=1)


def _rmsnorm(x, g):
    return x * lax.rsqrt(jnp.mean(x * x, axis=-1, keepdims=True) + NORM_EPS) * g


def _replace_step_rows(scr_ref, base, seq, steps):
    nb = base.shape[0] // seq
    for j in range(base.shape[1] // LANES_V7X):
        ls = slice(LANES_V7X * j, LANES_V7X * (j + 1))
        scr_ref[j] = base[:, ls]
        for t, vals in steps:
            scr_ref[j, pl.ds(t, nb, stride=seq), :] = vals[:, ls]
    return jnp.concatenate([scr_ref[j] for j in range(base.shape[1] // LANES_V7X)], axis=1)


def _pre_norm_kernel(prompt_tiles, xp_ref, xs_ref, g_ref, w_ref, h_ref, z_ref, wb_ref):
    i = pl.program_id(0)

    @pl.when(i == 0)
    def _():
        wb_ref[...] = w_ref[...].astype(BF16)

    @pl.when(i < prompt_tiles)
    def _():
        h_ref[...] = _rmsnorm(xp_ref[...], g_ref[...]).astype(BF16)

    @pl.when(i >= prompt_tiles)
    def _():
        h_ref[...] = _rmsnorm(xs_ref[...], g_ref[...]).astype(BF16)

    z_ref[...] = lax.dot_general(h_ref[...], wb_ref[...], (((1,), (1,)), ((), ())),
                                 preferred_element_type=F32)


def _pre_norm(xp, xs, g, w_t, width, tm):
    pt, st = xp.shape[0] // tm, xs.shape[0] // tm
    n = xp.shape[0] + xs.shape[0]
    return pl.pallas_call(
        functools.partial(_pre_norm_kernel, pt),
        grid=(pt + st,),
        in_specs=[pl.BlockSpec((tm, D_MODEL), lambda i: (jnp.minimum(i, pt - 1), 0)),
                  pl.BlockSpec((tm, D_MODEL), lambda i: (jnp.maximum(i - pt, 0), 0)),
                  _const_spec(g.shape),
                  pl.BlockSpec((width, D_MODEL), lambda i: (0, 0), pipeline_mode=pl.Buffered(1))],
        out_specs=[pl.BlockSpec((tm, D_MODEL), lambda i: (i, 0)), pl.BlockSpec((tm, width), lambda i: (i, 0))],
        out_shape=[jax.ShapeDtypeStruct((n, D_MODEL), BF16), jax.ShapeDtypeStruct((n, width), F32)],
        scratch_shapes=[pltpu.VMEM((width, D_MODEL), BF16)],
        compiler_params=_params("arbitrary"),
        name="pre_norm",
    )(xp, xs, g, w_t)


def _in_proj_kernel(h_ref, w_ref, o_ref, wb_ref):
    @pl.when(pl.program_id(1) == 0)
    def _():
        wb_ref[...] = w_ref[...].astype(BF16)

    o_ref[...] = lax.dot_general(h_ref[...], wb_ref[...], (((1,), (1,)), ((), ())),
                                 preferred_element_type=F32)


def _in_proj(h, w_t, off, width, tn, tm):
    n = h.shape[0]
    assert width % tn == 0 and n % tm == 0
    return pl.pallas_call(
        _in_proj_kernel,
        grid=(width // tn, n // tm),
        in_specs=[pl.BlockSpec((tm, D_MODEL), lambda j, i: (i, 0)),
                  pl.BlockSpec((pl.Element(tn), pl.Element(D_MODEL)),
                               lambda j, i: (SLAB * (off // SLAB + j * (tn // SLAB)), 0))],
        out_specs=pl.BlockSpec((tm, tn), lambda j, i: (i, j)),
        out_shape=jax.ShapeDtypeStruct((n, width), F32),
        scratch_shapes=[pltpu.VMEM((tn, D_MODEL), BF16)],
        compiler_params=_params("arbitrary", "arbitrary"),
        name="in_proj",
    )(h, w_t)


def _prep_math(zs, prev, mu, w_lora, w0, a0, k_k, k_a, r_k, ones_bd):
    f = zs + (prev - zs) * mu
    r = f[:, 0:D_RWKV]
    k = f[:, D_RWKV:2 * D_RWKV]
    v = f[:, 2 * D_RWKV:3 * D_RWKV]
    low = f[:, 3 * D_RWKV:3 * D_RWKV + D_LORA_PAD]
    lane = lax.broadcasted_iota(jnp.int32, low.shape, 1)
    act = jnp.where(lane < RANK_W, jnp.tanh(low),
                    jnp.where(lane < RANK_W + RANK_A, low,
                              jnp.where(lane < D_LORA, jax.nn.sigmoid(low), 0.0)))
    lora = _mm(act, w_lora)
    u = w0 + lora[:, 0:D_RWKV]
    lw = -math.exp(-0.5) * jax.nn.sigmoid(u)
    a = jax.nn.sigmoid(a0 + lora[:, D_RWKV:2 * D_RWKV])
    g = lora[:, 2 * D_RWKV:3 * D_RWKV]
    kk = k * k_k
    kk = kk * jnp.minimum(lax.rsqrt(_head_sum(kk * kk, ones_bd)), 1e12)
    k2 = k * (1.0 + (a - 1.0) * k_a)
    bonus = _head_sum(r * k2 * r_k, ones_bd) * v
    return r, lw, k2, v, -kk, kk * a, g, bonus


N_PREP_WEIGHTS = 8


def _prep_sample_kernel(seq, n_parts, *refs):
    z_refs, st_ref, refs = refs[:n_parts], refs[n_parts], refs[n_parts + 1:]
    w_refs, o_refs, prev_ref = refs[:N_PREP_WEIGHTS], refs[N_PREP_WEIGHTS:-1], refs[-1]
    zs = jnp.concatenate([z[...] for z in z_refs], axis=1)
    prev = _replace_step_rows(prev_ref, pltpu.roll(zs, 1, 0), seq, [(0, st_ref[...])])
    for o_ref, o in zip(o_refs, _prep_math(zs, prev, *[w[...] for w in w_refs])):
        o_ref[...] = o


def _const_spec(shape):
    nd = len(shape)
    return pl.BlockSpec(shape, lambda *_: (0,) * nd)


def _prep_sample(zparts, row0, n, st, seq, wts, tm):
    assert tm % (SLAB * seq) == 0 and n % tm == 0 and row0 % tm == 0 and len(wts) == N_PREP_WEIGHTS
    first = row0 // tm
    outs = [jax.ShapeDtypeStruct((n, D_RWKV), F32)] * 8
    row_spec = pl.BlockSpec((tm, D_RWKV), lambda i: (i, 0))
    return pl.pallas_call(
        functools.partial(_prep_sample_kernel, seq, len(zparts)),
        grid=(n // tm,),
        in_specs=[pl.BlockSpec((tm, z.shape[1]), lambda i: (first + i, 0)) for z in zparts]
        + [pl.BlockSpec((tm // seq, D_SHIFT_PAD), lambda i: (i, 0))]
        + [_const_spec(w.shape) for w in wts],
        out_specs=[row_spec] * 8,
        out_shape=outs,
        scratch_shapes=[pltpu.VMEM((D_SHIFT_PAD // LANES_V7X, tm, LANES_V7X), F32)],
        compiler_params=_params("parallel"),
        name="prep_sample",
    )(*zparts, st, *wts)


def _expand_bd(a, bd_mask):
    return jnp.where(bd_mask, jnp.concatenate([a] * HEADS_PER_GROUP, axis=0), 0.0)


def _chunk_cumsum(x):
    row = lax.broadcasted_iota(jnp.int32, x.shape, 0)
    s = 1
    while s < x.shape[0]:
        x = x + jnp.where(row >= s, pltpu.roll(x, s, 0), 0.0)
        s *= 2
    return x


def _wkv_intra(xs, bts, kts, vs, pzs, masks, order):
    bd_mask, strict_mask, incl_mask, eye_c, head_masks = masks
    c = vs[0].shape[0]
    ystacks = [jnp.concatenate([jnp.where(m, bt, 0.0) for m in head_masks]
                               + [jnp.where(m, kt, 0.0) for m in head_masks], axis=0)
               for bt, kt in zip(bts, kts)]
    grams = [_mm_nt(x, ys) for x, ys in zip(xs, ystacks)]
    l_abs = [jnp.where(strict_mask, g[0:c, 0:4 * c], 0.0) for g in grams]
    l_aks = [jnp.where(strict_mask, g[0:c, 4 * c:8 * c], 0.0) for g in grams]
    m_rbs = [jnp.where(incl_mask, g[c:2 * c, 0:4 * c], 0.0) for g in grams]
    m_rks = [jnp.where(incl_mask, g[c:2 * c, 4 * c:8 * c], 0.0) for g in grams]

    tinvs = [eye_c + l for l in l_abs]
    lps = [_mm(l, _expand_bd(l, bd_mask)) for l in l_abs]
    p = 2
    while True:
        rhss = [_expand_bd(lp, bd_mask) for lp in lps]
        if 2 * p >= order:
            tinvs = [t + _mm(t, rhs) for t, rhs in zip(tinvs, rhss)]
            break
        ress = [_mm(jnp.concatenate([lp, t], axis=0), rhs) for lp, t, rhs in zip(lps, tinvs, rhss)]
        lps = [res[0:c] for res in ress]
        tinvs = [t + res[c:2 * c] for t, res in zip(tinvs, ress)]
        p *= 2

    v_bds = [_expand_bd(v, bd_mask) for v in vs]
    ws = [pz[0:c] + _mm(l_ak, v_bd) for pz, l_ak, v_bd in zip(pzs, l_aks, v_bds)]
    us = [_mm(t, _expand_bd(w, bd_mask)) for t, w in zip(tinvs, ws)]
    ys = [pz[c:2 * c] + _mm(jnp.concatenate([m_rb, m_rk], axis=1),
                            jnp.concatenate([_expand_bd(u, bd_mask), v_bd], axis=0))
          for pz, m_rb, m_rk, u, v_bd in zip(pzs, m_rbs, m_rks, us, v_bds)]
    return ys, us


def _wkv_chunk(groups, masks):
    c = groups[0][0].shape[0]
    cls = [_chunk_cumsum(g[1]) for g in groups]
    xs = [jnp.concatenate([g[4] * jnp.exp(cl - g[1]), g[0] * jnp.exp(cl)], axis=0)
          for g, cl in zip(groups, cls)]
    pzs = [_mm_nt(x, g[6]) for x, g in zip(xs, groups)]
    e_negs = [jnp.exp(-cl) for cl in cls]
    ys, us = _wkv_intra(xs, [g[5] * e for g, e in zip(groups, e_negs)],
                        [g[2] * e for g, e in zip(groups, e_negs)], [g[3] for g in groups], pzs, masks, c)
    dss = []
    for g, cl, u in zip(groups, cls, us):
        e_rem = jnp.exp(cl[c - 1:c, :] - cl)
        uv_t = jnp.concatenate([u, g[3]], axis=0).T
        bk = jnp.concatenate([g[5] * e_rem, g[2] * e_rem], axis=0)
        dss.append(_mm(uv_t, bk))
    return [(y, g[6] * jnp.exp(cl[c - 1:c, :]) + jnp.where(masks[0], ds, 0.0))
            for y, g, cl, ds in zip(ys, groups, cls, dss)]


def _wkv_masks(c, seq):
    n = HEADS_PER_GROUP * c
    rr = lax.broadcasted_iota(jnp.int32, (n, n), 0)
    cc = lax.broadcasted_iota(jnp.int32, (n, n), 1)
    bd_mask = (rr // c) == (cc // c)
    t = lax.broadcasted_iota(jnp.int32, (c, n), 0)
    s = lax.broadcasted_iota(jnp.int32, (c, n), 1) % c
    same = (t // seq) == (s // seq)
    strict_mask = same & (t > s)
    incl_mask = same & (t >= s)
    eye_c = jnp.where(t == s, 1.0, 0.0).astype(F32)
    lane = lax.broadcasted_iota(jnp.int32, (c, MXU_DIM_V7X), 1)
    head_masks = [(lane // HEAD_DIM) == h for h in range(HEADS_PER_GROUP)]
    return bd_mask, strict_mask, incl_mask, eye_c, head_masks


def _wkv_prompt_kernel(nb, n_parts, n_casts, *refs):
    it = iter(refs)
    take = lambda k: [next(it) for _ in range(k)]
    z_refs = [take(n_parts) for _ in range(nb)]
    pw_refs, w_refs = take(N_PREP_WEIGHTS), take(n_casts)
    y_ref, bonus_ref, g_ref, sout_ref = take(4)
    wb_refs = take(n_casts)
    s_ref, carry_ref = take(2)
    ci = pl.program_id(0)

    @pl.when(ci == 0)
    def _():
        s_ref[...] = jnp.zeros_like(s_ref)
        carry_ref[...] = jnp.zeros_like(carry_ref)

    for w_ref, wb_ref in zip(w_refs, wb_refs):
        wb_ref[...] = w_ref[...].astype(BF16)

    zs_parts, prev_parts = [], []
    for bi in range(nb):
        zs = jnp.concatenate([z[...] for z in z_refs[bi]], axis=1)
        row = lax.broadcasted_iota(jnp.int32, zs.shape, 0)
        prev_parts.append(jnp.where(row == 0, carry_ref[bi, 0:1, :], pltpu.roll(zs, 1, 0)))
        carry_ref[bi, 0:1, :] = zs[CHUNK - 1:, :]
        zs_parts.append(zs)
    *scan_ops, g, bonus = _prep_math(
        jnp.concatenate(zs_parts, axis=0), jnp.concatenate(prev_parts, axis=0), *[w[...] for w in pw_refs])
    for bi in range(nb):
        rows = slice(CHUNK * bi, CHUNK * (bi + 1))
        g_ref[bi] = g[rows]
        bonus_ref[bi] = bonus[rows]

    masks = _wkv_masks(CHUNK, CHUNK)
    chains = [(bi, gi) for bi in range(nb) for gi in range(N_GROUPS)]
    lanes = lambda gi: slice(MXU_DIM_V7X * gi, MXU_DIM_V7X * (gi + 1))
    groups = [tuple(x[CHUNK * bi:CHUNK * (bi + 1), lanes(gi)] for x in scan_ops)
              + (s_ref[bi, gi],) for bi, gi in chains]
    for (bi, gi), (y, s_new) in zip(chains, _wkv_chunk(groups, masks)):
        y_ref[bi, :, lanes(gi)] = y
        s_ref[bi, gi] = s_new

    @pl.when(ci == pl.num_programs(0) - 1)
    def _():
        for bi, gi in chains:
            s_bd = s_ref[bi, gi]
            for h in range(HEADS_PER_GROUP):
                hs = slice(HEAD_DIM * h, HEAD_DIM * (h + 1))
                sout_ref[bi, HEADS_PER_GROUP * gi + h] = s_bd[hs, hs]


def _wkv_prompt(zparts, b, t, prep_wts, weights):
    steps = t // CHUNK
    assert len(prep_wts) == N_PREP_WEIGHTS
    seq_rows = lambda w, bi: pl.BlockSpec((CHUNK, w), lambda ci: (bi * steps + ci, 0))
    spec = pl.BlockSpec((b, CHUNK, D_RWKV), lambda ci: (0, ci, 0))
    state = (b, N_GROUPS, MXU_DIM_V7X, MXU_DIM_V7X)
    out_state = (b, N_HEADS, HEAD_DIM, HEAD_DIM)
    bf16_rows = 2 * SLAB
    assert all(w.shape[0] % (steps * bf16_rows) == 0 for w in weights)
    slabs = [pl.BlockSpec((w.shape[0] // steps, w.shape[1]), lambda ci: (ci, 0)) for w in weights]
    rows = jax.ShapeDtypeStruct((b, t, D_RWKV), F32)
    y, bonus, g, s, *wb = pl.pallas_call(
        functools.partial(_wkv_prompt_kernel, b, len(zparts), len(weights)),
        grid=(steps,),
        in_specs=[seq_rows(z.shape[1], bi) for bi in range(b) for z in zparts]
        + [_const_spec(w.shape) for w in prep_wts] + slabs,
        out_specs=[spec, spec, spec, _const_spec(out_state)] + slabs,
        out_shape=[rows, rows, rows, jax.ShapeDtypeStruct(out_state, F32)]
        + [jax.ShapeDtypeStruct(w.shape, BF16) for w in weights],
        scratch_shapes=[pltpu.VMEM(state, F32), pltpu.VMEM((b, SLAB, D_SHIFT_PAD), F32)],
        compiler_params=_params("arbitrary"),
        name="wkv_prompt",
    )(*(list(zparts) * b), *prep_wts, *weights)
    return y, bonus, g, s, wb


HEAD_PAIR = 2 * HEAD_DIM


def _wkv_steps_kernel(seq, r_ref, lw_ref, k2_ref, v_ref, as_ref, bs_ref, s0_ref, y_ref, sout_ref,
                      op_ref, vt_ref, yt_ref):
    nb = r_ref.shape[0] // seq
    for t in range(seq):
        rows = pl.ds(t, nb, stride=seq)
        op_ref[0, t] = r_ref[rows, :].T
        op_ref[1, t] = jnp.exp(lw_ref[rows, :]).T
        op_ref[2, t] = k2_ref[rows, :].T
        op_ref[3, t] = as_ref[rows, :].T
        op_ref[4, t] = bs_ref[rows, :].T
        vt_ref[t] = v_ref[rows, :].T

    for hh in range(HEAD_PAIR // HEAD_DIM):
        ks = slice(HEAD_DIM * hh, HEAD_DIM * (hh + 1))

        def body(vi, carry):
            s = s0_ref[hh, vi]
            for t in range(seq):
                sa = jnp.sum(s * op_ref[3, t, ks, :], axis=0, keepdims=True)
                vrow = vt_ref[t, pl.ds(HEAD_DIM * hh + vi, 1), :]
                s = s * op_ref[1, t, ks, :] + sa * op_ref[4, t, ks, :] + vrow * op_ref[2, t, ks, :]
                yt_ref[t, pl.ds(HEAD_DIM * hh + vi, 1), :] = jnp.sum(s * op_ref[0, t, ks, :], axis=0,
                                                                      keepdims=True)
            sout_ref[hh, vi] = s
            return carry

        lax.fori_loop(0, HEAD_DIM, body, 0, unroll=4)

    for t in range(seq):
        y_ref[pl.ds(t, nb, stride=seq), :] = yt_ref[t].T


def _wkv_steps(r, lw, k2, v, a_s, b_s, s0_native, seq):
    n = r.shape[0]
    nb = n // seq
    assert s0_native.shape == (N_HEADS, HEAD_DIM, HEAD_DIM, nb) and nb == LANES_V7X
    vec = pl.BlockSpec((n, HEAD_PAIR), lambda i: (0, i))
    st = pl.BlockSpec((HEAD_PAIR // HEAD_DIM, HEAD_DIM, HEAD_DIM, nb), lambda i: (i, 0, 0, 0))
    return pl.pallas_call(
        functools.partial(_wkv_steps_kernel, seq),
        grid=(N_HEADS * HEAD_DIM // HEAD_PAIR,),
        in_specs=[vec] * 6 + [st],
        out_specs=[vec, st],
        out_shape=[jax.ShapeDtypeStruct((n, D_RWKV), F32), jax.ShapeDtypeStruct(s0_native.shape, F32)],
        scratch_shapes=[pltpu.VMEM((5, seq, HEAD_PAIR, nb), F32), pltpu.VMEM((seq, HEAD_PAIR, nb), F32),
                        pltpu.VMEM((seq, HEAD_PAIR, nb), F32)],
        compiler_params=_params("parallel"),
        name="wkv_steps",
    )(r, lw, k2, v, a_s, b_s, s0_native)


POOL_CARRY = 16


def _pool_prompt_tile(zp, carry_ref, tile_in_seq):
    tm = zp.shape[0]

    @pl.when(tile_in_seq == 0)
    def _():
        carry_ref[...] = jnp.zeros_like(carry_ref)

    buf = jnp.concatenate([carry_ref[...], zp], axis=0)
    carry_ref[...] = zp[tm - POOL_CARRY:, :]
    pos = (tile_in_seq * tm + lax.broadcasted_iota(jnp.int32, (tm, POOL_GROUP), 0) + 1).astype(F32)
    ds = []
    for gi, win in enumerate(POOL_WINDOWS):
        sl = slice(POOL_GROUP * gi, POOL_GROUP * (gi + 1))
        acc = buf[:, sl]
        s = 1
        while s < win:
            acc = acc + pltpu.roll(acc, s, 0)
            s *= 2
        ds.append(acc[POOL_CARRY:, :] / jnp.minimum(float(win), pos) - zp[:, sl])
    return jnp.concatenate(ds, axis=1)


def _pool_sample_kernel(seq, hist_ref, zp_ref, d_ref, new_ref):
    nb = hist_ref.shape[1]
    step = lambda t: zp_ref[pl.ds(t, nb, stride=seq), :]
    past = lambda j: step(j) if j >= 0 else hist_ref[POOL_HIST + j]
    tiles_per_group = POOL_GROUP // LANES_V7X
    for gi, win in enumerate(POOL_WINDOWS):
        @pl.when(pl.program_id(0) // tiles_per_group == gi)
        def _():
            for t in range(seq):
                acc = past(t)
                for j in range(1, win):
                    acc = acc + past(t - j)
                d_ref[pl.ds(t, nb, stride=seq), :] = acc / float(min(win, PAST_LEN + 1)) - past(t)
    for j in range(POOL_HIST):
        src = j + seq
        new_ref[j] = hist_ref[src] if src < POOL_HIST else step(src - POOL_HIST)


def _pool_sample(hist, zp, row0, seq):
    nb = hist.shape[1]
    n = nb * seq
    assert row0 % n == 0
    hist_spec = pl.BlockSpec((POOL_HIST, nb, LANES_V7X), lambda j: (0, 0, j))
    return pl.pallas_call(
        functools.partial(_pool_sample_kernel, seq),
        grid=(D_POOL // LANES_V7X,),
        in_specs=[hist_spec, pl.BlockSpec((n, LANES_V7X), lambda j: (row0 // n, j))],
        out_specs=[pl.BlockSpec((n, LANES_V7X), lambda j: (0, j)), hist_spec],
        out_shape=[jax.ShapeDtypeStruct((n, D_POOL), F32), jax.ShapeDtypeStruct(hist.shape, F32)],
        compiler_params=_params("parallel"),
        name="pool_sample",
    )(hist, zp)


def _merge_kernel(seq_tiles, y_ref, bonus_ref, g_ref, d_ref, zga_ref, zgb_ref, x_ref,
                  lw_ref, lb_ref, ones_ref, wa_ref, pw_ref, ps_ref, wb_ref,
                  wo_ref, gpost_ref, gnext_ref, o_ref, h_ref, *carry):
    if seq_tiles is None:
        d = d_ref[...]
    else:
        d = _pool_prompt_tile(d_ref[...], carry[0], pl.program_id(0) % seq_tiles)
    ones_bd = ones_ref[...]
    y = y_ref[...]
    mu = _head_sum(y, ones_bd) * (1.0 / HEAD_DIM)
    yc = y - mu
    var = _head_sum(yc * yc, ones_bd) * (1.0 / HEAD_DIM)
    yn = yc * lax.rsqrt(var + GN_EPS) * lw_ref[...] + lb_ref[...]
    ya = (yn + bonus_ref[...]) * g_ref[...]
    yb = jnp.concatenate(
        [_mm(d[:, POOL_GROUP * gi:POOL_GROUP * (gi + 1)], pw_ref[gi]) for gi in range(len(POOL_WINDOWS))],
        axis=1) * ps_ref[...]
    m = (jax.nn.sigmoid(zga_ref[...]) * _mm(ya, wa_ref[...])
         + jax.nn.sigmoid(zgb_ref[...]) * _mm(yb, wb_ref[...]))
    x1 = x_ref[...] + _rmsnorm(_mm(m, wo_ref[...]), gpost_ref[...])
    o_ref[...] = x1
    h_ref[...] = _rmsnorm(x1, gnext_ref[...]).astype(BF16)


def _merge(y, bonus, g, d, zga, zgb, x2d, row0, wts, tm, seq_len=None):
    n = y.shape[0]
    assert row0 % tm == 0 and (seq_len is None or (seq_len % tm == 0 and tm >= POOL_CARRY))
    first = row0 // tm
    half = pl.BlockSpec((tm, D_RWKV), lambda i: (i, 0))
    full = pl.BlockSpec((tm, D_MODEL), lambda i: (i, 0))
    gate = pl.BlockSpec((tm, D_MODEL), lambda i: (first + i, 0))
    resident = [pl.BlockSpec(w.shape, functools.partial(lambda nd, i: (0,) * nd, w.ndim),
                             pipeline_mode=pl.Buffered(1)) for w in wts]
    return pl.pallas_call(
        functools.partial(_merge_kernel, None if seq_len is None else seq_len // tm),
        grid=(n // tm,),
        in_specs=[half] * 4 + [gate] * 2 + [full] + resident,
        out_specs=[full, full],
        out_shape=[jax.ShapeDtypeStruct((n, D_MODEL), F32), jax.ShapeDtypeStruct((n, D_MODEL), BF16)],
        scratch_shapes=[] if seq_len is None else [pltpu.VMEM((POOL_CARRY, D_POOL), F32)],
        compiler_params=_params("parallel" if seq_len is None else "arbitrary"),
        name="merge",
    )(y, bonus, g, d, zga, zgb, x2d, *wts)


def _gelu_tanh(x):
    return 0.5 * x * (1.0 + jnp.tanh(0.7978845608028654 * (x + 0.044715 * x * x * x)))


def _ffn_body(x_ref, h_ref, wg_ref, wu_ref, cw_ref, cb_ref, wo_ref, gp_ref,
              o_ref, acc_ref, prev1, prev2, emit_tail):
    c = pl.program_id(1)

    @pl.when(c == 0)
    def _():
        acc_ref[...] = jnp.zeros_like(acc_ref)

    h = h_ref[...]
    gate = jnp.dot(h, wg_ref[...], preferred_element_type=F32)
    up = jnp.dot(h, wu_ref[...], preferred_element_type=F32)
    cw = cw_ref[...]
    cv = cb_ref[...] + cw[0:1, :] * prev2(gate) + cw[1:2, :] * prev1(gate) + cw[2:3, :] * gate
    emit_tail(gate)
    acc_ref[...] += _mm(_gelu_tanh(cv) * up, wo_ref[...])

    @pl.when(c == pl.num_programs(1) - 1)
    def _():
        o_ref[...] = x_ref[...] + _rmsnorm(acc_ref[...], gp_ref[...])


def _ffn_prompt_kernel(tiles_per_seq, x_ref, h_ref, wg_ref, wu_ref, cw_ref, cb_ref, wo_ref, gp_ref,
                       o_ref, tail_ref, acc_ref, carry_ref):
    i = pl.program_id(0)
    c = pl.program_id(1)

    @pl.when(i % tiles_per_seq == 0)
    def _():
        carry_ref[c] = jnp.zeros(carry_ref.shape[1:], F32)

    hist = carry_ref[c]

    def shifted(gate, s):
        row = lax.broadcasted_iota(jnp.int32, gate.shape, 0)
        rolled = pltpu.roll(gate, s, 0)
        out = rolled
        for j in range(s):
            out = jnp.where(row == j, hist[8 - s + j:9 - s + j, :], out)
        return out

    def emit_tail(gate):
        tail_ref[...] = gate[gate.shape[0] - tail_ref.shape[0]:, :]

    _ffn_body(x_ref, h_ref, wg_ref, wu_ref, cw_ref, cb_ref, wo_ref, gp_ref, o_ref,
              acc_ref, lambda g: shifted(g, 1), lambda g: shifted(g, 2), emit_tail)
    carry_ref[c] = tail_ref[...]


def _ffn_sample_kernel(seq, x_ref, st_ref, h_ref, wg_ref, wu_ref, cw_ref, cb_ref, wo_ref, gp_ref,
                       o_ref, tail_ref, acc_ref, g_ref, p1_ref, p2_ref):
    nb = st_ref.shape[0]
    st0 = st_ref[:, 0, :]
    st1 = st_ref[:, 1, :]
    prev1 = lambda gate: _replace_step_rows(p1_ref, pltpu.roll(gate, 1, 0), seq, [(0, st1)])
    prev2 = lambda gate: _replace_step_rows(p2_ref, pltpu.roll(gate, 2, 0), seq, [(0, st0), (1, st1)])

    def emit_tail(gate):
        tiles = gate.shape[1] // LANES_V7X
        for j in range(tiles):
            g_ref[j] = gate[:, LANES_V7X * j:LANES_V7X * (j + 1)]
        for s in range(CONV_W - 1):
            rows = pl.ds(seq - (CONV_W - 1) + s, nb, stride=seq)
            tail_ref[:, s, :] = jnp.concatenate([g_ref[j, rows, :] for j in range(tiles)], axis=1)

    _ffn_body(x_ref, h_ref, wg_ref, wu_ref, cw_ref, cb_ref, wo_ref, gp_ref, o_ref,
              acc_ref, prev1, prev2, emit_tail)


def _ffn_prompt(x2d, h, seq_len, wts, tm, fk):
    n = x2d.shape[0]
    nc = D_FF // fk
    w_in, cw, cb, w_out, gp = wts
    full = pl.BlockSpec((tm, D_MODEL), lambda i, c: (i, 0))
    return pl.pallas_call(
        functools.partial(_ffn_prompt_kernel, seq_len // tm),
        grid=(n // tm, nc),
        in_specs=[full, full,
                  pl.BlockSpec((D_MODEL, fk), lambda i, c: (0, c)),
                  pl.BlockSpec((D_MODEL, fk), lambda i, c: (0, c + nc)),
                  pl.BlockSpec((CONV_W, fk), lambda i, c: (0, c)),
                  pl.BlockSpec((1, fk), lambda i, c: (0, c)),
                  pl.BlockSpec((fk, D_MODEL), lambda i, c: (c, 0)),
                  _const_spec(gp.shape)],
        out_specs=[full, pl.BlockSpec((None, 8, fk), lambda i, c: (i, 0, c))],
        out_shape=[jax.ShapeDtypeStruct((n, D_MODEL), F32),
                   jax.ShapeDtypeStruct((n // tm, 8, D_FF), F32)],
        scratch_shapes=[pltpu.VMEM((tm, D_MODEL), F32), pltpu.VMEM((nc, 8, fk), F32)],
        compiler_params=_params("arbitrary", "arbitrary"),
        name="ffn_prompt",
    )(x2d, h, w_in, w_in, cw, cb, w_out, gp)


def _ffn_sample(x2d, h, st, seq, wts, fk):
    n = x2d.shape[0]
    nb = n // seq
    assert seq >= CONV_W - 1 and st.shape == (nb, CONV_W - 1, D_FF)
    nc = D_FF // fk
    w_in, cw, cb, w_out, gp = wts
    full = pl.BlockSpec((n, D_MODEL), lambda i, c: (0, 0))
    cols = pl.BlockSpec((nb, CONV_W - 1, fk), lambda i, c: (0, 0, c))
    return pl.pallas_call(
        functools.partial(_ffn_sample_kernel, seq),
        grid=(1, nc),
        in_specs=[full, cols, full,
                  pl.BlockSpec((D_MODEL, fk), lambda i, c: (0, c)),
                  pl.BlockSpec((D_MODEL, fk), lambda i, c: (0, c + nc)),
                  pl.BlockSpec((CONV_W, fk), lambda i, c: (0, c)),
                  pl.BlockSpec((1, fk), lambda i, c: (0, c)),
                  pl.BlockSpec((fk, D_MODEL), lambda i, c: (c, 0)),
                  _const_spec(gp.shape)],
        out_specs=[full, cols],
        out_shape=[jax.ShapeDtypeStruct((n, D_MODEL), F32), jax.ShapeDtypeStruct(st.shape, F32)],
        scratch_shapes=[pltpu.VMEM((n, D_MODEL), F32)]
        + [pltpu.VMEM((fk // LANES_V7X, n, LANES_V7X), F32)] * 3,
        compiler_params=_params("arbitrary", "arbitrary"),
        name="ffn_sample",
    )(x2d, st, h, w_in, w_in, cw, cb, w_out, gp)


def _row(v):
    return v.reshape(1, -1).astype(F32)


def _ones_bd():
    i = jnp.arange(MXU_DIM_V7X) // HEAD_DIM
    return (i[:, None] == i[None, :]).astype(BF16)


def _layer_weights(l, norm_pre_mix, w_in, mu_shift, w0, w2, a0, a2, g2, k_k, k_a, r_k, lnx_w, lnx_b,
                   w_branch_a, pool_w, pool_scale, w_branch_b, w_out, norm_post_mix,
                   norm_pre_ffn, w_ffn_in, conv_w, conv_b, w_ffn_out, norm_post_ffn):
    w_t = jnp.swapaxes(w_in[l], 0, 1)
    mu = jnp.pad(mu_shift[l], (0, D_SHIFT_PAD - D_SHIFT)).reshape(1, -1)
    w_lora = jnp.zeros((D_LORA_PAD, 3 * D_RWKV), F32)
    w_lora = w_lora.at[0:RANK_W, 0:D_RWKV].set(w2[l])
    w_lora = w_lora.at[RANK_W:RANK_W + RANK_A, D_RWKV:2 * D_RWKV].set(a2[l])
    w_lora = w_lora.at[RANK_W + RANK_A:D_LORA, 2 * D_RWKV:].set(g2[l])
    ones_bd = _ones_bd()

    def with_bf16(wa, pw, wb, wo, wfi, wfo):
        return dict(
            merge=(_row(lnx_w[l]), _row(lnx_b[l]), ones_bd, wa,
                   pw.reshape(pool_w.shape[1:]), _row(pool_scale[l]), wb,
                   wo, _row(norm_post_mix[l]), _row(norm_pre_ffn[l])),
            ffn=(wfi, conv_w[l].astype(F32), _row(conv_b[l]), wfo, _row(norm_post_ffn[l])),
        )

    return dict(
        in_proj=(_row(norm_pre_mix[l]), w_t),
        prep=(mu, w_lora.astype(BF16), _row(w0[l]), _row(a0[l]), _row(k_k[l]), _row(k_a[l]), _row(r_k[l]),
              ones_bd),
        f32_matmul_weights=[w_branch_a[l], pool_w[l].reshape(-1, POOL_GROUP), w_branch_b[l], w_out[l],
                            w_ffn_in[l], w_ffn_out[l]],
        with_bf16=with_bf16,
    )


def _largest_tile(n, cap, mult=16):
    best = None
    for d in range(mult, min(n, cap) + 1, mult):
        if n % d == 0:
            best = d
    assert best is not None, (n, cap)
    return best


def _project(xp2d, xs2d, wts):
    g, w_t = wts["in_proj"]
    tm_norm = _largest_tile(math.gcd(xp2d.shape[0], xs2d.shape[0]), 512)
    h, zr = _pre_norm(xp2d, xs2d, g, w_t, D_RWKV, tm_norm)
    tm = _largest_tile(h.shape[0], 1088)
    zkv = _in_proj(h, w_t, D_RWKV, 2 * D_RWKV, D_RWKV, tm)
    zl = _in_proj(h, w_t, 3 * D_RWKV, D_LORA_PAD, D_LORA_PAD, tm)
    zp = _in_proj(h, w_t, D_SHIFT, D_POOL, D_POOL, tm)
    zga = _in_proj(h, w_t, D_SHIFT + D_POOL, D_MODEL, D_MODEL // 2, tm)
    zgb = _in_proj(h, w_t, D_SHIFT + D_POOL + D_MODEL, D_MODEL, D_MODEL // 2, tm)
    return (zr, zkv, zl), zp, zga, zgb


def _last_shift_row(zparts, row0, b, t):
    if b <= 8:
        last = lambda z: jnp.concatenate(
            [lax.slice(z, (row0 + (i + 1) * t - 1, 0), (row0 + (i + 1) * t, z.shape[1])) for i in range(b)])
    else:
        last = lambda z: lax.slice(z, (row0 + t - 1, 0), (row0 + b * t, z.shape[1]), (t, 1))
    return jnp.concatenate([last(z) for z in zparts], axis=-1)[:, None, :D_SHIFT]


def _prompt_layer(x, z, wts):
    b, t, _ = x.shape
    n = b * t
    x2d = x.reshape(n, D_MODEL)
    zparts, zp, zga, zgb = z
    y, bonus, g, wkv, bf16_weights = _wkv_prompt(zparts, b, t, wts["prep"], wts["f32_matmul_weights"])
    wts = wts["with_bf16"](*bf16_weights)
    flat = lambda a: a.reshape(n, a.shape[-1])
    x1, h1 = _merge(flat(y), flat(bonus), flat(g), zp, zga, zgb, x2d, 0, wts["merge"], tm=256, seq_len=t)
    ffn_tm = _largest_tile(t, 512)
    out, tail = _ffn_prompt(x1, h1, t, wts["ffn"], tm=ffn_tm, fk=512)
    shift = _last_shift_row(zparts, 0, b, t)
    pool =jnp.stack([zp[(i + 1) * t - POOL_HIST:(i + 1) * t] for i in range(b)])
    tiles = t // ffn_tm
    conv = tail.reshape(b, tiles, 8, D_FF)[:, tiles - 1, 8 - (CONV_W - 1):, :]
    return wts, (out.reshape(b, t, D_MODEL), shift, wkv, pool, conv)


def _sample_layer(x, z, row0, st_shift, st_wkv, st_pool, st_conv, prep_wts, wts):
    b, t, _ = x.shape
    n = b * t
    x2d = x.reshape(n, D_MODEL)
    zparts, zp, zga, zgb = z
    st = jnp.pad(st_shift.reshape(b, D_SHIFT), ((0, 0), (0, D_SHIFT_PAD - D_SHIFT)))
    tm = _largest_tile(math.gcd(row0, n), 128, mult=SLAB * t)
    r, lw, k2, v, a_s, b_s, g, bonus = _prep_sample(zparts, row0, n, st, t, prep_wts, tm=tm)
    y, wkv = _wkv_steps(r, lw, k2, v, a_s, b_s, jnp.transpose(st_wkv, (1, 2, 3, 0)), t)
    wkv = jnp.transpose(wkv, (3, 0, 1, 2))
    d, pool = _pool_sample(jnp.swapaxes(st_pool, 0, 1), zp, row0, t)
    pool = jnp.swapaxes(pool, 0, 1)
    x1, h1 = _merge(y, bonus, g, d, zga, zgb, x2d, row0, wts["merge"],
                    tm=_largest_tile(math.gcd(row0, n), 256, mult=16))
    out, conv = _ffn_sample(x1, h1, st_conv, t, wts["ffn"], fk=512)
    shift = _last_shift_row(zparts, row0, b, t)
    return out.reshape(b, t, D_MODEL), shift, wkv, pool, conv


def kernel(x_prompt, x_sample, state_shift, state_wkv, state_pool, state_conv, norm_pre_mix, w_in, mu_shift, w0, w2, a0, a2, g2, k_k, k_a, r_k, lnx_w, lnx_b, w_branch_a, pool_w, pool_scale, w_branch_b, w_out, norm_post_mix, norm_pre_ffn, w_ffn_in, conv_w, conv_b, w_ffn_out, norm_post_ffn):
    weights = (norm_pre_mix, w_in, mu_shift, w0, w2, a0, a2, g2, k_k, k_a, r_k, lnx_w, lnx_b,
               w_branch_a, pool_w, pool_scale, w_branch_b, w_out, norm_post_mix,
               norm_pre_ffn, w_ffn_in, conv_w, conv_b, w_ffn_out, norm_post_ffn)
    depth = w_in.shape[0]
    yp, ys = x_prompt, x_sample
    p_states, s_states = [], []
    for l in range(depth):
        wts = _layer_weights(l, *weights)
        n_prompt = yp.shape[0] * yp.shape[1]
        z = _project(yp.reshape(n_prompt, D_MODEL), ys.reshape(-1, D_MODEL), wts)
        bf16_wts, (yp, *ps) = _prompt_layer(yp, z, wts)
        ys, *ss = _sample_layer(ys, z, n_prompt, state_shift[l], state_wkv[l], state_pool[l], state_conv[l],
                                wts["prep"], bf16_wts)
        p_states.append(ps)
        s_states.append(ss)
    stack = lambda states, i: jnp.stack([s[i] for s in states])
    return (yp, ys,
            stack(p_states, 0), stack(p_states, 1), stack(p_states, 2), stack(p_states, 3),
            stack(s_states, 0), stack(s_states, 1), stack(s_states, 2), stack(s_states, 3))
```

```python
import functools
import math

import jax
import jax.numpy as jnp
from jax import lax
from jax.experimental import pallas as pl
from jax.experimental.pallas import tpu as pltpu

F32 = jnp.float32
BF16 = jnp.bfloat16

D_MODEL = 2048
HEAD_DIM = 64
D_RWKV = 1024
N_HEADS = 16
RANK_W, RANK_A, RANK_G = 64, 64, 160
D_LORA = RANK_W + RANK_A + RANK_G
D_LORA_PAD = 384
D_SHIFT = 3 * D_RWKV + D_LORA
D_SHIFT_PAD = 3 * D_RWKV + D_LORA_PAD
D_POOL = 1024
POOL_WINDOWS = (2, 4, 8, 16)
POOL_GROUP = 256
POOL_HIST = 15
D_FF = 5632
CONV_W = 3
NORM_EPS = 1e-6
GN_EPS = 64e-5
PAST_LEN = 16384

LANES_V7X = 128
SLAB = 8
MXU_DIM_V7X = 256
HEADS_PER_GROUP = MXU_DIM_V7X // HEAD_DIM
N_GROUPS = N_HEADS // HEADS_PER_GROUP
CHUNK = 64
VMEM_LIMIT_V7X = 56 * 1024 * 1024


def _params(*sem):
    return pltpu.CompilerParams(dimension_semantics=sem, vmem_limit_bytes=VMEM_LIMIT_V7X)


def _mm(a, b):
    return jnp.dot(a.astype(BF16), b.astype(BF16), preferred_element_type=F32)


def _mm_nt(a, b):
    return lax.dot_general(a.astype(BF16), b.astype(BF16), (((1,), (1,)), ((), ())),
                           preferred_element_type=F32)


def _split_hi_lo(x):
    hi = x.astype(BF16)
    lo = (x - hi.astype(F32)).astype(BF16)
    return hi, lo


def _head_sum(x, ones_bd):
    hi, lo = _split_hi_lo(x)
    outs = []
    for gi in range(x.shape[1] // MXU_DIM_V7X):
        sl = slice(MXU_DIM_V7X * gi, MXU_DIM_V7X * (gi + 1))
        outs.append(jnp.dot(hi[:, sl], ones_bd, preferred_element_type=F32)
                    + jnp.dot(lo[:, sl], ones_bd, preferred_element_type=F32))
    return jnp.concatenate(outs, axis=1)


def _rmsnorm(x, g):
    return x * lax.rsqrt(jnp.mean(x * x, axis=-1, keepdims=True) + NORM_EPS) * g


def _replace_step_rows(scr_ref, base, seq, steps):
    nb = base.shape[0] // seq
    for j in range(base.shape[1] // LANES_V7X):
        ls = slice(LANES_V7X * j, LANES_V7X * (j + 1))
        scr_ref[j] = base[:, ls]
        for t, vals in steps:
            scr_ref[j, pl.ds(t, nb, stride=seq), :] = vals[:, ls]
    return jnp.concatenate([scr_ref[j] for j in range(base.shape[1] // LANES_V7X)], axis=1)


def _pre_norm_kernel(prompt_tiles, n_sections, xp_ref, xs_ref, g_ref, *refs):
    w_refs, h_ref, z_refs, wb_refs = (refs[:n_sections], refs[n_sections],
                                      refs[n_sections + 1:2 * n_sections + 1], refs[2 * n_sections + 1:])
    i = pl.program_id(0)

    @pl.when(i == 0)
    def _():
        for w_ref, wb_ref in zip(w_refs, wb_refs):
            wb_ref[...] = w_ref[...].astype(BF16)

    @pl.when(i < prompt_tiles)
    def _():
        h_ref[...] = _rmsnorm(xp_ref[...], g_ref[...]).astype(BF16)

    @pl.when(i >= prompt_tiles)
    def _():
        h_ref[...] = _rmsnorm(xs_ref[...], g_ref[...]).astype(BF16)

    h = h_ref[...]
    for z_ref, wb_ref in zip(z_refs, wb_refs):
        z_ref[...] = lax.dot_general(h, wb_ref[...], (((1,), (1,)), ((), ())), preferred_element_type=F32)


def _pre_norm(xp, xs, g, w_t, sections, tm):
    pt, st = xp.shape[0] // tm, xs.shape[0] // tm
    n = xp.shape[0] + xs.shape[0]
    assert all(off % SLAB == 0 for off, _ in sections)
    rows = pl.BlockSpec((tm, D_MODEL), lambda i: (i, 0))
    h, *zs = pl.pallas_call(
        functools.partial(_pre_norm_kernel, pt, len(sections)),
        grid=(pt + st,),
        in_specs=[pl.BlockSpec((tm, D_MODEL), lambda i: (jnp.minimum(i, pt - 1), 0)),
                  pl.BlockSpec((tm, D_MODEL), lambda i: (jnp.maximum(i - pt, 0), 0)),
                  _const_spec(g.shape)]
        + [pl.BlockSpec((pl.Element(width), pl.Element(D_MODEL)),
                        functools.partial(lambda o, i: (SLAB * o, 0), off // SLAB),
                        pipeline_mode=pl.Buffered(1)) for off, width in sections],
        out_specs=[rows] + [pl.BlockSpec((tm, width), lambda i: (i, 0)) for _, width in sections],
        out_shape=[jax.ShapeDtypeStruct((n, D_MODEL), BF16)]
        + [jax.ShapeDtypeStruct((n, width), F32) for _, width in sections],
        scratch_shapes=[pltpu.VMEM((width, D_MODEL), BF16) for _, width in sections],
        compiler_params=_params("arbitrary"),
        name="pre_norm",
    )(xp, xs, g, *([w_t] * len(sections)))
    return h, zs


def _in_proj_kernel(h_ref, w_ref, o_ref, wb_ref):
    @pl.when(pl.program_id(1) == 0)
    def _():
        wb_ref[...] = w_ref[...].astype(BF16)

    o_ref[...] = lax.dot_general(h_ref[...], wb_ref[...], (((1,), (1,)), ((), ())),
                                 preferred_element_type=F32)


def _in_proj(h, w_t, off, width, tn, tm):
    n = h.shape[0]
    assert width % tn == 0 and n % tm == 0
    return pl.pallas_call(
        _in_proj_kernel,
        grid=(width // tn, n // tm),
        in_specs=[pl.BlockSpec((tm, D_MODEL), lambda j, i: (i, 0)),
                  pl.BlockSpec((pl.Element(tn), pl.Element(D_MODEL)),
                               lambda j, i: (SLAB * (off // SLAB + j * (tn // SLAB)), 0))],
        out_specs=pl.BlockSpec((tm, tn), lambda j, i: (i, j)),
        out_shape=jax.ShapeDtypeStruct((n, width), F32),
        scratch_shapes=[pltpu.VMEM((tn, D_MODEL), BF16)],
        compiler_params=_params("arbitrary", "arbitrary"),
        name="in_proj",
    )(h, w_t)


def _prep_math(zs, prev, mu, w_lora, w0, a0, k_k, k_a, r_k, ones_bd):
    f = zs + (prev - zs) * mu
    r = f[:, 0:D_RWKV]
    k = f[:, D_RWKV:2 * D_RWKV]
    v = f[:, 2 * D_RWKV:3 * D_RWKV]
    low = f[:, 3 * D_RWKV:3 * D_RWKV + D_LORA_PAD]
    lane = lax.broadcasted_iota(jnp.int32, low.shape, 1)
    act = jnp.where(lane < RANK_W, jnp.tanh(low),
                    jnp.where(lane < RANK_W + RANK_A, low,
                              jnp.where(lane < D_LORA, jax.nn.sigmoid(low), 0.0)))
    lora = _mm(act, w_lora)
    u = w0 + lora[:, 0:D_RWKV]
    lw = -math.exp(-0.5) * jax.nn.sigmoid(u)
    a = jax.nn.sigmoid(a0 + lora[:, D_RWKV:2 * D_RWKV])
    g = lora[:, 2 * D_RWKV:3 * D_RWKV]
    kk = k * k_k
    kk = kk * jnp.minimum(lax.rsqrt(_head_sum(kk * kk, ones_bd)), 1e12)
    k2 = k * (1.0 + (a - 1.0) * k_a)
    bonus = _head_sum(r * k2 * r_k, ones_bd) * v
    return r, lw, k2, v, -kk, kk * a, g, bonus


N_PREP_WEIGHTS = 8


def _prep_sample_kernel(seq, n_parts, *refs):
    z_refs, st_ref, refs = refs[:n_parts], refs[n_parts], refs[n_parts + 1:]
    w_refs, o_refs, prev_ref = refs[:N_PREP_WEIGHTS], refs[N_PREP_WEIGHTS:-1], refs[-1]
    zs = jnp.concatenate([z[...] for z in z_refs], axis=1)
    prev = _replace_step_rows(prev_ref, pltpu.roll(zs, 1, 0), seq, [(0, st_ref[...])])
    for o_ref, o in zip(o_refs, _prep_math(zs, prev, *[w[...] for w in w_refs])):
        o_ref[...] = o


def _const_spec(shape):
    nd = len(shape)
    return pl.BlockSpec(shape, lambda *_: (0,) * nd)


def _prep_sample(zparts, row0, n, st, seq, wts, tm):
    assert tm % (SLAB * seq) == 0 and n % tm == 0 and row0 % tm == 0 and len(wts) == N_PREP_WEIGHTS
    first = row0 // tm
    outs = [jax.ShapeDtypeStruct((n, D_RWKV), F32)] * 8
    row_spec = pl.BlockSpec((tm, D_RWKV), lambda i: (i, 0))
    return pl.pallas_call(
        functools.partial(_prep_sample_kernel, seq, len(zparts)),
        grid=(n // tm,),
        in_specs=[pl.BlockSpec((tm, z.shape[1]), lambda i: (first + i, 0)) for z in zparts]
        + [pl.BlockSpec((tm // seq, D_SHIFT_PAD), lambda i: (i, 0))]
        + [_const_spec(w.shape) for w in wts],
        out_specs=[row_spec] * 8,
        out_shape=outs,
        scratch_shapes=[pltpu.VMEM((D_SHIFT_PAD // LANES_V7X, tm, LANES_V7X), F32)],
        compiler_params=_params("parallel"),
        name="prep_sample",
    )(*zparts, st, *wts)


def _expand_bd(a, bd_mask):
    return jnp.where(bd_mask, jnp.concatenate([a] * HEADS_PER_GROUP, axis=0), 0.0)


def _chunk_cumsum(x):
    row = lax.broadcasted_iota(jnp.int32, x.shape, 0)
    s = 1
    while s < x.shape[0]:
        x = x + jnp.where(row >= s, pltpu.roll(x, s, 0), 0.0)
        s *= 2
    return x


def _wkv_intra(xs, bts, kts, vs, pzs, masks, order):
    bd_mask, strict_mask, incl_mask, eye_c, head_masks = masks
    c = vs[0].shape[0]
    ystacks = [jnp.concatenate([jnp.where(m, bt, 0.0) for m in head_masks]
                               + [jnp.where(m, kt, 0.0) for m in head_masks], axis=0)
               for bt, kt in zip(bts, kts)]
    grams = [_mm_nt(x, ys) for x, ys in zip(xs, ystacks)]
    l_abs = [jnp.where(strict_mask, g[0:c, 0:4 * c], 0.0) for g in grams]
    l_aks = [jnp.where(strict_mask, g[0:c, 4 * c:8 * c], 0.0) for g in grams]
    m_rbs = [jnp.where(incl_mask, g[c:2 * c, 0:4 * c], 0.0) for g in grams]
    m_rks = [jnp.where(incl_mask, g[c:2 * c, 4 * c:8 * c], 0.0) for g in grams]

    tinvs = [eye_c + l for l in l_abs]
    lps = [_mm(l, _expand_bd(l, bd_mask)) for l in l_abs]
    p = 2
    while True:
        rhss = [_expand_bd(lp, bd_mask) for lp in lps]
        if 2 * p >= order:
            tinvs = [t + _mm(t, rhs) for t, rhs in zip(tinvs, rhss)]
            break
        ress = [_mm(jnp.concatenate([lp, t], axis=0), rhs) for lp, t, rhs in zip(lps, tinvs, rhss)]
        lps = [res[0:c] for res in ress]
        tinvs = [t + res[c:2 * c] for t, res in zip(tinvs, ress)]
        p *= 2

    v_bds = [_expand_bd(v, bd_mask) for v in vs]
    ws = [pz[0:c] + _mm(l_ak, v_bd) for pz, l_ak, v_bd in zip(pzs, l_aks, v_bds)]
    us = [_mm(t, _expand_bd(w, bd_mask)) for t, w in zip(tinvs, ws)]
    ys = [pz[c:2 * c] + _mm(jnp.concatenate([m_rb, m_rk], axis=1),
                            jnp.concatenate([_expand_bd(u, bd_mask), v_bd], axis=0))
          for pz, m_rb, m_rk, u, v_bd in zip(pzs, m_rbs, m_rks, us, v_bds)]
    return ys, us


def _wkv_chunk(groups, masks):
    c = groups[0][0].shape[0]
    cls = [_chunk_cumsum(g[1]) for g in groups]
    xs = [jnp.concatenate([g[4] * jnp.exp(cl - g[1]), g[0] * jnp.exp(cl)], axis=0)
          for g, cl in zip(groups, cls)]
    pzs = [_mm_nt(x, g[6]) for x, g in zip(xs, groups)]
    e_negs = [jnp.exp(-cl) for cl in cls]
    ys, us = _wkv_intra(xs, [g[5] * e for g, e in zip(groups, e_negs)],
                        [g[2] * e for g, e in zip(groups, e_negs)], [g[3] for g in groups], pzs, masks, c)
    dss = []
    for g, cl, u in zip(groups, cls, us):
        e_rem = jnp.exp(cl[c - 1:c, :] - cl)
        uv_t = jnp.concatenate([u, g[3]], axis=0).T
        bk = jnp.concatenate([g[5] * e_rem, g[2] * e_rem], axis=0)
        dss.append(_mm(uv_t, bk))
    return [(y, g[6] * jnp.exp(cl[c - 1:c, :]) + jnp.where(masks[0], ds, 0.0))
            for y, g, cl, ds in zip(ys, groups, cls, dss)]


def _wkv_masks(c, seq):
    n = HEADS_PER_GROUP * c
    rr = lax.broadcasted_iota(jnp.int32, (n, n), 0)
    cc = lax.broadcasted_iota(jnp.int32, (n, n), 1)
    bd_mask = (rr // c) == (cc // c)
    t = lax.broadcasted_iota(jnp.int32, (c, n), 0)
    s = lax.broadcasted_iota(jnp.int32, (c, n), 1) % c
    same = (t // seq) == (s // seq)
    strict_mask = same & (t > s)
    incl_mask = same & (t >= s)
    eye_c = jnp.where(t == s, 1.0, 0.0).astype(F32)
    lane = lax.broadcasted_iota(jnp.int32, (c, MXU_DIM_V7X), 1)
    head_masks = [(lane // HEAD_DIM) == h for h in range(HEADS_PER_GROUP)]
    return bd_mask, strict_mask, incl_mask, eye_c, head_masks


def _wkv_prompt_kernel(nb, n_parts, n_casts, *refs):
    it = iter(refs)
    take = lambda k: [next(it) for _ in range(k)]
    z_refs = [take(n_parts) for _ in range(nb)]
    pw_refs, w_refs = take(N_PREP_WEIGHTS), take(n_casts)
    y_ref, bonus_ref, g_ref, sout_ref = take(4)
    wb_refs = take(n_casts)
    s_ref, carry_ref = take(2)
    ci = pl.program_id(0)

    @pl.when(ci == 0)
    def _():
        s_ref[...] = jnp.zeros_like(s_ref)
        carry_ref[...] = jnp.zeros_like(carry_ref)

    for w_ref, wb_ref in zip(w_refs, wb_refs):
        wb_ref[...] = w_ref[...].astype(BF16)

    zs_parts, prev_parts = [], []
    for bi in range(nb):
        zs = jnp.concatenate([z[...] for z in z_refs[bi]], axis=1)
        row = lax.broadcasted_iota(jnp.int32, zs.shape, 0)
        prev_parts.append(jnp.where(row == 0, carry_ref[bi, 0:1, :], pltpu.roll(zs, 1, 0)))
        carry_ref[bi, 0:1, :] = zs[CHUNK - 1:, :]
        zs_parts.append(zs)
    *scan_ops, g, bonus = _prep_math(
        jnp.concatenate(zs_parts, axis=0), jnp.concatenate(prev_parts, axis=0), *[w[...] for w in pw_refs])
    for bi in range(nb):
        rows = slice(CHUNK * bi, CHUNK * (bi + 1))
        g_ref[bi] = g[rows]
        bonus_ref[bi] = bonus[rows]

    masks = _wkv_masks(CHUNK, CHUNK)
    chains = [(bi, gi) for bi in range(nb) for gi in range(N_GROUPS)]
    lanes = lambda gi: slice(MXU_DIM_V7X * gi, MXU_DIM_V7X * (gi + 1))
    groups = [tuple(x[CHUNK * bi:CHUNK * (bi + 1), lanes(gi)] for x in scan_ops)
              + (s_ref[bi, gi],) for bi, gi in chains]
    for (bi, gi), (y, s_new) in zip(chains, _wkv_chunk(groups, masks)):
        y_ref[bi, :, lanes(gi)] = y
        s_ref[bi, gi] = s_new

    @pl.when(ci == pl.num_programs(0) - 1)
    def _():
        for bi, gi in chains:
            s_bd = s_ref[bi, gi]
            for h in range(HEADS_PER_GROUP):
                hs = slice(HEAD_DIM * h, HEAD_DIM * (h + 1))
                sout_ref[bi, HEADS_PER_GROUP * gi + h] = s_bd[hs, hs]


def _wkv_prompt(zparts, b, t, prep_wts, weights):
    steps = t // CHUNK
    assert len(prep_wts) == N_PREP_WEIGHTS
    seq_rows = lambda w, bi: pl.BlockSpec((CHUNK, w), lambda ci: (bi * steps + ci, 0))
    spec = pl.BlockSpec((b, CHUNK, D_RWKV), lambda ci: (0, ci, 0))
    state = (b, N_GROUPS, MXU_DIM_V7X, MXU_DIM_V7X)
    out_state = (b, N_HEADS, HEAD_DIM, HEAD_DIM)
    bf16_rows = 2 * SLAB
    assert all(w.shape[0] % (steps * bf16_rows) == 0 for w in weights)
    slabs = [pl.BlockSpec((w.shape[0] // steps, w.shape[1]), lambda ci: (ci, 0)) for w in weights]
    rows = jax.ShapeDtypeStruct((b, t, D_RWKV), F32)
    y, bonus, g, s, *wb = pl.pallas_call(
        functools.partial(_wkv_prompt_kernel, b, len(zparts), len(weights)),
        grid=(steps,),
        in_specs=[seq_rows(z.shape[1], bi) for bi in range(b) for z in zparts]
        + [_const_spec(w.shape) for w in prep_wts] + slabs,
        out_specs=[spec, spec, spec, _const_spec(out_state)] + slabs,
        out_shape=[rows, rows, rows, jax.ShapeDtypeStruct(out_state, F32)]
        + [jax.ShapeDtypeStruct(w.shape, BF16) for w in weights],
        scratch_shapes=[pltpu.VMEM(state, F32), pltpu.VMEM((b, SLAB, D_SHIFT_PAD), F32)],
        compiler_params=_params("arbitrary"),
        name="wkv_prompt",
    )(*(list(zparts) * b), *prep_wts, *weights)
    return y, bonus, g, s, wb


HEAD_PAIR = 2 * HEAD_DIM


def _wkv_steps_kernel(seq, r_ref, lw_ref, k2_ref, v_ref, as_ref, bs_ref, s0_ref, y_ref, sout_ref,
                      op_ref, vt_ref, yt_ref):
    nb = r_ref.shape[0] // seq
    for t in range(seq):
        rows = pl.ds(t, nb, stride=seq)
        op_ref[0, t] = r_ref[rows, :].T
        op_ref[1, t] = jnp.exp(lw_ref[rows, :]).T
        op_ref[2, t] = k2_ref[rows, :].T
        op_ref[3, t] = as_ref[rows, :].T
        op_ref[4, t] = bs_ref[rows, :].T
        vt_ref[t] = v_ref[rows, :].T

    for hh in range(HEAD_PAIR // HEAD_DIM):
        ks = slice(HEAD_DIM * hh, HEAD_DIM * (hh + 1))

        def body(vi, carry):
            s = s0_ref[hh, vi]
            for t in range(seq):
                sa = jnp.sum(s * op_ref[3, t, ks, :], axis=0, keepdims=True)
                vrow = vt_ref[t, pl.ds(HEAD_DIM * hh + vi, 1), :]
                s = s * op_ref[1, t, ks, :] + sa * op_ref[4, t, ks, :] + vrow * op_ref[2, t, ks, :]
                yt_ref[t, pl.ds(HEAD_DIM * hh + vi, 1), :] = jnp.sum(s * op_ref[0, t, ks, :], axis=0,
                                                                      keepdims=True)
            sout_ref[hh, vi] = s
            return carry

        lax.fori_loop(0, HEAD_DIM, body, 0, unroll=4)

    for t in range(seq):
        y_ref[pl.ds(t, nb, stride=seq), :] = yt_ref[t].T


def _wkv_steps(r, lw, k2, v, a_s, b_s, s0_native, seq):
    n = r.shape[0]
    nb = n // seq
    assert s0_native.shape == (N_HEADS, HEAD_DIM, HEAD_DIM, nb) and nb == LANES_V7X
    vec = pl.BlockSpec((n, HEAD_PAIR), lambda i: (0, i))
    st = pl.BlockSpec((HEAD_PAIR // HEAD_DIM, HEAD_DIM, HEAD_DIM, nb), lambda i: (i, 0, 0, 0))
    return pl.pallas_call(
        functools.partial(_wkv_steps_kernel, seq),
        grid=(N_HEADS * HEAD_DIM // HEAD_PAIR,),
        in_specs=[vec] * 6 + [st],
        out_specs=[vec, st],
        out_shape=[jax.ShapeDtypeStruct((n, D_RWKV), F32), jax.ShapeDtypeStruct(s0_native.shape, F32)],
        scratch_shapes=[pltpu.VMEM((5, seq, HEAD_PAIR, nb), F32), pltpu.VMEM((seq, HEAD_PAIR, nb), F32),
                        pltpu.VMEM((seq, HEAD_PAIR, nb), F32)],
        compiler_params=_params("parallel"),
        name="wkv_steps",
    )(r, lw, k2, v, a_s, b_s, s0_native)


POOL_CARRY = 16


def _pool_prompt_tile(zp, carry_ref, tile_in_seq):
    tm = zp.shape[0]

    @pl.when(tile_in_seq == 0)
    def _():
        carry_ref[...] = jnp.zeros_like(carry_ref)

    buf = jnp.concatenate([carry_ref[...], zp], axis=0)
    carry_ref[...] = zp[tm - POOL_CARRY:, :]
    pos = (tile_in_seq * tm + lax.broadcasted_iota(jnp.int32, (tm, POOL_GROUP), 0) + 1).astype(F32)
    ds = []
    for gi, win in enumerate(POOL_WINDOWS):
        sl = slice(POOL_GROUP * gi, POOL_GROUP * (gi + 1))
        acc = buf[:, sl]
        s = 1
        while s < win:
            acc = acc + pltpu.roll(acc, s, 0)
            s *= 2
        ds.append(acc[POOL_CARRY:, :] / jnp.minimum(float(win), pos) - zp[:, sl])
    return jnp.concatenate(ds, axis=1)


def _pool_sample_kernel(seq, hist_ref, zp_ref, d_ref, new_ref):
    nb = hist_ref.shape[1]
    step = lambda t: zp_ref[pl.ds(t, nb, stride=seq), :]
    past = lambda j: step(j) if j >= 0 else hist_ref[POOL_HIST + j]
    tiles_per_group = POOL_GROUP // LANES_V7X
    for gi, win in enumerate(POOL_WINDOWS):
        @pl.when(pl.program_id(0) // tiles_per_group == gi)
        def _():
            for t in range(seq):
                acc = past(t)
                for j in range(1, win):
                    acc = acc + past(t - j)
                d_ref[pl.ds(t, nb, stride=seq), :] = acc / float(min(win, PAST_LEN + 1)) - past(t)
    for j in range(POOL_HIST):
        src = j + seq
        new_ref[j] = hist_ref[src] if src < POOL_HIST else step(src - POOL_HIST)


def _pool_sample(hist, zp, row0, seq):
    nb = hist.shape[1]
    n = nb * seq
    assert row0 % n == 0
    hist_spec = pl.BlockSpec((POOL_HIST, nb, LANES_V7X), lambda j: (0, 0, j))
    return pl.pallas_call(
        functools.partial(_pool_sample_kernel, seq),
        grid=(D_POOL // LANES_V7X,),
        in_specs=[hist_spec, pl.BlockSpec((n, LANES_V7X), lambda j: (row0 // n, j))],
        out_specs=[pl.BlockSpec((n, LANES_V7X), lambda j: (0, j)), hist_spec],
        out_shape=[jax.ShapeDtypeStruct((n, D_POOL), F32), jax.ShapeDtypeStruct(hist.shape, F32)],
        compiler_params=_params("parallel"),
        name="pool_sample",
    )(hist, zp)


def _merge_kernel(seq_tiles, y_ref, bonus_ref, g_ref, d_ref, zga_ref, zgb_ref, x_ref,
                  lw_ref, lb_ref, ones_ref, wa_ref, pw_ref, ps_ref, wb_ref,
                  wo_ref, gpost_ref, gnext_ref, o_ref, h_ref, *carry):
    if seq_tiles is None:
        d = d_ref[...]
    else:
        d = _pool_prompt_tile(d_ref[...], carry[0], pl.program_id(0) % seq_tiles)
    ones_bd = ones_ref[...]
    y = y_ref[...]
    mu = _head_sum(y, ones_bd) * (1.0 / HEAD_DIM)
    yc = y - mu
    var = _head_sum(yc * yc, ones_bd) * (1.0 / HEAD_DIM)
    yn = yc * lax.rsqrt(var + GN_EPS) * lw_ref[...] + lb_ref[...]
    ya = (yn + bonus_ref[...]) * g_ref[...]
    yb = jnp.concatenate(
        [_mm(d[:, POOL_GROUP * gi:POOL_GROUP * (gi + 1)], pw_ref[gi]) for gi in range(len(POOL_WINDOWS))],
        axis=1) * ps_ref[...]
    m = (jax.nn.sigmoid(zga_ref[...]) * _mm(ya, wa_ref[...])
         + jax.nn.sigmoid(zgb_ref[...]) * _mm(yb, wb_ref[...]))
    x1 = x_ref[...] + _rmsnorm(_mm(m, wo_ref[...]), gpost_ref[...])
    o_ref[...] = x1
    h_ref[...] = _rmsnorm(x1, gnext_ref[...]).astype(BF16)


def _merge(y, bonus, g, d, zga, zgb, x2d, row0, wts, tm, seq_len=None):
    n = y.shape[0]
    assert row0 % tm == 0 and (seq_len is None or (seq_len % tm == 0 and tm >= POOL_CARRY))
    first = row0 // tm
    half = pl.BlockSpec((tm, D_RWKV), lambda i: (i, 0))
    full = pl.BlockSpec((tm, D_MODEL), lambda i: (i, 0))
    gate = pl.BlockSpec((tm, D_MODEL), lambda i: (first + i, 0))
    resident = [pl.BlockSpec(w.shape, functools.partial(lambda nd, i: (0,) * nd, w.ndim),
                             pipeline_mode=pl.Buffered(1)) for w in wts]
    return pl.pallas_call(
        functools.partial(_merge_kernel, None if seq_len is None else seq_len // tm),
        grid=(n // tm,),
        in_specs=[half] * 4 + [gate] * 2 + [full] + resident,
        out_specs=[full, full],
        out_shape=[jax.ShapeDtypeStruct((n, D_MODEL), F32), jax.ShapeDtypeStruct((n, D_MODEL), BF16)],
        scratch_shapes=[] if seq_len is None else [pltpu.VMEM((POOL_CARRY, D_POOL), F32)],
        compiler_params=_params("parallel" if seq_len is None else "arbitrary"),
        name="merge",
    )(y, bonus, g, d, zga, zgb, x2d, *wts)


def _gelu_tanh(x):
    return 0.5 * x * (1.0 + jnp.tanh(0.7978845608028654 * (x + 0.044715 * x * x * x)))


def _ffn_body(x_ref, h_ref, wg_ref, wu_ref, cw_ref, cb_ref, wo_ref, gp_ref,
              o_ref, acc_ref, prev1, prev2, emit_tail):
    c = pl.program_id(1)

    @pl.when(c == 0)
    def _():
        acc_ref[...] = jnp.zeros_like(acc_ref)

    h = h_ref[...]
    gate = jnp.dot(h, wg_ref[...], preferred_element_type=F32)
    up = jnp.dot(h, wu_ref[...], preferred_element_type=F32)
    cw = cw_ref[...]
    cv = cb_ref[...] + cw[0:1, :] * prev2(gate) + cw[1:2, :] * prev1(gate) + cw[2:3, :] * gate
    emit_tail(gate)
    acc_ref[...] += _mm(_gelu_tanh(cv) * up, wo_ref[...])

    @pl.when(c == pl.num_programs(1) - 1)
    def _():
        o_ref[...] = x_ref[...] + _rmsnorm(acc_ref[...], gp_ref[...])


def _ffn_prompt_kernel(tiles_per_seq, x_ref, h_ref, wg_ref, wu_ref, cw_ref, cb_ref, wo_ref, gp_ref,
                       o_ref, tail_ref, acc_ref, carry_ref):
    i = pl.program_id(0)
    c = pl.program_id(1)

    @pl.when(i % tiles_per_seq == 0)
    def _():
        carry_ref[c] = jnp.zeros(carry_ref.shape[1:], F32)

    hist = carry_ref[c]

    def shifted(gate, s):
        row = lax.broadcasted_iota(jnp.int32, gate.shape, 0)
        rolled = pltpu.roll(gate, s, 0)
        out = rolled
        for j in range(s):
            out = jnp.where(row == j, hist[8 - s + j:9 - s + j, :], out)
        return out

    def emit_tail(gate):
        tail_ref[...] = gate[gate.shape[0] - tail_ref.shape[0]:, :]

    _ffn_body(x_ref, h_ref, wg_ref, wu_ref, cw_ref, cb_ref, wo_ref, gp_ref, o_ref,
              acc_ref, lambda g: shifted(g, 1), lambda g: shifted(g, 2), emit_tail)
    carry_ref[c] = tail_ref[...]


def _ffn_sample_kernel(seq, x_ref, st_ref, h_ref, wg_ref, wu_ref, cw_ref, cb_ref, wo_ref, gp_ref,
                       o_ref, tail_ref, acc_ref, g_ref, p1_ref, p2_ref):
    nb = st_ref.shape[0]
    st0 = st_ref[:, 0, :]
    st1 = st_ref[:, 1, :]
    prev1 = lambda gate: _replace_step_rows(p1_ref, pltpu.roll(gate, 1, 0), seq, [(0, st1)])
    prev2 = lambda gate: _replace_step_rows(p2_ref, pltpu.roll(gate, 2, 0), seq, [(0, st0), (1, st1)])

    def emit_tail(gate):
        tiles = gate.shape[1] // LANES_V7X
        for j in range(tiles):
            g_ref[j] = gate[:, LANES_V7X * j:LANES_V7X * (j + 1)]
        for s in range(CONV_W - 1):
            rows = pl.ds(seq - (CONV_W - 1) + s, nb, stride=seq)
            tail_ref[:, s, :] = jnp.concatenate([g_ref[j, rows, :] for j in range(tiles)], axis=1)

    _ffn_body(x_ref, h_ref, wg_ref, wu_ref, cw_ref, cb_ref, wo_ref, gp_ref, o_ref,
              acc_ref, prev1, prev2, emit_tail)


def _ffn_prompt(x2d, h, seq_len, wts, tm, fk):
    n = x2d.shape[0]
    nc = D_FF // fk
    w_in, cw, cb, w_out, gp = wts
    full = pl.BlockSpec((tm, D_MODEL), lambda i, c: (i, 0))
    return pl.pallas_call(
        functools.partial(_ffn_prompt_kernel, seq_len // tm),
        grid=(n // tm, nc),
        in_specs=[full, full,
                  pl.BlockSpec((D_MODEL, fk), lambda i, c: (0, c)),
                  pl.BlockSpec((D_MODEL, fk), lambda i, c: (0, c + nc)),
                  pl.BlockSpec((CONV_W, fk), lambda i, c: (0, c)),
                  pl.BlockSpec((1, fk), lambda i, c: (0, c)),
                  pl.BlockSpec((fk, D_MODEL), lambda i, c: (c, 0)),
                  _const_spec(gp.shape)],
        out_specs=[full, pl.BlockSpec((None, 8, fk), lambda i, c: (i, 0, c))],
        out_shape=[jax.ShapeDtypeStruct((n, D_MODEL), F32),
                   jax.ShapeDtypeStruct((n // tm, 8, D_FF), F32)],
        scratch_shapes=[pltpu.VMEM((tm, D_MODEL), F32), pltpu.VMEM((nc, 8, fk), F32)],
        compiler_params=_params("arbitrary", "arbitrary"),
        name="ffn_prompt",
    )(x2d, h, w_in, w_in, cw, cb, w_out, gp)


def _ffn_sample(x2d, h, st, seq, wts, fk):
    n = x2d.shape[0]
    nb = n // seq
    assert seq >= CONV_W - 1 and st.shape == (nb, CONV_W - 1, D_FF)
    nc = D_FF // fk
    w_in, cw, cb, w_out, gp = wts
    full = pl.BlockSpec((n, D_MODEL), lambda i, c: (0, 0))
    cols = pl.BlockSpec((nb, CONV_W - 1, fk), lambda i, c: (0, 0, c))
    return pl.pallas_call(
        functools.partial(_ffn_sample_kernel, seq),
        grid=(1, nc),
        in_specs=[full, cols, full,
                  pl.BlockSpec((D_MODEL, fk), lambda i, c: (0, c)),
                  pl.BlockSpec((D_MODEL, fk), lambda i, c: (0, c + nc)),
                  pl.BlockSpec((CONV_W, fk), lambda i, c: (0, c)),
                  pl.BlockSpec((1, fk), lambda i, c: (0, c)),
                  pl.BlockSpec((fk, D_MODEL), lambda i, c: (c, 0)),
                  _const_spec(gp.shape)],
        out_specs=[full, cols],
        out_shape=[jax.ShapeDtypeStruct((n, D_MODEL), F32), jax.ShapeDtypeStruct(st.shape, F32)],
        scratch_shapes=[pltpu.VMEM((n, D_MODEL), F32)]
        + [pltpu.VMEM((fk // LANES_V7X, n, LANES_V7X), F32)] * 3,
        compiler_params=_params("arbitrary", "arbitrary"),
        name="ffn_sample",
    )(x2d, st, h, w_in, w_in, cw, cb, w_out, gp)


def _row(v):
    return v.reshape(1, -1).astype(F32)


def _ones_bd():
    i = jnp.arange(MXU_DIM_V7X) // HEAD_DIM
    return (i[:, None] == i[None, :]).astype(BF16)


def _layer_weights(l, norm_pre_mix, w_in, mu_shift, w0, w2, a0, a2, g2, k_k, k_a, r_k, lnx_w, lnx_b,
                   w_branch_a, pool_w, pool_scale, w_branch_b, w_out, norm_post_mix,
                   norm_pre_ffn, w_ffn_in, conv_w, conv_b, w_ffn_out, norm_post_ffn):
    w_t = jnp.swapaxes(w_in[l], 0, 1)
    mu = jnp.pad(mu_shift[l], (0, D_SHIFT_PAD - D_SHIFT)).reshape(1, -1)
    w_lora = jnp.zeros((D_LORA_PAD, 3 * D_RWKV), F32)
    w_lora = w_lora.at[0:RANK_W, 0:D_RWKV].set(w2[l])
    w_lora = w_lora.at[RANK_W:RANK_W + RANK_A, D_RWKV:2 * D_RWKV].set(a2[l])
    w_lora = w_lora.at[RANK_W + RANK_A:D_LORA, 2 * D_RWKV:].set(g2[l])
    ones_bd = _ones_bd()

    def with_bf16(wa, pw, wb, wo, wfi, wfo):
        return dict(
            merge=(_row(lnx_w[l]), _row(lnx_b[l]), ones_bd, wa,
                   pw.reshape(pool_w.shape[1:]), _row(pool_scale[l]), wb,
                   wo, _row(norm_post_mix[l]), _row(norm_pre_ffn[l])),
            ffn=(wfi, conv_w[l].astype(F32), _row(conv_b[l]), wfo, _row(norm_post_ffn[l])),
        )

    return dict(
        in_proj=(_row(norm_pre_mix[l]), w_t),
        prep=(mu, w_lora.astype(BF16), _row(w0[l]), _row(a0[l]), _row(k_k[l]), _row(k_a[l]), _row(r_k[l]),
              ones_bd),
        f32_matmul_weights=[w_branch_a[l], pool_w[l].reshape(-1, POOL_GROUP), w_branch_b[l], w_out[l],
                            w_ffn_in[l], w_ffn_out[l]],
        with_bf16=with_bf16,
    )


def _largest_tile(n, cap, mult=16):
    best = None
    for d in range(mult, min(n, cap) + 1, mult):
        if n % d == 0:
            best = d
    assert best is not None, (n, cap)
    return best


def _project(xp2d, xs2d, wts):
    g, w_t = wts["in_proj"]
    tm_norm = _largest_tile(math.gcd(xp2d.shape[0], xs2d.shape[0]), 512)
    h, (zr, zl) = _pre_norm(xp2d, xs2d, g, w_t, [(0, D_RWKV), (3 * D_RWKV, D_LORA_PAD)], tm_norm)
    tm = _largest_tile(h.shape[0], 1088)
    zkv = _in_proj(h, w_t, D_RWKV, 2 * D_RWKV, D_RWKV, tm)
    zp =_in_proj(h, w_t, D_SHIFT, D_POOL, D_POOL, tm)
    zga = _in_proj(h, w_t, D_SHIFT + D_POOL, D_MODEL, D_MODEL // 2, tm)
    zgb = _in_proj(h, w_t, D_SHIFT + D_POOL + D_MODEL, D_MODEL, D_MODEL // 2, tm)
    return (zr, zkv, zl), zp, zga, zgb


def _last_shift_row(zparts, row0, b, t):
    if b <= 8:
        last = lambda z: jnp.concatenate(
            [lax.slice(z, (row0 + (i + 1) * t - 1, 0), (row0 + (i + 1) * t, z.shape[1])) for i in range(b)])
    else:
        last = lambda z: lax.slice(z, (row0 + t - 1, 0), (row0 + b * t, z.shape[1]), (t, 1))
    return jnp.concatenate([last(z) for z in zparts], axis=-1)[:, None, :D_SHIFT]


def _prompt_layer(x, z, wts):
    b, t, _ = x.shape
    n = b * t
    x2d = x.reshape(n, D_MODEL)
    zparts, zp, zga, zgb = z
    y, bonus, g, wkv, bf16_weights = _wkv_prompt(zparts, b, t, wts["prep"], wts["f32_matmul_weights"])
    wts = wts["with_bf16"](*bf16_weights)
    flat = lambda a: a.reshape(n, a.shape[-1])
    x1, h1 = _merge(flat(y), flat(bonus), flat(g), zp, zga, zgb, x2d, 0, wts["merge"], tm=256, seq_len=t)
    ffn_tm = _largest_tile(t, 512)
    out, tail = _ffn_prompt(x1, h1, t, wts["ffn"], tm=ffn_tm, fk=512)
    shift = _last_shift_row(zparts, 0, b, t)
    pool =jnp.stack([zp[(i + 1) * t - POOL_HIST:(i + 1) * t] for i in range(b)])
    tiles = t // ffn_tm
    conv = tail.reshape(b, tiles, 8, D_FF)[:, tiles - 1, 8 - (CONV_W - 1):, :]
    return wts, (out.reshape(b, t, D_MODEL), shift, wkv, pool, conv)


def _sample_layer(x, z, row0, st_shift, st_wkv, st_pool, st_conv, prep_wts, wts):
    b, t, _ = x.shape
    n = b * t
    x2d = x.reshape(n, D_MODEL)
    zparts, zp, zga, zgb = z
    st = jnp.pad(st_shift.reshape(b, D_SHIFT), ((0, 0), (0, D_SHIFT_PAD - D_SHIFT)))
    tm = _largest_tile(math.gcd(row0, n), 128, mult=SLAB * t)
    r, lw, k2, v, a_s, b_s, g, bonus = _prep_sample(zparts, row0, n, st, t, prep_wts, tm=tm)
    y, wkv = _wkv_steps(r, lw, k2, v, a_s, b_s, jnp.transpose(st_wkv, (1, 2, 3, 0)), t)
    wkv = jnp.transpose(wkv, (3, 0, 1, 2))
    d, pool = _pool_sample(jnp.swapaxes(st_pool, 0, 1), zp, row0, t)
    pool = jnp.swapaxes(pool, 0, 1)
    x1, h1 = _merge(y, bonus, g, d, zga, zgb, x2d, row0, wts["merge"],
                    tm=_largest_tile(math.gcd(row0, n), 256, mult=16))
    out, conv = _ffn_sample(x1, h1, st_conv, t, wts["ffn"], fk=512)
    shift = _last_shift_row(zparts, row0, b, t)
    return out.reshape(b, t, D_MODEL), shift, wkv, pool, conv


def kernel(x_prompt, x_sample, state_shift, state_wkv, state_pool, state_conv, norm_pre_mix, w_in, mu_shift, w0, w2, a0, a2, g2, k_k, k_a, r_k, lnx_w, lnx_b, w_branch_a, pool_w, pool_scale, w_branch_b, w_out, norm_post_mix, norm_pre_ffn, w_ffn_in, conv_w, conv_b, w_ffn_out, norm_post_ffn):
    weights = (norm_pre_mix, w_in, mu_shift, w0, w2, a0, a2, g2, k_k, k_a, r_k, lnx_w, lnx_b,
               w_branch_a, pool_w, pool_scale, w_branch_b, w_out, norm_post_mix,
               norm_pre_ffn, w_ffn_in, conv_w, conv_b, w_ffn_out, norm_post_ffn)
    depth = w_in.shape[0]
    yp, ys = x_prompt, x_sample
    p_states, s_states = [], []
    for l in range(depth):
        wts = _layer_weights(l, *weights)
        n_prompt = yp.shape[0] * yp.shape[1]
        z = _project(yp.reshape(n_prompt, D_MODEL), ys.reshape(-1, D_MODEL), wts)
        bf16_wts, (yp, *ps) = _prompt_layer(yp, z, wts)
        ys, *ss = _sample_layer(ys, z, n_prompt, state_shift[l], state_wkv[l], state_pool[l], state_conv[l],
                                wts["prep"], bf16_wts)
        p_states.append(ps)
        s_states.append(ss)
    stack = lambda states, i: jnp.stack([s[i] for s in states])
    return (yp, ys,
            stack(p_states, 0), stack(p_states, 1), stack(p_states, 2), stack(p_states, 3),
            stack(s_states, 0), stack(s_states, 1), stack(s_states, 2), stack(s_states, 3))
```

```python
import functools
import math

import jax
import jax.numpy as jnp
from jax import lax
from jax.experimental import pallas as pl
from jax.experimental.pallas import tpu as pltpu

F32 = jnp.float32
BF16 = jnp.bfloat16

D_MODEL = 2048
HEAD_DIM = 64
D_RWKV = 1024
N_HEADS = 16
RANK_W, RANK_A, RANK_G = 64, 64, 160
D_LORA = RANK_W + RANK_A + RANK_G
D_LORA_PAD = 384
D_SHIFT = 3 * D_RWKV + D_LORA
D_SHIFT_PAD = 3 * D_RWKV + D_LORA_PAD
D_POOL = 1024
POOL_WINDOWS = (2, 4, 8, 16)
POOL_GROUP = 256
POOL_HIST = 15
D_FF = 5632
CONV_W = 3
NORM_EPS = 1e-6
GN_EPS = 64e-5
PAST_LEN = 16384

LANES_V7X = 128
SLAB = 8
MXU_DIM_V7X = 256
HEADS_PER_GROUP = MXU_DIM_V7X // HEAD_DIM
N_GROUPS = N_HEADS // HEADS_PER_GROUP
CHUNK = 64
VMEM_LIMIT_V7X = 56 * 1024 * 1024


def _params(*sem):
    return pltpu.CompilerParams(dimension_semantics=sem, vmem_limit_bytes=VMEM_LIMIT_V7X)


def _mm(a, b):
    return jnp.dot(a.astype(BF16), b.astype(BF16), preferred_element_type=F32)


def _mm_nt(a, b):
    return lax.dot_general(a.astype(BF16), b.astype(BF16), (((1,), (1,)), ((), ())),
                           preferred_element_type=F32)


def _split_hi_lo(x):
    hi = x.astype(BF16)
    lo = (x - hi.astype(F32)).astype(BF16)
    return hi, lo


def _head_sum(x, ones_bd):
    hi, lo = _split_hi_lo(x)
    outs = []
    for gi in range(x.shape[1] // MXU_DIM_V7X):
        sl = slice(MXU_DIM_V7X * gi, MXU_DIM_V7X * (gi + 1))
        outs.append(jnp.dot(hi[:, sl], ones_bd, preferred_element_type=F32)
                    + jnp.dot(lo[:, sl], ones_bd, preferred_element_type=F32))
    return jnp.concatenate(outs, axis=1)


def _rmsnorm(x, g):
    return x * lax.rsqrt(jnp.mean(x * x, axis=-1, keepdims=True) + NORM_EPS) * g


def _replace_step_rows(scr_ref, base, seq, steps):
    nb = base.shape[0] // seq
    for j in range(base.shape[1] // LANES_V7X):
        ls = slice(LANES_V7X * j, LANES_V7X * (j + 1))
        scr_ref[j] = base[:, ls]
        for t, vals in steps:
            scr_ref[j, pl.ds(t, nb, stride=seq), :] = vals[:, ls]
    return jnp.concatenate([scr_ref[j] for j in range(base.shape[1] // LANES_V7X)], axis=1)


def _pre_norm_kernel(prompt_tiles, n_sections, xp_ref, xs_ref, g_ref, *refs):
    w_refs, h_ref, z_refs, wb_refs = (refs[:n_sections], refs[n_sections],
                                      refs[n_sections + 1:2 * n_sections + 1], refs[2 * n_sections + 1:])
    i = pl.program_id(0)

    @pl.when(i == 0)
    def _():
        for w_ref, wb_ref in zip(w_refs, wb_refs):
            wb_ref[...] = w_ref[...].astype(BF16)

    @pl.when(i < prompt_tiles)
    def _():
        h_ref[...] = _rmsnorm(xp_ref[...], g_ref[...]).astype(BF16)

    @pl.when(i >= prompt_tiles)
    def _():
        h_ref[...] = _rmsnorm(xs_ref[...], g_ref[...]).astype(BF16)

    h = h_ref[...]
    for z_ref, wb_ref in zip(z_refs, wb_refs):
        z_ref[...] = lax.dot_general(h, wb_ref[...], (((1,), (1,)), ((), ())), preferred_element_type=F32)


def _pre_norm(xp, xs, g, w_t, sections, tm):
    pt, st = xp.shape[0] // tm, xs.shape[0] // tm
    n = xp.shape[0] + xs.shape[0]
    assert all(off % SLAB == 0 for off, _ in sections)
    rows = pl.BlockSpec((tm, D_MODEL), lambda i: (i, 0))
    h, *zs = pl.pallas_call(
        functools.partial(_pre_norm_kernel, pt, len(sections)),
        grid=(pt + st,),
        in_specs=[pl.BlockSpec((tm, D_MODEL), lambda i: (jnp.minimum(i, pt - 1), 0)),
                  pl.BlockSpec((tm, D_MODEL), lambda i: (jnp.maximum(i - pt, 0), 0)),
                  _const_spec(g.shape)]
        + [pl.BlockSpec((pl.Element(width), pl.Element(D_MODEL)),
                        functools.partial(lambda o, i: (SLAB * o, 0), off // SLAB),
                        pipeline_mode=pl.Buffered(1)) for off, width in sections],
        out_specs=[rows] + [pl.BlockSpec((tm, width), lambda i: (i, 0)) for _, width in sections],
        out_shape=[jax.ShapeDtypeStruct((n, D_MODEL), BF16)]
        + [jax.ShapeDtypeStruct((n, width), F32) for _, width in sections],
        scratch_shapes=[pltpu.VMEM((width, D_MODEL), BF16) for _, width in sections],
        compiler_params=_params("arbitrary"),
        name="pre_norm",
    )(xp, xs, g, *([w_t] * len(sections)))
    return h, zs


def _in_proj_kernel(h_ref, w_ref, o_ref, wb_ref):
    @pl.when(pl.program_id(1) == 0)
    def _():
        wb_ref[...] = w_ref[...].astype(BF16)

    o_ref[...] = lax.dot_general(h_ref[...], wb_ref[...], (((1,), (1,)), ((), ())),
                                 preferred_element_type=F32)


def _in_proj(h, w_t, off, width, tn, tm):
    n = h.shape[0]
    assert width % tn == 0 and n % tm == 0
    return pl.pallas_call(
        _in_proj_kernel,
        grid=(width // tn, n // tm),
        in_specs=[pl.BlockSpec((tm, D_MODEL), lambda j, i: (i, 0)),
                  pl.BlockSpec((pl.Element(tn), pl.Element(D_MODEL)),
                               lambda j, i: (SLAB * (off // SLAB + j * (tn // SLAB)), 0))],
        out_specs=pl.BlockSpec((tm, tn), lambda j, i: (i, j)),
        out_shape=jax.ShapeDtypeStruct((n, width), F32),
        scratch_shapes=[pltpu.VMEM((tn, D_MODEL), BF16)],
        compiler_params=_params("arbitrary", "arbitrary"),
        name="in_proj",
    )(h, w_t)


def _prep_math(zs, prev, mu, w_lora, w0, a0, k_k, k_a, r_k, ones_bd):
    f = zs + (prev - zs) * mu
    r = f[:, 0:D_RWKV]
    k = f[:, D_RWKV:2 * D_RWKV]
    v = f[:, 2 * D_RWKV:3 * D_RWKV]
    low = f[:, 3 * D_RWKV:3 * D_RWKV + D_LORA_PAD]
    lane = lax.broadcasted_iota(jnp.int32, low.shape, 1)
    act = jnp.where(lane < RANK_W, jnp.tanh(low),
                    jnp.where(lane < RANK_W + RANK_A, low,
                              jnp.where(lane < D_LORA, jax.nn.sigmoid(low), 0.0)))
    lora = _mm(act, w_lora)
    u = w0 + lora[:, 0:D_RWKV]
    lw = -math.exp(-0.5) * jax.nn.sigmoid(u)
    a = jax.nn.sigmoid(a0 + lora[:, D_RWKV:2 * D_RWKV])
    g = lora[:, 2 * D_RWKV:3 * D_RWKV]
    kk = k * k_k
    kk = kk * jnp.minimum(lax.rsqrt(_head_sum(kk * kk, ones_bd)), 1e12)
    k2 = k * (1.0 + (a - 1.0) * k_a)
    bonus = _head_sum(r * k2 * r_k, ones_bd) * v
    return r, lw, k2, v, -kk, kk * a, g, bonus


N_PREP_WEIGHTS = 8


def _prep_sample_kernel(seq, n_parts, *refs):
    z_refs, st_ref, refs = refs[:n_parts], refs[n_parts], refs[n_parts + 1:]
    w_refs, o_refs, prev_ref = refs[:N_PREP_WEIGHTS], refs[N_PREP_WEIGHTS:-1], refs[-1]
    zs = jnp.concatenate([z[...] for z in z_refs], axis=1)
    prev = _replace_step_rows(prev_ref, pltpu.roll(zs, 1, 0), seq, [(0, st_ref[...])])
    for o_ref, o in zip(o_refs, _prep_math(zs, prev, *[w[...] for w in w_refs])):
        o_ref[...] = o


def _const_spec(shape):
    nd = len(shape)
    return pl.BlockSpec(shape, lambda *_: (0,) * nd)


def _prep_sample(zparts, row0, n, st, seq, wts, tm):
    assert tm % (SLAB * seq) == 0 and n % tm == 0 and row0 % tm == 0 and len(wts) == N_PREP_WEIGHTS
    first = row0 // tm
    outs = [jax.ShapeDtypeStruct((n, D_RWKV), F32)] * 8
    row_spec = pl.BlockSpec((tm, D_RWKV), lambda i: (i, 0))
    return pl.pallas_call(
        functools.partial(_prep_sample_kernel, seq, len(zparts)),
        grid=(n // tm,),
        in_specs=[pl.BlockSpec((tm, z.shape[1]), lambda i: (first + i, 0)) for z in zparts]
        + [pl.BlockSpec((tm // seq, D_SHIFT_PAD), lambda i: (i, 0))]
        + [_const_spec(w.shape) for w in wts],
        out_specs=[row_spec] * 8,
        out_shape=outs,
        scratch_shapes=[pltpu.VMEM((D_SHIFT_PAD // LANES_V7X, tm, LANES_V7X), F32)],
        compiler_params=_params("parallel"),
        name="prep_sample",
    )(*zparts, st, *wts)


def _expand_bd(a, bd_mask):
    return jnp.where(bd_mask, jnp.concatenate([a] * HEADS_PER_GROUP, axis=0), 0.0)


def _chunk_cumsum(x):
    row = lax.broadcasted_iota(jnp.int32, x.shape, 0)
    s = 1
    while s < x.shape[0]:
        x = x + jnp.where(row >= s, pltpu.roll(x, s, 0), 0.0)
        s *= 2
    return x


def _wkv_intra(xs, bts, kts, vs, pzs, masks, order):
    bd_mask, strict_mask, incl_mask, eye_c, head_masks = masks
    c = vs[0].shape[0]
    ystacks = [jnp.concatenate([jnp.where(m, bt, 0.0) for m in head_masks]
                               + [jnp.where(m, kt, 0.0) for m in head_masks], axis=0)
               for bt, kt in zip(bts, kts)]
    grams = [_mm_nt(x, ys) for x, ys in zip(xs, ystacks)]
    l_abs = [jnp.where(strict_mask, g[0:c, 0:4 * c], 0.0) for g in grams]
    l_aks = [jnp.where(strict_mask, g[0:c, 4 * c:8 * c], 0.0) for g in grams]
    m_rbs = [jnp.where(incl_mask, g[c:2 * c, 0:4 * c], 0.0) for g in grams]
    m_rks = [jnp.where(incl_mask, g[c:2 * c, 4 * c:8 * c], 0.0) for g in grams]

    tinvs = [eye_c + l for l in l_abs]
    lps = [_mm(l, _expand_bd(l, bd_mask)) for l in l_abs]
    p = 2
    while True:
        rhss = [_expand_bd(lp, bd_mask) for lp in lps]
        if 2 * p >= order:
            tinvs = [t + _mm(t, rhs) for t, rhs in zip(tinvs, rhss)]
            break
        ress = [_mm(jnp.concatenate([lp, t], axis=0), rhs) for lp, t, rhs in zip(lps, tinvs, rhss)]
        lps = [res[0:c] for res in ress]
        tinvs = [t + res[c:2 * c] for t, res in zip(tinvs, ress)]
        p *= 2

    v_bds = [_expand_bd(v, bd_mask) for v in vs]
    ws = [pz[0:c] + _mm(l_ak, v_bd) for pz, l_ak, v_bd in zip(pzs, l_aks, v_bds)]
    us = [_mm(t, _expand_bd(w, bd_mask)) for t, w in zip(tinvs, ws)]
    ys = [pz[c:2 * c] + _mm(jnp.concatenate([m_rb, m_rk], axis=1),
                            jnp.concatenate([_expand_bd(u, bd_mask), v_bd], axis=0))
          for pz, m_rb, m_rk, u, v_bd in zip(pzs, m_rbs, m_rks, us, v_bds)]
    return ys, us


def _wkv_chunk(groups, masks):
    c = groups[0][0].shape[0]
    cls = [_chunk_cumsum(g[1]) for g in groups]
    xs = [jnp.concatenate([g[4] * jnp.exp(cl - g[1]), g[0] * jnp.exp(cl)], axis=0)
          for g, cl in zip(groups, cls)]
    pzs = [_mm_nt(x, g[6]) for x, g in zip(xs, groups)]
    e_negs = [jnp.exp(-cl) for cl in cls]
    ys, us = _wkv_intra(xs, [g[5] * e for g, e in zip(groups, e_negs)],
                        [g[2] * e for g, e in zip(groups, e_negs)], [g[3] for g in groups], pzs, masks, c)
    dss = []
    for g, cl, u in zip(groups, cls, us):
        e_rem = jnp.exp(cl[c - 1:c, :] - cl)
        uv_t = jnp.concatenate([u, g[3]], axis=0).T
        bk = jnp.concatenate([g[5] * e_rem, g[2] * e_rem], axis=0)
        dss.append(_mm(uv_t, bk))
    return [(y, g[6] * jnp.exp(cl[c - 1:c, :]) + jnp.where(masks[0], ds, 0.0))
            for y, g, cl, ds in zip(ys, groups, cls, dss)]


def _wkv_masks(c, seq):
    n = HEADS_PER_GROUP * c
    rr = lax.broadcasted_iota(jnp.int32, (n, n), 0)
    cc = lax.broadcasted_iota(jnp.int32, (n, n), 1)
    bd_mask = (rr // c) == (cc // c)
    t = lax.broadcasted_iota(jnp.int32, (c, n), 0)
    s = lax.broadcasted_iota(jnp.int32, (c, n), 1) % c
    same = (t // seq) == (s // seq)
    strict_mask = same & (t > s)
    incl_mask = same & (t >= s)
    eye_c = jnp.where(t == s, 1.0, 0.0).astype(F32)
    lane = lax.broadcasted_iota(jnp.int32, (c, MXU_DIM_V7X), 1)
    head_masks = [(lane // HEAD_DIM) == h for h in range(HEADS_PER_GROUP)]
    return bd_mask, strict_mask, incl_mask, eye_c, head_masks


def _wkv_prompt_kernel(nb, n_parts, n_casts, *refs):
    it = iter(refs)
    take = lambda k: [next(it) for _ in range(k)]
    z_refs = [take(n_parts) for _ in range(nb)]
    pw_refs, w_refs = take(N_PREP_WEIGHTS), take(n_casts)
    y_ref, bonus_ref, g_ref, sout_ref = take(4)
    wb_refs = take(n_casts)
    s_ref, carry_ref = take(2)
    ci = pl.program_id(0)

    @pl.when(ci == 0)
    def _():
        s_ref[...] = jnp.zeros_like(s_ref)
        carry_ref[...] = jnp.zeros_like(carry_ref)

    for w_ref, wb_ref in zip(w_refs, wb_refs):
        wb_ref[...] = w_ref[...].astype(BF16)

    zs_parts, prev_parts = [], []
    for bi in range(nb):
        zs = jnp.concatenate([z[...] for z in z_refs[bi]], axis=1)
        row = lax.broadcasted_iota(jnp.int32, zs.shape, 0)
        prev_parts.append(jnp.where(row == 0, carry_ref[bi, 0:1, :], pltpu.roll(zs, 1, 0)))
        carry_ref[bi, 0:1, :] = zs[CHUNK - 1:, :]
        zs_parts.append(zs)
    *scan_ops, g, bonus = _prep_math(
        jnp.concatenate(zs_parts, axis=0), jnp.concatenate(prev_parts, axis=0), *[w[...] for w in pw_refs])
    for bi in range(nb):
        rows = slice(CHUNK * bi, CHUNK * (bi + 1))
        g_ref[bi] = g[rows]
        bonus_ref[bi] = bonus[rows]

    masks = _wkv_masks(CHUNK, CHUNK)
    chains = [(bi, gi) for bi in range(nb) for gi in range(N_GROUPS)]
    lanes = lambda gi: slice(MXU_DIM_V7X * gi, MXU_DIM_V7X * (gi + 1))
    groups = [tuple(x[CHUNK * bi:CHUNK * (bi + 1), lanes(gi)] for x in scan_ops)
              + (s_ref[bi, gi],) for bi, gi in chains]
    for (bi, gi), (y, s_new) in zip(chains, _wkv_chunk(groups, masks)):
        y_ref[bi, :, lanes(gi)] = y
        s_ref[bi, gi] = s_new

    @pl.when(ci == pl.num_programs(0) - 1)
    def _():
        for bi, gi in chains:
            s_bd = s_ref[bi, gi]
            for h in range(HEADS_PER_GROUP):
                hs = slice(HEAD_DIM * h, HEAD_DIM * (h + 1))
                sout_ref[bi, HEADS_PER_GROUP * gi + h] = s_bd[hs, hs]


def _wkv_prompt(zparts, b, t, prep_wts, weights):
    steps = t // CHUNK
    assert len(prep_wts) == N_PREP_WEIGHTS
    seq_rows = lambda w, bi: pl.BlockSpec((CHUNK, w), lambda ci: (bi * steps + ci, 0))
    spec = pl.BlockSpec((b, CHUNK, D_RWKV), lambda ci: (0, ci, 0))
    state = (b, N_GROUPS, MXU_DIM_V7X, MXU_DIM_V7X)
    out_state = (b, N_HEADS, HEAD_DIM, HEAD_DIM)
    bf16_rows = 2 * SLAB
    assert all(w.shape[0] % (steps * bf16_rows) == 0 for w in weights)
    slabs = [pl.BlockSpec((w.shape[0] // steps, w.shape[1]), lambda ci: (ci, 0)) for w in weights]
    rows = jax.ShapeDtypeStruct((b, t, D_RWKV), F32)
    y, bonus, g, s, *wb = pl.pallas_call(
        functools.partial(_wkv_prompt_kernel, b, len(zparts), len(weights)),
        grid=(steps,),
        in_specs=[seq_rows(z.shape[1], bi) for bi in range(b) for z in zparts]
        + [_const_spec(w.shape) for w in prep_wts] + slabs,
        out_specs=[spec, spec, spec, _const_spec(out_state)] + slabs,
        out_shape=[rows, rows, rows, jax.ShapeDtypeStruct(out_state, F32)]
        + [jax.ShapeDtypeStruct(w.shape, BF16) for w in weights],
        scratch_shapes=[pltpu.VMEM(state, F32), pltpu.VMEM((b, SLAB, D_SHIFT_PAD), F32)],
        compiler_params=_params("arbitrary"),
        name="wkv_prompt",
    )(*(list(zparts) * b), *prep_wts, *weights)
    return y, bonus, g, s, wb


HEAD_PAIR = 2 * HEAD_DIM


def _wkv_steps_kernel(seq, r_ref, lw_ref, k2_ref, v_ref, as_ref, bs_ref, s0_ref, y_ref, sout_ref,
                      op_ref, vt_ref, yt_ref):
    nb = r_ref.shape[0] // seq
    for t in range(seq):
        rows = pl.ds(t, nb, stride=seq)
        op_ref[0, t] = r_ref[rows, :].T
        op_ref[1, t] = jnp.exp(lw_ref[rows, :]).T
        op_ref[2, t] = k2_ref[rows, :].T
        op_ref[3, t] = as_ref[rows, :].T
        op_ref[4, t] = bs_ref[rows, :].T
        vt_ref[t] = v_ref[rows, :].T

    for hh in range(HEAD_PAIR // HEAD_DIM):
        ks = slice(HEAD_DIM * hh, HEAD_DIM * (hh + 1))

        def body(vi, carry):
            s = s0_ref[hh, vi]
            for t in range(seq):
                sa = jnp.sum(s * op_ref[3, t, ks, :], axis=0, keepdims=True)
                vrow = vt_ref[t, pl.ds(HEAD_DIM * hh + vi, 1), :]
                s = s * op_ref[1, t, ks, :] + sa * op_ref[4, t, ks, :] + vrow * op_ref[2, t, ks, :]
                yt_ref[t, pl.ds(HEAD_DIM * hh + vi, 1), :] = jnp.sum(s * op_ref[0, t, ks, :], axis=0,
                                                                      keepdims=True)
            sout_ref[hh, vi] = s
            return carry

        lax.fori_loop(0, HEAD_DIM, body, 0, unroll=16)

    for t in range(seq):
        y_ref[pl.ds(t, nb, stride=seq), :] = yt_ref[t].T


def _wkv_steps(r, lw, k2, v, a_s, b_s, s0_native, seq):
    n = r.shape[0]
    nb = n // seq
    assert s0_native.shape == (N_HEADS, HEAD_DIM, HEAD_DIM, nb) and nb == LANES_V7X
    vec = pl.BlockSpec((n, HEAD_PAIR), lambda i: (0, i))
    st = pl.BlockSpec((HEAD_PAIR // HEAD_DIM, HEAD_DIM, HEAD_DIM, nb), lambda i: (i, 0, 0, 0))
    return pl.pallas_call(
        functools.partial(_wkv_steps_kernel, seq),
        grid=(N_HEADS * HEAD_DIM // HEAD_PAIR,),
        in_specs=[vec] * 6 + [st],
        out_specs=[vec, st],
        out_shape=[jax.ShapeDtypeStruct((n, D_RWKV), F32), jax.ShapeDtypeStruct(s0_native.shape, F32)],
        scratch_shapes=[pltpu.VMEM((5, seq, HEAD_PAIR, nb), F32), pltpu.VMEM((seq, HEAD_PAIR, nb), F32),
                        pltpu.VMEM((seq, HEAD_PAIR, nb), F32)],
        compiler_params=_params("parallel"),
        name="wkv_steps",
    )(r, lw, k2, v, a_s, b_s, s0_native)


POOL_CARRY = 16


def _pool_prompt_tile(zp, carry_ref, tile_in_seq):
    tm = zp.shape[0]

    @pl.when(tile_in_seq == 0)
    def _():
        carry_ref[...] = jnp.zeros_like(carry_ref)

    buf = jnp.concatenate([carry_ref[...], zp], axis=0)
    carry_ref[...] = zp[tm - POOL_CARRY:, :]
    pos = (tile_in_seq * tm + lax.broadcasted_iota(jnp.int32, (tm, POOL_GROUP), 0) + 1).astype(F32)
    ds = []
    for gi, win in enumerate(POOL_WINDOWS):
        sl = slice(POOL_GROUP * gi, POOL_GROUP * (gi + 1))
        acc = buf[:, sl]
        s = 1
        while s < win:
            acc = acc + pltpu.roll(acc, s, 0)
            s *= 2
        ds.append(acc[POOL_CARRY:, :] / jnp.minimum(float(win), pos) - zp[:, sl])
    return jnp.concatenate(ds, axis=1)


def _pool_sample_kernel(seq, hist_ref, zp_ref, d_ref, new_ref):
    nb = hist_ref.shape[1]
    step = lambda t: zp_ref[pl.ds(t, nb, stride=seq), :]
    past = lambda j: step(j) if j >= 0 else hist_ref[POOL_HIST + j]
    tiles_per_group = POOL_GROUP // LANES_V7X
    for gi, win in enumerate(POOL_WINDOWS):
        @pl.when(pl.program_id(0) // tiles_per_group == gi)
        def _():
            for t in range(seq):
                acc = past(t)
                for j in range(1, win):
                    acc = acc + past(t - j)
                d_ref[pl.ds(t, nb, stride=seq), :] = acc / float(min(win, PAST_LEN + 1)) - past(t)
    for j in range(POOL_HIST):
        src = j + seq
        new_ref[j] = hist_ref[src] if src < POOL_HIST else step(src - POOL_HIST)


def _pool_sample(hist, zp, row0, seq):
    nb = hist.shape[1]
    n = nb * seq
    assert row0 % n == 0
    hist_spec = pl.BlockSpec((POOL_HIST, nb, LANES_V7X), lambda j: (0, 0, j))
    return pl.pallas_call(
        functools.partial(_pool_sample_kernel, seq),
        grid=(D_POOL // LANES_V7X,),
        in_specs=[hist_spec, pl.BlockSpec((n, LANES_V7X), lambda j: (row0 // n, j))],
        out_specs=[pl.BlockSpec((n, LANES_V7X), lambda j: (0, j)), hist_spec],
        out_shape=[jax.ShapeDtypeStruct((n, D_POOL), F32), jax.ShapeDtypeStruct(hist.shape, F32)],
        compiler_params=_params("parallel"),
        name="pool_sample",
    )(hist, zp)


def _merge_kernel(seq_tiles, y_ref, bonus_ref, g_ref, d_ref, zga_ref, zgb_ref, x_ref,
                  lw_ref, lb_ref, ones_ref, wa_ref, pw_ref, ps_ref, wb_ref,
                  wo_ref, gpost_ref, gnext_ref, o_ref, h_ref, *carry):
    if seq_tiles is None:
        d = d_ref[...]
    else:
        d = _pool_prompt_tile(d_ref[...], carry[0], pl.program_id(0) % seq_tiles)
    ones_bd = ones_ref[...]
    y = y_ref[...]
    mu = _head_sum(y, ones_bd) * (1.0 / HEAD_DIM)
    yb = jnp.concatenate(
        [_mm(d[:, POOL_GROUP * gi:POOL_GROUP * (gi + 1)], pw_ref[gi]) for gi in range(len(POOL_WINDOWS))],
        axis=1) * ps_ref[...]
    yc = y - mu
    var = _head_sum(yc * yc, ones_bd) * (1.0 / HEAD_DIM)
    mb = jax.nn.sigmoid(zgb_ref[...]) * _mm(yb, wb_ref[...])
    yn = yc * lax.rsqrt(var + GN_EPS) * lw_ref[...] + lb_ref[...]
    ya = (yn + bonus_ref[...]) * g_ref[...]
    m = jax.nn.sigmoid(zga_ref[...]) * _mm(ya, wa_ref[...]) + mb
    x1 = x_ref[...] + _rmsnorm(_mm(m, wo_ref[...]), gpost_ref[...])
    o_ref[...] = x1
    h_ref[...] = _rmsnorm(x1, gnext_ref[...]).astype(BF16)


def _merge(y, bonus, g, d, zpg, x2d, row0, wts, tm, seq_len=None):
    n = y.shape[0]
    assert row0 % tm == 0 and tm % SLAB == 0 and (seq_len is None or (seq_len % tm == 0 and tm >= POOL_CARRY))
    first = row0 // tm
    half = pl.BlockSpec((tm, D_RWKV), lambda i: (i, 0))
    full = pl.BlockSpec((tm, D_MODEL), lambda i: (i, 0))
    gate_at = lambda col: pl.BlockSpec((pl.Element(tm), pl.Element(D_MODEL)), lambda i: (tm * (first + i), col))
    resident = [pl.BlockSpec(w.shape, functools.partial(lambda nd, i: (0,) * nd, w.ndim),
                             pipeline_mode=pl.Buffered(1)) for w in wts]
    return pl.pallas_call(
        functools.partial(_merge_kernel, None if seq_len is None else seq_len // tm),
        grid=(n // tm,),
        in_specs=[half] * 4 + [gate_at(D_POOL), gate_at(D_POOL + D_MODEL)] + [full] + resident,
        out_specs=[full, full],
        out_shape=[jax.ShapeDtypeStruct((n, D_MODEL), F32), jax.ShapeDtypeStruct((n, D_MODEL), BF16)],
        scratch_shapes=[] if seq_len is None else [pltpu.VMEM((POOL_CARRY, D_POOL), F32)],
        compiler_params=_params("parallel" if seq_len is None else "arbitrary"),
        name="merge",
    )(y, bonus, g, d, zpg, zpg, x2d, *wts)


def _gelu_tanh(x):
    return 0.5 * x * (1.0 + jnp.tanh(0.7978845608028654 * (x + 0.044715 * x * x * x)))


def _ffn_body(x_ref, h_ref, wg_ref, wu_ref, cw_ref, cb_ref, wo_ref, gp_ref,
              o_ref, acc_ref, prev1, prev2, emit_tail):
    c = pl.program_id(1)

    @pl.when(c == 0)
    def _():
        acc_ref[...] = jnp.zeros_like(acc_ref)

    h = h_ref[...]
    gate = jnp.dot(h, wg_ref[...], preferred_element_type=F32)
    up = jnp.dot(h, wu_ref[...], preferred_element_type=F32)
    cw = cw_ref[...]
    cv = cb_ref[...] + cw[0:1, :] * prev2(gate) + cw[1:2, :] * prev1(gate) + cw[2:3, :] * gate
    emit_tail(gate)
    acc_ref[...] += _mm(_gelu_tanh(cv) * up, wo_ref[...])

    @pl.when(c == pl.num_programs(1) - 1)
    def _():
        o_ref[...] = x_ref[...] + _rmsnorm(acc_ref[...], gp_ref[...])


def _ffn_prompt_kernel(tiles_per_seq, x_ref, h_ref, wg_ref, wu_ref, cw_ref, cb_ref, wo_ref, gp_ref,
                       o_ref, tail_ref, acc_ref, carry_ref):
    i = pl.program_id(0)
    c = pl.program_id(1)

    @pl.when(i % tiles_per_seq == 0)
    def _():
        carry_ref[c] = jnp.zeros(carry_ref.shape[1:], F32)

    hist = carry_ref[c]

    def shifted(gate, s):
        row = lax.broadcasted_iota(jnp.int32, gate.shape, 0)
        rolled = pltpu.roll(gate, s, 0)
        out = rolled
        for j in range(s):
            out = jnp.where(row == j, hist[8 - s + j:9 - s + j, :], out)
        return out

    def emit_tail(gate):
        tail_ref[...] = gate[gate.shape[0] - tail_ref.shape[0]:, :]

    _ffn_body(x_ref, h_ref, wg_ref, wu_ref, cw_ref, cb_ref, wo_ref, gp_ref, o_ref,
              acc_ref, lambda g: shifted(g, 1), lambda g: shifted(g, 2), emit_tail)
    carry_ref[c] = tail_ref[...]


def _ffn_sample_kernel(seq, x_ref, st_ref, h_ref, wg_ref, wu_ref, cw_ref, cb_ref, wo_ref, gp_ref,
                       o_ref, tail_ref, acc_ref, g_ref, p1_ref, p2_ref):
    nb = st_ref.shape[0]
    st0 = st_ref[:, 0, :]
    st1 = st_ref[:, 1, :]
    prev1 = lambda gate: _replace_step_rows(p1_ref, pltpu.roll(gate, 1, 0), seq, [(0, st1)])
    prev2 = lambda gate: _replace_step_rows(p2_ref, pltpu.roll(gate, 2, 0), seq, [(0, st0), (1, st1)])

    def emit_tail(gate):
        tiles = gate.shape[1] // LANES_V7X
        for j in range(tiles):
            g_ref[j] = gate[:, LANES_V7X * j:LANES_V7X * (j + 1)]
        for s in range(CONV_W - 1):
            rows = pl.ds(seq - (CONV_W - 1) + s, nb, stride=seq)
            tail_ref[:, s, :] = jnp.concatenate([g_ref[j, rows, :] for j in range(tiles)], axis=1)

    _ffn_body(x_ref, h_ref, wg_ref, wu_ref, cw_ref, cb_ref, wo_ref, gp_ref, o_ref,
              acc_ref, prev1, prev2, emit_tail)


def _ffn_prompt(x2d, h, seq_len, wts, tm, fk):
    n = x2d.shape[0]
    nc = D_FF // fk
    w_in, cw, cb, w_out, gp = wts
    full = pl.BlockSpec((tm, D_MODEL), lambda i, c: (i, 0))
    return pl.pallas_call(
        functools.partial(_ffn_prompt_kernel, seq_len // tm),
        grid=(n // tm, nc),
        in_specs=[full, full,
                  pl.BlockSpec((D_MODEL, fk), lambda i, c: (0, c)),
                  pl.BlockSpec((D_MODEL, fk), lambda i, c: (0, c + nc)),
                  pl.BlockSpec((CONV_W, fk), lambda i, c: (0, c)),
                  pl.BlockSpec((1, fk), lambda i, c: (0, c)),
                  pl.BlockSpec((fk, D_MODEL), lambda i, c: (c, 0)),
                  _const_spec(gp.shape)],
        out_specs=[full, pl.BlockSpec((None, 8, fk), lambda i, c: (i, 0, c))],
        out_shape=[jax.ShapeDtypeStruct((n, D_MODEL), F32),
                   jax.ShapeDtypeStruct((n // tm, 8, D_FF), F32)],
        scratch_shapes=[pltpu.VMEM((tm, D_MODEL), F32), pltpu.VMEM((nc, 8, fk), F32)],
        compiler_params=_params("arbitrary", "arbitrary"),
        name="ffn_prompt",
    )(x2d, h, w_in, w_in, cw, cb, w_out, gp)


def _ffn_sample(x2d, h, st, seq, wts, fk):
    n = x2d.shape[0]
    nb = n // seq
    assert seq >= CONV_W - 1 and st.shape == (nb, CONV_W - 1, D_FF)
    nc = D_FF // fk
    w_in, cw, cb, w_out, gp = wts
    full = pl.BlockSpec((n, D_MODEL), lambda i, c: (0, 0))
    cols = pl.BlockSpec((nb, CONV_W - 1, fk), lambda i, c: (0, 0, c))
    return pl.pallas_call(
        functools.partial(_ffn_sample_kernel, seq),
        grid=(1, nc),
        in_specs=[full, cols, full,
                  pl.BlockSpec((D_MODEL, fk), lambda i, c: (0, c)),
                  pl.BlockSpec((D_MODEL, fk), lambda i, c: (0, c + nc)),
                  pl.BlockSpec((CONV_W, fk), lambda i, c: (0, c)),
                  pl.BlockSpec((1, fk), lambda i, c: (0, c)),
                  pl.BlockSpec((fk, D_MODEL), lambda i, c: (c, 0)),
                  _const_spec(gp.shape)],
        out_specs=[full, cols],
        out_shape=[jax.ShapeDtypeStruct((n, D_MODEL), F32), jax.ShapeDtypeStruct(st.shape, F32)],
        scratch_shapes=[pltpu.VMEM((n, D_MODEL), F32)]
        + [pltpu.VMEM((fk // LANES_V7X, n, LANES_V7X), F32)] * 3,
        compiler_params=_params("arbitrary", "arbitrary"),
        name="ffn_sample",
    )(x2d, st, h, w_in, w_in, cw, cb, w_out, gp)


def _row(v):
    return v.reshape(1, -1).astype(F32)


def _ones_bd():
    i = jnp.arange(MXU_DIM_V7X) // HEAD_DIM
    return (i[:, None] == i[None, :]).astype(BF16)


def _layer_weights(l, norm_pre_mix, w_in, mu_shift, w0, w2, a0, a2, g2, k_k, k_a, r_k, lnx_w, lnx_b,
                   w_branch_a, pool_w, pool_scale, w_branch_b, w_out, norm_post_mix,
                   norm_pre_ffn, w_ffn_in, conv_w, conv_b, w_ffn_out, norm_post_ffn):
    w_t = jnp.swapaxes(w_in[l], 0, 1)
    mu = jnp.pad(mu_shift[l], (0, D_SHIFT_PAD - D_SHIFT)).reshape(1, -1)
    w_lora = jnp.zeros((D_LORA_PAD, 3 * D_RWKV), F32)
    w_lora = w_lora.at[0:RANK_W, 0:D_RWKV].set(w2[l])
    w_lora = w_lora.at[RANK_W:RANK_W + RANK_A, D_RWKV:2 * D_RWKV].set(a2[l])
    w_lora = w_lora.at[RANK_W + RANK_A:D_LORA, 2 * D_RWKV:].set(g2[l])
    ones_bd = _ones_bd()

    def with_bf16(wa, pw, wb, wo, wfi, wfo):
        return dict(
            merge=(_row(lnx_w[l]), _row(lnx_b[l]), ones_bd, wa,
                   pw.reshape(pool_w.shape[1:]), _row(pool_scale[l]), wb,
                   wo, _row(norm_post_mix[l]), _row(norm_pre_ffn[l])),
            ffn=(wfi, conv_w[l].astype(F32), _row(conv_b[l]), wfo, _row(norm_post_ffn[l])),
        )

    return dict(
        in_proj=(_row(norm_pre_mix[l]), w_t),
        prep=(mu, w_lora.astype(BF16), _row(w0[l]), _row(a0[l]), _row(k_k[l]), _row(k_a[l]), _row(r_k[l]),
              ones_bd),
        f32_matmul_weights=[w_branch_a[l], pool_w[l].reshape(-1, POOL_GROUP), w_branch_b[l], w_out[l],
                            w_ffn_in[l], w_ffn_out[l]],
        with_bf16=with_bf16,
    )


def _largest_tile(n, cap, mult=16):
    best = None
    for d in range(mult, min(n, cap) + 1, mult):
        if n % d == 0:
            best = d
    assert best is not None, (n, cap)
    return best


def _project(xp2d, xs2d, wts):
    g, w_t = wts["in_proj"]
    tm_norm = _largest_tile(math.gcd(xp2d.shape[0], xs2d.shape[0]), 512)
    h, (zr, zl) = _pre_norm(xp2d, xs2d, g, w_t, [(0, D_RWKV), (3 * D_RWKV, D_LORA_PAD)], tm_norm)
    tm = _largest_tile(h.shape[0], 1088)
    zkv = _in_proj(h, w_t, D_RWKV, 2 * D_RWKV, D_RWKV, tm)
    zpg = _in_proj(h, w_t, D_SHIFT, D_POOL + 2 * D_MODEL, D_RWKV, tm)
    return (zr, zkv, zl), zpg


def _last_shift_row(zparts, row0, b, t):
    if b <= 8:
        last = lambda z: jnp.concatenate(
            [lax.slice(z, (row0 + (i + 1) * t - 1, 0), (row0 + (i + 1) * t, z.shape[1])) for i in range(b)])
    else:
        last = lambda z: lax.slice(z, (row0 + t - 1, 0), (row0 + b * t, z.shape[1]), (t, 1))
    return jnp.concatenate([last(z) for z in zparts], axis=-1)[:, None, :D_SHIFT]


def _prompt_layer(x, z, wts):
    b, t, _ = x.shape
    n = b * t
    x2d = x.reshape(n, D_MODEL)
    zparts, zpg = z
    y, bonus, g, wkv, bf16_weights = _wkv_prompt(zparts, b, t, wts["prep"], wts["f32_matmul_weights"])
    wts = wts["with_bf16"](*bf16_weights)
    flat = lambda a: a.reshape(n, a.shape[-1])
    x1, h1 = _merge(flat(y), flat(bonus), flat(g), zpg, zpg, x2d, 0, wts["merge"], tm=256, seq_len=t)
    ffn_tm = _largest_tile(t, 512)
    out, tail = _ffn_prompt(x1, h1, t, wts["ffn"], tm=ffn_tm, fk=512)
    shift = _last_shift_row(zparts, 0, b, t)
    pool = jnp.stack([zpg[(i + 1) * t - POOL_HIST:(i + 1) * t, :D_POOL] for i in range(b)])
    tiles = t // ffn_tm
    conv = tail.reshape(b, tiles, 8, D_FF)[:, tiles - 1, 8 - (CONV_W - 1):, :]
    return wts, (out.reshape(b, t, D_MODEL), shift, wkv, pool, conv)


def _sample_layer(x, z, row0, st_shift, st_wkv, st_pool, st_conv, prep_wts, wts):
    b, t, _ = x.shape
    n = b * t
    x2d = x.reshape(n, D_MODEL)
    zparts, zpg = z
    st = jnp.pad(st_shift.reshape(b, D_SHIFT), ((0, 0), (0, D_SHIFT_PAD - D_SHIFT)))
    tm = _largest_tile(math.gcd(row0, n), 128, mult=SLAB * t)
    r, lw, k2, v, a_s, b_s, g, bonus = _prep_sample(zparts, row0, n, st, t, prep_wts, tm=tm)
    y, wkv = _wkv_steps(r, lw, k2, v, a_s, b_s, jnp.transpose(st_wkv, (1, 2, 3, 0)), t)
    wkv = jnp.transpose(wkv, (3, 0, 1, 2))
    d, pool = _pool_sample(jnp.swapaxes(st_pool, 0, 1), zpg, row0, t)
    pool = jnp.swapaxes(pool, 0, 1)
    x1, h1 = _merge(y, bonus, g, d, zpg, x2d, row0, wts["merge"],
                    tm=_largest_tile(math.gcd(row0, n), 256, mult=16))
    out, conv = _ffn_sample(x1, h1, st_conv, t, wts["ffn"], fk=512)
    shift = _last_shift_row(zparts, row0, b, t)
    return out.reshape(b, t, D_MODEL), shift, wkv, pool, conv


def kernel(x_prompt, x_sample, state_shift, state_wkv, state_pool, state_conv, norm_pre_mix, w_in, mu_shift, w0, w2, a0, a2, g2, k_k, k_a, r_k, lnx_w, lnx_b, w_branch_a, pool_w, pool_scale, w_branch_b, w_out, norm_post_mix, norm_pre_ffn, w_ffn_in, conv_w, conv_b, w_ffn_out, norm_post_ffn):
    weights = (norm_pre_mix, w_in, mu_shift, w0, w2, a0, a2, g2, k_k, k_a, r_k, lnx_w, lnx_b,
               w_branch_a, pool_w, pool_scale, w_branch_b, w_out, norm_post_mix,
               norm_pre_ffn, w_ffn_in, conv_w, conv_b, w_ffn_out, norm_post_ffn)
    depth = w_in.shape[0]
    yp, ys = x_prompt, x_sample
    p_states, s_states = [], []
    for l in range(depth):
        wts = _layer_weights(l, *weights)
        n_prompt = yp.shape[0] * yp.shape[1]
        z = _project(yp.reshape(n_prompt, D_MODEL), ys.reshape(-1, D_MODEL), wts)
        bf16_wts, (yp, *ps) = _prompt_layer(yp, z, wts)
        ys, *ss = _sample_layer(ys, z, n_prompt, state_shift[l], state_wkv[l], state_pool[l], state_conv[l],
                                wts["prep"], bf16_wts)
        p_states.append(ps)
        s_states.append(ss)
    stack = lambda states, i: jnp.stack([s[i] for s in states])
    return (yp, ys,
            stack(p_states, 0), stack(p_states, 1), stack(p_states, 2), stack(p_states, 3),
            stack(s_states, 0), stack(s_states, 1), stack(s_states, 2), stack(s_states, 3))
```

```python
import functools
import math

import jax
import jax.numpy as jnp
from jax import lax
from jax.experimental import pallas as pl
from jax.experimental.pallas import tpu as pltpu

F32 = jnp.float32
BF16 = jnp.bfloat16

D_MODEL = 2048
HEAD_DIM = 64
D_RWKV = 1024
N_HEADS = 16
RANK_W, RANK_A, RANK_G = 64, 64, 160
D_LORA = RANK_W + RANK_A + RANK_G
D_LORA_PAD = 384
D_SHIFT = 3 * D_RWKV + D_LORA
D_SHIFT_PAD = 3 * D_RWKV + D_LORA_PAD
D_POOL = 1024
POOL_WINDOWS = (2, 4, 8, 16)
POOL_GROUP = 256
POOL_HIST = 15
D_FF = 5632
CONV_W = 3
NORM_EPS = 1e-6
GN_EPS = 64e-5
PAST_LEN = 16384

LANES_V7X = 128
SLAB = 8
MXU_DIM_V7X = 256
HEADS_PER_GROUP = MXU_DIM_V7X // HEAD_DIM
N_GROUPS = N_HEADS // HEADS_PER_GROUP
CHUNK = 64
VMEM_LIMIT_V7X = 56 * 1024 * 1024

ROWS_NORM = 512
ROWS_PROJ = 1088
COLS_PROJ = 1024
ROWS_MERGE = 256
ROWS_FFN = 512
COLS_FFN = 512
ROWS_PREP_SAMPLE = 128


def _params(*sem):
    return pltpu.CompilerParams(dimension_semantics=sem, vmem_limit_bytes=VMEM_LIMIT_V7X)


def _mm(a, b):
    return jnp.dot(a.astype(BF16), b.astype(BF16), preferred_element_type=F32)


def _mm_nt(a, b):
    return lax.dot_general(a.astype(BF16), b.astype(BF16), (((1,), (1,)), ((), ())),
                           preferred_element_type=F32)


def _split_hi_lo(x):
    hi = x.astype(BF16)
    lo = (x - hi.astype(F32)).astype(BF16)
    return hi, lo


def _head_sum(x, ones_bd):
    hi, lo = _split_hi_lo(x)
    outs = []
    for gi in range(x.shape[1] // MXU_DIM_V7X):
        sl = slice(MXU_DIM_V7X * gi, MXU_DIM_V7X * (gi + 1))
        outs.append(jnp.dot(hi[:, sl], ones_bd, preferred_element_type=F32)
                    + jnp.dot(lo[:, sl], ones_bd, preferred_element_type=F32))
    return jnp.concatenate(outs, axis=1)


def _rmsnorm(x, g):
    return x * lax.rsqrt(jnp.mean(x * x, axis=-1, keepdims=True) + NORM_EPS) * g


def _replace_step_rows(scr_ref, base, seq, steps):
    nb = base.shape[0] // seq
    for j in range(base.shape[1] // LANES_V7X):
        ls = slice(LANES_V7X * j, LANES_V7X * (j + 1))
        scr_ref[j] = base[:, ls]
        for t, vals in steps:
            scr_ref[j, pl.ds(t, nb, stride=seq), :] = vals[:, ls]
    return jnp.concatenate([scr_ref[j] for j in range(base.shape[1] // LANES_V7X)], axis=1)


def _pre_norm_kernel(prompt_tiles, n_sections, xp_ref, xs_ref, g_ref, *refs):
    w_refs, h_ref, z_refs, wb_refs = (refs[:n_sections], refs[n_sections],
                                      refs[n_sections + 1:2 * n_sections + 1], refs[2 * n_sections + 1:])
    i = pl.program_id(0)

    @pl.when(i == 0)
    def _():
        for w_ref, wb_ref in zip(w_refs, wb_refs):
            wb_ref[...] = w_ref[...].astype(BF16)

    @pl.when(i < prompt_tiles)
    def _():
        h_ref[...] = _rmsnorm(xp_ref[...], g_ref[...]).astype(BF16)

    @pl.when(i >= prompt_tiles)
    def _():
        h_ref[...] = _rmsnorm(xs_ref[...], g_ref[...]).astype(BF16)

    h = h_ref[...]
    for z_ref, wb_ref in zip(z_refs, wb_refs):
        z_ref[...] = lax.dot_general(h, wb_ref[...], (((1,), (1,)), ((), ())), preferred_element_type=F32)


def _pre_norm(xp, xs, g, w_t, sections, tm):
    pt, st = xp.shape[0] // tm, xs.shape[0] // tm
    n = xp.shape[0] + xs.shape[0]
    assert all(off % SLAB == 0 for off, _ in sections)
    rows = pl.BlockSpec((tm, D_MODEL), lambda i: (i, 0))
    h, *zs = pl.pallas_call(
        functools.partial(_pre_norm_kernel, pt, len(sections)),
        grid=(pt + st,),
        in_specs=[pl.BlockSpec((tm, D_MODEL), lambda i: (jnp.minimum(i, pt - 1), 0)),
                  pl.BlockSpec((tm, D_MODEL), lambda i: (jnp.maximum(i - pt, 0), 0)),
                  _const_spec(g.shape)]
        + [pl.BlockSpec((pl.Element(width), pl.Element(D_MODEL)),
                        functools.partial(lambda o, i: (SLAB * o, 0), off // SLAB),
                        pipeline_mode=pl.Buffered(1)) for off, width in sections],
        out_specs=[rows] + [pl.BlockSpec((tm, width), lambda i: (i, 0)) for _, width in sections],
        out_shape=[jax.ShapeDtypeStruct((n, D_MODEL), BF16)]
        + [jax.ShapeDtypeStruct((n, width), F32) for _, width in sections],
        scratch_shapes=[pltpu.VMEM((width, D_MODEL), BF16) for _, width in sections],
        compiler_params=_params("arbitrary"),
        name="pre_norm",
    )(xp, xs, g, *([w_t] * len(sections)))
    return h, zs


def _in_proj_kernel(h_ref, w_ref, o_ref, wb_ref):
    @pl.when(pl.program_id(1) == 0)
    def _():
        wb_ref[...] = w_ref[...].astype(BF16)

    o_ref[...] = lax.dot_general(h_ref[...], wb_ref[...], (((1,), (1,)), ((), ())),
                                 preferred_element_type=F32)


def _in_proj(h, w_t, off, width, tn, tm):
    n = h.shape[0]
    assert width % tn == 0 and n % tm == 0
    return pl.pallas_call(
        _in_proj_kernel,
        grid=(width // tn, n // tm),
        in_specs=[pl.BlockSpec((tm, D_MODEL), lambda j, i: (i, 0)),
                  pl.BlockSpec((pl.Element(tn), pl.Element(D_MODEL)),
                               lambda j, i: (SLAB * (off // SLAB + j * (tn // SLAB)), 0))],
        out_specs=pl.BlockSpec((tm, tn), lambda j, i: (i, j)),
        out_shape=jax.ShapeDtypeStruct((n, width), F32),
        scratch_shapes=[pltpu.VMEM((tn, D_MODEL), BF16)],
        compiler_params=_params("arbitrary", "arbitrary"),
        name="in_proj",
    )(h, w_t)


def _prep_math(zs, prev, mu, w_lora, w0, a0, k_k, k_a, r_k, ones_bd):
    f = zs + (prev - zs) * mu
    r = f[:, 0:D_RWKV]
    k = f[:, D_RWKV:2 * D_RWKV]
    v = f[:, 2 * D_RWKV:3 * D_RWKV]
    low = f[:, 3 * D_RWKV:3 * D_RWKV + D_LORA_PAD]
    lane = lax.broadcasted_iota(jnp.int32, low.shape, 1)
    act = jnp.where(lane < RANK_W, jnp.tanh(low),
                    jnp.where(lane < RANK_W + RANK_A, low,
                              jnp.where(lane < D_LORA, jax.nn.sigmoid(low), 0.0)))
    lora = _mm(act, w_lora)
    u = w0 + lora[:, 0:D_RWKV]
    lw = -math.exp(-0.5) * jax.nn.sigmoid(u)
    a = jax.nn.sigmoid(a0 + lora[:, D_RWKV:2 * D_RWKV])
    g = lora[:, 2 * D_RWKV:3 * D_RWKV]
    kk = k * k_k
    kk = kk * jnp.minimum(lax.rsqrt(_head_sum(kk * kk, ones_bd)), 1e12)
    k2 = k * (1.0 + (a - 1.0) * k_a)
    bonus = _head_sum(r * k2 * r_k, ones_bd) * v
    return r, lw, k2, v, -kk, kk * a, g, bonus


N_PREP_WEIGHTS = 8


def _prep_sample_kernel(seq, n_parts, *refs):
    z_refs, st_ref, refs = refs[:n_parts], refs[n_parts], refs[n_parts + 1:]
    w_refs, o_refs, prev_ref = refs[:N_PREP_WEIGHTS], refs[N_PREP_WEIGHTS:-1], refs[-1]
    zs = jnp.concatenate([z[...] for z in z_refs], axis=1)
    prev = _replace_step_rows(prev_ref, pltpu.roll(zs, 1, 0), seq, [(0, st_ref[...])])
    for o_ref, o in zip(o_refs, _prep_math(zs, prev, *[w[...] for w in w_refs])):
        o_ref[...] = o


def _const_spec(shape):
    nd = len(shape)
    return pl.BlockSpec(shape, lambda *_: (0,) * nd)


def _prep_sample(zparts, row0, n, st, seq, wts, tm):
    assert tm % (SLAB * seq) == 0 and n % tm == 0 and row0 % tm == 0 and len(wts) == N_PREP_WEIGHTS
    first = row0 // tm
    outs = [jax.ShapeDtypeStruct((n, D_RWKV), F32)] * 8
    row_spec = pl.BlockSpec((tm, D_RWKV), lambda i: (i, 0))
    return pl.pallas_call(
        functools.partial(_prep_sample_kernel, seq, len(zparts)),
        grid=(n // tm,),
        in_specs=[pl.BlockSpec((tm, z.shape[1]), lambda i: (first + i, 0)) for z in zparts]
        + [pl.BlockSpec((tm // seq, D_SHIFT_PAD), lambda i: (i, 0))]
        + [_const_spec(w.shape) for w in wts],
        out_specs=[row_spec] * 8,
        out_shape=outs,
        scratch_shapes=[pltpu.VMEM((D_SHIFT_PAD // LANES_V7X, tm, LANES_V7X), F32)],
        compiler_params=_params("parallel"),
        name="prep_sample",
    )(*zparts, st, *wts)


def _expand_bd(a, bd_mask):
    return jnp.where(bd_mask, jnp.concatenate([a] * HEADS_PER_GROUP, axis=0), 0.0)


def _chunk_cumsum(x):
    row = lax.broadcasted_iota(jnp.int32, x.shape, 0)
    s = 1
    while s < x.shape[0]:
        x = x + jnp.where(row >= s, pltpu.roll(x, s, 0), 0.0)
        s *= 2
    return x


def _wkv_intra(xs, bts, kts, vs, pzs, masks):
    bd_mask, strict_mask, incl_mask, eye_c, head_masks = masks
    c = vs[0].shape[0]
    ystacks = [jnp.concatenate([jnp.where(m, bt, 0.0) for m in head_masks]
                               + [jnp.where(m, kt, 0.0) for m in head_masks], axis=0)
               for bt, kt in zip(bts, kts)]
    grams = [_mm_nt(x, ys) for x, ys in zip(xs, ystacks)]
    l_abs = [jnp.where(strict_mask, g[0:c, 0:4 * c], 0.0) for g in grams]
    l_aks = [jnp.where(strict_mask, g[0:c, 4 * c:8 * c], 0.0) for g in grams]
    m_rbs = [jnp.where(incl_mask, g[c:2 * c, 0:4 * c], 0.0) for g in grams]
    m_rks = [jnp.where(incl_mask, g[c:2 * c, 4 * c:8 * c], 0.0) for g in grams]

    tinvs = [eye_c + l for l in l_abs]
    lps = [_mm(l, _expand_bd(l, bd_mask)) for l in l_abs]
    p = 2
    while True:
        rhss = [_expand_bd(lp, bd_mask) for lp in lps]
        if 2 * p >= c:
            tinvs = [t + _mm(t, rhs) for t, rhs in zip(tinvs, rhss)]
            break
        ress = [_mm(jnp.concatenate([lp, t], axis=0), rhs) for lp, t, rhs in zip(lps, tinvs, rhss)]
        lps = [res[0:c] for res in ress]
        tinvs = [t + res[c:2 * c] for t, res in zip(tinvs, ress)]
        p *= 2

    v_bds = [_expand_bd(v, bd_mask) for v in vs]
    ws = [pz[0:c] + _mm(l_ak, v_bd) for pz, l_ak, v_bd in zip(pzs, l_aks, v_bds)]
    us = [_mm(t, _expand_bd(w, bd_mask)) for t, w in zip(tinvs, ws)]
    ys = [pz[c:2 * c] + _mm(jnp.concatenate([m_rb, m_rk], axis=1),
                            jnp.concatenate([_expand_bd(u, bd_mask), v_bd], axis=0))
          for pz, m_rb, m_rk, u, v_bd in zip(pzs, m_rbs, m_rks, us, v_bds)]
    return ys, us


def _wkv_chunk(groups, masks):
    c = groups[0][0].shape[0]
    cls = [_chunk_cumsum(g[1]) for g in groups]
    xs = [jnp.concatenate([g[4] * jnp.exp(cl - g[1]), g[0] * jnp.exp(cl)], axis=0)
          for g, cl in zip(groups, cls)]
    pzs = [_mm_nt(x, g[6]) for x, g in zip(xs, groups)]
    e_negs = [jnp.exp(-cl) for cl in cls]
    ys, us = _wkv_intra(xs, [g[5] * e for g, e in zip(groups, e_negs)],
                        [g[2] * e for g, e in zip(groups, e_negs)], [g[3] for g in groups], pzs, masks)
    dss = []
    for g, cl, u in zip(groups, cls, us):
        e_rem = jnp.exp(cl[c - 1:c, :] - cl)
        uv_t = jnp.concatenate([u, g[3]], axis=0).T
        bk = jnp.concatenate([g[5] * e_rem, g[2] * e_rem], axis=0)
        dss.append(_mm(uv_t, bk))
    return [(y, g[6] * jnp.exp(cl[c - 1:c, :]) + jnp.where(masks[0], ds, 0.0))
            for y, g, cl, ds in zip(ys, groups, cls, dss)]


def _wkv_masks(c):
    n = HEADS_PER_GROUP * c
    rr = lax.broadcasted_iota(jnp.int32, (n, n), 0)
    cc = lax.broadcasted_iota(jnp.int32, (n, n), 1)
    bd_mask = (rr // c) == (cc // c)
    t = lax.broadcasted_iota(jnp.int32, (c, n), 0)
    s = lax.broadcasted_iota(jnp.int32, (c, n), 1) % c
    strict_mask = t > s
    incl_mask = t >= s
    eye_c = jnp.where(t == s, 1.0, 0.0).astype(F32)
    lane = lax.broadcasted_iota(jnp.int32, (c, MXU_DIM_V7X), 1)
    head_masks = [(lane // HEAD_DIM) == h for h in range(HEADS_PER_GROUP)]
    return bd_mask, strict_mask, incl_mask, eye_c, head_masks


def _wkv_prompt_kernel(nb, n_parts, n_casts, *refs):
    it = iter(refs)
    take = lambda k: [next(it) for _ in range(k)]
    z_refs = [take(n_parts) for _ in range(nb)]
    pw_refs, w_refs = take(N_PREP_WEIGHTS), take(n_casts)
    y_ref, bonus_ref, g_ref, sout_ref = take(4)
    wb_refs = take(n_casts)
    s_ref, carry_ref = take(2)
    ci = pl.program_id(0)

    @pl.when(ci == 0)
    def _():
        s_ref[...] = jnp.zeros_like(s_ref)
        carry_ref[...] = jnp.zeros_like(carry_ref)

    for w_ref, wb_ref in zip(w_refs, wb_refs):
        wb_ref[...] = w_ref[...].astype(BF16)

    zs_parts, prev_parts = [], []
    for bi in range(nb):
        zs = jnp.concatenate([z[...] for z in z_refs[bi]], axis=1)
        row = lax.broadcasted_iota(jnp.int32, zs.shape, 0)
        prev_parts.append(jnp.where(row == 0, carry_ref[bi, 0:1, :], pltpu.roll(zs, 1, 0)))
        carry_ref[bi, 0:1, :] = zs[CHUNK - 1:, :]
        zs_parts.append(zs)
    *scan_ops, g, bonus = _prep_math(
        jnp.concatenate(zs_parts, axis=0), jnp.concatenate(prev_parts, axis=0), *[w[...] for w in pw_refs])
    for bi in range(nb):
        rows = slice(CHUNK * bi, CHUNK * (bi + 1))
        g_ref[bi] = g[rows]
        bonus_ref[bi] = bonus[rows]

    masks = _wkv_masks(CHUNK)
    chains = [(bi, gi) for bi in range(nb) for gi in range(N_GROUPS)]
    lanes = lambda gi: slice(MXU_DIM_V7X * gi, MXU_DIM_V7X * (gi + 1))
    groups = [tuple(x[CHUNK * bi:CHUNK * (bi + 1), lanes(gi)] for x in scan_ops)
              + (s_ref[bi, gi],) for bi, gi in chains]
    for (bi, gi), (y, s_new) in zip(chains, _wkv_chunk(groups, masks)):
        y_ref[bi, :, lanes(gi)] = y
        s_ref[bi, gi] = s_new

    @pl.when(ci == pl.num_programs(0) - 1)
    def _():
        for bi, gi in chains:
            s_bd = s_ref[bi, gi]
            for h in range(HEADS_PER_GROUP):
                hs = slice(HEAD_DIM * h, HEAD_DIM * (h + 1))
                sout_ref[bi, HEADS_PER_GROUP * gi + h] = s_bd[hs, hs]


def _wkv_prompt(zparts, b, t, prep_wts, weights):
    steps = t // CHUNK
    assert len(prep_wts) == N_PREP_WEIGHTS
    seq_rows = lambda w, bi: pl.BlockSpec((CHUNK, w), lambda ci: (bi * steps + ci, 0))
    spec = pl.BlockSpec((b, CHUNK, D_RWKV), lambda ci: (0, ci, 0))
    state = (b, N_GROUPS, MXU_DIM_V7X, MXU_DIM_V7X)
    out_state = (b, N_HEADS, HEAD_DIM, HEAD_DIM)
    bf16_rows = 2 * SLAB
    assert all(w.shape[0] % (steps * bf16_rows) == 0 for w in weights)
    slabs = [pl.BlockSpec((w.shape[0] // steps, w.shape[1]), lambda ci: (ci, 0)) for w in weights]
    rows = jax.ShapeDtypeStruct((b, t, D_RWKV), F32)
    y, bonus, g, s, *wb = pl.pallas_call(
        functools.partial(_wkv_prompt_kernel, b, len(zparts), len(weights)),
        grid=(steps,),
        in_specs=[seq_rows(z.shape[1], bi) for bi in range(b) for z in zparts]
        + [_const_spec(w.shape) for w in prep_wts] + slabs,
        out_specs=[spec, spec, spec, _const_spec(out_state)] + slabs,
        out_shape=[rows, rows, rows, jax.ShapeDtypeStruct(out_state, F32)]
        + [jax.ShapeDtypeStruct(w.shape, BF16) for w in weights],
        scratch_shapes=[pltpu.VMEM(state, F32), pltpu.VMEM((b, SLAB, D_SHIFT_PAD), F32)],
        compiler_params=_params("arbitrary"),
        name="wkv_prompt",
    )(*(list(zparts) * b), *prep_wts, *weights)
    return y, bonus, g, s, wb


HEAD_PAIR = 2 * HEAD_DIM


def _wkv_steps_kernel(seq, r_ref, lw_ref, k2_ref, v_ref, as_ref, bs_ref, s0_ref, y_ref, sout_ref,
                      op_ref, vt_ref, yt_ref):
    nb = r_ref.shape[0] // seq
    for t in range(seq):
        rows = pl.ds(t, nb, stride=seq)
        op_ref[0, t] = r_ref[rows, :].T
        op_ref[1, t] = jnp.exp(lw_ref[rows, :]).T
        op_ref[2, t] = k2_ref[rows, :].T
        op_ref[3, t] = as_ref[rows, :].T
        op_ref[4, t] = bs_ref[rows, :].T
        vt_ref[t] = v_ref[rows, :].T

    for hh in range(HEAD_PAIR // HEAD_DIM):
        ks = slice(HEAD_DIM * hh, HEAD_DIM * (hh + 1))

        def body(vi, carry):
            s = s0_ref[hh, vi]
            for t in range(seq):
                sa = jnp.sum(s * op_ref[3, t, ks, :], axis=0, keepdims=True)
                vrow = vt_ref[t, pl.ds(HEAD_DIM * hh + vi, 1), :]
                s = s * op_ref[1, t, ks, :] + sa * op_ref[4, t, ks, :] + vrow * op_ref[2, t, ks, :]
                yt_ref[t, pl.ds(HEAD_DIM * hh + vi, 1), :] = jnp.sum(s * op_ref[0, t, ks, :], axis=0,
                                                                      keepdims=True)
            sout_ref[hh, vi] = s
            return carry

        lax.fori_loop(0, HEAD_DIM, body, 0, unroll=16)

    for t in range(seq):
        y_ref[pl.ds(t, nb, stride=seq), :] = yt_ref[t].T


def _wkv_steps(r, lw, k2, v, a_s, b_s, s0_native, seq):
    n = r.shape[0]
    nb = n // seq
    assert s0_native.shape == (N_HEADS, HEAD_DIM, HEAD_DIM, nb) and nb == LANES_V7X
    vec = pl.BlockSpec((n, HEAD_PAIR), lambda i: (0, i))
    st = pl.BlockSpec((HEAD_PAIR // HEAD_DIM, HEAD_DIM, HEAD_DIM, nb), lambda i: (i, 0, 0, 0))
    return pl.pallas_call(
        functools.partial(_wkv_steps_kernel, seq),
        grid=(N_HEADS * HEAD_DIM // HEAD_PAIR,),
        in_specs=[vec] * 6 + [st],
        out_specs=[vec, st],
        out_shape=[jax.ShapeDtypeStruct((n, D_RWKV), F32), jax.ShapeDtypeStruct(s0_native.shape, F32)],
        scratch_shapes=[pltpu.VMEM((5, seq, HEAD_PAIR, nb), F32), pltpu.VMEM((seq, HEAD_PAIR, nb), F32),
                        pltpu.VMEM((seq, HEAD_PAIR, nb), F32)],
        compiler_params=_params("parallel"),
        name="wkv_steps",
    )(r, lw, k2, v, a_s, b_s, s0_native)


POOL_CARRY = 16


def _pool_prompt_tile(zp, carry_ref, tile_in_seq):
    tm = zp.shape[0]

    @pl.when(tile_in_seq == 0)
    def _():
        carry_ref[...] = jnp.zeros_like(carry_ref)

    buf = jnp.concatenate([carry_ref[...], zp], axis=0)
    carry_ref[...] = zp[tm - POOL_CARRY:, :]
    pos = (tile_in_seq * tm + lax.broadcasted_iota(jnp.int32, (tm, POOL_GROUP), 0) + 1).astype(F32)
    ds = []
    for gi, win in enumerate(POOL_WINDOWS):
        sl = slice(POOL_GROUP * gi, POOL_GROUP * (gi + 1))
        acc = buf[:, sl]
        s = 1
        while s < win:
            acc = acc + pltpu.roll(acc, s, 0)
            s *= 2
        ds.append(acc[POOL_CARRY:, :] / jnp.minimum(float(win), pos) - zp[:, sl])
    return jnp.concatenate(ds, axis=1)


def _pool_sample_kernel(seq, hist_ref, zp_ref, d_ref, new_ref):
    nb = hist_ref.shape[1]
    step = lambda t: zp_ref[pl.ds(t, nb, stride=seq), :]
    past = lambda j: step(j) if j >= 0 else hist_ref[POOL_HIST + j]
    tiles_per_group = POOL_GROUP // LANES_V7X
    for gi, win in enumerate(POOL_WINDOWS):
        @pl.when(pl.program_id(0) // tiles_per_group == gi)
        def _():
            for t in range(seq):
                acc = past(t)
                for j in range(1, win):
                    acc = acc + past(t - j)
                d_ref[pl.ds(t, nb, stride=seq), :] = acc / float(min(win, PAST_LEN + 1)) - past(t)
    for j in range(POOL_HIST):
        src = j + seq
        new_ref[j] = hist_ref[src] if src < POOL_HIST else step(src - POOL_HIST)


def _pool_sample(hist, zp, row0, seq):
    nb = hist.shape[1]
    n = nb * seq
    assert row0 % n == 0
    hist_spec = pl.BlockSpec((POOL_HIST, nb, LANES_V7X), lambda j: (0, 0, j))
    return pl.pallas_call(
        functools.partial(_pool_sample_kernel, seq),
        grid=(D_POOL // LANES_V7X,),
        in_specs=[hist_spec, pl.BlockSpec((n, LANES_V7X), lambda j: (row0 // n, j))],
        out_specs=[pl.BlockSpec((n, LANES_V7X), lambda j: (0, j)), hist_spec],
        out_shape=[jax.ShapeDtypeStruct((n, D_POOL), F32), jax.ShapeDtypeStruct(hist.shape, F32)],
        compiler_params=_params("parallel"),
        name="pool_sample",
    )(hist, zp)


def _merge_kernel(seq_tiles, y_ref, bonus_ref, g_ref, d_ref, zga_ref, zgb_ref, x_ref,
                  lw_ref, lb_ref, ones_ref, wa_ref, pw_ref, ps_ref, wb_ref,
                  wo_ref, gpost_ref, gnext_ref, o_ref, h_ref, *carry):
    if seq_tiles is None:
        d = d_ref[...]
    else:
        d = _pool_prompt_tile(d_ref[...], carry[0], pl.program_id(0) % seq_tiles)
    ones_bd = ones_ref[...]
    y = y_ref[...]
    mu = _head_sum(y, ones_bd) * (1.0 / HEAD_DIM)
    yb = jnp.concatenate(
        [_mm(d[:, POOL_GROUP * gi:POOL_GROUP * (gi + 1)], pw_ref[gi]) for gi in range(len(POOL_WINDOWS))],
        axis=1) * ps_ref[...]
    yc = y - mu
    var = _head_sum(yc * yc, ones_bd) * (1.0 / HEAD_DIM)
    mb = jax.nn.sigmoid(zgb_ref[...]) * _mm(yb, wb_ref[...])
    yn = yc * lax.rsqrt(var + GN_EPS) * lw_ref[...] + lb_ref[...]
    ya = (yn + bonus_ref[...]) * g_ref[...]
    m = jax.nn.sigmoid(zga_ref[...]) * _mm(ya, wa_ref[...]) + mb
    x1 = x_ref[...] + _rmsnorm(_mm(m, wo_ref[...]), gpost_ref[...])
    o_ref[...] = x1
    h_ref[...] = _rmsnorm(x1, gnext_ref[...]).astype(BF16)


def _merge(y, bonus, g, d, zpg, x2d, row0, wts, tm, seq_len=None):
    n = y.shape[0]
    assert row0 % tm == 0 and tm % SLAB == 0 and (seq_len is None or (seq_len % tm == 0 and tm >= POOL_CARRY))
    first = row0 // tm
    half = pl.BlockSpec((tm, D_RWKV), lambda i: (i, 0))
    full = pl.BlockSpec((tm, D_MODEL), lambda i: (i, 0))
    gate_at = lambda col: pl.BlockSpec((pl.Element(tm), pl.Element(D_MODEL)), lambda i: (tm * (first + i), col))
    resident = [pl.BlockSpec(w.shape, functools.partial(lambda nd, i: (0,) * nd, w.ndim),
                             pipeline_mode=pl.Buffered(1)) for w in wts]
    return pl.pallas_call(
        functools.partial(_merge_kernel, None if seq_len is None else seq_len // tm),
        grid=(n // tm,),
        in_specs=[half] * 4 + [gate_at(D_POOL), gate_at(D_POOL + D_MODEL)] + [full] + resident,
        out_specs=[full, full],
        out_shape=[jax.ShapeDtypeStruct((n, D_MODEL), F32), jax.ShapeDtypeStruct((n, D_MODEL), BF16)],
        scratch_shapes=[] if seq_len is None else [pltpu.VMEM((POOL_CARRY, D_POOL), F32)],
        compiler_params=_params("parallel" if seq_len is None else "arbitrary"),
        name="merge",
    )(y, bonus, g, d, zpg, zpg, x2d, *wts)


def _gelu_tanh(x):
    return 0.5 * x * (1.0 + jnp.tanh(0.7978845608028654 * (x + 0.044715 * x * x * x)))


def _ffn_body(x_ref, h_ref, wg_ref, wu_ref, cw_ref, cb_ref, wo_ref, gp_ref,
              o_ref, acc_ref, prev1, prev2, emit_tail):
    c = pl.program_id(1)

    @pl.when(c == 0)
    def _():
        acc_ref[...] = jnp.zeros_like(acc_ref)

    h = h_ref[...]
    gate = jnp.dot(h, wg_ref[...], preferred_element_type=F32)
    up = jnp.dot(h, wu_ref[...], preferred_element_type=F32)
    cw = cw_ref[...]
    cv = cb_ref[...] + cw[0:1, :] * prev2(gate) + cw[1:2, :] * prev1(gate) + cw[2:3, :] * gate
    emit_tail(gate)
    acc_ref[...] += _mm(_gelu_tanh(cv) * up, wo_ref[...])

    @pl.when(c == pl.num_programs(1) - 1)
    def _():
        o_ref[...] = x_ref[...] + _rmsnorm(acc_ref[...], gp_ref[...])


def _ffn_prompt_kernel(tiles_per_seq, x_ref, h_ref, wg_ref, wu_ref, cw_ref, cb_ref, wo_ref, gp_ref,
                       o_ref, tail_ref, acc_ref, carry_ref):
    i = pl.program_id(0)
    c = pl.program_id(1)

    @pl.when(i % tiles_per_seq == 0)
    def _():
        carry_ref[c] = jnp.zeros(carry_ref.shape[1:], F32)

    hist = carry_ref[c]

    def shifted(gate, s):
        row = lax.broadcasted_iota(jnp.int32, gate.shape, 0)
        rolled = pltpu.roll(gate, s, 0)
        out = rolled
        for j in range(s):
            out = jnp.where(row == j, hist[8 - s + j:9 - s + j, :], out)
        return out

    def emit_tail(gate):
        tail_ref[...] = gate[gate.shape[0] - tail_ref.shape[0]:, :]

    _ffn_body(x_ref, h_ref, wg_ref, wu_ref, cw_ref, cb_ref, wo_ref, gp_ref, o_ref,
              acc_ref, lambda g: shifted(g, 1), lambda g: shifted(g, 2), emit_tail)
    carry_ref[c] = tail_ref[...]


def _ffn_sample_kernel(seq, x_ref, st_ref, h_ref, wg_ref, wu_ref, cw_ref, cb_ref, wo_ref, gp_ref,
                       o_ref, tail_ref, acc_ref, g_ref, p1_ref, p2_ref):
    nb = st_ref.shape[0]
    st0 = st_ref[:, 0, :]
    st1 = st_ref[:, 1, :]
    prev1 = lambda gate: _replace_step_rows(p1_ref, pltpu.roll(gate, 1, 0), seq, [(0, st1)])
    prev2 = lambda gate: _replace_step_rows(p2_ref, pltpu.roll(gate, 2, 0), seq, [(0, st0), (1, st1)])

    def emit_tail(gate):
        tiles = gate.shape[1] // LANES_V7X
        for j in range(tiles):
            g_ref[j] = gate[:, LANES_V7X * j:LANES_V7X * (j + 1)]
        for s in range(CONV_W - 1):
            rows = pl.ds(seq - (CONV_W - 1) + s, nb, stride=seq)
            tail_ref[:, s, :] = jnp.concatenate([g_ref[j, rows, :] for j in range(tiles)], axis=1)

    _ffn_body(x_ref, h_ref, wg_ref, wu_ref, cw_ref, cb_ref, wo_ref, gp_ref, o_ref,
              acc_ref, prev1, prev2, emit_tail)


def _ffn_prompt(x2d, h, seq_len, wts, tm, fk):
    n = x2d.shape[0]
    nc = D_FF // fk
    w_in, cw, cb, w_out, gp = wts
    full = pl.BlockSpec((tm, D_MODEL), lambda i, c: (i, 0))
    return pl.pallas_call(
        functools.partial(_ffn_prompt_kernel, seq_len // tm),
        grid=(n // tm, nc),
        in_specs=[full, full,
                  pl.BlockSpec((D_MODEL, fk), lambda i, c: (0, c)),
                  pl.BlockSpec((D_MODEL, fk), lambda i, c: (0, c + nc)),
                  pl.BlockSpec((CONV_W, fk), lambda i, c: (0, c)),
                  pl.BlockSpec((1, fk), lambda i, c: (0, c)),
                  pl.BlockSpec((fk, D_MODEL), lambda i, c: (c, 0)),
                  _const_spec(gp.shape)],
        out_specs=[full, pl.BlockSpec((None, 8, fk), lambda i, c: (i, 0, c))],
        out_shape=[jax.ShapeDtypeStruct((n, D_MODEL), F32),
                   jax.ShapeDtypeStruct((n // tm, 8, D_FF), F32)],
        scratch_shapes=[pltpu.VMEM((tm, D_MODEL), F32), pltpu.VMEM((nc, 8, fk), F32)],
        compiler_params=_params("arbitrary", "arbitrary"),
        name="ffn_prompt",
    )(x2d, h, w_in, w_in, cw, cb, w_out, gp)


def _ffn_sample(x2d, h, st, seq, wts, fk):
    n = x2d.shape[0]
    nb = n // seq
    assert seq >= CONV_W - 1 and st.shape == (nb, CONV_W - 1, D_FF)
    nc = D_FF // fk
    w_in, cw, cb, w_out, gp = wts
    full = pl.BlockSpec((n, D_MODEL), lambda i, c: (0, 0))
    cols = pl.BlockSpec((nb, CONV_W - 1, fk), lambda i, c: (0, 0, c))
    return pl.pallas_call(
        functools.partial(_ffn_sample_kernel, seq),
        grid=(1, nc),
        in_specs=[full, cols, full,
                  pl.BlockSpec((D_MODEL, fk), lambda i, c: (0, c)),
                  pl.BlockSpec((D_MODEL, fk), lambda i, c: (0, c + nc)),
                  pl.BlockSpec((CONV_W, fk), lambda i, c: (0, c)),
                  pl.BlockSpec((1, fk), lambda i, c: (0, c)),
                  pl.BlockSpec((fk, D_MODEL), lambda i, c: (c, 0)),
                  _const_spec(gp.shape)],
        out_specs=[full, cols],
        out_shape=[jax.ShapeDtypeStruct((n, D_MODEL), F32), jax.ShapeDtypeStruct(st.shape, F32)],
        scratch_shapes=[pltpu.VMEM((n, D_MODEL), F32)]
        + [pltpu.VMEM((fk // LANES_V7X, n, LANES_V7X), F32)] * 3,
        compiler_params=_params("arbitrary", "arbitrary"),
        name="ffn_sample",
    )(x2d, st, h, w_in, w_in, cw, cb, w_out, gp)


def _row(v):
    return v.reshape(1, -1).astype(F32)


def _ones_bd():
    i = jnp.arange(MXU_DIM_V7X) // HEAD_DIM
    return (i[:, None] == i[None, :]).astype(BF16)


def _layer_weights(l, norm_pre_mix, w_in, mu_shift, w0, w2, a0, a2, g2, k_k, k_a, r_k, lnx_w, lnx_b,
                   w_branch_a, pool_w, pool_scale, w_branch_b, w_out, norm_post_mix,
                   norm_pre_ffn, w_ffn_in, conv_w, conv_b, w_ffn_out, norm_post_ffn):
    w_t = jnp.swapaxes(w_in[l], 0, 1)
    mu = jnp.pad(mu_shift[l], (0, D_SHIFT_PAD - D_SHIFT)).reshape(1, -1)
    w_lora = jnp.zeros((D_LORA_PAD, 3 * D_RWKV), F32)
    w_lora = w_lora.at[0:RANK_W, 0:D_RWKV].set(w2[l])
    w_lora = w_lora.at[RANK_W:RANK_W + RANK_A, D_RWKV:2 * D_RWKV].set(a2[l])
    w_lora = w_lora.at[RANK_W + RANK_A:D_LORA, 2 * D_RWKV:].set(g2[l])
    ones_bd = _ones_bd()

    def with_bf16(wa, pw, wb, wo, wfi, wfo):
        return dict(
            merge=(_row(lnx_w[l]), _row(lnx_b[l]), ones_bd, wa,
                   pw.reshape(pool_w.shape[1:]), _row(pool_scale[l]), wb,
                   wo, _row(norm_post_mix[l]), _row(norm_pre_ffn[l])),
            ffn=(wfi, conv_w[l].astype(F32), _row(conv_b[l]), wfo, _row(norm_post_ffn[l])),
        )

    return dict(
        in_proj=(_row(norm_pre_mix[l]), w_t),
        prep=(mu, w_lora.astype(BF16), _row(w0[l]), _row(a0[l]), _row(k_k[l]), _row(k_a[l]), _row(r_k[l]),
              ones_bd),
        f32_matmul_weights=[w_branch_a[l], pool_w[l].reshape(-1, POOL_GROUP), w_branch_b[l], w_out[l],
                            w_ffn_in[l], w_ffn_out[l]],
        with_bf16=with_bf16,
    )


def _largest_tile(n, cap, mult=16):
    best = None
    for d in range(mult, min(n, cap) + 1, mult):
        if n % d == 0:
            best = d
    assert best is not None, (n, cap)
    return best


def _project(xp2d, xs2d, wts):
    g, w_t = wts["in_proj"]
    tm_norm = _largest_tile(math.gcd(xp2d.shape[0], xs2d.shape[0]), ROWS_NORM)
    h, (zr, zl) = _pre_norm(xp2d, xs2d, g, w_t, [(0, D_RWKV), (3 * D_RWKV, D_LORA_PAD)], tm_norm)
    tm = _largest_tile(h.shape[0], ROWS_PROJ)
    zkv = _in_proj(h, w_t, D_RWKV, 2 * D_RWKV, COLS_PROJ, tm)
    zpg = _in_proj(h, w_t, D_SHIFT, D_POOL + 2 * D_MODEL, COLS_PROJ, tm)
    return (zr, zkv, zl), zpg


def _last_shift_row(zparts, row0, b, t):
    if b <= 8:
        last = lambda z: jnp.concatenate(
            [lax.slice(z, (row0 + (i + 1) * t - 1, 0), (row0 + (i + 1) * t, z.shape[1])) for i in range(b)])
    else:
        last = lambda z: lax.slice(z, (row0 + t - 1, 0), (row0 + b * t, z.shape[1]), (t, 1))
    return jnp.concatenate([last(z) for z in zparts], axis=-1)[:, None, :D_SHIFT]


def _prompt_layer(x, z, wts):
    b, t, _ = x.shape
    n = b * t
    x2d = x.reshape(n, D_MODEL)
    zparts, zpg = z
    y, bonus, g, wkv, bf16_weights = _wkv_prompt(zparts, b, t, wts["prep"], wts["f32_matmul_weights"])
    wts = wts["with_bf16"](*bf16_weights)
    flat = lambda a: a.reshape(n, a.shape[-1])
    x1, h1 = _merge(flat(y), flat(bonus), flat(g), zpg, zpg, x2d, 0, wts["merge"],
                    tm=_largest_tile(t, ROWS_MERGE), seq_len=t)
    ffn_tm = _largest_tile(t, ROWS_FFN)
    out, tail = _ffn_prompt(x1, h1, t, wts["ffn"], tm=ffn_tm, fk=COLS_FFN)
    shift = _last_shift_row(zparts, 0, b, t)
    pool = jnp.stack([zpg[(i + 1) * t - POOL_HIST:(i + 1) * t, :D_POOL] for i in range(b)])
    tiles = t // ffn_tm
    conv = tail.reshape(b, tiles, 8, D_FF)[:, tiles - 1, 8 - (CONV_W - 1):, :]
    return wts, (out.reshape(b, t, D_MODEL), shift, wkv, pool, conv)


def _sample_layer(x, z, row0, st_shift, st_wkv, st_pool, st_conv, prep_wts, wts):
    b, t, _ = x.shape
    n = b * t
    x2d = x.reshape(n, D_MODEL)
    zparts, zpg = z
    st = jnp.pad(st_shift.reshape(b, D_SHIFT), ((0, 0), (0, D_SHIFT_PAD - D_SHIFT)))
    tm = _largest_tile(math.gcd(row0, n), ROWS_PREP_SAMPLE, mult=SLAB * t)
    r, lw, k2, v, a_s, b_s, g, bonus = _prep_sample(zparts, row0, n, st, t, prep_wts, tm=tm)
    y, wkv = _wkv_steps(r, lw, k2, v, a_s, b_s, jnp.transpose(st_wkv, (1, 2, 3, 0)), t)
    wkv = jnp.transpose(wkv, (3, 0, 1, 2))
    d, pool = _pool_sample(jnp.swapaxes(st_pool, 0, 1), zpg, row0, t)
    pool = jnp.swapaxes(pool, 0, 1)
    x1, h1 = _merge(y, bonus, g, d, zpg, x2d, row0, wts["merge"],
                    tm=_largest_tile(math.gcd(row0, n), ROWS_MERGE))
    out, conv = _ffn_sample(x1, h1, st_conv, t, wts["ffn"], fk=COLS_FFN)
    shift = _last_shift_row(zparts, row0, b, t)
    return out.reshape(b, t, D_MODEL), shift, wkv, pool, conv


def kernel(x_prompt, x_sample, state_shift, state_wkv, state_pool, state_conv, norm_pre_mix, w_in, mu_shift, w0, w2, a0, a2, g2, k_k, k_a, r_k, lnx_w, lnx_b, w_branch_a, pool_w, pool_scale, w_branch_b, w_out, norm_post_mix, norm_pre_ffn, w_ffn_in, conv_w, conv_b, w_ffn_out, norm_post_ffn):
    weights = (norm_pre_mix, w_in, mu_shift, w0, w2, a0, a2, g2, k_k, k_a, r_k, lnx_w, lnx_b,
               w_branch_a, pool_w, pool_scale, w_branch_b, w_out, norm_post_mix,
               norm_pre_ffn, w_ffn_in, conv_w, conv_b, w_ffn_out, norm_post_ffn)
    depth = w_in.shape[0]
    yp, ys = x_prompt, x_sample
    p_states, s_states = [], []
    for l in range(depth):
        wts = _layer_weights(l, *weights)
        n_prompt = yp.shape[0] * yp.shape[1]
        z = _project(yp.reshape(n_prompt, D_MODEL), ys.reshape(-1, D_MODEL), wts)
        bf16_wts, (yp, *ps) = _prompt_layer(yp, z, wts)
        ys, *ss = _sample_layer(ys, z, n_prompt, state_shift[l], state_wkv[l], state_pool[l], state_conv[l],
                                wts["prep"], bf16_wts)
        p_states.append(ps)
        s_states.append(ss)
    stack = lambda states, i: jnp.stack([s[i] for s in states])
    return (yp, ys,
            stack(p_states, 0), stack(p_states, 1), stack(p_states, 2), stack(p_states, 3),
            stack(s_states, 0), stack(s_states, 1), stack(s_states, 2), stack(s_states, 3))
```

```python
import functools
import math

import jax
import jax.numpy as jnp
from jax import lax
from jax.experimental import pallas as pl
from jax.experimental.pallas import tpu as pltpu

F32 = jnp.float32
BF16 = jnp.bfloat16

D_MODEL = 2048
HEAD_DIM = 64
D_RWKV = 1024
N_HEADS = 16
RANK_W, RANK_A, RANK_G = 64, 64, 160
D_LORA = RANK_W + RANK_A + RANK_G
D_LORA_PAD = 384
D_SHIFT = 3 * D_RWKV + D_LORA
D_SHIFT_PAD = 3 * D_RWKV + D_LORA_PAD
D_POOL = 1024
POOL_WINDOWS = (2, 4, 8, 16)
POOL_GROUP = 256
POOL_HIST = 15
D_FF = 5632
CONV_W = 3
NORM_EPS = 1e-6
GN_EPS = 64e-5
PAST_LEN = 16384

LANES_V7X = 128
SLAB = 8
MXU_DIM_V7X = 256
HEADS_PER_GROUP = MXU_DIM_V7X // HEAD_DIM
N_GROUPS = N_HEADS // HEADS_PER_GROUP
CHUNK = 64
VMEM_LIMIT_V7X = 56 * 1024 * 1024

ROWS_NORM = 512
ROWS_PROJ = 1088
COLS_PROJ = 1024
ROWS_MERGE = 256
ROWS_FFN = 512
COLS_FFN = 512
ROWS_PREP_SAMPLE = 128


def _params(*sem):
    return pltpu.CompilerParams(dimension_semantics=sem, vmem_limit_bytes=VMEM_LIMIT_V7X)


def _mm(a, b):
    return jnp.dot(a.astype(BF16), b.astype(BF16), preferred_element_type=F32)


def _mm_nt(a, b):
    return lax.dot_general(a.astype(BF16), b.astype(BF16), (((1,), (1,)), ((), ())),
                           preferred_element_type=F32)


def _split_hi_lo(x):
    hi = x.astype(BF16)
    lo = (x - hi.astype(F32)).astype(BF16)
    return hi, lo


def _head_sum(x, ones_bd):
    hi, lo = _split_hi_lo(x)
    outs = []
    for gi in range(x.shape[1] // MXU_DIM_V7X):
        sl = slice(MXU_DIM_V7X * gi, MXU_DIM_V7X * (gi + 1))
        outs.append(jnp.dot(hi[:, sl], ones_bd, preferred_element_type=F32)
                    + jnp.dot(lo[:, sl], ones_bd, preferred_element_type=F32))
    return jnp.concatenate(outs, axis=1)


def _rmsnorm(x, g):
    return x * lax.rsqrt(jnp.mean(x * x, axis=-1, keepdims=True) + NORM_EPS) * g


def _replace_step_rows(scr_ref, base, seq, steps):
    nb = base.shape[0] // seq
    for j in range(base.shape[1] // LANES_V7X):
        ls = slice(LANES_V7X * j, LANES_V7X * (j + 1))
        scr_ref[j] = base[:, ls]
        for t, vals in steps:
            scr_ref[j, pl.ds(t, nb, stride=seq), :] = vals[:, ls]
    return jnp.concatenate([scr_ref[j] for j in range(base.shape[1] // LANES_V7X)], axis=1)


def _pre_norm_kernel(prompt_tiles, n_sections, xp_ref, xs_ref, g_ref, *refs):
    w_refs, h_ref, z_refs, wb_refs = (refs[:n_sections], refs[n_sections],
                                      refs[n_sections + 1:2 * n_sections + 1], refs[2 * n_sections + 1:])
    i = pl.program_id(0)

    @pl.when(i == 0)
    def _():
        for w_ref, wb_ref in zip(w_refs, wb_refs):
            wb_ref[...] = w_ref[...].astype(BF16)

    @pl.when(i < prompt_tiles)
    def _():
        h_ref[...] = _rmsnorm(xp_ref[...], g_ref[...]).astype(BF16)

    @pl.when(i >= prompt_tiles)
    def _():
        h_ref[...] = _rmsnorm(xs_ref[...], g_ref[...]).astype(BF16)

    h = h_ref[...]
    for z_ref, wb_ref in zip(z_refs, wb_refs):
        z_ref[...] = lax.dot_general(h, wb_ref[...], (((1,), (1,)), ((), ())), preferred_element_type=F32)


def _pre_norm(xp, xs, g, w_t, sections, tm):
    pt, st = xp.shape[0] // tm, xs.shape[0] // tm
    n = xp.shape[0] + xs.shape[0]
    assert all(off % SLAB == 0 for off, _ in sections)
    rows = pl.BlockSpec((tm, D_MODEL), lambda i: (i, 0))
    h, *zs = pl.pallas_call(
        functools.partial(_pre_norm_kernel, pt, len(sections)),
        grid=(pt + st,),
        in_specs=[pl.BlockSpec((tm, D_MODEL), lambda i: (jnp.minimum(i, pt - 1), 0)),
                  pl.BlockSpec((tm, D_MODEL), lambda i: (jnp.maximum(i - pt, 0), 0)),
                  _const_spec(g.shape)]
        + [pl.BlockSpec((pl.Element(width), pl.Element(D_MODEL)),
                        functools.partial(lambda o, i: (SLAB * o, 0), off // SLAB),
                        pipeline_mode=pl.Buffered(1)) for off, width in sections],
        out_specs=[rows] + [pl.BlockSpec((tm, width), lambda i: (i, 0)) for _, width in sections],
        out_shape=[jax.ShapeDtypeStruct((n, D_MODEL), BF16)]
        + [jax.ShapeDtypeStruct((n, width), F32) for _, width in sections],
        scratch_shapes=[pltpu.VMEM((width, D_MODEL), BF16) for _, width in sections],
        compiler_params=_params("arbitrary"),
        name="pre_norm",
    )(xp, xs, g, *([w_t] * len(sections)))
    return h, zs


def _in_proj_kernel(h_ref, w_ref, o_ref, wb_ref):
    @pl.when(pl.program_id(1) == 0)
    def _():
        wb_ref[...] = w_ref[...].astype(BF16)

    o_ref[...] = lax.dot_general(h_ref[...], wb_ref[...], (((1,), (1,)), ((), ())),
                                 preferred_element_type=F32)


def _in_proj(h, w_t, off, width, tn, tm):
    n = h.shape[0]
    assert width % tn == 0 and n % tm == 0
    return pl.pallas_call(
        _in_proj_kernel,
        grid=(width // tn, n // tm),
        in_specs=[pl.BlockSpec((tm, D_MODEL), lambda j, i: (i, 0)),
                  pl.BlockSpec((pl.Element(tn), pl.Element(D_MODEL)),
                               lambda j, i: (SLAB * (off // SLAB + j * (tn // SLAB)), 0))],
        out_specs=pl.BlockSpec((tm, tn), lambda j, i: (i, j)),
        out_shape=jax.ShapeDtypeStruct((n, width), F32),
        scratch_shapes=[pltpu.VMEM((tn, D_MODEL), BF16)],
        compiler_params=_params("arbitrary", "arbitrary"),
        name="in_proj",
    )(h, w_t)


def _prep_math(zs, prev, mu, w_lora, w0, a0, k_k, k_a, r_k, ones_bd):
    f = zs + (prev - zs) * mu
    r = f[:, 0:D_RWKV]
    k = f[:, D_RWKV:2 * D_RWKV]
    v = f[:, 2 * D_RWKV:3 * D_RWKV]
    low = f[:, 3 * D_RWKV:3 * D_RWKV + D_LORA_PAD]
    lane = lax.broadcasted_iota(jnp.int32, low.shape, 1)
    act = jnp.where(lane < RANK_W, jnp.tanh(low),
                    jnp.where(lane < RANK_W + RANK_A, low,
                              jnp.where(lane < D_LORA, jax.nn.sigmoid(low), 0.0)))
    lora = _mm(act, w_lora)
    u = w0 + lora[:, 0:D_RWKV]
    lw = -math.exp(-0.5) * jax.nn.sigmoid(u)
    a = jax.nn.sigmoid(a0 + lora[:, D_RWKV:2 * D_RWKV])
    g = lora[:, 2 * D_RWKV:3 * D_RWKV]
    kk = k * k_k
    kk = kk * jnp.minimum(lax.rsqrt(_head_sum(kk * kk, ones_bd)), 1e12)
    k2 = k * (1.0 + (a - 1.0) * k_a)
    bonus = _head_sum(r * k2 * r_k, ones_bd) * v
    return r, lw, k2, v, -kk, kk * a, g, bonus


N_PREP_WEIGHTS = 8


def _prep_sample_kernel(seq, n_parts, *refs):
    z_refs, st_ref, refs = refs[:n_parts], refs[n_parts], refs[n_parts + 1:]
    w_refs, o_refs, prev_ref = refs[:N_PREP_WEIGHTS], refs[N_PREP_WEIGHTS:-1], refs[-1]
    zs = jnp.concatenate([z[...] for z in z_refs], axis=1)
    prev = _replace_step_rows(prev_ref, pltpu.roll(zs, 1, 0), seq, [(0, st_ref[...])])
    for o_ref, o in zip(o_refs, _prep_math(zs, prev, *[w[...] for w in w_refs])):
        o_ref[...] = o


def _const_spec(shape):
    nd = len(shape)
    return pl.BlockSpec(shape, lambda *_: (0,) * nd)


def _prep_sample(zparts, row0, n, st, seq, wts, tm):
    assert tm % (SLAB * seq) == 0 and n % tm == 0 and row0 % tm == 0 and len(wts) == N_PREP_WEIGHTS
    first = row0 // tm
    outs = [jax.ShapeDtypeStruct((n, D_RWKV), F32)] * 8
    row_spec = pl.BlockSpec((tm, D_RWKV), lambda i: (i, 0))
    return pl.pallas_call(
        functools.partial(_prep_sample_kernel, seq, len(zparts)),
        grid=(n // tm,),
        in_specs=[pl.BlockSpec((tm, z.shape[1]), lambda i: (first + i, 0)) for z in zparts]
        + [pl.BlockSpec((tm // seq, D_SHIFT_PAD), lambda i: (i, 0))]
        + [_const_spec(w.shape) for w in wts],
        out_specs=[row_spec] * 8,
        out_shape=outs,
        scratch_shapes=[pltpu.VMEM((D_SHIFT_PAD // LANES_V7X, tm, LANES_V7X), F32)],
        compiler_params=_params("parallel"),
        name="prep_sample",
    )(*zparts, st, *wts)


def _expand_bd(a, bd_mask):
    return jnp.where(bd_mask, jnp.concatenate([a] * HEADS_PER_GROUP, axis=0), 0.0)


def _chunk_cumsum(x):
    row = lax.broadcasted_iota(jnp.int32, x.shape, 0)
    s = 1
    while s < x.shape[0]:
        x = x + jnp.where(row >= s, pltpu.roll(x, s, 0), 0.0)
        s *= 2
    return x


def _wkv_intra(xs, bts, kts, vs, pzs, masks):
    bd_mask, strict_mask, incl_mask, eye_c, head_masks = masks
    c = vs[0].shape[0]
    ystacks = [jnp.concatenate([jnp.where(m, bt, 0.0) for m in head_masks]
                               + [jnp.where(m, kt, 0.0) for m in head_masks], axis=0)
               for bt, kt in zip(bts, kts)]
    grams = [_mm_nt(x, ys) for x, ys in zip(xs, ystacks)]
    l_abs = [jnp.where(strict_mask, g[0:c, 0:4 * c], 0.0) for g in grams]
    l_aks = [jnp.where(strict_mask, g[0:c, 4 * c:8 * c], 0.0) for g in grams]
    m_rbs = [jnp.where(incl_mask, g[c:2 * c, 0:4 * c], 0.0) for g in grams]
    m_rks = [jnp.where(incl_mask, g[c:2 * c, 4 * c:8 * c], 0.0) for g in grams]

    tinvs = [eye_c + l for l in l_abs]
    lps = [_mm(l, _expand_bd(l, bd_mask)) for l in l_abs]
    p = 2
    while True:
        rhss = [_expand_bd(lp, bd_mask) for lp in lps]
        if 2 * p >= c:
            tinvs = [t + _mm(t, rhs) for t, rhs in zip(tinvs, rhss)]
            break
        ress = [_mm(jnp.concatenate([lp, t], axis=0), rhs) for lp, t, rhs in zip(lps, tinvs, rhss)]
        lps = [res[0:c] for res in ress]
        tinvs = [t + res[c:2 * c] for t, res in zip(tinvs, ress)]
        p *= 2

    v_bds = [_expand_bd(v, bd_mask) for v in vs]
    ws = [pz[0:c] + _mm(l_ak, v_bd) for pz, l_ak, v_bd in zip(pzs, l_aks, v_bds)]
    us = [_mm(t, _expand_bd(w, bd_mask)) for t, w in zip(tinvs, ws)]
    ys = [pz[c:2 * c] + _mm(jnp.concatenate([m_rb, m_rk], axis=1),
                            jnp.concatenate([_expand_bd(u, bd_mask), v_bd], axis=0))
          for pz, m_rb, m_rk, u, v_bd in zip(pzs, m_rbs, m_rks, us, v_bds)]
    return ys, us


def _wkv_chunk(groups, masks):
    c = groups[0][0].shape[0]
    cls = [_chunk_cumsum(g[1]) for g in groups]
    xs = [jnp.concatenate([g[4] * jnp.exp(cl - g[1]), g[0] * jnp.exp(cl)], axis=0)
          for g, cl in zip(groups, cls)]
    pzs = [_mm_nt(x, g[6]) for x, g in zip(xs, groups)]
    e_negs = [jnp.exp(-cl) for cl in cls]
    ys, us = _wkv_intra(xs, [g[5] * e for g, e in zip(groups, e_negs)],
                        [g[2] * e for g, e in zip(groups, e_negs)], [g[3] for g in groups], pzs, masks)
    dss = []
    for g, cl, u in zip(groups, cls, us):
        e_rem = jnp.exp(cl[c - 1:c, :] - cl)
        uv_t = jnp.concatenate([u, g[3]], axis=0).T
        bk = jnp.concatenate([g[5] * e_rem, g[2] * e_rem], axis=0)
        dss.append(_mm(uv_t, bk))
    return [(y, g[6] * jnp.exp(cl[c - 1:c, :]) + jnp.where(masks[0], ds, 0.0))
            for y, g, cl, ds in zip(ys, groups, cls, dss)]


def _wkv_masks(c):
    n = HEADS_PER_GROUP * c
    rr = lax.broadcasted_iota(jnp.int32, (n, n), 0)
    cc = lax.broadcasted_iota(jnp.int32, (n, n), 1)
    bd_mask = (rr // c) == (cc // c)
    t = lax.broadcasted_iota(jnp.int32, (c, n), 0)
    s = lax.broadcasted_iota(jnp.int32, (c, n), 1) % c
    strict_mask = t > s
    incl_mask = t >= s
    eye_c = jnp.where(t == s, 1.0, 0.0).astype(F32)
    lane = lax.broadcasted_iota(jnp.int32, (c, MXU_DIM_V7X), 1)
    head_masks = [(lane // HEAD_DIM) == h for h in range(HEADS_PER_GROUP)]
    return bd_mask, strict_mask, incl_mask, eye_c, head_masks


def _wkv_prompt_kernel(nb, n_parts, n_casts, *refs):
    it = iter(refs)
    take = lambda k: [next(it) for _ in range(k)]
    z_refs = [take(n_parts) for _ in range(nb)]
    pw_refs, w_refs = take(N_PREP_WEIGHTS), take(n_casts)
    y_ref, bonus_ref, g_ref, sout_ref = take(4)
    wb_refs = take(n_casts)
    s_ref, carry_ref = take(2)
    ci = pl.program_id(0)

    @pl.when(ci == 0)
    def _():
        s_ref[...] = jnp.zeros_like(s_ref)
        carry_ref[...] = jnp.zeros_like(carry_ref)

    for w_ref, wb_ref in zip(w_refs, wb_refs):
        wb_ref[...] = w_ref[...].astype(BF16)

    zs_parts, prev_parts = [], []
    for bi in range(nb):
        zs = jnp.concatenate([z[...] for z in z_refs[bi]], axis=1)
        row = lax.broadcasted_iota(jnp.int32, zs.shape, 0)
        prev_parts.append(jnp.where(row == 0, carry_ref[bi, 0:1, :], pltpu.roll(zs, 1, 0)))
        carry_ref[bi, 0:1, :] = zs[CHUNK - 1:, :]
        zs_parts.append(zs)
    *scan_ops, g, bonus = _prep_math(
        jnp.concatenate(zs_parts, axis=0), jnp.concatenate(prev_parts, axis=0), *[w[...] for w in pw_refs])
    for bi in range(nb):
        rows = slice(CHUNK * bi, CHUNK * (bi + 1))
        g_ref[bi] = g[rows]
        bonus_ref[bi] = bonus[rows]

    masks = _wkv_masks(CHUNK)
    chains = [(bi, gi) for bi in range(nb) for gi in range(N_GROUPS)]
    lanes = lambda gi: slice(MXU_DIM_V7X * gi, MXU_DIM_V7X * (gi + 1))
    groups = [tuple(x[CHUNK * bi:CHUNK * (bi + 1), lanes(gi)] for x in scan_ops)
              + (s_ref[bi, gi],) for bi, gi in chains]
    for (bi, gi), (y, s_new) in zip(chains, _wkv_chunk(groups, masks)):
        y_ref[bi, :, lanes(gi)] = y
        s_ref[bi, gi] = s_new

    @pl.when(ci == pl.num_programs(0) - 1)
    def _():
        for bi, gi in chains:
            s_bd = s_ref[bi, gi]
            for h in range(HEADS_PER_GROUP):
                hs = slice(HEAD_DIM * h, HEAD_DIM * (h + 1))
                sout_ref[bi, HEADS_PER_GROUP * gi + h] = s_bd[hs, hs]


def _wkv_prompt(zparts, b, t, prep_wts, weights):
    steps = t // CHUNK
    assert len(prep_wts) == N_PREP_WEIGHTS
    seq_rows = lambda w, bi: pl.BlockSpec((CHUNK, w), lambda ci: (bi * steps + ci, 0))
    spec = pl.BlockSpec((b, CHUNK, D_RWKV), lambda ci: (0, ci, 0))
    state = (b, N_GROUPS, MXU_DIM_V7X, MXU_DIM_V7X)
    out_state = (b, N_HEADS, HEAD_DIM, HEAD_DIM)
    bf16_rows = 2 * SLAB
    assert all(w.shape[0] % (steps * bf16_rows) == 0 for w in weights)
    slabs = [pl.BlockSpec((w.shape[0] // steps, w.shape[1]), lambda ci: (ci, 0)) for w in weights]
    rows = jax.ShapeDtypeStruct((b, t, D_RWKV), F32)
    y, bonus, g, s, *wb = pl.pallas_call(
        functools.partial(_wkv_prompt_kernel, b, len(zparts), len(weights)),
        grid=(steps,),
        in_specs=[seq_rows(z.shape[1], bi) for bi in range(b) for z in zparts]
        + [_const_spec(w.shape) for w in prep_wts] + slabs,
        out_specs=[spec, spec, spec, _const_spec(out_state)] + slabs,
        out_shape=[rows, rows, rows, jax.ShapeDtypeStruct(out_state, F32)]
        + [jax.ShapeDtypeStruct(w.shape, BF16) for w in weights],
        scratch_shapes=[pltpu.VMEM(state, F32), pltpu.VMEM((b, SLAB, D_SHIFT_PAD), F32)],
        compiler_params=_params("arbitrary"),
        name="wkv_prompt",
    )(*(list(zparts) * b), *prep_wts, *weights)
    return y, bonus, g, s, wb


HEAD_PAIR = 2 * HEAD_DIM


STEP_ROWS = HEAD_DIM // 2
STEP_UNITS = N_HEADS * HEAD_DIM // STEP_ROWS


def _wkv_steps_unit(seq, unit, r_ref, lw_ref, k2_ref, v_ref, as_ref, bs_ref, s0_ref, y_ref, sout_ref,
                    op_ref, vt_ref, yt_ref, companion=lambda: None):
    nb = r_ref.shape[0] // seq
    sub = unit % 4

    @pl.when(sub == 0)
    def _():
        for t in range(seq):
            rows = pl.ds(t, nb, stride=seq)
            op_ref[0, t] = r_ref[rows, :].T
            op_ref[1, t] = jnp.exp(lw_ref[rows, :]).T
            op_ref[2, t] = k2_ref[rows, :].T
            op_ref[3, t] = as_ref[rows, :].T
            op_ref[4, t] = bs_ref[rows, :].T
            vt_ref[t] = v_ref[rows, :].T

    companion()
    k0 = pl.multiple_of((sub // 2) * HEAD_DIM, HEAD_DIM)
    row0 = k0 + (sub % 2) * STEP_ROWS
    ks = pl.ds(k0, HEAD_DIM)
    for vi in range(STEP_ROWS):
        s = s0_ref[0, vi]
        for t in range(seq):
            sa = jnp.sum(s * op_ref[3, t, ks, :], axis=0, keepdims=True)
            vrow = vt_ref[t, pl.ds(row0 + vi, 1), :]
            s = s * op_ref[1, t, ks, :] + sa * op_ref[4, t, ks, :] + vrow * op_ref[2, t, ks, :]
            yt_ref[t, pl.ds(row0 + vi, 1), :] = jnp.sum(s * op_ref[0, t, ks, :], axis=0, keepdims=True)
        sout_ref[0, vi] = s

    @pl.when(sub == 3)
    def _():
        for t in range(seq):
            y_ref[pl.ds(t, nb, stride=seq), :] = yt_ref[t].T


def _in_proj_wkv_kernel(seq, steps_per_tile, h_ref, w_ref, r_ref, lw_ref, k2_ref, v_ref, as_ref, bs_ref,
                        s0_ref, o_ref, y_ref, sout_ref, wb_ref, op_ref, vt_ref, yt_ref):
    j, i = pl.program_id(0), pl.program_id(1)

    @pl.when(i == 0)
    def _():
        wb_ref[...] = w_ref[...].astype(BF16)

    def project():
        o_ref[...] = lax.dot_general(h_ref[...], wb_ref[...], (((1,), (1,)), ((), ())),
                                     preferred_element_type=F32)

    unit = jnp.minimum(j * steps_per_tile + i, STEP_UNITS - 1)
    _wkv_steps_unit(seq, unit, r_ref, lw_ref, k2_ref, v_ref, as_ref, bs_ref, s0_ref, y_ref, sout_ref,
                    op_ref, vt_ref, yt_ref, companion=project)


def _in_proj_wkv(h, w_t, off, width, tn, tm, r, lw, k2, v, a_s, b_s, s0_native, seq):
    n, n_s = h.shape[0], r.shape[0]
    nb = n_s // seq
    tiles = n // tm
    assert width % tn == 0 and n % tm == 0 and (width // tn) * tiles >= STEP_UNITS
    assert s0_native.shape == (N_HEADS, HEAD_DIM, HEAD_DIM, nb) and nb == LANES_V7X
    unit = lambda j, i: jnp.minimum(j * tiles + i, STEP_UNITS - 1)
    vec = pl.BlockSpec((n_s, HEAD_PAIR), lambda j, i: (0, unit(j, i) // 4))
    slab = pl.BlockSpec((1, STEP_ROWS, HEAD_DIM, nb), lambda j, i: (unit(j, i) // 2, unit(j, i) % 2, 0, 0))
    return pl.pallas_call(
        functools.partial(_in_proj_wkv_kernel, seq, tiles),
        grid=(width // tn, tiles),
        in_specs=[pl.BlockSpec((tm, D_MODEL), lambda j, i: (i, 0)),
                  pl.BlockSpec((pl.Element(tn), pl.Element(D_MODEL)),
                               lambda j, i: (SLAB * (off // SLAB + j * (tn // SLAB)), 0))]
        + [vec] * 6 + [slab],
        out_specs=[pl.BlockSpec((tm, tn), lambda j, i: (i, j)), vec, slab],
        out_shape=[jax.ShapeDtypeStruct((n, width), F32), jax.ShapeDtypeStruct((n_s, D_RWKV), F32),
                   jax.ShapeDtypeStruct(s0_native.shape, F32)],
        scratch_shapes=[pltpu.VMEM((tn, D_MODEL), BF16), pltpu.VMEM((5, seq, HEAD_PAIR, nb), F32),
                        pltpu.VMEM((seq, HEAD_PAIR, nb), F32), pltpu.VMEM((seq, HEAD_PAIR, nb), F32)],
        compiler_params=_params("arbitrary", "arbitrary"),
        name="in_proj_wkv",
    )(h, w_t, r, lw, k2, v, a_s, b_s, s0_native)


POOL_CARRY = 16


def _pool_prompt_tile(zp, carry_ref, tile_in_seq):
    tm = zp.shape[0]

    @pl.when(tile_in_seq == 0)
    def _():
        carry_ref[...] = jnp.zeros_like(carry_ref)

    buf = jnp.concatenate([carry_ref[...], zp], axis=0)
    carry_ref[...] = zp[tm - POOL_CARRY:, :]
    pos = (tile_in_seq * tm + lax.broadcasted_iota(jnp.int32, (tm, POOL_GROUP), 0) + 1).astype(F32)
    ds = []
    for gi, win in enumerate(POOL_WINDOWS):
        sl = slice(POOL_GROUP * gi, POOL_GROUP * (gi + 1))
        acc = buf[:, sl]
        s = 1
        while s < win:
            acc = acc + pltpu.roll(acc, s, 0)
            s *= 2
        ds.append(acc[POOL_CARRY:, :] / jnp.minimum(float(win), pos) - zp[:, sl])
    return jnp.concatenate(ds, axis=1)


def _pool_sample_kernel(seq, hist_ref, zp_ref, d_ref, new_ref):
    nb = hist_ref.shape[1]
    step = lambda t: zp_ref[pl.ds(t, nb, stride=seq), :]
    past = lambda j: step(j) if j >= 0 else hist_ref[POOL_HIST + j]
    tiles_per_group = POOL_GROUP // LANES_V7X
    for gi, win in enumerate(POOL_WINDOWS):
        @pl.when(pl.program_id(0) // tiles_per_group == gi)
        def _():
            for t in range(seq):
                acc = past(t)
                for j in range(1, win):
                    acc = acc + past(t - j)
                d_ref[pl.ds(t, nb, stride=seq), :] = acc / float(min(win, PAST_LEN + 1)) - past(t)
    for j in range(POOL_HIST):
        src = j + seq
        new_ref[j] = hist_ref[src] if src < POOL_HIST else step(src - POOL_HIST)


def _pool_sample(hist, zp, row0, seq):
    nb = hist.shape[1]
    n = nb * seq
    assert row0 % n == 0
    hist_spec = pl.BlockSpec((POOL_HIST, nb, LANES_V7X), lambda j: (0, 0, j))
    return pl.pallas_call(
        functools.partial(_pool_sample_kernel, seq),
        grid=(D_POOL // LANES_V7X,),
        in_specs=[hist_spec, pl.BlockSpec((n, LANES_V7X), lambda j: (row0 // n, j))],
        out_specs=[pl.BlockSpec((n, LANES_V7X), lambda j: (0, j)), hist_spec],
        out_shape=[jax.ShapeDtypeStruct((n, D_POOL), F32), jax.ShapeDtypeStruct(hist.shape, F32)],
        compiler_params=_params("parallel"),
        name="pool_sample",
    )(hist, zp)


def _merge_kernel(seq_tiles, y_ref, bonus_ref, g_ref, d_ref, zga_ref, zgb_ref, x_ref,
                  lw_ref, lb_ref, ones_ref, wa_ref, pw_ref, ps_ref, wb_ref,
                  wo_ref, gpost_ref, gnext_ref, o_ref, h_ref, *carry):
    if seq_tiles is None:
        d = d_ref[...]
    else:
        d = _pool_prompt_tile(d_ref[...], carry[0], pl.program_id(0) % seq_tiles)
    ones_bd = ones_ref[...]
    y = y_ref[...]
    mu = _head_sum(y, ones_bd) * (1.0 / HEAD_DIM)
    yb = jnp.concatenate(
        [_mm(d[:, POOL_GROUP * gi:POOL_GROUP * (gi + 1)], pw_ref[gi]) for gi in range(len(POOL_WINDOWS))],
        axis=1) * ps_ref[...]
    yc = y - mu
    var = _head_sum(yc * yc, ones_bd) * (1.0 / HEAD_DIM)
    mb = jax.nn.sigmoid(zgb_ref[...]) * _mm(yb, wb_ref[...])
    yn = yc * lax.rsqrt(var + GN_EPS) * lw_ref[...] + lb_ref[...]
    ya = (yn + bonus_ref[...]) * g_ref[...]
    m = jax.nn.sigmoid(zga_ref[...]) * _mm(ya, wa_ref[...]) + mb
    x1 = x_ref[...] + _rmsnorm(_mm(m, wo_ref[...]), gpost_ref[...])
    o_ref[...] = x1
    h_ref[...] = _rmsnorm(x1, gnext_ref[...]).astype(BF16)


def _merge(y, bonus, g, d, zpg, x2d, row0, wts, tm, seq_len=None):
    n = y.shape[0]
    assert row0 % tm == 0 and tm % SLAB == 0 and (seq_len is None or (seq_len % tm == 0 and tm >= POOL_CARRY))
    first = row0 // tm
    half = pl.BlockSpec((tm, D_RWKV), lambda i: (i, 0))
    full = pl.BlockSpec((tm, D_MODEL), lambda i: (i, 0))
    gate_at = lambda col: pl.BlockSpec((pl.Element(tm), pl.Element(D_MODEL)), lambda i: (tm * (first + i), col))
    resident = [pl.BlockSpec(w.shape, functools.partial(lambda nd, i: (0,) * nd, w.ndim),
                             pipeline_mode=pl.Buffered(1)) for w in wts]
    return pl.pallas_call(
        functools.partial(_merge_kernel, None if seq_len is None else seq_len // tm),
        grid=(n // tm,),
        in_specs=[half] * 4 + [gate_at(D_POOL), gate_at(D_POOL + D_MODEL)] + [full] + resident,
        out_specs=[full, full],
        out_shape=[jax.ShapeDtypeStruct((n, D_MODEL), F32), jax.ShapeDtypeStruct((n, D_MODEL), BF16)],
        scratch_shapes=[] if seq_len is None else [pltpu.VMEM((POOL_CARRY, D_POOL), F32)],
        compiler_params=_params("parallel" if seq_len is None else "arbitrary"),
        name="merge",
    )(y, bonus, g, d, zpg, zpg, x2d, *wts)


def _gelu_tanh(x):
    return 0.5 * x * (1.0 + jnp.tanh(0.7978845608028654 * (x + 0.044715 * x * x * x)))


def _ffn_body(x_ref, h_ref, wg_ref, wu_ref, cw_ref, cb_ref, wo_ref, gp_ref,
              o_ref, acc_ref, prev1, prev2, emit_tail):
    c = pl.program_id(1)

    @pl.when(c == 0)
    def _():
        acc_ref[...] = jnp.zeros_like(acc_ref)

    h = h_ref[...]
    gate = jnp.dot(h, wg_ref[...], preferred_element_type=F32)
    up = jnp.dot(h, wu_ref[...], preferred_element_type=F32)
    cw = cw_ref[...]
    cv = cb_ref[...] + cw[0:1, :] * prev2(gate) + cw[1:2, :] * prev1(gate) + cw[2:3, :] * gate
    emit_tail(gate)
    acc_ref[...] += _mm(_gelu_tanh(cv) * up, wo_ref[...])

    @pl.when(c == pl.num_programs(1) - 1)
    def _():
        o_ref[...] = x_ref[...] + _rmsnorm(acc_ref[...], gp_ref[...])


def _ffn_prompt_kernel(tiles_per_seq, x_ref, h_ref, wg_ref, wu_ref, cw_ref, cb_ref, wo_ref, gp_ref,
                       o_ref, tail_ref, acc_ref, carry_ref):
    i = pl.program_id(0)
    c = pl.program_id(1)

    @pl.when(i % tiles_per_seq == 0)
    def _():
        carry_ref[c] = jnp.zeros(carry_ref.shape[1:], F32)

    hist = carry_ref[c]

    def shifted(gate, s):
        row = lax.broadcasted_iota(jnp.int32, gate.shape, 0)
        rolled = pltpu.roll(gate, s, 0)
        out = rolled
        for j in range(s):
            out = jnp.where(row == j, hist[8 - s + j:9 - s + j, :], out)
        return out

    def emit_tail(gate):
        tail_ref[...] = gate[gate.shape[0] - tail_ref.shape[0]:, :]

    _ffn_body(x_ref, h_ref, wg_ref, wu_ref, cw_ref, cb_ref, wo_ref, gp_ref, o_ref,
              acc_ref, lambda g: shifted(g, 1), lambda g: shifted(g, 2), emit_tail)
    carry_ref[c] = tail_ref[...]


def _ffn_sample_kernel(seq, x_ref, st_ref, h_ref, wg_ref, wu_ref, cw_ref, cb_ref, wo_ref, gp_ref,
                       o_ref, tail_ref, acc_ref, g_ref, p1_ref, p2_ref):
    nb = st_ref.shape[0]
    st0 = st_ref[:, 0, :]
    st1 = st_ref[:, 1, :]
    prev1 = lambda gate: _replace_step_rows(p1_ref, pltpu.roll(gate, 1, 0), seq, [(0, st1)])
    prev2 = lambda gate: _replace_step_rows(p2_ref, pltpu.roll(gate, 2, 0), seq, [(0, st0), (1, st1)])

    def emit_tail(gate):
        tiles = gate.shape[1] // LANES_V7X
        for j in range(tiles):
            g_ref[j] = gate[:, LANES_V7X * j:LANES_V7X * (j + 1)]
        for s in range(CONV_W - 1):
            rows = pl.ds(seq - (CONV_W - 1) + s, nb, stride=seq)
            tail_ref[:, s, :] = jnp.concatenate([g_ref[j, rows, :] for j in range(tiles)], axis=1)

    _ffn_body(x_ref, h_ref, wg_ref, wu_ref, cw_ref, cb_ref, wo_ref, gp_ref, o_ref,
              acc_ref, prev1, prev2, emit_tail)


def _ffn_prompt(x2d, h, seq_len, wts, tm, fk):
    n = x2d.shape[0]
    nc = D_FF // fk
    w_in, cw, cb, w_out, gp = wts
    full = pl.BlockSpec((tm, D_MODEL), lambda i, c: (i, 0))
    return pl.pallas_call(
        functools.partial(_ffn_prompt_kernel, seq_len // tm),
        grid=(n // tm, nc),
        in_specs=[full, full,
                  pl.BlockSpec((D_MODEL, fk), lambda i, c: (0, c)),
                  pl.BlockSpec((D_MODEL, fk), lambda i, c: (0, c + nc)),
                  pl.BlockSpec((CONV_W, fk), lambda i, c: (0, c)),
                  pl.BlockSpec((1, fk), lambda i, c: (0, c)),
                  pl.BlockSpec((fk, D_MODEL), lambda i, c: (c, 0)),
                  _const_spec(gp.shape)],
        out_specs=[full, pl.BlockSpec((None, 8, fk), lambda i, c: (i, 0, c))],
        out_shape=[jax.ShapeDtypeStruct((n, D_MODEL), F32),
                   jax.ShapeDtypeStruct((n // tm, 8, D_FF), F32)],
        scratch_shapes=[pltpu.VMEM((tm, D_MODEL), F32), pltpu.VMEM((nc, 8, fk), F32)],
        compiler_params=_params("arbitrary", "arbitrary"),
        name="ffn_prompt",
    )(x2d, h, w_in, w_in, cw, cb, w_out, gp)


def _ffn_sample(x2d, h, st, seq, wts, fk):
    n = x2d.shape[0]
    nb = n // seq
    assert seq >= CONV_W - 1 and st.shape == (nb, CONV_W - 1, D_FF)
    nc = D_FF // fk
    w_in, cw, cb, w_out, gp = wts
    full = pl.BlockSpec((n, D_MODEL), lambda i, c: (0, 0))
    cols = pl.BlockSpec((nb, CONV_W - 1, fk), lambda i, c: (0, 0, c))
    return pl.pallas_call(
        functools.partial(_ffn_sample_kernel, seq),
        grid=(1, nc),
        in_specs=[full, cols, full,
                  pl.BlockSpec((D_MODEL, fk), lambda i, c: (0, c)),
                  pl.BlockSpec((D_MODEL, fk), lambda i, c: (0, c + nc)),
                  pl.BlockSpec((CONV_W, fk), lambda i, c: (0, c)),
                  pl.BlockSpec((1, fk), lambda i, c: (0, c)),
                  pl.BlockSpec((fk, D_MODEL), lambda i, c: (c, 0)),
                  _const_spec(gp.shape)],
        out_specs=[full, cols],
        out_shape=[jax.ShapeDtypeStruct((n, D_MODEL), F32), jax.ShapeDtypeStruct(st.shape, F32)],
        scratch_shapes=[pltpu.VMEM((n, D_MODEL), F32)]
        + [pltpu.VMEM((fk // LANES_V7X, n, LANES_V7X), F32)] * 3,
        compiler_params=_params("arbitrary", "arbitrary"),
        name="ffn_sample",
    )(x2d, st, h, w_in, w_in, cw, cb, w_out, gp)


def _row(v):
    return v.reshape(1, -1).astype(F32)


def _ones_bd():
    i = jnp.arange(MXU_DIM_V7X) // HEAD_DIM
    return (i[:, None] == i[None, :]).astype(BF16)


def _layer_weights(l, norm_pre_mix, w_in, mu_shift, w0, w2, a0, a2, g2, k_k, k_a, r_k, lnx_w, lnx_b,
                   w_branch_a, pool_w, pool_scale, w_branch_b, w_out, norm_post_mix,
                   norm_pre_ffn, w_ffn_in, conv_w, conv_b, w_ffn_out, norm_post_ffn):
    w_t = jnp.swapaxes(w_in[l], 0, 1)
    mu = jnp.pad(mu_shift[l], (0, D_SHIFT_PAD - D_SHIFT)).reshape(1, -1)
    w_lora = jnp.zeros((D_LORA_PAD, 3 * D_RWKV), F32)
    w_lora = w_lora.at[0:RANK_W, 0:D_RWKV].set(w2[l])
    w_lora = w_lora.at[RANK_W:RANK_W + RANK_A, D_RWKV:2 * D_RWKV].set(a2[l])
    w_lora = w_lora.at[RANK_W + RANK_A:D_LORA, 2 * D_RWKV:].set(g2[l])
    ones_bd = _ones_bd()

    def with_bf16(wa, pw, wb, wo, wfi, wfo):
        return dict(
            merge=(_row(lnx_w[l]), _row(lnx_b[l]), ones_bd, wa,
                   pw.reshape(pool_w.shape[1:]), _row(pool_scale[l]), wb,
                   wo, _row(norm_post_mix[l]), _row(norm_pre_ffn[l])),
            ffn=(wfi, conv_w[l].astype(F32), _row(conv_b[l]), wfo, _row(norm_post_ffn[l])),
        )

    return dict(
        in_proj=(_row(norm_pre_mix[l]), w_t),
        prep=(mu, w_lora.astype(BF16), _row(w0[l]), _row(a0[l]), _row(k_k[l]), _row(k_a[l]), _row(r_k[l]),
              ones_bd),
        f32_matmul_weights=[w_branch_a[l], pool_w[l].reshape(-1, POOL_GROUP), w_branch_b[l], w_out[l],
                            w_ffn_in[l], w_ffn_out[l]],
        with_bf16=with_bf16,
    )


def _largest_tile(n, cap, mult=16):
    best = None
    for d in range(mult, min(n, cap) + 1, mult):
        if n % d == 0:
            best = d
    assert best is not None, (n, cap)
    return best


def _project(xp2d, xs2d, seq, st_shift, st_wkv, wts):
    g, w_t = wts["in_proj"]
    n_p, n_s = xp2d.shape[0], xs2d.shape[0]
    tm_norm = _largest_tile(math.gcd(n_p, n_s), ROWS_NORM)
    h, (zr, zl) = _pre_norm(xp2d, xs2d, g, w_t, [(0, D_RWKV), (3 * D_RWKV, D_LORA_PAD)], tm_norm)
    tm = _largest_tile(h.shape[0], ROWS_PROJ)
    zkv = _in_proj(h, w_t, D_RWKV, 2 * D_RWKV, COLS_PROJ, tm)
    zparts = (zr, zkv, zl)
    st = jnp.pad(st_shift.reshape(n_s // seq, D_SHIFT), ((0, 0), (0, D_SHIFT_PAD - D_SHIFT)))
    tm_s = _largest_tile(math.gcd(n_p, n_s), ROWS_PREP_SAMPLE, mult=SLAB * seq)
    r, lw, k2, v, a_s, b_s, g_s, bonus_s = _prep_sample(zparts, n_p, n_s, st, seq, wts["prep"], tm=tm_s)
    zpg, y_s, wkv_s = _in_proj_wkv(h, w_t, D_SHIFT, D_POOL + 2 * D_MODEL, COLS_PROJ, tm,
                                   r, lw, k2, v, a_s, b_s, jnp.transpose(st_wkv, (1, 2, 3, 0)), seq)
    return zparts, zpg, (y_s, bonus_s, g_s, jnp.transpose(wkv_s, (3, 0, 1, 2)))


def _last_shift_row(zparts, row0, b, t):
    if b <= 8:
        last = lambda z: jnp.concatenate(
            [lax.slice(z, (row0 + (i + 1) * t - 1, 0), (row0 + (i + 1) * t, z.shape[1])) for i in range(b)])
    else:
        last = lambda z: lax.slice(z, (row0 + t - 1, 0), (row0 + b * t, z.shape[1]), (t, 1))
    return jnp.concatenate([last(z) for z in zparts], axis=-1)[:, None, :D_SHIFT]


def _prompt_layer(x, z, wts):
    b, t, _ = x.shape
    n = b * t
    x2d = x.reshape(n, D_MODEL)
    zparts, zpg = z
    y, bonus, g, wkv, bf16_weights = _wkv_prompt(zparts, b, t, wts["prep"], wts["f32_matmul_weights"])
    wts = wts["with_bf16"](*bf16_weights)
    flat = lambda a: a.reshape(n, a.shape[-1])
    x1, h1 = _merge(flat(y), flat(bonus), flat(g), zpg, zpg, x2d, 0, wts["merge"],
                    tm=_largest_tile(t, ROWS_MERGE), seq_len=t)
    ffn_tm = _largest_tile(t, ROWS_FFN)
    out, tail = _ffn_prompt(x1, h1, t, wts["ffn"], tm=ffn_tm, fk=COLS_FFN)
    shift = _last_shift_row(zparts, 0, b, t)
    pool = jnp.stack([zpg[(i + 1) * t - POOL_HIST:(i + 1) * t, :D_POOL] for i in range(b)])
    tiles = t // ffn_tm
    conv = tail.reshape(b, tiles, 8, D_FF)[:, tiles - 1, 8 - (CONV_W - 1):, :]
    return wts, (out.reshape(b, t, D_MODEL), shift, wkv, pool, conv)


def _sample_layer(x, z, row0, st_pool, st_conv, wts):
    b, t, _ = x.shape
    n = b * t
    x2d = x.reshape(n, D_MODEL)
    zparts, zpg, (y, bonus, g, wkv) = z
    d, pool = _pool_sample(jnp.swapaxes(st_pool, 0, 1), zpg, row0, t)
    pool = jnp.swapaxes(pool, 0, 1)
    x1, h1 = _merge(y, bonus, g, d, zpg, x2d, row0, wts["merge"],
                    tm=_largest_tile(math.gcd(row0, n), ROWS_MERGE))
    out, conv = _ffn_sample(x1, h1, st_conv, t, wts["ffn"], fk=COLS_FFN)
    shift = _last_shift_row(zparts, row0, b, t)
    return out.reshape(b, t, D_MODEL), shift, wkv, pool, conv


def kernel(x_prompt, x_sample, state_shift, state_wkv, state_pool, state_conv, norm_pre_mix, w_in, mu_shift, w0, w2, a0, a2, g2, k_k, k_a, r_k, lnx_w, lnx_b, w_branch_a, pool_w, pool_scale, w_branch_b, w_out, norm_post_mix, norm_pre_ffn, w_ffn_in, conv_w, conv_b, w_ffn_out, norm_post_ffn):
    weights = (norm_pre_mix, w_in, mu_shift, w0, w2, a0, a2, g2, k_k, k_a, r_k, lnx_w, lnx_b,
               w_branch_a, pool_w, pool_scale, w_branch_b, w_out, norm_post_mix,
               norm_pre_ffn, w_ffn_in, conv_w, conv_b, w_ffn_out, norm_post_ffn)
    depth = w_in.shape[0]
    yp, ys = x_prompt, x_sample
    p_states, s_states = [], []
    for l in range(depth):
        wts = _layer_weights(l, *weights)
        n_prompt = yp.shape[0] * yp.shape[1]
        z = _project(yp.reshape(n_prompt, D_MODEL), ys.reshape(-1, D_MODEL), ys.shape[1],
                     state_shift[l], state_wkv[l], wts)
        bf16_wts, (yp, *ps) = _prompt_layer(yp, z[:2], wts)
        ys, *ss = _sample_layer(ys, z, n_prompt, state_pool[l], state_conv[l], bf16_wts)
        p_states.append(ps)
        s_states.append(ss)
    stack = lambda states, i: jnp.stack([s[i] for s in states])
    return (yp, ys,
            stack(p_states, 0), stack(p_states, 1), stack(p_states, 2), stack(p_states, 3),
            stack(s_states, 0), stack(s_states, 1), stack(s_states, 2), stack(s_states, 3))
```

```python
import functools
import math

import jax
import jax.numpy as jnp
from jax import lax
from jax.experimental import pallas as pl
from jax.experimental.pallas import tpu as pltpu

F32 = jnp.float32
BF16 = jnp.bfloat16

D_MODEL = 2048
HEAD_DIM = 64
D_RWKV = 1024
N_HEADS = 16
RANK_W, RANK_A, RANK_G = 64, 64, 160
D_LORA = RANK_W + RANK_A + RANK_G
D_LORA_PAD = 384
D_SHIFT = 3 * D_RWKV + D_LORA
D_SHIFT_PAD = 3 * D_RWKV + D_LORA_PAD
D_POOL = 1024
POOL_WINDOWS = (2, 4, 8, 16)
POOL_GROUP = 256
POOL_HIST = 15
D_FF = 5632
CONV_W = 3
NORM_EPS = 1e-6
GN_EPS = 64e-5
PAST_LEN = 16384

LANES_V7X = 128
SLAB = 8
MXU_DIM_V7X = 256
HEADS_PER_GROUP = MXU_DIM_V7X // HEAD_DIM
N_GROUPS = N_HEADS // HEADS_PER_GROUP
CHUNK = 64
VMEM_LIMIT_V7X = 56 * 1024 * 1024

ROWS_NORM = 512
ROWS_PROJ = 1088
COLS_PROJ = 1024
ROWS_MERGE = 256
ROWS_FFN = 512
COLS_FFN = 512
ROWS_PREP_SAMPLE = 128


def _params(*sem):
    return pltpu.CompilerParams(dimension_semantics=sem, vmem_limit_bytes=VMEM_LIMIT_V7X)


def _mm(a, b):
    return jnp.dot(a.astype(BF16), b.astype(BF16), preferred_element_type=F32)


def _mm_nt(a, b):
    return lax.dot_general(a.astype(BF16), b.astype(BF16), (((1,), (1,)), ((), ())),
                           preferred_element_type=F32)


def _split_hi_lo(x):
    hi = x.astype(BF16)
    lo = (x - hi.astype(F32)).astype(BF16)
    return hi, lo


def _head_sum(x, ones_bd):
    hi, lo = _split_hi_lo(x)
    outs = []
    for gi in range(x.shape[1] // MXU_DIM_V7X):
        sl = slice(MXU_DIM_V7X * gi, MXU_DIM_V7X * (gi + 1))
        outs.append(jnp.dot(hi[:, sl], ones_bd, preferred_element_type=F32)
                    + jnp.dot(lo[:, sl], ones_bd, preferred_element_type=F32))
    return jnp.concatenate(outs, axis=1)


def _rmsnorm(x, g):
    return x * lax.rsqrt(jnp.mean(x * x, axis=-1, keepdims=True) + NORM_EPS) * g


def _replace_step_rows(scr_ref, base, seq, steps):
    nb = base.shape[0] // seq
    for j in range(base.shape[1] // LANES_V7X):
        ls = slice(LANES_V7X * j, LANES_V7X * (j + 1))
        scr_ref[j] = base[:, ls]
        for t, vals in steps:
            scr_ref[j, pl.ds(t, nb, stride=seq), :] = vals[:, ls]
    return jnp.concatenate([scr_ref[j] for j in range(base.shape[1] // LANES_V7X)], axis=1)


def _pre_norm_kernel(prompt_tiles, n_sections, xp_ref, xs_ref, g_ref, *refs):
    w_refs, h_ref, z_refs, wb_refs = (refs[:n_sections], refs[n_sections],
                                      refs[n_sections + 1:2 * n_sections + 1], refs[2 * n_sections + 1:])
    i = pl.program_id(0)

    @pl.when(i == 0)
    def _():
        for w_ref, wb_ref in zip(w_refs, wb_refs):
            wb_ref[...] = w_ref[...].astype(BF16)

    @pl.when(i < prompt_tiles)
    def _():
        h_ref[...] = _rmsnorm(xp_ref[...], g_ref[...]).astype(BF16)

    @pl.when(i >= prompt_tiles)
    def _():
        h_ref[...] = _rmsnorm(xs_ref[...], g_ref[...]).astype(BF16)

    h = h_ref[...]
    for z_ref, wb_ref in zip(z_refs, wb_refs):
        z_ref[...] = lax.dot_general(h, wb_ref[...], (((1,), (1,)), ((), ())), preferred_element_type=F32)


def _pre_norm(xp, xs, g, w_t, sections, tm):
    pt, st = xp.shape[0] // tm, xs.shape[0] // tm
    n = xp.shape[0] + xs.shape[0]
    assert all(off % SLAB == 0 for off, _ in sections)
    rows = pl.BlockSpec((tm, D_MODEL), lambda i: (i, 0))
    h, *zs = pl.pallas_call(
        functools.partial(_pre_norm_kernel, pt, len(sections)),
        grid=(pt + st,),
        in_specs=[pl.BlockSpec((tm, D_MODEL), lambda i: (jnp.minimum(i, pt - 1), 0)),
                  pl.BlockSpec((tm, D_MODEL), lambda i: (jnp.maximum(i - pt, 0), 0)),
                  _const_spec(g.shape)]
        + [pl.BlockSpec((pl.Element(width), pl.Element(D_MODEL)),
                        functools.partial(lambda o, i: (SLAB * o, 0), off // SLAB),
                        pipeline_mode=pl.Buffered(1)) for off, width in sections],
        out_specs=[rows] + [pl.BlockSpec((tm, width), lambda i: (i, 0)) for _, width in sections],
        out_shape=[jax.ShapeDtypeStruct((n, D_MODEL), BF16)]
        + [jax.ShapeDtypeStruct((n, width), F32) for _, width in sections],
        scratch_shapes=[pltpu.VMEM((width, D_MODEL), BF16) for _, width in sections],
        compiler_params=_params("arbitrary"),
        name="pre_norm",
    )(xp, xs, g, *([w_t] * len(sections)))
    return h, zs


def _in_proj_kernel(h_ref, w_ref, o_ref, wb_ref):
    @pl.when(pl.program_id(1) == 0)
    def _():
        wb_ref[...] = w_ref[...].astype(BF16)

    o_ref[...] = lax.dot_general(h_ref[...], wb_ref[...], (((1,), (1,)), ((), ())),
                                 preferred_element_type=F32)


def _in_proj(h, w_t, off, width, tn, tm):
    n = h.shape[0]
    assert width % tn == 0 and n % tm == 0
    return pl.pallas_call(
        _in_proj_kernel,
        grid=(width // tn, n // tm),
        in_specs=[pl.BlockSpec((tm, D_MODEL), lambda j, i: (i, 0)),
                  pl.BlockSpec((pl.Element(tn), pl.Element(D_MODEL)),
                               lambda j, i: (SLAB * (off // SLAB + j * (tn // SLAB)), 0))],
        out_specs=pl.BlockSpec((tm, tn), lambda j, i: (i, j)),
        out_shape=jax.ShapeDtypeStruct((n, width), F32),
        scratch_shapes=[pltpu.VMEM((tn, D_MODEL), BF16)],
        compiler_params=_params("arbitrary", "arbitrary"),
        name="in_proj",
    )(h, w_t)


def _prep_math(zs, prev, mu, w_lora, w0, a0, k_k, k_a, r_k, ones_bd):
    f = zs + (prev - zs) * mu
    r = f[:, 0:D_RWKV]
    k = f[:, D_RWKV:2 * D_RWKV]
    v = f[:, 2 * D_RWKV:3 * D_RWKV]
    low = f[:, 3 * D_RWKV:3 * D_RWKV + D_LORA_PAD]
    lane = lax.broadcasted_iota(jnp.int32, low.shape, 1)
    act = jnp.where(lane < RANK_W, jnp.tanh(low),
                    jnp.where(lane < RANK_W + RANK_A, low,
                              jnp.where(lane < D_LORA, jax.nn.sigmoid(low), 0.0)))
    lora = _mm(act, w_lora)
    u = w0 + lora[:, 0:D_RWKV]
    lw = -math.exp(-0.5) * jax.nn.sigmoid(u)
    a = jax.nn.sigmoid(a0 + lora[:, D_RWKV:2 * D_RWKV])
    g = lora[:, 2 * D_RWKV:3 * D_RWKV]
    kk = k * k_k
    kk = kk * jnp.minimum(lax.rsqrt(_head_sum(kk * kk, ones_bd)), 1e12)
    k2 = k * (1.0 + (a - 1.0) * k_a)
    bonus = _head_sum(r * k2 * r_k, ones_bd) * v
    return r, lw, k2, v, -kk, kk * a, g, bonus


N_PREP_WEIGHTS = 8


def _prep_sample_kernel(seq, n_parts, *refs):
    z_refs, st_ref, refs = refs[:n_parts], refs[n_parts], refs[n_parts + 1:]
    w_refs, o_refs, prev_ref = refs[:N_PREP_WEIGHTS], refs[N_PREP_WEIGHTS:-1], refs[-1]
    zs = jnp.concatenate([z[...] for z in z_refs], axis=1)
    prev = _replace_step_rows(prev_ref, pltpu.roll(zs, 1, 0), seq, [(0, st_ref[...])])
    for o_ref, o in zip(o_refs, _prep_math(zs, prev, *[w[...] for w in w_refs])):
        o_ref[...] = o


def _const_spec(shape):
    nd = len(shape)
    return pl.BlockSpec(shape, lambda *_: (0,) * nd)


def _prep_sample(zparts, row0, n, st, seq, wts, tm):
    assert tm % (SLAB * seq) == 0 and n % tm == 0 and row0 % tm == 0 and len(wts) == N_PREP_WEIGHTS
    first = row0 // tm
    outs = [jax.ShapeDtypeStruct((n, D_RWKV), F32)] * 8
    row_spec = pl.BlockSpec((tm, D_RWKV), lambda i: (i, 0))
    return pl.pallas_call(
        functools.partial(_prep_sample_kernel, seq, len(zparts)),
        grid=(n // tm,),
        in_specs=[pl.BlockSpec((tm, z.shape[1]), lambda i: (first + i, 0)) for z in zparts]
        + [pl.BlockSpec((tm // seq, D_SHIFT_PAD), lambda i: (i, 0))]
        + [_const_spec(w.shape) for w in wts],
        out_specs=[row_spec] * 8,
        out_shape=outs,
        scratch_shapes=[pltpu.VMEM((D_SHIFT_PAD // LANES_V7X, tm, LANES_V7X), F32)],
        compiler_params=_params("parallel"),
        name="prep_sample",
    )(*zparts, st, *wts)


def _expand_bd(a, bd_mask):
    return jnp.where(bd_mask, jnp.concatenate([a] * HEADS_PER_GROUP, axis=0), 0.0)


def _chunk_cumsum(x):
    row = lax.broadcasted_iota(jnp.int32, x.shape, 0)
    s = 1
    while s < x.shape[0]:
        x = x + jnp.where(row >= s, pltpu.roll(x, s, 0), 0.0)
        s *= 2
    return x


def _wkv_intra(xs, bts, kts, vs, pzs, masks):
    bd_mask, strict_mask, incl_mask, eye_c, head_masks = masks
    c = vs[0].shape[0]
    ystacks = [jnp.concatenate([jnp.where(m, bt, 0.0) for m in head_masks]
                               + [jnp.where(m, kt, 0.0) for m in head_masks], axis=0)
               for bt, kt in zip(bts, kts)]
    grams = [_mm_nt(x, ys) for x, ys in zip(xs, ystacks)]
    l_abs = [jnp.where(strict_mask, g[0:c, 0:4 * c], 0.0) for g in grams]
    l_aks = [jnp.where(strict_mask, g[0:c, 4 * c:8 * c], 0.0) for g in grams]
    m_rbs = [jnp.where(incl_mask, g[c:2 * c, 0:4 * c], 0.0) for g in grams]
    m_rks = [jnp.where(incl_mask, g[c:2 * c, 4 * c:8 * c], 0.0) for g in grams]

    tinvs = [eye_c + l for l in l_abs]
    lps = [_mm(l, _expand_bd(l, bd_mask)) for l in l_abs]
    p = 2
    while True:
        rhss = [_expand_bd(lp, bd_mask) for lp in lps]
        if 2 * p >= c:
            tinvs = [t + _mm(t, rhs) for t, rhs in zip(tinvs, rhss)]
            break
        ress = [_mm(jnp.concatenate([lp, t], axis=0), rhs) for lp, t, rhs in zip(lps, tinvs, rhss)]
        lps = [res[0:c] for res in ress]
        tinvs = [t + res[c:2 * c] for t, res in zip(tinvs, ress)]
        p *= 2

    v_bds = [_expand_bd(v, bd_mask) for v in vs]
    ws = [pz[0:c] + _mm(l_ak, v_bd) for pz, l_ak, v_bd in zip(pzs, l_aks, v_bds)]
    us = [_mm(t, _expand_bd(w, bd_mask)) for t, w in zip(tinvs, ws)]
    ys = [pz[c:2 * c] + _mm(jnp.concatenate([m_rb, m_rk], axis=1),
                            jnp.concatenate([_expand_bd(u, bd_mask), v_bd], axis=0))
          for pz, m_rb, m_rk, u, v_bd in zip(pzs, m_rbs, m_rks, us, v_bds)]
    return ys, us


def _wkv_chunk(groups, masks):
    c = groups[0][0].shape[0]
    cls = [_chunk_cumsum(g[1]) for g in groups]
    xs = [jnp.concatenate([g[4] * jnp.exp(cl - g[1]), g[0] * jnp.exp(cl)], axis=0)
          for g, cl in zip(groups, cls)]
    pzs = [_mm_nt(x, g[6]) for x, g in zip(xs, groups)]
    e_negs = [jnp.exp(-cl) for cl in cls]
    ys, us = _wkv_intra(xs, [g[5] * e for g, e in zip(groups, e_negs)],
                        [g[2] * e for g, e in zip(groups, e_negs)], [g[3] for g in groups], pzs, masks)
    dss = []
    for g, cl, u in zip(groups, cls, us):
        e_rem = jnp.exp(cl[c - 1:c, :] - cl)
        uv_t = jnp.concatenate([u, g[3]], axis=0).T
        bk = jnp.concatenate([g[5] * e_rem, g[2] * e_rem], axis=0)
        dss.append(_mm(uv_t, bk))
    return [(y, g[6] * jnp.exp(cl[c - 1:c, :]) + jnp.where(masks[0], ds, 0.0))
            for y, g, cl, ds in zip(ys, groups, cls, dss)]


def _wkv_masks(c):
    n = HEADS_PER_GROUP * c
    rr = lax.broadcasted_iota(jnp.int32, (n, n), 0)
    cc = lax.broadcasted_iota(jnp.int32, (n, n), 1)
    bd_mask = (rr // c) == (cc // c)
    t = lax.broadcasted_iota(jnp.int32, (c, n), 0)
    s = lax.broadcasted_iota(jnp.int32, (c, n), 1) % c
    strict_mask = t > s
    incl_mask = t >= s
    eye_c = jnp.where(t == s, 1.0, 0.0).astype(F32)
    lane = lax.broadcasted_iota(jnp.int32, (c, MXU_DIM_V7X), 1)
    head_masks = [(lane // HEAD_DIM) == h for h in range(HEADS_PER_GROUP)]
    return bd_mask, strict_mask, incl_mask, eye_c, head_masks


def _wkv_prompt_kernel(nb, n_parts, n_casts, *refs):
    it = iter(refs)
    take = lambda k: [next(it) for _ in range(k)]
    z_refs = [take(n_parts) for _ in range(nb)]
    pw_refs, w_refs = take(N_PREP_WEIGHTS), take(n_casts)
    y_ref, bonus_ref, g_ref, sout_ref = take(4)
    wb_refs = take(n_casts)
    s_ref, carry_ref = take(2)
    ci = pl.program_id(0)

    @pl.when(ci == 0)
    def _():
        s_ref[...] = jnp.zeros_like(s_ref)
        carry_ref[...] = jnp.zeros_like(carry_ref)

    for w_ref, wb_ref in zip(w_refs, wb_refs):
        wb_ref[...] = w_ref[...].astype(BF16)

    zs_parts, prev_parts = [], []
    for bi in range(nb):
        zs = jnp.concatenate([z[...] for z in z_refs[bi]], axis=1)
        row = lax.broadcasted_iota(jnp.int32, zs.shape, 0)
        prev_parts.append(jnp.where(row == 0, carry_ref[bi, 0:1, :], pltpu.roll(zs, 1, 0)))
        carry_ref[bi, 0:1, :] = zs[CHUNK - 1:, :]
        zs_parts.append(zs)
    *scan_ops, g, bonus = _prep_math(
        jnp.concatenate(zs_parts, axis=0), jnp.concatenate(prev_parts, axis=0), *[w[...] for w in pw_refs])
    for bi in range(nb):
        rows = slice(CHUNK * bi, CHUNK * (bi + 1))
        g_ref[bi] = g[rows]
        bonus_ref[bi] = bonus[rows]

    masks = _wkv_masks(CHUNK)
    chains = [(bi, gi) for bi in range(nb) for gi in range(N_GROUPS)]
    lanes = lambda gi: slice(MXU_DIM_V7X * gi, MXU_DIM_V7X * (gi + 1))
    groups = [tuple(x[CHUNK * bi:CHUNK * (bi + 1), lanes(gi)] for x in scan_ops)
              + (s_ref[bi, gi],) for bi, gi in chains]
    for (bi, gi), (y, s_new) in zip(chains, _wkv_chunk(groups, masks)):
        y_ref[bi, :, lanes(gi)] = y
        s_ref[bi, gi] = s_new

    @pl.when(ci == pl.num_programs(0) - 1)
    def _():
        for bi, gi in chains:
            s_bd = s_ref[bi, gi]
            for h in range(HEADS_PER_GROUP):
                hs = slice(HEAD_DIM * h, HEAD_DIM * (h + 1))
                sout_ref[bi, HEADS_PER_GROUP * gi + h] = s_bd[hs, hs]


def _wkv_prompt(zparts, b, t, prep_wts, weights):
    steps = t // CHUNK
    assert len(prep_wts) == N_PREP_WEIGHTS
    seq_rows = lambda w, bi: pl.BlockSpec((CHUNK, w), lambda ci: (bi * steps + ci, 0))
    spec = pl.BlockSpec((b, CHUNK, D_RWKV), lambda ci: (0, ci, 0))
    state = (b, N_GROUPS, MXU_DIM_V7X, MXU_DIM_V7X)
    out_state = (b, N_HEADS, HEAD_DIM, HEAD_DIM)
    bf16_rows = 2 * SLAB
    assert all(w.shape[0] % (steps * bf16_rows) == 0 for w in weights)
    slabs = [pl.BlockSpec((w.shape[0] // steps, w.shape[1]), lambda ci: (ci, 0)) for w in weights]
    rows = jax.ShapeDtypeStruct((b, t, D_RWKV), F32)
    y, bonus, g, s, *wb = pl.pallas_call(
        functools.partial(_wkv_prompt_kernel, b, len(zparts), len(weights)),
        grid=(steps,),
        in_specs=[seq_rows(z.shape[1], bi) for bi in range(b) for z in zparts]
        + [_const_spec(w.shape) for w in prep_wts] + slabs,
        out_specs=[spec, spec, spec, _const_spec(out_state)] + slabs,
        out_shape=[rows, rows, rows, jax.ShapeDtypeStruct(out_state, F32)]
        + [jax.ShapeDtypeStruct(w.shape, BF16) for w in weights],
        scratch_shapes=[pltpu.VMEM(state, F32), pltpu.VMEM((b, SLAB, D_SHIFT_PAD), F32)],
        compiler_params=_params("arbitrary"),
        name="wkv_prompt",
    )(*(list(zparts) * b), *prep_wts, *weights)
    return y, bonus, g, s, wb


HEAD_PAIR = 2 * HEAD_DIM


STEP_ROWS = HEAD_DIM // 2
STEP_UNITS = N_HEADS * HEAD_DIM // STEP_ROWS
COMPANION_PARTS = 4


def _wkv_steps_unit(seq, unit, r_ref, lw_ref, k2_ref, v_ref, as_ref, bs_ref, s0_ref, y_ref, sout_ref,
                    op_ref, vt_ref, yt_ref, companion=lambda part: None):
    nb = r_ref.shape[0] // seq
    sub = unit % 4

    @pl.when(sub == 0)
    def _():
        for t in range(seq):
            rows = pl.ds(t, nb, stride=seq)
            op_ref[0, t] = r_ref[rows, :].T
            op_ref[1, t] = jnp.exp(lw_ref[rows, :]).T
            op_ref[2, t] = k2_ref[rows, :].T
            op_ref[3, t] = as_ref[rows, :].T
            op_ref[4, t] = bs_ref[rows, :].T
            vt_ref[t] = v_ref[rows, :].T

    k0 = pl.multiple_of((sub // 2) * HEAD_DIM, HEAD_DIM)
    row0 = k0 + (sub % 2) * STEP_ROWS
    ks = pl.ds(k0, HEAD_DIM)
    for vi in range(STEP_ROWS):
        if vi % (STEP_ROWS // COMPANION_PARTS) == 0:
            companion(vi // (STEP_ROWS // COMPANION_PARTS))
        s = s0_ref[0, vi]
        for t in range(seq):
            sa = jnp.sum(s * op_ref[3, t, ks, :], axis=0, keepdims=True)
            vrow = vt_ref[t, pl.ds(row0 + vi, 1), :]
            s = s * op_ref[1, t, ks, :] + sa * op_ref[4, t, ks, :] + vrow * op_ref[2, t, ks, :]
            yt_ref[t, pl.ds(row0 + vi, 1), :] = jnp.sum(s * op_ref[0, t, ks, :], axis=0, keepdims=True)
        sout_ref[0, vi] = s

    @pl.when(sub == 3)
    def _():
        for t in range(seq):
            y_ref[pl.ds(t, nb, stride=seq), :] = yt_ref[t].T


def _in_proj_wkv_kernel(seq, steps_per_tile, h_ref, w_ref, r_ref, lw_ref, k2_ref, v_ref, as_ref, bs_ref,
                        s0_ref, o_ref, y_ref, sout_ref, wb_ref, op_ref, vt_ref, yt_ref):
    j, i = pl.program_id(0), pl.program_id(1)

    @pl.when(i == 0)
    def _():
        wb_ref[...] = w_ref[...].astype(BF16)

    def project(part):
        tn = o_ref.shape[1] // COMPANION_PARTS
        cols = slice(part * tn, (part + 1) * tn)
        o_ref[:, cols] = lax.dot_general(h_ref[...], wb_ref[cols, :], (((1,), (1,)), ((), ())),
                                         preferred_element_type=F32)

    step = j * steps_per_tile + i

    @pl.when(step < STEP_UNITS)
    def _():
        _wkv_steps_unit(seq, step, r_ref, lw_ref, k2_ref, v_ref, as_ref, bs_ref, s0_ref, y_ref, sout_ref,
                        op_ref, vt_ref, yt_ref, companion=project)

    @pl.when(step >= STEP_UNITS)
    def _():
        for part in range(COMPANION_PARTS):
            project(part)


def _in_proj_wkv(h, w_t, off, width, tn, tm, r, lw, k2, v, a_s, b_s, s0_native, seq):
    n, n_s = h.shape[0], r.shape[0]
    nb = n_s // seq
    tiles = n // tm
    assert width % tn == 0 and n % tm == 0 and (width // tn) * tiles >= STEP_UNITS
    assert s0_native.shape == (N_HEADS, HEAD_DIM, HEAD_DIM, nb) and nb == LANES_V7X
    unit = lambda j, i: jnp.minimum(j * tiles + i, STEP_UNITS - 1)
    vec = pl.BlockSpec((n_s, HEAD_PAIR), lambda j, i: (0, unit(j, i) // 4))
    slab = pl.BlockSpec((1, STEP_ROWS, HEAD_DIM, nb), lambda j, i: (unit(j, i) // 2, unit(j, i) % 2, 0, 0))
    return pl.pallas_call(
        functools.partial(_in_proj_wkv_kernel, seq, tiles),
        grid=(width // tn, tiles),
        in_specs=[pl.BlockSpec((tm, D_MODEL), lambda j, i: (i, 0)),
                  pl.BlockSpec((pl.Element(tn), pl.Element(D_MODEL)),
                               lambda j, i: (SLAB * (off // SLAB + j * (tn // SLAB)), 0))]
        + [vec] * 6 + [slab],
        out_specs=[pl.BlockSpec((tm, tn), lambda j, i: (i, j)), vec, slab],
        out_shape=[jax.ShapeDtypeStruct((n, width), F32), jax.ShapeDtypeStruct((n_s, D_RWKV), F32),
                   jax.ShapeDtypeStruct(s0_native.shape, F32)],
        scratch_shapes=[pltpu.VMEM((tn, D_MODEL), BF16), pltpu.VMEM((5, seq, HEAD_PAIR, nb), F32),
                        pltpu.VMEM((seq, HEAD_PAIR, nb), F32), pltpu.VMEM((seq, HEAD_PAIR, nb), F32)],
        compiler_params=_params("arbitrary", "arbitrary"),
        name="in_proj_wkv",
    )(h, w_t, r, lw, k2, v, a_s, b_s, s0_native)


POOL_CARRY = 16


def _pool_prompt_tile(zp, carry_ref, tile_in_seq):
    tm = zp.shape[0]

    @pl.when(tile_in_seq == 0)
    def _():
        carry_ref[...] = jnp.zeros_like(carry_ref)

    buf = jnp.concatenate([carry_ref[...], zp], axis=0)
    carry_ref[...] = zp[tm - POOL_CARRY:, :]
    pos = (tile_in_seq * tm + lax.broadcasted_iota(jnp.int32, (tm, POOL_GROUP), 0) + 1).astype(F32)
    ds = []
    for gi, win in enumerate(POOL_WINDOWS):
        sl = slice(POOL_GROUP * gi, POOL_GROUP * (gi + 1))
        acc = buf[:, sl]
        s = 1
        while s < win:
            acc = acc + pltpu.roll(acc, s, 0)
            s *= 2
        ds.append(acc[POOL_CARRY:, :] / jnp.minimum(float(win), pos) - zp[:, sl])
    return jnp.concatenate(ds, axis=1)


def _pool_sample_kernel(seq, hist_ref, zp_ref, d_ref, new_ref):
    nb = hist_ref.shape[1]
    step = lambda t: zp_ref[pl.ds(t, nb, stride=seq), :]
    past = lambda j: step(j) if j >= 0 else hist_ref[POOL_HIST + j]
    tiles_per_group = POOL_GROUP // LANES_V7X
    for gi, win in enumerate(POOL_WINDOWS):
        @pl.when(pl.program_id(0) // tiles_per_group == gi)
        def _():
            for t in range(seq):
                acc = past(t)
                for j in range(1, win):
                    acc = acc + past(t - j)
                d_ref[pl.ds(t, nb, stride=seq), :] = acc / float(min(win, PAST_LEN + 1)) - past(t)
    for j in range(POOL_HIST):
        src = j + seq
        new_ref[j] = hist_ref[src] if src < POOL_HIST else step(src - POOL_HIST)


def _pool_sample(hist, zp, row0, seq):
    nb = hist.shape[1]
    n = nb * seq
    assert row0 % n == 0
    hist_spec = pl.BlockSpec((POOL_HIST, nb, LANES_V7X), lambda j: (0, 0, j))
    return pl.pallas_call(
        functools.partial(_pool_sample_kernel, seq),
        grid=(D_POOL // LANES_V7X,),
        in_specs=[hist_spec, pl.BlockSpec((n, LANES_V7X), lambda j: (row0 // n, j))],
        out_specs=[pl.BlockSpec((n, LANES_V7X), lambda j: (0, j)), hist_spec],
        out_shape=[jax.ShapeDtypeStruct((n, D_POOL), F32), jax.ShapeDtypeStruct(hist.shape, F32)],
        compiler_params=_params("parallel"),
        name="pool_sample",
    )(hist, zp)


def _merge_kernel(seq_tiles, y_ref, bonus_ref, g_ref, d_ref, zga_ref, zgb_ref, x_ref,
                  lw_ref, lb_ref, ones_ref, wa_ref, pw_ref, ps_ref, wb_ref,
                  wo_ref, gpost_ref, gnext_ref, o_ref, h_ref, *carry):
    if seq_tiles is None:
        d = d_ref[...]
    else:
        d = _pool_prompt_tile(d_ref[...], carry[0], pl.program_id(0) % seq_tiles)
    ones_bd = ones_ref[...]
    y = y_ref[...]
    mu = _head_sum(y, ones_bd) * (1.0 / HEAD_DIM)
    yb = jnp.concatenate(
        [_mm(d[:, POOL_GROUP * gi:POOL_GROUP * (gi + 1)], pw_ref[gi]) for gi in range(len(POOL_WINDOWS))],
        axis=1) * ps_ref[...]
    yc = y - mu
    var = _head_sum(yc * yc, ones_bd) * (1.0 / HEAD_DIM)
    mb = jax.nn.sigmoid(zgb_ref[...]) * _mm(yb, wb_ref[...])
    yn = yc * lax.rsqrt(var + GN_EPS) * lw_ref[...] + lb_ref[...]
    ya = (yn + bonus_ref[...]) * g_ref[...]
    m = jax.nn.sigmoid(zga_ref[...]) * _mm(ya, wa_ref[...]) + mb
    x1 = x_ref[...] + _rmsnorm(_mm(m, wo_ref[...]), gpost_ref[...])
    o_ref[...] = x1
    h_ref[...] = _rmsnorm(x1, gnext_ref[...]).astype(BF16)


def _merge(y, bonus, g, d, zpg, x2d, row0, wts, tm, seq_len=None):
    n = y.shape[0]
    assert row0 % tm == 0 and tm % SLAB == 0 and (seq_len is None or (seq_len % tm == 0 and tm >= POOL_CARRY))
    first = row0 // tm
    half = pl.BlockSpec((tm, D_RWKV), lambda i: (i, 0))
    full = pl.BlockSpec((tm, D_MODEL), lambda i: (i, 0))
    gate_at = lambda col: pl.BlockSpec((pl.Element(tm), pl.Element(D_MODEL)), lambda i: (tm * (first + i), col))
    resident = [pl.BlockSpec(w.shape, functools.partial(lambda nd, i: (0,) * nd, w.ndim),
                             pipeline_mode=pl.Buffered(1)) for w in wts]
    return pl.pallas_call(
        functools.partial(_merge_kernel, None if seq_len is None else seq_len // tm),
        grid=(n // tm,),
        in_specs=[half] * 4 + [gate_at(D_POOL), gate_at(D_POOL + D_MODEL)] + [full] + resident,
        out_specs=[full, full],
        out_shape=[jax.ShapeDtypeStruct((n, D_MODEL), F32), jax.ShapeDtypeStruct((n, D_MODEL), BF16)],
        scratch_shapes=[] if seq_len is None else [pltpu.VMEM((POOL_CARRY, D_POOL), F32)],
        compiler_params=_params("parallel" if seq_len is None else "arbitrary"),
        name="merge",
    )(y, bonus, g, d, zpg, zpg, x2d, *wts)


def _gelu_tanh(x):
    return 0.5 * x * (1.0 + jnp.tanh(0.7978845608028654 * (x + 0.044715 * x * x * x)))


def _ffn_body(x_ref, h_ref, wg_ref, wu_ref, cw_ref, cb_ref, wo_ref, gp_ref,
              o_ref, acc_ref, prev1, prev2, emit_tail):
    c = pl.program_id(1)

    @pl.when(c == 0)
    def _():
        acc_ref[...] = jnp.zeros_like(acc_ref)

    h = h_ref[...]
    gate = jnp.dot(h, wg_ref[...], preferred_element_type=F32)
    up = jnp.dot(h, wu_ref[...], preferred_element_type=F32)
    cw = cw_ref[...]
    cv = cb_ref[...] + cw[0:1, :] * prev2(gate) + cw[1:2, :] * prev1(gate) + cw[2:3, :] * gate
    emit_tail(gate)
    acc_ref[...] += _mm(_gelu_tanh(cv) * up, wo_ref[...])

    @pl.when(c == pl.num_programs(1) - 1)
    def _():
        o_ref[...] = x_ref[...] + _rmsnorm(acc_ref[...], gp_ref[...])


def _ffn_prompt_kernel(tiles_per_seq, x_ref, h_ref, wg_ref, wu_ref, cw_ref, cb_ref, wo_ref, gp_ref,
                       o_ref, tail_ref, acc_ref, carry_ref):
    i = pl.program_id(0)
    c = pl.program_id(1)

    @pl.when(i % tiles_per_seq == 0)
    def _():
        carry_ref[c] = jnp.zeros(carry_ref.shape[1:], F32)

    hist = carry_ref[c]

    def shifted(gate, s):
        row = lax.broadcasted_iota(jnp.int32, gate.shape, 0)
        rolled = pltpu.roll(gate, s, 0)
        out = rolled
        for j in range(s):
            out = jnp.where(row == j, hist[8 - s + j:9 - s + j, :], out)
        return out

    def emit_tail(gate):
        tail_ref[...] = gate[gate.shape[0] - tail_ref.shape[0]:, :]

    _ffn_body(x_ref, h_ref, wg_ref, wu_ref, cw_ref, cb_ref, wo_ref, gp_ref, o_ref,
              acc_ref, lambda g: shifted(g, 1), lambda g: shifted(g, 2), emit_tail)
    carry_ref[c] = tail_ref[...]


def _ffn_sample_kernel(seq, x_ref, st_ref, h_ref, wg_ref, wu_ref, cw_ref, cb_ref, wo_ref, gp_ref,
                       o_ref, tail_ref, acc_ref, g_ref, p1_ref, p2_ref):
    nb = st_ref.shape[0]
    st0 = st_ref[:, 0, :]
    st1 = st_ref[:, 1, :]
    prev1 = lambda gate: _replace_step_rows(p1_ref, pltpu.roll(gate, 1, 0), seq, [(0, st1)])
    prev2 = lambda gate: _replace_step_rows(p2_ref, pltpu.roll(gate, 2, 0), seq, [(0, st0), (1, st1)])

    def emit_tail(gate):
        tiles = gate.shape[1] // LANES_V7X
        for j in range(tiles):
            g_ref[j] = gate[:, LANES_V7X * j:LANES_V7X * (j + 1)]
        for s in range(CONV_W - 1):
            rows = pl.ds(seq - (CONV_W - 1) + s, nb, stride=seq)
            tail_ref[:, s, :] = jnp.concatenate([g_ref[j, rows, :] for j in range(tiles)], axis=1)

    _ffn_body(x_ref, h_ref, wg_ref, wu_ref, cw_ref, cb_ref, wo_ref, gp_ref, o_ref,
              acc_ref, prev1, prev2, emit_tail)


def _ffn_prompt(x2d, h, seq_len, wts, tm, fk):
    n = x2d.shape[0]
    nc = D_FF // fk
    w_in, cw, cb, w_out, gp = wts
    full = pl.BlockSpec((tm, D_MODEL), lambda i, c: (i, 0))
    return pl.pallas_call(
        functools.partial(_ffn_prompt_kernel, seq_len // tm),
        grid=(n // tm, nc),
        in_specs=[full, full,
                  pl.BlockSpec((D_MODEL, fk), lambda i, c: (0, c)),
                  pl.BlockSpec((D_MODEL, fk), lambda i, c: (0, c + nc)),
                  pl.BlockSpec((CONV_W, fk), lambda i, c: (0, c)),
                  pl.BlockSpec((1, fk), lambda i, c: (0, c)),
                  pl.BlockSpec((fk, D_MODEL), lambda i, c: (c, 0)),
                  _const_spec(gp.shape)],
        out_specs=[full, pl.BlockSpec((None, 8, fk), lambda i, c: (i, 0, c))],
        out_shape=[jax.ShapeDtypeStruct((n, D_MODEL), F32),
                   jax.ShapeDtypeStruct((n // tm, 8, D_FF), F32)],
        scratch_shapes=[pltpu.VMEM((tm, D_MODEL), F32), pltpu.VMEM((nc, 8, fk), F32)],
        compiler_params=_params("arbitrary", "arbitrary"),
        name="ffn_prompt",
    )(x2d, h, w_in, w_in, cw, cb, w_out, gp)


def _ffn_sample(x2d, h, st, seq, wts, fk):
    n = x2d.shape[0]
    nb = n // seq
    assert seq >= CONV_W - 1 and st.shape == (nb, CONV_W - 1, D_FF)
    nc = D_FF // fk
    w_in, cw, cb, w_out, gp = wts
    full = pl.BlockSpec((n, D_MODEL), lambda i, c: (0, 0))
    cols = pl.BlockSpec((nb, CONV_W - 1, fk), lambda i, c: (0, 0, c))
    return pl.pallas_call(
        functools.partial(_ffn_sample_kernel, seq),
        grid=(1, nc),
        in_specs=[full, cols, full,
                  pl.BlockSpec((D_MODEL, fk), lambda i, c: (0, c)),
                  pl.BlockSpec((D_MODEL, fk), lambda i, c: (0, c + nc)),
                  pl.BlockSpec((CONV_W, fk), lambda i, c: (0, c)),
                  pl.BlockSpec((1, fk), lambda i, c: (0, c)),
                  pl.BlockSpec((fk, D_MODEL), lambda i, c: (c, 0)),
                  _const_spec(gp.shape)],
        out_specs=[full, cols],
        out_shape=[jax.ShapeDtypeStruct((n, D_MODEL), F32), jax.ShapeDtypeStruct(st.shape, F32)],
        scratch_shapes=[pltpu.VMEM((n, D_MODEL), F32)]
        + [pltpu.VMEM((fk // LANES_V7X, n, LANES_V7X), F32)] * 3,
        compiler_params=_params("arbitrary", "arbitrary"),
        name="ffn_sample",
    )(x2d, st, h, w_in, w_in, cw, cb, w_out, gp)


def _row(v):
    return v.reshape(1, -1).astype(F32)


def _ones_bd():
    i = jnp.arange(MXU_DIM_V7X) // HEAD_DIM
    return (i[:, None] == i[None, :]).astype(BF16)


def _layer_weights(l, norm_pre_mix, w_in, mu_shift, w0, w2, a0, a2, g2, k_k, k_a, r_k, lnx_w, lnx_b,
                   w_branch_a, pool_w, pool_scale, w_branch_b, w_out, norm_post_mix,
                   norm_pre_ffn, w_ffn_in, conv_w, conv_b, w_ffn_out, norm_post_ffn):
    w_t = jnp.swapaxes(w_in[l], 0, 1)
    mu = jnp.pad(mu_shift[l], (0, D_SHIFT_PAD - D_SHIFT)).reshape(1, -1)
    w_lora = jnp.zeros((D_LORA_PAD, 3 * D_RWKV), F32)
    w_lora = w_lora.at[0:RANK_W, 0:D_RWKV].set(w2[l])
    w_lora = w_lora.at[RANK_W:RANK_W + RANK_A, D_RWKV:2 * D_RWKV].set(a2[l])
    w_lora = w_lora.at[RANK_W + RANK_A:D_LORA, 2 * D_RWKV:].set(g2[l])
    ones_bd = _ones_bd()

    def with_bf16(wa, pw, wb, wo, wfi, wfo):
        return dict(
            merge=(_row(lnx_w[l]), _row(lnx_b[l]), ones_bd, wa,
                   pw.reshape(pool_w.shape[1:]), _row(pool_scale[l]), wb,
                   wo, _row(norm_post_mix[l]), _row(norm_pre_ffn[l])),
            ffn=(wfi, conv_w[l].astype(F32), _row(conv_b[l]), wfo, _row(norm_post_ffn[l])),
        )

    return dict(
        in_proj=(_row(norm_pre_mix[l]), w_t),
        prep=(mu, w_lora.astype(BF16), _row(w0[l]), _row(a0[l]), _row(k_k[l]), _row(k_a[l]), _row(r_k[l]),
              ones_bd),
        f32_matmul_weights=[w_branch_a[l], pool_w[l].reshape(-1, POOL_GROUP), w_branch_b[l], w_out[l],
                            w_ffn_in[l], w_ffn_out[l]],
        with_bf16=with_bf16,
    )


def _largest_tile(n, cap, mult=16):
    best = None
    for d in range(mult, min(n, cap) + 1, mult):
        if n % d == 0:
            best = d
    assert best is not None, (n, cap)
    return best


def _project(xp2d, xs2d, seq, st_shift, st_wkv, wts):
    g, w_t = wts["in_proj"]
    n_p, n_s = xp2d.shape[0], xs2d.shape[0]
    tm_norm = _largest_tile(math.gcd(n_p, n_s), ROWS_NORM)
    h, (zr, zl) = _pre_norm(xp2d, xs2d, g, w_t, [(0, D_RWKV), (3 * D_RWKV, D_LORA_PAD)], tm_norm)
    tm = _largest_tile(h.shape[0], ROWS_PROJ)
    zkv = _in_proj(h, w_t, D_RWKV, 2 * D_RWKV, COLS_PROJ, tm)
    zparts = (zr, zkv, zl)
    st = jnp.pad(st_shift.reshape(n_s // seq, D_SHIFT), ((0, 0), (0, D_SHIFT_PAD - D_SHIFT)))
    tm_s = _largest_tile(math.gcd(n_p, n_s), ROWS_PREP_SAMPLE, mult=SLAB * seq)
    r, lw, k2, v, a_s, b_s, g_s, bonus_s = _prep_sample(zparts, n_p, n_s, st, seq, wts["prep"], tm=tm_s)
    zpg, y_s, wkv_s = _in_proj_wkv(h, w_t, D_SHIFT, D_POOL + 2 * D_MODEL, COLS_PROJ, tm,
                                   r, lw, k2, v, a_s, b_s, jnp.transpose(st_wkv, (1, 2, 3, 0)), seq)
    return zparts, zpg, (y_s, bonus_s, g_s, jnp.transpose(wkv_s, (3, 0, 1, 2)))


def _last_shift_row(zparts, row0, b, t):
    if b <= 8:
        last = lambda z: jnp.concatenate(
            [lax.slice(z, (row0 + (i + 1) * t - 1, 0), (row0 + (i + 1) * t, z.shape[1])) for i in range(b)])
    else:
        last = lambda z: lax.slice(z, (row0 + t - 1, 0), (row0 + b * t, z.shape[1]), (t, 1))
    return jnp.concatenate([last(z) for z in zparts], axis=-1)[:, None, :D_SHIFT]


def _prompt_layer(x, z, wts):
    b, t, _ = x.shape
    n = b * t
    x2d = x.reshape(n, D_MODEL)
    zparts, zpg = z
    y, bonus, g, wkv, bf16_weights = _wkv_prompt(zparts, b, t, wts["prep"], wts["f32_matmul_weights"])
    wts = wts["with_bf16"](*bf16_weights)
    flat = lambda a: a.reshape(n, a.shape[-1])
    x1, h1 = _merge(flat(y), flat(bonus), flat(g), zpg, zpg, x2d, 0, wts["merge"],
                    tm=_largest_tile(t, ROWS_MERGE), seq_len=t)
    ffn_tm = _largest_tile(t, ROWS_FFN)
    out, tail = _ffn_prompt(x1, h1, t, wts["ffn"], tm=ffn_tm, fk=COLS_FFN)
    shift = _last_shift_row(zparts, 0, b, t)
    pool = jnp.stack([zpg[(i + 1) * t - POOL_HIST:(i + 1) * t, :D_POOL] for i in range(b)])
    tiles = t // ffn_tm
    conv = tail.reshape(b, tiles, 8, D_FF)[:, tiles - 1, 8 - (CONV_W - 1):, :]
    return wts, (out.reshape(b, t, D_MODEL), shift, wkv, pool, conv)


def _sample_layer(x, z, row0, st_pool, st_conv, wts):
    b, t, _ = x.shape
    n = b * t
    x2d = x.reshape(n, D_MODEL)
    zparts, zpg, (y, bonus, g, wkv) = z
    d, pool = _pool_sample(jnp.swapaxes(st_pool, 0, 1), zpg, row0, t)
    pool = jnp.swapaxes(pool, 0, 1)
    x1, h1 = _merge(y, bonus, g, d, zpg, x2d, row0, wts["merge"],
                    tm=_largest_tile(math.gcd(row0, n), ROWS_MERGE))
    out, conv = _ffn_sample(x1, h1, st_conv, t, wts["ffn"], fk=COLS_FFN)
    shift = _last_shift_row(zparts, row0, b, t)
    return out.reshape(b, t, D_MODEL), shift, wkv, pool, conv


def kernel(x_prompt, x_sample, state_shift, state_wkv, state_pool, state_conv, norm_pre_mix, w_in, mu_shift, w0, w2, a0, a2, g2, k_k, k_a, r_k, lnx_w, lnx_b, w_branch_a, pool_w, pool_scale, w_branch_b, w_out, norm_post_mix, norm_pre_ffn, w_ffn_in, conv_w, conv_b, w_ffn_out, norm_post_ffn):
    weights = (norm_pre_mix, w_in, mu_shift, w0, w2, a0, a2, g2, k_k, k_a, r_k, lnx_w, lnx_b,
               w_branch_a, pool_w, pool_scale, w_branch_b, w_out, norm_post_mix,
               norm_pre_ffn, w_ffn_in, conv_w, conv_b, w_ffn_out, norm_post_ffn)
    depth = w_in.shape[0]
    yp, ys = x_prompt, x_sample
    p_states, s_states = [], []
    for l in range(depth):
        wts = _layer_weights(l, *weights)
        n_prompt = yp.shape[0] * yp.shape[1]
        z = _project(yp.reshape(n_prompt, D_MODEL), ys.reshape(-1, D_MODEL), ys.shape[1],
                     state_shift[l], state_wkv[l], wts)
        bf16_wts, (yp, *ps) = _prompt_layer(yp, z[:2], wts)
        ys, *ss = _sample_layer(ys, z, n_prompt, state_pool[l], state_conv[l], bf16_wts)
        p_states.append(ps)
        s_states.append(ss)
    stack = lambda states, i: jnp.stack([s[i] for s in states])
    return (yp, ys,
            stack(p_states, 0), stack(p_states, 1), stack(p_states, 2), stack(p_states, 3),
            stack(s_states, 0), stack(s_states, 1), stack(s_states, 2), stack(s_states, 3))
```

```python
import functools
import math

import jax
import jax.numpy as jnp
from jax import lax
from jax.experimental import pallas as pl
from jax.experimental.pallas import tpu as pltpu

F32 = jnp.float32
BF16 = jnp.bfloat16

D_MODEL = 2048
HEAD_DIM = 64
D_RWKV = 1024
N_HEADS = 16
RANK_W, RANK_A, RANK_G = 64, 64, 160
D_LORA = RANK_W + RANK_A + RANK_G
D_LORA_PAD = 384
D_SHIFT = 3 * D_RWKV + D_LORA
D_SHIFT_PAD = 3 * D_RWKV + D_LORA_PAD
D_POOL = 1024
POOL_WINDOWS = (2, 4, 8, 16)
POOL_GROUP = 256
POOL_HIST = 15
D_FF = 5632
CONV_W = 3
NORM_EPS = 1e-6
GN_EPS = 64e-5
PAST_LEN = 16384

LANES_V7X = 128
SLAB = 8
MXU_DIM_V7X = 256
HEADS_PER_GROUP = MXU_DIM_V7X // HEAD_DIM
N_GROUPS = N_HEADS // HEADS_PER_GROUP
CHUNK = 64
VMEM_LIMIT_V7X = 56 * 1024 * 1024

ROWS_NORM = 512
ROWS_PROJ = 1088
COLS_PROJ = 1024
ROWS_MERGE = 256
ROWS_FFN = 512
COLS_FFN = 512
ROWS_PREP_SAMPLE = 128


def _params(*sem):
    return pltpu.CompilerParams(dimension_semantics=sem, vmem_limit_bytes=VMEM_LIMIT_V7X)


def _mm(a, b):
    return jnp.dot(a.astype(BF16), b.astype(BF16), preferred_element_type=F32)


def _mm_nt(a, b):
    return lax.dot_general(a.astype(BF16), b.astype(BF16), (((1,), (1,)), ((), ())),
                           preferred_element_type=F32)


def _split_hi_lo(x):
    hi = x.astype(BF16)
    lo = (x - hi.astype(F32)).astype(BF16)
    return hi, lo


def _head_sum(x, ones_bd):
    hi, lo = _split_hi_lo(x)
    outs = []
    for gi in range(x.shape[1] // MXU_DIM_V7X):
        sl = slice(MXU_DIM_V7X * gi, MXU_DIM_V7X * (gi + 1))
        outs.append(jnp.dot(hi[:, sl], ones_bd, preferred_element_type=F32)
                    + jnp.dot(lo[:, sl], ones_bd, preferred_element_type=F32))
    return jnp.concatenate(outs, axis=1)


def _rmsnorm(x, g):
    return x * lax.rsqrt(jnp.mean(x * x, axis=-1, keepdims=True) + NORM_EPS) * g


def _replace_step_rows(scr_ref, base, seq, steps):
    nb = base.shape[0] // seq
    for j in range(base.shape[1] // LANES_V7X):
        ls = slice(LANES_V7X * j, LANES_V7X * (j + 1))
        scr_ref[j] = base[:, ls]
        for t, vals in steps:
            scr_ref[j, pl.ds(t, nb, stride=seq), :] = vals[:, ls]
    return jnp.concatenate([scr_ref[j] for j in range(base.shape[1] // LANES_V7X)], axis=1)


def _pre_norm_kernel(prompt_tiles, n_sections, xp_ref, xs_ref, g_ref, *refs):
    w_refs, h_ref, z_refs, wb_refs = (refs[:n_sections], refs[n_sections],
                                      refs[n_sections + 1:2 * n_sections + 1], refs[2 * n_sections + 1:])
    i = pl.program_id(0)

    @pl.when(i == 0)
    def _():
        for w_ref, wb_ref in zip(w_refs, wb_refs):
            wb_ref[...] = w_ref[...].astype(BF16)

    @pl.when(i < prompt_tiles)
    def _():
        h_ref[...] = _rmsnorm(xp_ref[...], g_ref[...]).astype(BF16)

    @pl.when(i >= prompt_tiles)
    def _():
        h_ref[...] = _rmsnorm(xs_ref[...], g_ref[...]).astype(BF16)

    h = h_ref[...]
    for z_ref, wb_ref in zip(z_refs, wb_refs):
        z_ref[...] = lax.dot_general(h, wb_ref[...], (((1,), (1,)), ((), ())), preferred_element_type=F32)


def _pre_norm(xp, xs, g, w_t, sections, tm):
    pt, st = xp.shape[0] // tm, xs.shape[0] // tm
    n = xp.shape[0] + xs.shape[0]
    assert all(off % SLAB == 0 for off, _ in sections)
    rows = pl.BlockSpec((tm, D_MODEL), lambda i: (i, 0))
    h, *zs = pl.pallas_call(
        functools.partial(_pre_norm_kernel, pt, len(sections)),
        grid=(pt + st,),
        in_specs=[pl.BlockSpec((tm, D_MODEL), lambda i: (jnp.minimum(i, pt - 1), 0)),
                  pl.BlockSpec((tm, D_MODEL), lambda i: (jnp.maximum(i - pt, 0), 0)),
                  _const_spec(g.shape)]
        + [pl.BlockSpec((pl.Element(width), pl.Element(D_MODEL)),
                        functools.partial(lambda o, i: (SLAB * o, 0), off // SLAB),
                        pipeline_mode=pl.Buffered(1)) for off, width in sections],
        out_specs=[rows] + [pl.BlockSpec((tm, width), lambda i: (i, 0)) for _, width in sections],
        out_shape=[jax.ShapeDtypeStruct((n, D_MODEL), BF16)]
        + [jax.ShapeDtypeStruct((n, width), F32) for _, width in sections],
        scratch_shapes=[pltpu.VMEM((width, D_MODEL), BF16) for _, width in sections],
        compiler_params=_params("arbitrary"),
        name="pre_norm",
    )(xp, xs, g, *([w_t] * len(sections)))
    return h, zs


def _in_proj_kernel(h_ref, w_ref, o_ref, wb_ref):
    @pl.when(pl.program_id(1) == 0)
    def _():
        wb_ref[...] = w_ref[...].astype(BF16)

    o_ref[...] = lax.dot_general(h_ref[...], wb_ref[...], (((1,), (1,)), ((), ())),
                                 preferred_element_type=F32)


def _in_proj(h, w_t, off, width, tn, tm):
    n = h.shape[0]
    assert width % tn == 0 and n % tm == 0
    return pl.pallas_call(
        _in_proj_kernel,
        grid=(width // tn, n // tm),
        in_specs=[pl.BlockSpec((tm, D_MODEL), lambda j, i: (i, 0)),
                  pl.BlockSpec((pl.Element(tn), pl.Element(D_MODEL)),
                               lambda j, i: (SLAB * (off // SLAB + j * (tn // SLAB)), 0))],
        out_specs=pl.BlockSpec((tm, tn), lambda j, i: (i, j)),
        out_shape=jax.ShapeDtypeStruct((n, width), F32),
        scratch_shapes=[pltpu.VMEM((tn, D_MODEL), BF16)],
        compiler_params=_params("arbitrary", "arbitrary"),
        name="in_proj",
    )(h, w_t)


def _prep_math(zs, prev, mu, w_lora, w0, a0, k_k, k_a, r_k, ones_bd):
    f = zs + (prev - zs) * mu
    r = f[:, 0:D_RWKV]
    k = f[:, D_RWKV:2 * D_RWKV]
    v = f[:, 2 * D_RWKV:3 * D_RWKV]
    low = f[:, 3 * D_RWKV:3 * D_RWKV + D_LORA_PAD]
    lane = lax.broadcasted_iota(jnp.int32, low.shape, 1)
    act = jnp.where(lane < RANK_W, jnp.tanh(low),
                    jnp.where(lane < RANK_W + RANK_A, low,
                              jnp.where(lane < D_LORA, jax.nn.sigmoid(low), 0.0)))
    lora = _mm(act, w_lora)
    u = w0 + lora[:, 0:D_RWKV]
    lw = -math.exp(-0.5) * jax.nn.sigmoid(u)
    a = jax.nn.sigmoid(a0 + lora[:, D_RWKV:2 * D_RWKV])
    g = lora[:, 2 * D_RWKV:3 * D_RWKV]
    kk = k * k_k
    kk = kk * jnp.minimum(lax.rsqrt(_head_sum(kk * kk, ones_bd)), 1e12)
    k2 = k * (1.0 + (a - 1.0) * k_a)
    bonus = _head_sum(r * k2 * r_k, ones_bd) * v
    return r, lw, k2, v, -kk, kk * a, g, bonus


N_PREP_WEIGHTS = 8


def _prep_sample_kernel(seq, n_parts, *refs):
    z_refs, st_ref, refs = refs[:n_parts], refs[n_parts], refs[n_parts + 1:]
    w_refs, o_refs, prev_ref = refs[:N_PREP_WEIGHTS], refs[N_PREP_WEIGHTS:-1], refs[-1]
    zs = jnp.concatenate([z[...] for z in z_refs], axis=1)
    prev = _replace_step_rows(prev_ref, pltpu.roll(zs, 1, 0), seq, [(0, st_ref[...])])
    for o_ref, o in zip(o_refs, _prep_math(zs, prev, *[w[...] for w in w_refs])):
        o_ref[...] = o


def _const_spec(shape):
    nd = len(shape)
    return pl.BlockSpec(shape, lambda *_: (0,) * nd)


def _prep_sample(zparts, row0, n, st, seq, wts, tm):
    assert tm % (SLAB * seq) == 0 and n % tm == 0 and row0 % tm == 0 and len(wts) == N_PREP_WEIGHTS
    first = row0 // tm
    outs = [jax.ShapeDtypeStruct((n, D_RWKV), F32)] * 8
    row_spec = pl.BlockSpec((tm, D_RWKV), lambda i: (i, 0))
    return pl.pallas_call(
        functools.partial(_prep_sample_kernel, seq, len(zparts)),
        grid=(n // tm,),
        in_specs=[pl.BlockSpec((tm, z.shape[1]), lambda i: (first + i, 0)) for z in zparts]
        + [pl.BlockSpec((tm // seq, D_SHIFT_PAD), lambda i: (i, 0))]
        + [_const_spec(w.shape) for w in wts],
        out_specs=[row_spec] * 8,
        out_shape=outs,
        scratch_shapes=[pltpu.VMEM((D_SHIFT_PAD // LANES_V7X, tm, LANES_V7X), F32)],
        compiler_params=_params("parallel"),
        name="prep_sample",
    )(*zparts, st, *wts)


def _expand_bd(a, bd_mask):
    return jnp.where(bd_mask, jnp.concatenate([a] * HEADS_PER_GROUP, axis=0), 0.0)


def _chunk_cumsum(x):
    row = lax.broadcasted_iota(jnp.int32, x.shape, 0)
    s = 1
    while s < x.shape[0]:
        x = x + jnp.where(row >= s, pltpu.roll(x, s, 0), 0.0)
        s *= 2
    return x


def _wkv_intra(xs, bts, kts, vs, pzs, masks):
    bd_mask, strict_mask, incl_mask, eye_c, head_masks = masks
    c = vs[0].shape[0]
    ystacks = [jnp.concatenate([jnp.where(m, bt, 0.0) for m in head_masks]
                               + [jnp.where(m, kt, 0.0) for m in head_masks], axis=0)
               for bt, kt in zip(bts, kts)]
    grams = [_mm_nt(x, ys) for x, ys in zip(xs, ystacks)]
    l_abs = [jnp.where(strict_mask, g[0:c, 0:4 * c], 0.0) for g in grams]
    l_aks = [jnp.where(strict_mask, g[0:c, 4 * c:8 * c], 0.0) for g in grams]
    m_rbs = [jnp.where(incl_mask, g[c:2 * c, 0:4 * c], 0.0) for g in grams]
    m_rks = [jnp.where(incl_mask, g[c:2 * c, 4 * c:8 * c], 0.0) for g in grams]

    tinvs = [eye_c + l for l in l_abs]
    lps = [_mm(l, _expand_bd(l, bd_mask)) for l in l_abs]
    p = 2
    while True:
        rhss = [_expand_bd(lp, bd_mask) for lp in lps]
        if 2 * p >= c:
            tinvs = [t + _mm(t, rhs) for t, rhs in zip(tinvs, rhss)]
            break
        ress = [_mm(jnp.concatenate([lp, t], axis=0), rhs) for lp, t, rhs in zip(lps, tinvs, rhss)]
        lps = [res[0:c] for res in ress]
        tinvs = [t + res[c:2 * c] for t, res in zip(tinvs, ress)]
        p *= 2

    v_bds = [_expand_bd(v, bd_mask) for v in vs]
    ws = [pz[0:c] + _mm(l_ak, v_bd) for pz, l_ak, v_bd in zip(pzs, l_aks, v_bds)]
    us = [_mm(t, _expand_bd(w, bd_mask)) for t, w in zip(tinvs, ws)]
    ys = [pz[c:2 * c] + _mm(jnp.concatenate([m_rb, m_rk], axis=1),
                            jnp.concatenate([_expand_bd(u, bd_mask), v_bd], axis=0))
          for pz, m_rb, m_rk, u, v_bd in zip(pzs, m_rbs, m_rks, us, v_bds)]
    return ys, us


def _wkv_chunk(groups, masks):
    c = groups[0][0].shape[0]
    cls = [_chunk_cumsum(g[1]) for g in groups]
    xs = [jnp.concatenate([g[4] * jnp.exp(cl - g[1]), g[0] * jnp.exp(cl)], axis=0)
          for g, cl in zip(groups, cls)]
    pzs = [_mm_nt(x, g[6]) for x, g in zip(xs, groups)]
    e_negs = [jnp.exp(-cl) for cl in cls]
    ys, us = _wkv_intra(xs, [g[5] * e for g, e in zip(groups, e_negs)],
                        [g[2] * e for g, e in zip(groups, e_negs)], [g[3] for g in groups], pzs, masks)
    dss = []
    for g, cl, u in zip(groups, cls, us):
        e_rem = jnp.exp(cl[c - 1:c, :] - cl)
        uv_t = jnp.concatenate([u, g[3]], axis=0).T
        bk = jnp.concatenate([g[5] * e_rem, g[2] * e_rem], axis=0)
        dss.append(_mm(uv_t, bk))
    return [(y, g[6] * jnp.exp(cl[c - 1:c, :]) + jnp.where(masks[0], ds, 0.0))
            for y, g, cl, ds in zip(ys, groups, cls, dss)]


def _wkv_masks(c):
    n = HEADS_PER_GROUP * c
    rr = lax.broadcasted_iota(jnp.int32, (n, n), 0)
    cc = lax.broadcasted_iota(jnp.int32, (n, n), 1)
    bd_mask = (rr // c) == (cc // c)
    t = lax.broadcasted_iota(jnp.int32, (c, n), 0)
    s = lax.broadcasted_iota(jnp.int32, (c, n), 1) % c
    strict_mask = t > s
    incl_mask = t >= s
    eye_c = jnp.where(t == s, 1.0, 0.0).astype(F32)
    lane = lax.broadcasted_iota(jnp.int32, (c, MXU_DIM_V7X), 1)
    head_masks = [(lane // HEAD_DIM) == h for h in range(HEADS_PER_GROUP)]
    return bd_mask, strict_mask, incl_mask, eye_c, head_masks


def _wkv_prompt_kernel(nb, n_parts, n_casts, *refs):
    it = iter(refs)
    take = lambda k: [next(it) for _ in range(k)]
    z_refs = [take(n_parts) for _ in range(nb)]
    pw_refs, w_refs = take(N_PREP_WEIGHTS), take(n_casts)
    y_ref, bonus_ref, g_ref, sout_ref = take(4)
    wb_refs = take(n_casts)
    s_ref, carry_ref = take(2)
    ci = pl.program_id(0)

    @pl.when(ci == 0)
    def _():
        s_ref[...] = jnp.zeros_like(s_ref)
        carry_ref[...] = jnp.zeros_like(carry_ref)

    for w_ref, wb_ref in zip(w_refs, wb_refs):
        wb_ref[...] = w_ref[...].astype(BF16)

    zs_parts, prev_parts = [], []
    for bi in range(nb):
        zs = jnp.concatenate([z[...] for z in z_refs[bi]], axis=1)
        row = lax.broadcasted_iota(jnp.int32, zs.shape, 0)
        prev_parts.append(jnp.where(row == 0, carry_ref[bi, 0:1, :], pltpu.roll(zs, 1, 0)))
        carry_ref[bi, 0:1, :] = zs[CHUNK - 1:, :]
        zs_parts.append(zs)
    *scan_ops, g, bonus = _prep_math(
        jnp.concatenate(zs_parts, axis=0), jnp.concatenate(prev_parts, axis=0), *[w[...] for w in pw_refs])
    for bi in range(nb):
        rows = slice(CHUNK * bi, CHUNK * (bi + 1))
        g_ref[bi] = g[rows]
        bonus_ref[bi] = bonus[rows]

    masks = _wkv_masks(CHUNK)
    chains = [(bi, gi) for bi in range(nb) for gi in range(N_GROUPS)]
    lanes = lambda gi: slice(MXU_DIM_V7X * gi, MXU_DIM_V7X * (gi + 1))
    groups = [tuple(x[CHUNK * bi:CHUNK * (bi + 1), lanes(gi)] for x in scan_ops)
              + (s_ref[bi, gi],) for bi, gi in chains]
    for (bi, gi), (y, s_new) in zip(chains, _wkv_chunk(groups, masks)):
        y_ref[bi, :, lanes(gi)] = y
        s_ref[bi, gi] = s_new

    @pl.when(ci == pl.num_programs(0) - 1)
    def _():
        for bi, gi in chains:
            s_bd = s_ref[bi, gi]
            for h in range(HEADS_PER_GROUP):
                hs = slice(HEAD_DIM * h, HEAD_DIM * (h + 1))
                sout_ref[bi, HEADS_PER_GROUP * gi + h] = s_bd[hs, hs]


def _wkv_prompt(zparts, b, t, prep_wts, weights):
    steps = t // CHUNK
    assert len(prep_wts) == N_PREP_WEIGHTS
    seq_rows = lambda w, bi: pl.BlockSpec((CHUNK, w), lambda ci: (bi * steps + ci, 0))
    spec = pl.BlockSpec((b, CHUNK, D_RWKV), lambda ci: (0, ci, 0))
    state = (b, N_GROUPS, MXU_DIM_V7X, MXU_DIM_V7X)
    out_state = (b, N_HEADS, HEAD_DIM, HEAD_DIM)
    bf16_rows = 2 * SLAB
    assert all(w.shape[0] % (steps * bf16_rows) == 0 for w in weights)
    slabs = [pl.BlockSpec((w.shape[0] // steps, w.shape[1]), lambda ci: (ci, 0)) for w in weights]
    rows = jax.ShapeDtypeStruct((b, t, D_RWKV), F32)
    y, bonus, g, s, *wb = pl.pallas_call(
        functools.partial(_wkv_prompt_kernel, b, len(zparts), len(weights)),
        grid=(steps,),
        in_specs=[seq_rows(z.shape[1], bi) for bi in range(b) for z in zparts]
        + [_const_spec(w.shape) for w in prep_wts] + slabs,
        out_specs=[spec, spec, spec, _const_spec(out_state)] + slabs,
        out_shape=[rows, rows, rows, jax.ShapeDtypeStruct(out_state, F32)]
        + [jax.ShapeDtypeStruct(w.shape, BF16) for w in weights],
        scratch_shapes=[pltpu.VMEM(state, F32), pltpu.VMEM((b, SLAB, D_SHIFT_PAD), F32)],
        compiler_params=_params("arbitrary"),
        name="wkv_prompt",
    )(*(list(zparts) * b), *prep_wts, *weights)
    return y, bonus, g, s, wb


HEAD_PAIR = 2 * HEAD_DIM


STEP_ROWS = HEAD_DIM // 2
STEP_UNITS = N_HEADS * HEAD_DIM // STEP_ROWS
COMPANION_PARTS = 8


def _wkv_steps_unit(seq, unit, r_ref, lw_ref, k2_ref, v_ref, as_ref, bs_ref, s0_ref, y_ref, sout_ref,
                    op_ref, vt_ref, yt_ref, companion=lambda part: None):
    nb = r_ref.shape[0] // seq
    sub = unit % 4

    @pl.when(sub == 0)
    def _():
        for t in range(seq):
            rows = pl.ds(t, nb, stride=seq)
            op_ref[0, t] = r_ref[rows, :].T
            op_ref[1, t] = jnp.exp(lw_ref[rows, :]).T
            op_ref[2, t] = k2_ref[rows, :].T
            op_ref[3, t] = as_ref[rows, :].T
            op_ref[4, t] = bs_ref[rows, :].T
            vt_ref[t] = v_ref[rows, :].T

    k0 = pl.multiple_of((sub // 2) * HEAD_DIM, HEAD_DIM)
    row0 = k0 + (sub % 2) * STEP_ROWS
    ks = pl.ds(k0, HEAD_DIM)
    for vi in range(STEP_ROWS):
        if vi % (STEP_ROWS // COMPANION_PARTS) == 0:
            companion(vi // (STEP_ROWS // COMPANION_PARTS))
        s = s0_ref[0, vi]
        for t in range(seq):
            sa = jnp.sum(s * op_ref[3, t, ks, :], axis=0, keepdims=True)
            vrow = vt_ref[t, pl.ds(row0 + vi, 1), :]
            s = s * op_ref[1, t, ks, :] + sa * op_ref[4, t, ks, :] + vrow * op_ref[2, t, ks, :]
            yt_ref[t, pl.ds(row0 + vi, 1), :] = jnp.sum(s * op_ref[0, t, ks, :], axis=0, keepdims=True)
        sout_ref[0, vi] = s

    @pl.when(sub == 3)
    def _():
        for t in range(seq):
            y_ref[pl.ds(t, nb, stride=seq), :] = yt_ref[t].T


def _in_proj_wkv_kernel(seq, steps_per_tile, h_ref, w_ref, r_ref, lw_ref, k2_ref, v_ref, as_ref, bs_ref,
                        s0_ref, o_ref, y_ref, sout_ref, wb_ref, op_ref, vt_ref, yt_ref):
    j, i = pl.program_id(0), pl.program_id(1)

    @pl.when(i == 0)
    def _():
        wb_ref[...] = w_ref[...].astype(BF16)

    def project(part):
        tn = o_ref.shape[1] // (COMPANION_PARTS // 2)
        tm = o_ref.shape[0] // 2
        cols = slice((part // 2) * tn, (part // 2 + 1) * tn)
        rows = slice((part % 2) * tm, (part % 2 + 1) * tm)
        o_ref[rows, cols] = lax.dot_general(h_ref[rows, :], wb_ref[cols, :], (((1,), (1,)), ((), ())),
                                            preferred_element_type=F32)

    step = j * steps_per_tile + i

    @pl.when(step < STEP_UNITS)
    def _():
        _wkv_steps_unit(seq, step, r_ref, lw_ref, k2_ref, v_ref, as_ref, bs_ref, s0_ref, y_ref, sout_ref,
                        op_ref, vt_ref, yt_ref, companion=project)

    @pl.when(step >= STEP_UNITS)
    def _():
        for part in range(COMPANION_PARTS):
            project(part)


def _in_proj_wkv(h, w_t, off, width, tn, tm, r, lw, k2, v, a_s, b_s, s0_native, seq):
    n, n_s = h.shape[0], r.shape[0]
    nb = n_s // seq
    tiles = n // tm
    assert width % tn == 0 and n % tm == 0 and (width // tn) * tiles >= STEP_UNITS
    assert s0_native.shape == (N_HEADS, HEAD_DIM, HEAD_DIM, nb) and nb == LANES_V7X
    unit = lambda j, i: jnp.minimum(j * tiles + i, STEP_UNITS - 1)
    vec = pl.BlockSpec((n_s, HEAD_PAIR), lambda j, i: (0, unit(j, i) // 4))
    slab = pl.BlockSpec((1, STEP_ROWS, HEAD_DIM, nb), lambda j, i: (unit(j, i) // 2, unit(j, i) % 2, 0, 0))
    return pl.pallas_call(
        functools.partial(_in_proj_wkv_kernel, seq, tiles),
        grid=(width // tn, tiles),
        in_specs=[pl.BlockSpec((tm, D_MODEL), lambda j, i: (i, 0)),
                  pl.BlockSpec((pl.Element(tn), pl.Element(D_MODEL)),
                               lambda j, i: (SLAB * (off // SLAB + j * (tn // SLAB)), 0))]
        + [vec] * 6 + [slab],
        out_specs=[pl.BlockSpec((tm, tn), lambda j, i: (i, j)), vec, slab],
        out_shape=[jax.ShapeDtypeStruct((n, width), F32), jax.ShapeDtypeStruct((n_s, D_RWKV), F32),
                   jax.ShapeDtypeStruct(s0_native.shape, F32)],
        scratch_shapes=[pltpu.VMEM((tn, D_MODEL), BF16), pltpu.VMEM((5, seq, HEAD_PAIR, nb), F32),
                        pltpu.VMEM((seq, HEAD_PAIR, nb), F32), pltpu.VMEM((seq, HEAD_PAIR, nb), F32)],
        compiler_params=_params("arbitrary", "arbitrary"),
        name="in_proj_wkv",
    )(h, w_t, r, lw, k2, v, a_s, b_s, s0_native)


POOL_CARRY = 16


def _pool_prompt_tile(zp, carry_ref, tile_in_seq):
    tm = zp.shape[0]

    @pl.when(tile_in_seq == 0)
    def _():
        carry_ref[...] = jnp.zeros_like(carry_ref)

    buf = jnp.concatenate([carry_ref[...], zp], axis=0)
    carry_ref[...] = zp[tm - POOL_CARRY:, :]
    pos = (tile_in_seq * tm + lax.broadcasted_iota(jnp.int32, (tm, POOL_GROUP), 0) + 1).astype(F32)
    ds = []
    for gi, win in enumerate(POOL_WINDOWS):
        sl = slice(POOL_GROUP * gi, POOL_GROUP * (gi + 1))
        acc = buf[:, sl]
        s = 1
        while s < win:
            acc = acc + pltpu.roll(acc, s, 0)
            s *= 2
        ds.append(acc[POOL_CARRY:, :] / jnp.minimum(float(win), pos) - zp[:, sl])
    return jnp.concatenate(ds, axis=1)


def _pool_sample_kernel(seq, hist_ref, zp_ref, d_ref, new_ref):
    nb = hist_ref.shape[1]
    step = lambda t: zp_ref[pl.ds(t, nb, stride=seq), :]
    past = lambda j: step(j) if j >= 0 else hist_ref[POOL_HIST + j]
    tiles_per_group = POOL_GROUP // LANES_V7X
    for gi, win in enumerate(POOL_WINDOWS):
        @pl.when(pl.program_id(0) // tiles_per_group == gi)
        def _():
            for t in range(seq):
                acc = past(t)
                for j in range(1, win):
                    acc = acc + past(t - j)
                d_ref[pl.ds(t, nb, stride=seq), :] = acc / float(min(win, PAST_LEN + 1)) - past(t)
    for j in range(POOL_HIST):
        src = j + seq
        new_ref[j] = hist_ref[src] if src < POOL_HIST else step(src - POOL_HIST)


def _pool_sample(hist, zp, row0, seq):
    nb = hist.shape[1]
    n = nb * seq
    assert row0 % n == 0
    hist_spec = pl.BlockSpec((POOL_HIST, nb, LANES_V7X), lambda j: (0, 0, j))
    return pl.pallas_call(
        functools.partial(_pool_sample_kernel, seq),
        grid=(D_POOL // LANES_V7X,),
        in_specs=[hist_spec, pl.BlockSpec((n, LANES_V7X), lambda j: (row0 // n, j))],
        out_specs=[pl.BlockSpec((n, LANES_V7X), lambda j: (0, j)), hist_spec],
        out_shape=[jax.ShapeDtypeStruct((n, D_POOL), F32), jax.ShapeDtypeStruct(hist.shape, F32)],
        compiler_params=_params("parallel"),
        name="pool_sample",
    )(hist, zp)


def _merge_kernel(seq_tiles, y_ref, bonus_ref, g_ref, d_ref, zga_ref, zgb_ref, x_ref,
                  lw_ref, lb_ref, ones_ref, wa_ref, pw_ref, ps_ref, wb_ref,
                  wo_ref, gpost_ref, gnext_ref, o_ref, h_ref, *carry):
    if seq_tiles is None:
        d = d_ref[...]
    else:
        d = _pool_prompt_tile(d_ref[...], carry[0], pl.program_id(0) % seq_tiles)
    ones_bd = ones_ref[...]
    y = y_ref[...]
    mu = _head_sum(y, ones_bd) * (1.0 / HEAD_DIM)
    yb = jnp.concatenate(
        [_mm(d[:, POOL_GROUP * gi:POOL_GROUP * (gi + 1)], pw_ref[gi]) for gi in range(len(POOL_WINDOWS))],
        axis=1) * ps_ref[...]
    yc = y - mu
    var = _head_sum(yc * yc, ones_bd) * (1.0 / HEAD_DIM)
    mb = jax.nn.sigmoid(zgb_ref[...]) * _mm(yb, wb_ref[...])
    yn = yc * lax.rsqrt(var + GN_EPS) * lw_ref[...] + lb_ref[...]
    ya = (yn + bonus_ref[...]) * g_ref[...]
    m = jax.nn.sigmoid(zga_ref[...]) * _mm(ya, wa_ref[...]) + mb
    x1 = x_ref[...] + _rmsnorm(_mm(m, wo_ref[...]), gpost_ref[...])
    o_ref[...] = x1
    h_ref[...] = _rmsnorm(x1, gnext_ref[...]).astype(BF16)


def _merge(y, bonus, g, d, zpg, x2d, row0, wts, tm, seq_len=None):
    n = y.shape[0]
    assert row0 % tm == 0 and tm % SLAB == 0 and (seq_len is None or (seq_len % tm == 0 and tm >= POOL_CARRY))
    first = row0 // tm
    half = pl.BlockSpec((tm, D_RWKV), lambda i: (i, 0))
    full = pl.BlockSpec((tm, D_MODEL), lambda i: (i, 0))
    gate_at = lambda col: pl.BlockSpec((pl.Element(tm), pl.Element(D_MODEL)), lambda i: (tm * (first + i), col))
    resident = [pl.BlockSpec(w.shape, functools.partial(lambda nd, i: (0,) * nd, w.ndim),
                             pipeline_mode=pl.Buffered(1)) for w in wts]
    return pl.pallas_call(
        functools.partial(_merge_kernel, None if seq_len is None else seq_len // tm),
        grid=(n // tm,),
        in_specs=[half] * 4 + [gate_at(D_POOL), gate_at(D_POOL + D_MODEL)] + [full] + resident,
        out_specs=[full, full],
        out_shape=[jax.ShapeDtypeStruct((n, D_MODEL), F32), jax.ShapeDtypeStruct((n, D_MODEL), BF16)],
        scratch_shapes=[] if seq_len is None else [pltpu.VMEM((POOL_CARRY, D_POOL), F32)],
        compiler_params=_params("parallel" if seq_len is None else "arbitrary"),
        name="merge",
    )(y, bonus, g, d, zpg, zpg, x2d, *wts)


def _gelu_tanh(x):
    return 0.5 * x * (1.0 + jnp.tanh(0.7978845608028654 * (x + 0.044715 * x * x * x)))


def _ffn_body(x_ref, h_ref, wg_ref, wu_ref, cw_ref, cb_ref, wo_ref, gp_ref,
              o_ref, acc_ref, prev1, prev2, emit_tail):
    c = pl.program_id(1)

    @pl.when(c == 0)
    def _():
        acc_ref[...] = jnp.zeros_like(acc_ref)

    h = h_ref[...]
    gate = jnp.dot(h, wg_ref[...], preferred_element_type=F32)
    up = jnp.dot(h, wu_ref[...], preferred_element_type=F32)
    cw = cw_ref[...]
    cv = cb_ref[...] + cw[0:1, :] * prev2(gate) + cw[1:2, :] * prev1(gate) + cw[2:3, :] * gate
    emit_tail(gate)
    acc_ref[...] += _mm(_gelu_tanh(cv) * up, wo_ref[...])

    @pl.when(c == pl.num_programs(1) - 1)
    def _():
        o_ref[...] = x_ref[...] + _rmsnorm(acc_ref[...], gp_ref[...])


def _ffn_prompt_kernel(tiles_per_seq, x_ref, h_ref, wg_ref, wu_ref, cw_ref, cb_ref, wo_ref, gp_ref,
                       o_ref, tail_ref, acc_ref, carry_ref):
    i = pl.program_id(0)
    c = pl.program_id(1)

    @pl.when(i % tiles_per_seq == 0)
    def _():
        carry_ref[c] = jnp.zeros(carry_ref.shape[1:], F32)

    hist = carry_ref[c]

    def shifted(gate, s):
        row = lax.broadcasted_iota(jnp.int32, gate.shape, 0)
        rolled = pltpu.roll(gate, s, 0)
        out = rolled
        for j in range(s):
            out = jnp.where(row == j, hist[8 - s + j:9 - s + j, :], out)
        return out

    def emit_tail(gate):
        tail_ref[...] = gate[gate.shape[0] - tail_ref.shape[0]:, :]

    _ffn_body(x_ref, h_ref, wg_ref, wu_ref, cw_ref, cb_ref, wo_ref, gp_ref, o_ref,
              acc_ref, lambda g: shifted(g, 1), lambda g: shifted(g, 2), emit_tail)
    carry_ref[c] = tail_ref[...]


def _ffn_sample_kernel(seq, x_ref, st_ref, h_ref, wg_ref, wu_ref, cw_ref, cb_ref, wo_ref, gp_ref,
                       o_ref, tail_ref, acc_ref, g_ref, p1_ref, p2_ref):
    nb = st_ref.shape[0]
    st0 = st_ref[:, 0, :]
    st1 = st_ref[:, 1, :]
    prev1 = lambda gate: _replace_step_rows(p1_ref, pltpu.roll(gate, 1, 0), seq, [(0, st1)])
    prev2 = lambda gate: _replace_step_rows(p2_ref, pltpu.roll(gate, 2, 0), seq, [(0, st0), (1, st1)])

    def emit_tail(gate):
        tiles = gate.shape[1] // LANES_V7X
        for j in range(tiles):
            g_ref[j] = gate[:, LANES_V7X * j:LANES_V7X * (j + 1)]
        for s in range(CONV_W - 1):
            rows = pl.ds(seq - (CONV_W - 1) + s, nb, stride=seq)
            tail_ref[:, s, :] = jnp.concatenate([g_ref[j, rows, :] for j in range(tiles)], axis=1)

    _ffn_body(x_ref, h_ref, wg_ref, wu_ref, cw_ref, cb_ref, wo_ref, gp_ref, o_ref,
              acc_ref, prev1, prev2, emit_tail)


def _ffn_prompt(x2d, h, seq_len, wts, tm, fk):
    n = x2d.shape[0]
    nc = D_FF // fk
    w_in, cw, cb, w_out, gp = wts
    full = pl.BlockSpec((tm, D_MODEL), lambda i, c: (i, 0))
    return pl.pallas_call(
        functools.partial(_ffn_prompt_kernel, seq_len // tm),
        grid=(n // tm, nc),
        in_specs=[full, full,
                  pl.BlockSpec((D_MODEL, fk), lambda i, c: (0, c)),
                  pl.BlockSpec((D_MODEL, fk), lambda i, c: (0, c + nc)),
                  pl.BlockSpec((CONV_W, fk), lambda i, c: (0, c)),
                  pl.BlockSpec((1, fk), lambda i, c: (0, c)),
                  pl.BlockSpec((fk, D_MODEL), lambda i, c: (c, 0)),
                  _const_spec(gp.shape)],
        out_specs=[full, pl.BlockSpec((None, 8, fk), lambda i, c: (i, 0, c))],
        out_shape=[jax.ShapeDtypeStruct((n, D_MODEL), F32),
                   jax.ShapeDtypeStruct((n // tm, 8, D_FF), F32)],
        scratch_shapes=[pltpu.VMEM((tm, D_MODEL), F32), pltpu.VMEM((nc, 8, fk), F32)],
        compiler_params=_params("arbitrary", "arbitrary"),
        name="ffn_prompt",
    )(x2d, h, w_in, w_in, cw, cb, w_out, gp)


def _ffn_sample(x2d, h, st, seq, wts, fk):
    n = x2d.shape[0]
    nb = n // seq
    assert seq >= CONV_W - 1 and st.shape == (nb, CONV_W - 1, D_FF)
    nc = D_FF // fk
    w_in, cw, cb, w_out, gp = wts
    full = pl.BlockSpec((n, D_MODEL), lambda i, c: (0, 0))
    cols = pl.BlockSpec((nb, CONV_W - 1, fk), lambda i, c: (0, 0, c))
    return pl.pallas_call(
        functools.partial(_ffn_sample_kernel, seq),
        grid=(1, nc),
        in_specs=[full, cols, full,
                  pl.BlockSpec((D_MODEL, fk), lambda i, c: (0, c)),
                  pl.BlockSpec((D_MODEL, fk), lambda i, c: (0, c + nc)),
                  pl.BlockSpec((CONV_W, fk), lambda i, c: (0, c)),
                  pl.BlockSpec((1, fk), lambda i, c: (0, c)),
                  pl.BlockSpec((fk, D_MODEL), lambda i, c: (c, 0)),
                  _const_spec(gp.shape)],
        out_specs=[full, cols],
        out_shape=[jax.ShapeDtypeStruct((n, D_MODEL), F32), jax.ShapeDtypeStruct(st.shape, F32)],
        scratch_shapes=[pltpu.VMEM((n, D_MODEL), F32)]
        + [pltpu.VMEM((fk // LANES_V7X, n, LANES_V7X), F32)] * 3,
        compiler_params=_params("arbitrary", "arbitrary"),
        name="ffn_sample",
    )(x2d, st, h, w_in, w_in, cw, cb, w_out, gp)


def _row(v):
    return v.reshape(1, -1).astype(F32)


def _ones_bd():
    i = jnp.arange(MXU_DIM_V7X) // HEAD_DIM
    return (i[:, None] == i[None, :]).astype(BF16)


def _layer_weights(l, norm_pre_mix, w_in, mu_shift, w0, w2, a0, a2, g2, k_k, k_a, r_k, lnx_w, lnx_b,
                   w_branch_a, pool_w, pool_scale, w_branch_b, w_out, norm_post_mix,
                   norm_pre_ffn, w_ffn_in, conv_w, conv_b, w_ffn_out, norm_post_ffn):
    w_t = jnp.swapaxes(w_in[l], 0, 1)
    mu = jnp.pad(mu_shift[l], (0, D_SHIFT_PAD - D_SHIFT)).reshape(1, -1)
    w_lora = jnp.zeros((D_LORA_PAD, 3 * D_RWKV), F32)
    w_lora = w_lora.at[0:RANK_W, 0:D_RWKV].set(w2[l])
    w_lora = w_lora.at[RANK_W:RANK_W + RANK_A, D_RWKV:2 * D_RWKV].set(a2[l])
    w_lora = w_lora.at[RANK_W + RANK_A:D_LORA, 2 * D_RWKV:].set(g2[l])
    ones_bd = _ones_bd()

    def with_bf16(wa, pw, wb, wo, wfi, wfo):
        return dict(
            merge=(_row(lnx_w[l]), _row(lnx_b[l]), ones_bd, wa,
                   pw.reshape(pool_w.shape[1:]), _row(pool_scale[l]), wb,
                   wo, _row(norm_post_mix[l]), _row(norm_pre_ffn[l])),
            ffn=(wfi, conv_w[l].astype(F32), _row(conv_b[l]), wfo, _row(norm_post_ffn[l])),
        )

    return dict(
        in_proj=(_row(norm_pre_mix[l]), w_t),
        prep=(mu, w_lora.astype(BF16), _row(w0[l]), _row(a0[l]), _row(k_k[l]), _row(k_a[l]), _row(r_k[l]),
              ones_bd),
        f32_matmul_weights=[w_branch_a[l], pool_w[l].reshape(-1, POOL_GROUP), w_branch_b[l], w_out[l],
                            w_ffn_in[l], w_ffn_out[l]],
        with_bf16=with_bf16,
    )


def _largest_tile(n, cap, mult=16):
    best = None
    for d in range(mult, min(n, cap) + 1, mult):
        if n % d == 0:
            best = d
    assert best is not None, (n, cap)
    return best


def _project(xp2d, xs2d, seq, st_shift, st_wkv, wts):
    g, w_t = wts["in_proj"]
    n_p, n_s = xp2d.shape[0], xs2d.shape[0]
    tm_norm = _largest_tile(math.gcd(n_p, n_s), ROWS_NORM)
    h, (zr, zl) = _pre_norm(xp2d, xs2d, g, w_t, [(0, D_RWKV), (3 * D_RWKV, D_LORA_PAD)], tm_norm)
    tm = _largest_tile(h.shape[0], ROWS_PROJ)
    zkv = _in_proj(h, w_t, D_RWKV, 2 * D_RWKV, COLS_PROJ, tm)
    zparts = (zr, zkv, zl)
    st = jnp.pad(st_shift.reshape(n_s // seq, D_SHIFT), ((0, 0), (0, D_SHIFT_PAD - D_SHIFT)))
    tm_s = _largest_tile(math.gcd(n_p, n_s), ROWS_PREP_SAMPLE, mult=SLAB * seq)
    r, lw, k2, v, a_s, b_s, g_s, bonus_s = _prep_sample(zparts, n_p, n_s, st, seq, wts["prep"], tm=tm_s)
    zpg, y_s, wkv_s = _in_proj_wkv(h, w_t, D_SHIFT, D_POOL + 2 * D_MODEL, COLS_PROJ, tm,
                                   r, lw, k2, v, a_s, b_s, jnp.transpose(st_wkv, (1, 2, 3, 0)), seq)
    return zparts, zpg, (y_s, bonus_s, g_s, jnp.transpose(wkv_s, (3, 0, 1, 2)))


def _last_shift_row(zparts, row0, b, t):
    if b <= 8:
        last = lambda z: jnp.concatenate(
            [lax.slice(z, (row0 + (i + 1) * t - 1, 0), (row0 + (i + 1) * t, z.shape[1])) for i in range(b)])
    else:
        last = lambda z: lax.slice(z, (row0 + t - 1, 0), (row0 + b * t, z.shape[1]), (t, 1))
    return jnp.concatenate([last(z) for z in zparts], axis=-1)[:, None, :D_SHIFT]


def _prompt_layer(x, z, wts):
    b, t, _ = x.shape
    n = b * t
    x2d = x.reshape(n, D_MODEL)
    zparts, zpg = z
    y, bonus, g, wkv, bf16_weights = _wkv_prompt(zparts, b, t, wts["prep"], wts["f32_matmul_weights"])
    wts = wts["with_bf16"](*bf16_weights)
    flat = lambda a: a.reshape(n, a.shape[-1])
    x1, h1 = _merge(flat(y), flat(bonus), flat(g), zpg, zpg, x2d, 0, wts["merge"],
                    tm=_largest_tile(t, ROWS_MERGE), seq_len=t)
    ffn_tm = _largest_tile(t, ROWS_FFN)
    out, tail = _ffn_prompt(x1, h1, t, wts["ffn"], tm=ffn_tm, fk=COLS_FFN)
    shift = _last_shift_row(zparts, 0, b, t)
    pool = jnp.stack([zpg[(i + 1) * t - POOL_HIST:(i + 1) * t, :D_POOL] for i in range(b)])
    tiles = t // ffn_tm
    conv = tail.reshape(b, tiles, 8, D_FF)[:, tiles - 1, 8 - (CONV_W - 1):, :]
    return wts, (out.reshape(b, t, D_MODEL), shift, wkv, pool, conv)


def _sample_layer(x, z, row0, st_pool, st_conv, wts):
    b, t, _ = x.shape
    n = b * t
    x2d = x.reshape(n, D_MODEL)
    zparts, zpg, (y, bonus, g, wkv) = z
    d, pool = _pool_sample(jnp.swapaxes(st_pool, 0, 1), zpg, row0, t)
    pool = jnp.swapaxes(pool, 0, 1)
    x1, h1 = _merge(y, bonus, g, d, zpg, x2d, row0, wts["merge"],
                    tm=_largest_tile(math.gcd(row0, n), ROWS_MERGE))
    out, conv = _ffn_sample(x1, h1, st_conv, t, wts["ffn"], fk=COLS_FFN)
    shift = _last_shift_row(zparts, row0, b, t)
    return out.reshape(b, t, D_MODEL), shift, wkv, pool, conv


def kernel(x_prompt, x_sample, state_shift, state_wkv, state_pool, state_conv, norm_pre_mix, w_in, mu_shift, w0, w2, a0, a2, g2, k_k, k_a, r_k, lnx_w, lnx_b, w_branch_a, pool_w, pool_scale, w_branch_b, w_out, norm_post_mix, norm_pre_ffn, w_ffn_in, conv_w, conv_b, w_ffn_out, norm_post_ffn):
    weights = (norm_pre_mix, w_in, mu_shift, w0, w2, a0, a2, g2, k_k, k_a, r_k, lnx_w, lnx_b,
               w_branch_a, pool_w, pool_scale, w_branch_b, w_out, norm_post_mix,
               norm_pre_ffn, w_ffn_in, conv_w, conv_b, w_ffn_out, norm_post_ffn)
    depth = w_in.shape[0]
    yp, ys = x_prompt, x_sample
    p_states, s_states = [], []
    for l in range(depth):
        wts = _layer_weights(l, *weights)
        n_prompt = yp.shape[0] * yp.shape[1]
        z = _project(yp.reshape(n_prompt, D_MODEL), ys.reshape(-1, D_MODEL), ys.shape[1],
                     state_shift[l], state_wkv[l], wts)
        bf16_wts, (yp, *ps) = _prompt_layer(yp, z[:2], wts)
        ys, *ss = _sample_layer(ys, z, n_prompt, state_pool[l], state_conv[l], bf16_wts)
        p_states.append(ps)
        s_states.append(ss)
    stack = lambda states, i: jnp.stack([s[i] for s in states])
    return (yp, ys,
            stack(p_states, 0), stack(p_states, 1), stack(p_states, 2), stack(p_states, 3),
            stack(s_states, 0), stack(s_states, 1), stack(s_states, 2), stack(s_states, 3))
```
